```python
import jax
import jax.numpy as jnp
from jax import lax
import numpy as np

D_MODEL = 1024
BATCH = 32
SEQ = 256
DEPTH = 2
DEC_BATCH = 2
DEC_SEQ = 1024
PAST_LEN = 512

GRID_W = 64
HEAD_DIM = 64
A_HEADS = 4
A_KV_HEADS = 2
B_HEADS = 4
NA_ROWS = 8
NA_COLS = 16
C_WIDTH = 256
CONV_WIDTH = 31
SGU_WIDTH = 256
SGU_GROUPS = 4
SGU_CHUNK = 128
N_BRANCH = 4
BRANCH_W = 256
N_EXPERTS = 32
TOP_K = 4
D_FF = 1024
SWIGLU_ALPHA = 1.702
SWIGLU_LIMIT = 7.0
MOE_BLOCK = 128
ROPE_THETA = 10000.0
ROPE_PAIRS = HEAD_DIM // 4
Q_BLOCK = 128
N_MOD = 6
EPS = 1e-6
NEG_INF = -1e30

A_Q = A_HEADS * HEAD_DIM
A_KV = A_KV_HEADS * HEAD_DIM
B_W = B_HEADS * HEAD_DIM
IN_SIZES = (A_Q, A_KV, A_KV, B_W, B_W, B_W, 2 * C_WIDTH, 2 * SGU_WIDTH, N_BRANCH * D_MODEL)
IN_COLS = A_Q + 2 * A_KV + 3 * B_W + 2 * C_WIDTH + 2 * SGU_WIDTH + N_BRANCH * D_MODEL

kernel_name = 'hybrid_flow_prefix_trunk'


def rms_norm(x, g):
    xf = x.astype(jnp.float32)
    y = xf * lax.rsqrt(jnp.mean(xf * xf, axis=-1, keepdims=True) + EPS)
    return (y * g.astype(jnp.float32)).astype(x.dtype)


def layer_norm(x, g, b):
    xf = x.astype(jnp.float32)
    mu = jnp.mean(xf, axis=-1, keepdims=True)
    xc = xf - mu
    var = jnp.mean(xc * xc, axis=-1, keepdims=True)
    return (xc * lax.rsqrt(var + EPS) * g.astype(jnp.float32) + b.astype(jnp.float32)).astype(x.dtype)


def modulation(cvec, w_mod, b_mod):
    n = cvec.shape[0]
    m = (jnp.matmul(jax.nn.silu(cvec), w_mod) + b_mod).reshape(n, N_MOD, 1, D_MODEL)
    return tuple(m[:, i] for i in range(N_MOD))


def split_combined(z):
    offsets = [int(o) for o in np.cumsum(IN_SIZES)[:-1]]
    return jnp.split(z, offsets, axis=-1)


def axial_rope(n_tok):
    t = jnp.arange(n_tok)
    row = (t // GRID_W).astype(jnp.float32)
    col = (t % GRID_W).astype(jnp.float32)
    inv = ROPE_THETA ** (-jnp.arange(ROPE_PAIRS, dtype=jnp.float32) / ROPE_PAIRS)
    ang = jnp.concatenate([row[:, None] * inv, col[:, None] * inv], axis=-1)
    return jnp.cos(ang), jnp.sin(ang)


def apply_rope(x, cos, sin):
    xp = x.reshape(x.shape[:-1] + (HEAD_DIM // 2, 2))
    a, b = xp[..., 0], xp[..., 1]
    c = cos[None, :, None, :].astype(x.dtype)
    s = sin[None, :, None, :].astype(x.dtype)
    return jnp.stack([a * c - b * s, a * s + b * c], axis=-1).reshape(x.shape)


def blocked_attention(q, k, v):
    bsz, sq, nh, dh = q.shape
    hkv = k.shape[2]
    grp = nh // hkv
    nb = sq // Q_BLOCK
    qb = q.reshape(bsz, nb, Q_BLOCK, hkv, grp, dh).transpose(1, 0, 2, 3, 4, 5)
    scale = dh ** -0.5

    def one_block(qblk):
        s = jnp.einsum('bqhgd,bshd->bhgqs', qblk, k, preferred_element_type=jnp.float32) * scale
        p = jax.nn.softmax(s, axis=-1).astype(v.dtype)
        return jnp.einsum('bhgqs,bshd->bqhgd', p, v)

    o = lax.map(one_block, qb)
    return o.transpose(1, 0, 2, 3, 4, 5).reshape(bsz, sq, nh * dh)


def neighbourhood_attention(q, k, v, k_ctx, v_ctx, rpb):
    bsz, s_len, nh, dh = q.shape
    rows = s_len // GRID_W
    wr = min(NA_ROWS, rows)
    r = np.arange(rows)
    r_start = np.clip(r - wr // 2, 0, rows - wr)
    band_rows = r_start[:, None] + np.arange(wr)[None, :]
    cidx = np.arange(GRID_W)
    c_start = np.clip(cidx - NA_COLS // 2, 0, GRID_W - NA_COLS)
    col_in = (cidx[None, :] >= c_start[:, None]) & (cidx[None, :] < c_start[:, None] + NA_COLS)
    row_off = band_rows - r[:, None] + NA_ROWS - 1
    col_off = np.clip(cidx[None, :] - cidx[:, None] + NA_COLS - 1, 0, 2 * NA_COLS - 2)
    bias = rpb.astype(jnp.float32)[:, row_off][..., col_off]
    bias = jnp.where(col_in, bias, NEG_INF).transpose(1, 0, 3, 2, 4)
    qg = q.reshape(bsz, rows, GRID_W, nh, dh)
    k_band = k.reshape(bsz, rows, GRID_W, nh, dh)[:, band_rows]
    v_band = v.reshape(bsz, rows, GRID_W, nh, dh)[:, band_rows]
    scale = dh ** -0.5
    s_loc = jnp.einsum('brqhd,brjkhd->brhqjk', qg, k_band, preferred_element_type=jnp.float32) * scale + bias[None]
    s_ctx = jnp.einsum('brqhd,blhd->brhql', qg, k_ctx, preferred_element_type=jnp.float32) * scale
    n_loc = wr * GRID_W
    s_all = jnp.concatenate([s_loc.reshape(bsz, rows, nh, GRID_W, n_loc), s_ctx], axis=-1)
    p = jax.nn.softmax(s_all, axis=-1).astype(v.dtype)
    p_loc = p[..., :n_loc].reshape(bsz, rows, nh, GRID_W, wr, GRID_W)
    p_ctx = p[..., n_loc:]
    o = jnp.einsum('brhqjk,brjkhd->brqhd', p_loc, v_band) + jnp.einsum('brhql,blhd->brqhd', p_ctx, v_ctx)
    return o.reshape(bsz, s_len, nh * dh)


def conformer_conv(z, conv_w, conv_b, ln_g, ln_b):
    a, g = jnp.split(z, 2, axis=-1)
    u = a * jax.nn.sigmoid(g)
    y = lax.conv_general_dilated(u, conv_w[:, None, :], window_strides=(1,),
                                 padding=[(CONV_WIDTH // 2, CONV_WIDTH // 2)],
                                 dimension_numbers=('NWC', 'WIO', 'NWC'),
                                 feature_group_count=C_WIDTH) + conv_b
    return jax.nn.silu(layer_norm(y, ln_g, ln_b))


def chunk_sgu(z, ln_g, ln_b, w_s, b_s):
    z = jax.nn.gelu(z, approximate=False)
    u, v = jnp.split(z, 2, axis=-1)
    v = layer_norm(v, ln_g, ln_b)
    bsz, s_len, _ = v.shape
    vc = v.reshape(bsz, s_len // SGU_CHUNK, SGU_CHUNK, SGU_GROUPS, SGU_WIDTH // SGU_GROUPS)
    mixed = jnp.einsum('gpq,bnqgc->bnpgc', w_s, vc) + b_s.T[None, None, :, :, None]
    return u * mixed.reshape(bsz, s_len, SGU_WIDTH)


def gated_merge(branches, gate_logits, w_branch, w_out):
    bsz, s_len, _ = gate_logits.shape
    gates = jax.nn.sigmoid(gate_logits.reshape(bsz, s_len, N_BRANCH, D_MODEL))
    merged = gates[:, :, 0] * jnp.matmul(branches[0], w_branch[0])
    for i in range(1, N_BRANCH):
        merged = merged + gates[:, :, i] * jnp.matmul(branches[i], w_branch[i])
    return jnp.matmul(merged, w_out)


def moe_ffn(h, router_w, router_b, w1, b1, w2, b2):
    bsz, s_len, dm = h.shape
    n_tok = bsz * s_len
    tok = h.reshape(n_tok, dm)
    logits = jnp.matmul(tok, router_w, preferred_element_type=jnp.float32) + router_b.astype(jnp.float32)
    top_val, top_idx = lax.top_k(logits, TOP_K)
    top_w = jax.nn.softmax(top_val, axis=-1)
    n_assign = n_tok * TOP_K
    e_flat = top_idx.reshape(n_assign)
    t_flat = jnp.repeat(jnp.arange(n_tok, dtype=jnp.int32), TOP_K)
    w_flat = top_w.reshape(n_assign)
    order = jnp.argsort(e_flat)
    e_sorted = e_flat[order]
    counts = jnp.bincount(e_flat, length=N_EXPERTS)
    padded = (counts + MOE_BLOCK - 1) // MOE_BLOCK * MOE_BLOCK
    pad_end = jnp.cumsum(padded)
    pad_start = pad_end - padded
    grp_start = jnp.cumsum(counts) - counts
    dest = pad_start[e_sorted] + jnp.arange(n_assign) - grp_start[e_sorted]
    n_blocks = -(-n_assign // MOE_BLOCK) + N_EXPERTS
    n_rows = n_blocks * MOE_BLOCK
    row_tok = jnp.zeros((n_rows,), jnp.int32).at[dest].set(t_flat[order])
    row_w = jnp.zeros((n_rows,), jnp.float32).at[dest].set(w_flat[order])
    blk_exp = jnp.minimum(jnp.searchsorted(pad_end, jnp.arange(n_blocks) * MOE_BLOCK, side='right'), N_EXPERTS - 1)

    def expert_block(args):
        idx, e = args
        xb = tok[idx]
        hid = jnp.matmul(xb, w1[e]) + b1[e]
        glu = jnp.minimum(hid[:, :D_FF], SWIGLU_LIMIT)
        lin = jnp.clip(hid[:, D_FF:], -SWIGLU_LIMIT, SWIGLU_LIMIT)
        act = glu * jax.nn.sigmoid(SWIGLU_ALPHA * glu) * (lin + 1)
        return jnp.matmul(act, w2[e]) + b2[e]

    out = lax.map(expert_block, (row_tok.reshape(n_blocks, MOE_BLOCK), blk_exp))
    out = out.reshape(n_rows, dm) * row_w[:, None].astype(h.dtype)
    return jax.ops.segment_sum(out, row_tok, num_segments=n_tok).reshape(bsz, s_len, dm)


def trunk_layer(x, mods, lw, ctx=None, rope=None):
    bsz, s_len, _ = x.shape
    sh1, sc1, g1, sh2, sc2, g2 = mods
    h = rms_norm(x, lw['norm1_g']) * (1 + sc1) + sh1
    aq, ak, av, bq, bk, bv, cz, dz, gz = split_combined(jnp.matmul(h, lw['w_in']))
    aq = rms_norm(aq.reshape(bsz, s_len, A_HEADS, HEAD_DIM), lw['q_norm_g'])
    ak = rms_norm(ak.reshape(bsz, s_len, A_KV_HEADS, HEAD_DIM), lw['k_norm_g'])
    av = av.reshape(bsz, s_len, A_KV_HEADS, HEAD_DIM)
    bq = bq.reshape(bsz, s_len, B_HEADS, HEAD_DIM)
    bk = bk.reshape(bsz, s_len, B_HEADS, HEAD_DIM)
    bv = bv.reshape(bsz, s_len, B_HEADS, HEAD_DIM)
    if ctx is None:
        ya = blocked_attention(aq, ak, av)
        yb = blocked_attention(bq, bk, bv)
        new_ctx = (ak, av, bk, bv)
    else:
        cak, cav, cbk, cbv = ctx
        cos, sin = rope
        ya = blocked_attention(apply_rope(aq, cos, sin),
                               jnp.concatenate([apply_rope(ak, cos, sin), cak], axis=1),
                               jnp.concatenate([av, cav], axis=1))
        yb = neighbourhood_attention(bq, bk, bv, cbk, cbv, lw['na_rpb'])
        new_ctx = None
    yc = conformer_conv(cz, lw['conv_w'], lw['conv_b'], lw['conv_ln_g'], lw['conv_ln_b'])
    yd = chunk_sgu(dz, lw['sgu_ln_g'], lw['sgu_ln_b'], lw['sgu_w'], lw['sgu_b'])
    x = x + g1 * gated_merge((ya, yb, yc, yd), gz + lw['b_gate'], lw['w_branch'], lw['w_out'])
    h2 = rms_norm(x, lw['norm2_g']) * (1 + sc2) + sh2
    x = x + g2 * moe_ffn(h2, lw['router_w'], lw['router_b'], lw['exp_w1'], lw['exp_b1'], lw['exp_w2'], lw['exp_b2'])
    return x, new_ctx


def setup_inputs(seed: int = 0) -> dict:
    key = jax.random.key(seed)
    keys = jax.random.split(key, 40)

    def nrm(i, shape, scale=1.0):
        return jax.random.normal(keys[i], shape, jnp.float32) * scale

    L = DEPTH
    return {
        'x_prompt': nrm(0, (BATCH, SEQ, D_MODEL)),
        'x_sample': nrm(1, (DEC_BATCH, DEC_SEQ, D_MODEL)),
        'cache_attn_k': nrm(2, (DEC_BATCH, DEPTH, PAST_LEN, A_KV_HEADS, HEAD_DIM)),
        'cache_attn_v': nrm(3, (DEC_BATCH, DEPTH, PAST_LEN, A_KV_HEADS, HEAD_DIM)),
        'cache_na_k': nrm(4, (DEC_BATCH, DEPTH, PAST_LEN, B_HEADS, HEAD_DIM)),
        'cache_na_v': nrm(5, (DEC_BATCH, DEPTH, PAST_LEN, B_HEADS, HEAD_DIM)),
        'c': nrm(6, (DEC_BATCH, D_MODEL)),
        'c_ctx': nrm(7, (D_MODEL,)),
        'w_mod': nrm(8, (L, D_MODEL, N_MOD * D_MODEL), 0.5 * D_MODEL ** -0.5),
        'b_mod': nrm(9, (L, N_MOD * D_MODEL), 0.02),
        'norm1_g': 1.0 + nrm(10, (L, D_MODEL), 0.02),
        'norm2_g': 1.0 + nrm(11, (L, D_MODEL), 0.02),
        'w_in': nrm(12, (L, D_MODEL, IN_COLS), D_MODEL ** -0.5),
        'b_gate': nrm(13, (L, N_BRANCH * D_MODEL), 0.02),
        'q_norm_g': 1.0 + nrm(14, (L, HEAD_DIM), 0.02),
        'k_norm_g': 1.0 + nrm(15, (L, HEAD_DIM), 0.02),
        'na_rpb': nrm(16, (L, B_HEADS, 2 * NA_ROWS - 1, 2 * NA_COLS - 1), 0.1),
        'conv_w': nrm(17, (L, CONV_WIDTH, C_WIDTH), CONV_WIDTH ** -0.5),
        'conv_b': nrm(18, (L, C_WIDTH), 0.02),
        'conv_ln_g': 1.0 + nrm(19, (L, C_WIDTH), 0.02),
        'conv_ln_b': nrm(20, (L, C_WIDTH), 0.02),
        'sgu_ln_g': 1.0 + nrm(21, (L, SGU_WIDTH), 0.02),
        'sgu_ln_b': nrm(22, (L, SGU_WIDTH), 0.02),
        'sgu_w': nrm(23, (L, SGU_GROUPS, SGU_CHUNK, SGU_CHUNK), SGU_CHUNK ** -0.5),
        'sgu_b': 1.0 + nrm(24, (L, SGU_GROUPS, SGU_CHUNK), 0.02),
        'w_branch': nrm(25, (L, N_BRANCH, BRANCH_W, D_MODEL), BRANCH_W ** -0.5),
        'w_out': nrm(26, (L, D_MODEL, D_MODEL), D_MODEL ** -0.5),
        'router_w': nrm(27, (L, D_MODEL, N_EXPERTS), D_MODEL ** -0.5),
        'router_b': nrm(28, (L, N_EXPERTS), 0.01),
        'exp_w1': nrm(29, (L, N_EXPERTS, D_MODEL, 2 * D_FF), D_MODEL ** -0.5),
        'exp_b1': nrm(30, (L, N_EXPERTS, 2 * D_FF), 0.02),
        'exp_w2': nrm(31, (L, N_EXPERTS, D_FF, D_MODEL), D_FF ** -0.5),
        'exp_b2': nrm(32, (L, N_EXPERTS, D_MODEL), 0.02),
        'final_g': 1.0 + nrm(33, (D_MODEL,), 0.02),
    }


def reference(x_prompt, x_sample, cache_attn_k, cache_attn_v, cache_na_k, cache_na_v, c, c_ctx,
              w_mod, b_mod, norm1_g, norm2_g, w_in, b_gate, q_norm_g, k_norm_g, na_rpb,
              conv_w, conv_b, conv_ln_g, conv_ln_b, sgu_ln_g, sgu_ln_b, sgu_w, sgu_b,
              w_branch, w_out, router_w, router_b, exp_w1, exp_b1, exp_w2, exp_b2, final_g):
    rope = axial_rope(x_sample.shape[1])
    xp = x_prompt
    xs = x_sample
    ak_list, av_list, bk_list, bv_list = [], [], [], []
    for l in range(DEPTH):
        lw = dict(norm1_g=norm1_g[l], norm2_g=norm2_g[l], w_in=w_in[l], b_gate=b_gate[l],
                  q_norm_g=q_norm_g[l], k_norm_g=k_norm_g[l], na_rpb=na_rpb[l],
                  conv_w=conv_w[l], conv_b=conv_b[l], conv_ln_g=conv_ln_g[l], conv_ln_b=conv_ln_b[l],
                  sgu_ln_g=sgu_ln_g[l], sgu_ln_b=sgu_ln_b[l], sgu_w=sgu_w[l], sgu_b=sgu_b[l],
                  w_branch=w_branch[l], w_out=w_out[l], router_w=router_w[l], router_b=router_b[l],
                  exp_w1=exp_w1[l], exp_b1=exp_b1[l], exp_w2=exp_w2[l], exp_b2=exp_b2[l])
        mods_ctx = modulation(c_ctx[None, :], w_mod[l], b_mod[l])
        mods_lat = modulation(c, w_mod[l], b_mod[l])
        xp, (ak, av, bk, bv) = trunk_layer(xp, mods_ctx, lw)
        ak_list.append(ak)
        av_list.append(av)
        bk_list.append(bk)
        bv_list.append(bv)
        ctx = (cache_attn_k[:, l], cache_attn_v[:, l], cache_na_k[:, l], cache_na_v[:, l])
        xs, _ = trunk_layer(xs, mods_lat, lw, ctx=ctx, rope=rope)
    y_prompt = rms_norm(xp, final_g)
    y_sample = rms_norm(xs, final_g)
    new_attn_k = jnp.stack(ak_list, axis=1)
    new_attn_v = jnp.stack(av_list, axis=1)
    new_na_k = jnp.stack(bk_list, axis=1)
    new_na_v = jnp.stack(bv_list, axis=1)
    return (y_prompt, y_sample, new_attn_k, new_attn_v, new_na_k, new_na_v)
```

```python
import functools

import numpy as np
import jax
import jax.numpy as jnp
from jax import lax
from jax.experimental import pallas as pl
from jax.experimental.pallas import tpu as pltpu

D_MODEL = 1024
BATCH = 32
SEQ = 256
DEPTH = 2
DEC_BATCH = 2
DEC_SEQ = 1024
PAST_LEN = 512
GRID_W = 64
HEAD_DIM = 64
A_HEADS = 4
A_KV_HEADS = 2
B_HEADS = 4
NA_ROWS = 8
NA_COLS = 16
C_WIDTH = 256
CONV_WIDTH = 31
SGU_WIDTH = 256
SGU_GROUPS = 4
SGU_CHUNK = 128
N_BRANCH = 4
BRANCH_W = 256
N_EXPERTS = 32
TOP_K = 4
D_FF = 1024
SWIGLU_ALPHA = 1.702
SWIGLU_LIMIT = 7.0
MOE_BLOCK = 128
ROPE_THETA = 10000.0
ROPE_PAIRS = HEAD_DIM // 4
N_MOD = 6
EPS = 1e-6
NEG_INF = -1e30

A_Q = A_HEADS * HEAD_DIM
A_KV = A_KV_HEADS * HEAD_DIM
B_W = B_HEADS * HEAD_DIM
MIX_SIZES = (A_Q, A_KV, A_KV, B_W, B_W, B_W, 2 * C_WIDTH, 2 * SGU_WIDTH)
MIX_COLS = sum(MIX_SIZES)

N_P = BATCH * SEQ
N_S = DEC_BATCH * DEC_SEQ
N_TOK = N_P + N_S
N_ASSIGN = N_TOK * TOP_K
N_BLOCKS = N_ASSIGN // MOE_BLOCK + N_EXPERTS
N_ROWS = N_BLOCKS * MOE_BLOCK
GRID_ROWS = DEC_SEQ // GRID_W
NA_WR = min(NA_ROWS, GRID_ROWS)
N_LOC = NA_WR * GRID_W

SUBLANES = 8
LANES = 128
ROW_TILES = D_MODEL // LANES
VMEM_LIMIT = 56 * 1024 * 1024

F32 = jnp.float32
BF16 = jnp.bfloat16


def _params(n_axes, vmem=None):
    return pltpu.CompilerParams(
        dimension_semantics=("arbitrary",) * n_axes,
        vmem_limit_bytes=vmem if vmem is not None else VMEM_LIMIT)


def _mod_row(start):
    return jnp.where(start < N_P, 0, 1 + (start - N_P) // DEC_SEQ)


def _bdot(a, b):
    return jnp.dot(a.astype(BF16), b.astype(BF16), preferred_element_type=F32)


def _bdot_nt(a, b):
    return lax.dot_general(a.astype(BF16), b.astype(BF16), (((1,), (1,)), ((), ())),
                           preferred_element_type=F32)


def _sigmoid(x):
    return 1.0 / (1.0 + jnp.exp(-x))


MOD_TN = 1536


def _mod_kernel(c_ref, w_ref, b_ref, o_ref):
    c = c_ref[...]
    s = c * _sigmoid(c)
    o_ref[...] = _bdot(s, w_ref[...]) + b_ref[...]


def _modulation(cvec, w_mod, b_mod):
    n_col = N_MOD * D_MODEL
    return pl.pallas_call(
        _mod_kernel,
        grid=(DEPTH, n_col // MOD_TN),
        in_specs=[
            pl.BlockSpec((SUBLANES, D_MODEL), lambda l, j: (0, 0)),
            pl.BlockSpec((None, D_MODEL, MOD_TN), lambda l, j: (l, 0, j)),
            pl.BlockSpec((None, 1, MOD_TN), lambda l, j: (l, 0, j)),
        ],
        out_specs=pl.BlockSpec((None, SUBLANES, MOD_TN), lambda l, j: (l, 0, j)),
        out_shape=jax.ShapeDtypeStruct((DEPTH, SUBLANES, n_col), F32),
        compiler_params=_params(2),
        name="modulation",
    )(cvec, w_mod, b_mod.reshape(DEPTH, 1, n_col))


IN_TM = 512


def _norm_mod(x, g, shift, scale):
    y = x * lax.rsqrt(jnp.mean(x * x, axis=-1, keepdims=True) + EPS) * g
    return y * (1.0 + scale) + shift


def _in_kernel(x_ref, mod_ref, g_ref, w_ref, *out_refs):
    m = mod_ref[...]
    h = _norm_mod(x_ref[...], g_ref[...], m[0:1], m[1:2])
    z = jnp.dot(h.astype(BF16), w_ref[...], preferred_element_type=F32)
    off = 0
    for o_ref, sz in zip(out_refs, MIX_SIZES):
        o_ref[...] = z[:, off:off + sz]
        off += sz


def _in_proj(x, mods_l, g1, w_mix):
    return pl.pallas_call(
        _in_kernel,
        grid=(N_TOK // IN_TM,),
        in_specs=[
            pl.BlockSpec((IN_TM, D_MODEL), lambda i: (i, 0)),
            pl.BlockSpec((None, N_MOD, D_MODEL), lambda i: (_mod_row(i * IN_TM), 0, 0)),
            pl.BlockSpec((1, D_MODEL), lambda i: (0, 0)),
            pl.BlockSpec((D_MODEL, MIX_COLS), lambda i: (0, 0)),
        ],
        out_specs=[pl.BlockSpec((IN_TM, sz), lambda i: (i, 0)) for sz in MIX_SIZES],
        out_shape=[jax.ShapeDtypeStruct((N_TOK, sz), F32) for sz in MIX_SIZES],
        compiler_params=_params(1),
        name="in_proj",
    )(x, mods_l, g1, w_mix)


def _head_rms(x, g):
    n_heads = x.shape[-1] // HEAD_DIM
    seg = lax.broadcasted_iota(jnp.int32, x.shape, 1) // HEAD_DIM
    xx = x * x
    inv = jnp.zeros_like(x)
    for h in range(n_heads):
        ms = jnp.sum(jnp.where(seg == h, xx, 0.0), axis=-1, keepdims=True) * (1.0 / HEAD_DIM)
        inv = jnp.where(seg == h, lax.rsqrt(ms + EPS), inv)
    return x * inv * g


def _softmax_pv(score_parts, value_parts):
    m = score_parts[0].max(axis=-1, keepdims=True)
    for s in score_parts[1:]:
        m = jnp.maximum(m, s.max(axis=-1, keepdims=True))
    den = None
    acc = None
    for s, v in zip(score_parts, value_parts):
        e = jnp.exp(s - m)
        d = e.sum(axis=-1, keepdims=True)
        a = _bdot(e, v)
        den = d if den is None else den + d
        acc = a if acc is None else acc + a
    return acc / den


def _head(x, h):
    return x[:, h * HEAD_DIM:(h + 1) * HEAD_DIM]


SCALE = HEAD_DIM ** -0.5


def _prompt_attn_kernel(aq_ref, ak_ref, av_ref, bq_ref, bk_ref, bv_ref, gq_ref, gk_ref,
                        ya_ref, yb_ref, kn_ref):
    aq = _head_rms(aq_ref[...], gq_ref[...])
    ak = _head_rms(ak_ref[...], gk_ref[...])
    kn_ref[...] = ak
    av = av_ref[...]
    grp = A_HEADS // A_KV_HEADS
    outs = []
    for h in range(A_HEADS):
        s = _bdot_nt(_head(aq, h), _head(ak, h // grp)) * SCALE
        outs.append(_softmax_pv([s], [_head(av, h // grp)]))
    ya_ref[...] = jnp.concatenate(outs, axis=-1)
    bq = bq_ref[...]
    bk = bk_ref[...]
    bv = bv_ref[...]
    outs = []
    for h in range(B_HEADS):
        s = _bdot_nt(_head(bq, h), _head(bk, h)) * SCALE
        outs.append(_softmax_pv([s], [_head(bv, h)]))
    yb_ref[...] = jnp.concatenate(outs, axis=-1)


def _prompt_attn(aq, ak, av, bq, bk, bv, gq, gk):
    def spec(w):
        return pl.BlockSpec((SEQ, w), lambda b: (b, 0))

    return pl.pallas_call(
        _prompt_attn_kernel,
        grid=(BATCH,),
        in_specs=[spec(A_Q), spec(A_KV), spec(A_KV), spec(B_W), spec(B_W), spec(B_W),
                  pl.BlockSpec((1, A_Q), lambda b: (0, 0)),
                  pl.BlockSpec((1, A_KV), lambda b: (0, 0))],
        out_specs=[spec(A_Q), spec(B_W), spec(A_KV)],
        out_shape=[jax.ShapeDtypeStruct((N_P, A_Q), F32),
                   jax.ShapeDtypeStruct((N_P, B_W), F32),
                   jax.ShapeDtypeStruct((N_P, A_KV), F32)],
        compiler_params=_params(1),
        name="prompt_attn",
    )(aq, ak, av, bq, bk, bv, gq, gk)


QB = 128


def _rope(x, cos, sin_signed):
    n = x.shape[-1]
    nxt = pltpu.roll(x, n - 1, 1)
    prv = pltpu.roll(x, 1, 1)
    even = (lax.broadcasted_iota(jnp.int32, x.shape, 1) % 2) == 0
    return x * cos + jnp.where(even, nxt, prv) * sin_signed


def _sample_attn_kernel(q_ref, k_ref, v_ref, ck_ref, cv_ref, cosq_ref, sinq_ref, cosk_ref, sink_ref,
                        gq_ref, gk_ref, o_ref):
    q = _rope(_head_rms(q_ref[...], gq_ref[...]), cosq_ref[...], sinq_ref[...])
    k = _rope(_head_rms(k_ref[...], gk_ref[...]), cosk_ref[...], sink_ref[...])
    v = v_ref[...]
    ck = ck_ref[...]
    cv = cv_ref[...]
    grp = A_HEADS // A_KV_HEADS
    outs = []
    for h in range(A_HEADS):
        j = h // grp
        qh = _head(q, h)
        s1 = _bdot_nt(qh, _head(k, j)) * SCALE
        s2 = _bdot_nt(qh, _head(ck, j)) * SCALE
        outs.append(_softmax_pv([s1, s2], [_head(v, j), _head(cv, j)]))
    o_ref[...] = jnp.concatenate(outs, axis=-1)


def _sample_attn(aq, ak, av, cache_k, cache_v, cos_t, sin_t, gq, gk, layer):
    nqb = DEC_SEQ // QB
    q0 = N_P // QB
    k0 = N_P // DEC_SEQ
    return pl.pallas_call(
        _sample_attn_kernel,
        grid=(DEC_BATCH, nqb),
        in_specs=[
            pl.BlockSpec((QB, A_Q), lambda b, i: (q0 + b * nqb + i, 0)),
            pl.BlockSpec((DEC_SEQ, A_KV), lambda b, i: (k0 + b, 0)),
            pl.BlockSpec((DEC_SEQ, A_KV), lambda b, i: (k0 + b, 0)),
            pl.BlockSpec((None, None, PAST_LEN, A_KV), lambda b, i: (b, layer, 0, 0)),
            pl.BlockSpec((None, None, PAST_LEN, A_KV), lambda b, i: (b, layer, 0, 0)),
            pl.BlockSpec((QB, A_Q), lambda b, i: (i, 0)),
            pl.BlockSpec((QB, A_Q), lambda b, i: (i, 0)),
            pl.BlockSpec((DEC_SEQ, A_KV), lambda b, i: (0, 0)),
            pl.BlockSpec((DEC_SEQ, A_KV), lambda b, i: (0, 0)),
            pl.BlockSpec((1, A_Q), lambda b, i: (0, 0)),
            pl.BlockSpec((1, A_KV), lambda b, i: (0, 0)),
        ],
        out_specs=pl.BlockSpec((QB, A_Q), lambda b, i: (b * nqb + i, 0)),
        out_shape=jax.ShapeDtypeStruct((N_S, A_Q), F32),
        compiler_params=_params(2),
        name="sample_attn",
    )(aq, ak, av, cache_k, cache_v, cos_t, sin_t, cos_t, sin_t, gq, gk)


def _na_kernel(q_ref, k_ref, v_ref, ck_ref, cv_ref, bias_ref, o_ref):
    r = pl.program_id(1)
    r_start = jnp.clip(r - NA_WR // 2, 0, GRID_ROWS - NA_WR)
    base = pl.multiple_of(r_start * GRID_W, GRID_W)
    q = q_ref[...]
    kb = k_ref[pl.ds(base, N_LOC), :]
    vb = v_ref[pl.ds(base, N_LOC), :]
    ck = ck_ref[...]
    cv = cv_ref[...]
    outs = []
    for h in range(B_HEADS):
        qh = _head(q, h)
        s1 = _bdot_nt(qh, _head(kb, h)) * SCALE + bias_ref[h]
        s2 = _bdot_nt(qh, _head(ck, h)) * SCALE
        outs.append(_softmax_pv([s1, s2], [_head(vb, h), _head(cv, h)]))
    o_ref[...] = jnp.concatenate(outs, axis=-1)


def _na_bias(rpb):
    r = np.arange(GRID_ROWS)
    r_start = np.clip(r - NA_WR // 2, 0, GRID_ROWS - NA_WR)
    band_rows = r_start[:, None] + np.arange(NA_WR)[None, :]
    cidx = np.arange(GRID_W)
    c_start = np.clip(cidx - NA_COLS // 2, 0, GRID_W - NA_COLS)
    col_in = (cidx[None, :] >= c_start[:, None]) & (cidx[None, :] < c_start[:, None] + NA_COLS)
    row_off = band_rows - r[:, None] + NA_ROWS - 1
    col_off = np.clip(cidx[None, :] - cidx[:, None] + NA_COLS - 1, 0, 2 * NA_COLS - 2)
    bias = rpb.astype(F32)[:, row_off][..., col_off]
    bias = jnp.where(col_in, bias, NEG_INF).transpose(1, 0, 3, 2, 4)
    return bias.reshape(GRID_ROWS, B_HEADS, GRID_W, N_LOC)


def _na_attn(bq, bk, bv, cache_k, cache_v, bias, layer):
    q0 = N_P // GRID_W
    k0 = N_P // DEC_SEQ
    return pl.pallas_call(
        _na_kernel,
        grid=(DEC_BATCH, GRID_ROWS),
        in_specs=[
            pl.BlockSpec((GRID_W, B_W), lambda b, r: (q0 + b * GRID_ROWS + r, 0)),
            pl.BlockSpec((DEC_SEQ, B_W), lambda b, r: (k0 + b, 0)),
            pl.BlockSpec((DEC_SEQ, B_W), lambda b, r: (k0 + b, 0)),
            pl.BlockSpec((None, None, PAST_LEN, B_W), lambda b, r: (b, layer, 0, 0)),
            pl.BlockSpec((None, None, PAST_LEN, B_W), lambda b, r: (b, layer, 0, 0)),
            pl.BlockSpec((None, B_HEADS, GRID_W, N_LOC), lambda b, r: (r, 0, 0, 0)),
        ],
        out_specs=pl.BlockSpec((GRID_W, B_W), lambda b, r: (b * GRID_ROWS + r, 0)),
        out_shape=jax.ShapeDtypeStruct((N_S, B_W), F32),
        compiler_params=_params(2),
        name="na_attn",
    )(bq, bk, bv, cache_k, cache_v, bias)


CONV_PAD = 16
CONV_CHUNK = 128


def _layer_norm(x, g, b):
    mu = jnp.mean(x, axis=-1, keepdims=True)
    xc = x - mu
    var = jnp.mean(xc * xc, axis=-1, keepdims=True)
    return xc * lax.rsqrt(var + EPS) * g + b


def _conv_kernel(z_ref, w_ref, cb_ref, g_ref, b_ref, o_ref, pad_ref, *, s_len):
    z = z_ref[...]
    u = z[:, :C_WIDTH] * _sigmoid(z[:, C_WIDTH:])
    pad_ref[pl.ds(0, CONV_PAD), :] = jnp.zeros((CONV_PAD, C_WIDTH), F32)
    pad_ref[pl.ds(CONV_PAD + s_len, CONV_PAD), :] = jnp.zeros((CONV_PAD, C_WIDTH), F32)
    pad_ref[pl.ds(CONV_PAD, s_len), :] = u
    w = w_ref[...]
    shift = CONV_PAD - CONV_WIDTH // 2
    for c in range(s_len // CONV_CHUNK):
        acc = jnp.zeros((CONV_CHUNK, C_WIDTH), F32)
        for k in range(CONV_WIDTH):
            acc = acc + pad_ref[pl.ds(c * CONV_CHUNK + k + shift, CONV_CHUNK), :] * w[k:k + 1]
        y = _layer_norm(acc + cb_ref[...], g_ref[...], b_ref[...])
        o_ref[pl.ds(c * CONV_CHUNK, CONV_CHUNK), :] = y * _sigmoid(y)


def _conv_call(cz, w, cb, g, b, s_len, first_blk, n_seq):
    vec = pl.BlockSpec((1, C_WIDTH), lambda i: (0, 0))
    return pl.pallas_call(
        functools.partial(_conv_kernel, s_len=s_len),
        grid=(n_seq,),
        in_specs=[pl.BlockSpec((s_len, 2 * C_WIDTH), lambda i: (first_blk + i, 0)),
                  pl.BlockSpec((CONV_WIDTH, C_WIDTH), lambda i: (0, 0)), vec, vec, vec],
        out_specs=pl.BlockSpec((s_len, C_WIDTH), lambda i: (i, 0)),
        out_shape=jax.ShapeDtypeStruct((n_seq * s_len, C_WIDTH), F32),
        scratch_shapes=[pltpu.VMEM((s_len + 2 * CONV_PAD, C_WIDTH), F32)],
        compiler_params=_params(1),
        name="conformer_conv_%d" % s_len,
    )(cz, w, cb, g, b)


SGU_TM = 512
SGU_GW = SGU_WIDTH // SGU_GROUPS


def _sgu_kernel(z_ref, g_ref, b_ref, ws_ref, bs_ref, o_ref):
    z = z_ref[...]
    z = 0.5 * z * (1.0 + lax.erf(z * (2.0 ** -0.5)))
    u = z[:, :SGU_WIDTH]
    v = _layer_norm(z[:, SGU_WIDTH:], g_ref[...], b_ref[...])
    for c in range(SGU_TM // SGU_CHUNK):
        vc = v[c * SGU_CHUNK:(c + 1) * SGU_CHUNK]
        parts = [_bdot(ws_ref[g], vc[:, g * SGU_GW:(g + 1) * SGU_GW]) for g in range(SGU_GROUPS)]
        mixed = jnp.concatenate(parts, axis=-1) + bs_ref[...]
        o_ref[pl.ds(c * SGU_CHUNK, SGU_CHUNK), :] = u[c * SGU_CHUNK:(c + 1) * SGU_CHUNK] * mixed


def _sgu(dz, g, b, ws, bs_full):
    vec = pl.BlockSpec((1, SGU_WIDTH), lambda i: (0, 0))
    return pl.pallas_call(
        _sgu_kernel,
        grid=(N_TOK // SGU_TM,),
        in_specs=[pl.BlockSpec((SGU_TM, 2 * SGU_WIDTH), lambda i: (i, 0)), vec, vec,
                  pl.BlockSpec((SGU_GROUPS, SGU_CHUNK, SGU_CHUNK), lambda i: (0, 0, 0)),
                  pl.BlockSpec((SGU_CHUNK, SGU_WIDTH), lambda i: (0, 0))],
        out_specs=pl.BlockSpec((SGU_TM, SGU_WIDTH), lambda i: (i, 0)),
        out_shape=jax.ShapeDtypeStruct((N_TOK, SGU_WIDTH), F32),
        compiler_params=_params(1),
        name="chunk_sgu",
    )(dz, g, b, ws, bs_full)


MERGE_TM = 256


def _merge_kernel(x_ref, ya_ref, yb_ref, yc_ref, yd_ref, mod_ref, g1_ref, wg_ref, bg_ref, wb_ref, wo_ref,
                  g2_ref, rwh_ref, rwl_ref, rb_ref, xm_ref, h2_ref, lg_ref):
    m = mod_ref[...]
    x = x_ref[...]
    h = _norm_mod(x, g1_ref[...], m[0:1], m[1:2]).astype(BF16)
    merged = None
    for i, y_ref in enumerate((ya_ref, yb_ref, yc_ref, yd_ref)):
        logit = jnp.dot(h, wg_ref[:, i * D_MODEL:(i + 1) * D_MODEL], preferred_element_type=F32)
        gate = _sigmoid(logit + bg_ref[:, i * D_MODEL:(i + 1) * D_MODEL])
        term = gate * jnp.dot(y_ref[...].astype(BF16), wb_ref[i], preferred_element_type=F32)
        merged = term if merged is None else merged + term
    out = jnp.dot(merged.astype(BF16), wo_ref[...], preferred_element_type=F32)
    xm = x + m[2:3] * out
    xm_ref[...] = xm
    h2 = _norm_mod(xm, g2_ref[...], m[3:4], m[4:5])
    h2_hi = h2.astype(BF16)
    h2_lo = (h2 - h2_hi.astype(F32)).astype(BF16)
    lg = jnp.dot(h2_hi, rwh_ref[...], preferred_element_type=F32)
    lg = lg + jnp.dot(h2_hi, rwl_ref[...], preferred_element_type=F32)
    lg = lg + jnp.dot(h2_lo, rwh_ref[...], preferred_element_type=F32)
    lg_ref[...] = lg + rb_ref[...]
    h2r = h2_hi.astype(F32)
    for c in range(ROW_TILES):
        h2_ref[pl.ds(c, MERGE_TM, stride=ROW_TILES), :] = h2r[:, c * LANES:(c + 1) * LANES]


def _merge(x, ya, yb, yc, yd, mods_l, g1, wg, bg, wb, wo, g2, rwh, rwl, rb):
    def tok(w):
        return pl.BlockSpec((MERGE_TM, w), lambda i: (i, 0))

    def full(*shape):
        return pl.BlockSpec(shape, lambda i: (0,) * len(shape))

    return pl.pallas_call(
        _merge_kernel,
        grid=(N_TOK // MERGE_TM,),
        in_specs=[tok(D_MODEL), tok(BRANCH_W), tok(BRANCH_W), tok(BRANCH_W), tok(BRANCH_W),
                  pl.BlockSpec((None, N_MOD, D_MODEL), lambda i: (_mod_row(i * MERGE_TM), 0, 0)),
                  full(1, D_MODEL), full(D_MODEL, N_BRANCH * D_MODEL), full(1, N_BRANCH * D_MODEL),
                  full(N_BRANCH, BRANCH_W, D_MODEL), full(D_MODEL, D_MODEL), full(1, D_MODEL),
                  full(D_MODEL, N_EXPERTS), full(D_MODEL, N_EXPERTS), full(1, N_EXPERTS)],
        out_specs=[tok(D_MODEL),
                   pl.BlockSpec((MERGE_TM * ROW_TILES, LANES), lambda i: (i, 0)),
                   tok(N_EXPERTS)],
        out_shape=[jax.ShapeDtypeStruct((N_TOK, D_MODEL), F32),
                   jax.ShapeDtypeStruct((N_TOK * ROW_TILES, LANES), F32),
                   jax.ShapeDtypeStruct((N_TOK, N_EXPERTS), F32)],
        compiler_params=_params(1),
        name="merge",
    )(x, ya, yb, yc, yd, mods_l, g1, wg, bg, wb, wo, g2, rwh, rwl, rb)


ROUTE_TM = 256


def _route_kernel(lg_ref, e_ref, w_ref, pos_ref, cnt_ref, carry_ref):
    @pl.when(pl.program_id(0) == 0)
    def _():
        carry_ref[...] = jnp.zeros_like(carry_ref)

    lg = lg_ref[...]
    lane = lax.broadcasted_iota(jnp.int32, lg.shape, 1)
    sels, vals, idxs = [], [], []
    for _ in range(TOP_K):
        mx = lg.max(axis=-1, keepdims=True)
        idx = jnp.where(lg == mx, lane, N_EXPERTS).min(axis=-1, keepdims=True)
        sel = lane == idx
        sels.append(sel)
        vals.append(mx)
        idxs.append(idx)
        lg = jnp.where(sel, -jnp.inf, lg)
    exps = [jnp.exp(v - vals[0]) for v in vals]
    den = exps[0] + exps[1] + exps[2] + exps[3]
    onehot = jnp.zeros(lg.shape, F32)
    for sel in sels:
        onehot = onehot + sel.astype(F32)
    row = lax.broadcasted_iota(jnp.int32, (ROUTE_TM, ROUTE_TM), 0)
    col = lax.broadcasted_iota(jnp.int32, (ROUTE_TM, ROUTE_TM), 1)
    tri = jnp.where(col < row, 1.0, 0.0).astype(BF16)
    rank = jnp.dot(tri, onehot.astype(BF16), preferred_element_type=F32) + carry_ref[...]
    k_lane = lax.broadcasted_iota(jnp.int32, (ROUTE_TM, TOP_K), 1)
    e_out = jnp.zeros((ROUTE_TM, TOP_K), jnp.int32)
    w_out = jnp.zeros((ROUTE_TM, TOP_K), F32)
    p_out = jnp.zeros((ROUTE_TM, TOP_K), F32)
    for k in range(TOP_K):
        e_out = jnp.where(k_lane == k, idxs[k], e_out)
        w_out = jnp.where(k_lane == k, exps[k] / den, w_out)
        pk = jnp.sum(jnp.where(sels[k], rank, 0.0), axis=-1, keepdims=True)
        p_out = jnp.where(k_lane == k, pk, p_out)
    e_ref[...] = e_out
    w_ref[...] = w_out
    pos_ref[...] = p_out.astype(jnp.int32)
    carry = carry_ref[...] + jnp.sum(onehot, axis=0, keepdims=True)
    carry_ref[...] = carry
    cnt_ref[...] = carry.astype(jnp.int32)


def _route(logits):
    def tok(w):
        return pl.BlockSpec((ROUTE_TM, w), lambda i: (i, 0))

    return pl.pallas_call(
        _route_kernel,
        grid=(N_TOK // ROUTE_TM,),
        in_specs=[tok(N_EXPERTS)],
        out_specs=[tok(TOP_K), tok(TOP_K), tok(TOP_K), pl.BlockSpec((1, N_EXPERTS), lambda i: (0, 0))],
        out_shape=[jax.ShapeDtypeStruct((N_TOK, TOP_K), jnp.int32),
                   jax.ShapeDtypeStruct((N_TOK, TOP_K), F32),
                   jax.ShapeDtypeStruct((N_TOK, TOP_K), jnp.int32),
                   jax.ShapeDtypeStruct((1, N_EXPERTS), jnp.int32)],
        scratch_shapes=[pltpu.VMEM((1, N_EXPERTS), F32)],
        compiler_params=_params(1),
        name="route",
    )(logits)


DISP_TM = 128


def _dispatch_kernel(dest_ref, h2_ref, xs_ref, sem):
    def copy(j):
        t = j // TOP_K
        d = dest_ref[0, j]
        return pltpu.make_async_copy(
            h2_ref.at[pl.ds(pl.multiple_of(t * ROW_TILES, ROW_TILES), ROW_TILES)],
            xs_ref.at[pl.ds(pl.multiple_of(d * ROW_TILES, ROW_TILES), ROW_TILES)], sem)

    def start(j, carry):
        copy(j).start()
        return carry

    def wait(j, carry):
        copy(j).wait()
        return carry

    lax.fori_loop(0, DISP_TM * TOP_K, start, 0)
    lax.fori_loop(0, DISP_TM * TOP_K, wait, 0)


def _dispatch(dest, h2rows):
    n_tiles = N_TOK // DISP_TM
    return pl.pallas_call(
        _dispatch_kernel,
        grid=(n_tiles,),
        in_specs=[pl.BlockSpec((None, 1, DISP_TM * TOP_K), lambda i: (i, 0, 0), memory_space=pltpu.SMEM),
                  pl.BlockSpec((DISP_TM * ROW_TILES, LANES), lambda i: (i, 0))],
        out_specs=pl.BlockSpec(memory_space=pl.ANY),
        out_shape=jax.ShapeDtypeStruct((N_ROWS * ROW_TILES, LANES), F32),
        scratch_shapes=[pltpu.SemaphoreType.DMA(())],
        compiler_params=_params(1),
        name="dispatch",
    )(dest.reshape(n_tiles, 1, DISP_TM * TOP_K), h2rows)


CAST_ROWS = 128


def _expert_kernel(blk_exp_ref, nvalid_ref, xs_ref, w1_ref, b1_ref, w2_ref, b2_ref, y_ref, w1b_ref, w2b_ref):
    b = pl.program_id(0)
    e = blk_exp_ref[b]
    prev = blk_exp_ref[jnp.maximum(b - 1, 0)]
    nvalid = nvalid_ref[b]

    @pl.when(jnp.logical_and(nvalid > 0, jnp.logical_or(b == 0, e != prev)))
    def _():
        def cast1(i, carry):
            r = pl.multiple_of(i * CAST_ROWS, CAST_ROWS)
            w1b_ref[pl.ds(r, CAST_ROWS), :] = w1_ref[pl.ds(r, CAST_ROWS), :].astype(BF16)
            return carry

        def cast2(i, carry):
            r = pl.multiple_of(i * CAST_ROWS, CAST_ROWS)
            w2b_ref[pl.ds(r, CAST_ROWS), :] = w2_ref[pl.ds(r, CAST_ROWS), :].astype(BF16)
            return carry

        lax.fori_loop(0, D_MODEL // CAST_ROWS, cast1, 0)
        lax.fori_loop(0, D_FF // CAST_ROWS, cast2, 0)

    @pl.when(nvalid > 0)
    def _():
        valid = lax.broadcasted_iota(jnp.int32, (MOE_BLOCK, LANES), 0) < nvalid
        chunks = [jnp.where(valid, xs_ref[pl.ds(c, MOE_BLOCK, stride=ROW_TILES), :], 0.0).astype(BF16)
                  for c in range(ROW_TILES)]
        xb = jnp.concatenate(chunks, axis=-1)
        hid = jnp.dot(xb, w1b_ref[...], preferred_element_type=F32) + b1_ref[...]
        glu = jnp.minimum(hid[:, :D_FF], SWIGLU_LIMIT)
        lin = jnp.clip(hid[:, D_FF:], -SWIGLU_LIMIT, SWIGLU_LIMIT)
        act = glu * _sigmoid(SWIGLU_ALPHA * glu) * (lin + 1.0)
        y = jnp.dot(act.astype(BF16), w2b_ref[...], preferred_element_type=F32) + b2_ref[...]
        for c in range(ROW_TILES):
            y_ref[pl.ds(c, MOE_BLOCK, stride=ROW_TILES), :] = y[:, c * LANES:(c + 1) * LANES]

    @pl.when(nvalid == 0)
    def _():
        y_ref[...] = jnp.zeros_like(y_ref)


def _experts(blk_exp, nvalid, xs, w1, b1, w2, b2, layer):
    grid_spec = pltpu.PrefetchScalarGridSpec(
        num_scalar_prefetch=2,
        grid=(N_BLOCKS,),
        in_specs=[
            pl.BlockSpec((MOE_BLOCK * ROW_TILES, LANES), lambda b, be, nv: (b, 0)),
            pl.BlockSpec((None, None, D_MODEL, 2 * D_FF), lambda b, be, nv: (layer, be[b], 0, 0)),
            pl.BlockSpec((None, None, 1, 2 * D_FF), lambda b, be, nv: (layer, be[b], 0, 0)),
            pl.BlockSpec((None, None, D_FF, D_MODEL), lambda b, be, nv: (layer, be[b], 0, 0)),
            pl.BlockSpec((None, None, 1, D_MODEL), lambda b, be, nv: (layer, be[b], 0, 0)),
        ],
        out_specs=pl.BlockSpec((MOE_BLOCK * ROW_TILES, LANES), lambda b, be, nv: (b, 0)),
        scratch_shapes=[pltpu.VMEM((D_MODEL, 2 * D_FF), BF16), pltpu.VMEM((D_FF, D_MODEL), BF16)],
    )
    return pl.pallas_call(
        _expert_kernel,
        grid_spec=grid_spec,
        out_shape=jax.ShapeDtypeStruct((N_ROWS * ROW_TILES, LANES), F32),
        compiler_params=_params(1),
        name="experts",
    )(blk_exp, nvalid, xs, w1, b1.reshape(DEPTH, N_EXPERTS, 1, 2 * D_FF), w2,
      b2.reshape(DEPTH, N_EXPERTS, 1, D_MODEL))


COMB_TM = 128


def _combine_kernel(dest_ref, y_ref, w_ref, xm_ref, mod_ref, fg_ref, x_ref, xn_ref, buf_ref, sem):
    def copy(j):
        t = j // TOP_K
        k = j % TOP_K
        d = dest_ref[0, j]
        slot = k * COMB_TM + t
        return pltpu.make_async_copy(
            y_ref.at[pl.ds(pl.multiple_of(d * ROW_TILES, ROW_TILES), ROW_TILES)],
            buf_ref.at[pl.ds(pl.multiple_of(slot * ROW_TILES, ROW_TILES), ROW_TILES)], sem)

    def start(j, carry):
        copy(j).start()
        return carry

    def wait(j, carry):
        copy(j).wait()
        return carry

    lax.fori_loop(0, COMB_TM * TOP_K, start, 0)
    lax.fori_loop(0, COMB_TM * TOP_K, wait, 0)
    w = w_ref[...]
    cols = []
    for c in range(ROW_TILES):
        acc = None
        for k in range(TOP_K):
            rows = buf_ref[pl.ds(k * COMB_TM * ROW_TILES + c, COMB_TM, stride=ROW_TILES), :]
            term = w[:, k:k + 1] * rows
            acc = term if acc is None else acc + term
        cols.append(acc)
    moe = jnp.concatenate(cols, axis=-1)
    m = mod_ref[...]
    x = xm_ref[...] + m[5:6] * moe
    x_ref[...] = x
    xn_ref[...] = x * lax.rsqrt(jnp.mean(x * x, axis=-1, keepdims=True) + EPS) * fg_ref[...]


def _combine(dest, y, topw, xm, mods_l, final_g):
    n_tiles = N_TOK // COMB_TM

    def tok(w):
        return pl.BlockSpec((COMB_TM, w), lambda i: (i, 0))

    return pl.pallas_call(
        _combine_kernel,
        grid=(n_tiles,),
        in_specs=[pl.BlockSpec((None, 1, COMB_TM * TOP_K), lambda i: (i, 0, 0), memory_space=pltpu.SMEM),
                  pl.BlockSpec(memory_space=pl.ANY),
                  tok(TOP_K), tok(D_MODEL),
                  pl.BlockSpec((None, N_MOD, D_MODEL), lambda i: (_mod_row(i * COMB_TM), 0, 0)),
                  pl.BlockSpec((1, D_MODEL), lambda i: (0, 0))],
        out_specs=[tok(D_MODEL), tok(D_MODEL)],
        out_shape=[jax.ShapeDtypeStruct((N_TOK, D_MODEL), F32),
                   jax.ShapeDtypeStruct((N_TOK, D_MODEL), F32)],
        scratch_shapes=[pltpu.VMEM((COMB_TM * TOP_K * ROW_TILES, LANES), F32),
                        pltpu.SemaphoreType.DMA(())],
        compiler_params=_params(1),
        name="combine",
    )(dest.reshape(n_tiles, 1, COMB_TM * TOP_K), y, topw, xm, mods_l, final_g)


def _rope_tables():
    t = np.arange(DEC_SEQ)
    row = (t // GRID_W).astype(np.float32)
    col = (t % GRID_W).astype(np.float32)
    inv = jnp.asarray(ROPE_THETA, F32) ** (-jnp.arange(ROPE_PAIRS, dtype=F32) / ROPE_PAIRS)
    ang = jnp.concatenate([jnp.asarray(row)[:, None] * inv, jnp.asarray(col)[:, None] * inv], axis=-1)
    cos = jnp.repeat(jnp.cos(ang), 2, axis=-1)
    sin = jnp.repeat(jnp.sin(ang), 2, axis=-1)
    sign = jnp.asarray(np.tile(np.array([-1.0, 1.0], np.float32), HEAD_DIM // 2))
    return jnp.tile(cos, (1, A_HEADS)), jnp.tile(sin * sign, (1, A_HEADS))


def _block_tables(counts):
    padded = (counts + MOE_BLOCK - 1) // MOE_BLOCK * MOE_BLOCK
    pad_end = jnp.cumsum(padded)
    pad_start = pad_end - padded
    blk_row = jnp.arange(N_BLOCKS, dtype=jnp.int32) * MOE_BLOCK
    blk_exp = jnp.sum((blk_row[:, None] >= pad_end[None, :]).astype(jnp.int32), axis=1)
    blk_exp = jnp.minimum(blk_exp, N_EXPERTS - 1)
    nvalid = jnp.clip(counts[blk_exp] - (blk_row - pad_start[blk_exp]), 0, MOE_BLOCK).astype(jnp.int32)
    return pad_start.astype(jnp.int32), blk_exp, nvalid


def kernel(x_prompt, x_sample, cache_attn_k, cache_attn_v, cache_na_k, cache_na_v, c, c_ctx, w_mod, b_mod, norm1_g, norm2_g, w_in, b_gate, q_norm_g, k_norm_g, na_rpb, conv_w, conv_b, conv_ln_g, conv_ln_b, sgu_ln_g, sgu_ln_b, sgu_w, sgu_b, w_branch, w_out, router_w, router_b, exp_w1, exp_b1, exp_w2, exp_b2, final_g):
    x = jnp.concatenate([x_prompt.reshape(N_P, D_MODEL), x_sample.reshape(N_S, D_MODEL)], axis=0)
    cvec = jnp.zeros((SUBLANES, D_MODEL), F32).at[0].set(c_ctx).at[1:1 + DEC_BATCH].set(c)
    mods = _modulation(cvec, w_mod, b_mod).reshape(DEPTH, SUBLANES, N_MOD, D_MODEL)
    cos_t, sin_t = _rope_tables()
    cak = cache_attn_k.reshape(DEC_BATCH, DEPTH, PAST_LEN, A_KV)
    cav = cache_attn_v.reshape(DEC_BATCH, DEPTH, PAST_LEN, A_KV)
    cbk = cache_na_k.reshape(DEC_BATCH, DEPTH, PAST_LEN, B_W)
    cbv = cache_na_v.reshape(DEC_BATCH, DEPTH, PAST_LEN, B_W)
    w_mix = w_in[:, :, :MIX_COLS].astype(BF16)
    w_gate = w_in[:, :, MIX_COLS:].astype(BF16)
    w_br = w_branch.astype(BF16)
    w_o = w_out.astype(BF16)
    rw_hi = router_w.astype(BF16)
    rw_lo = (router_w - rw_hi.astype(F32)).astype(BF16)
    final_g2 = final_g.reshape(1, D_MODEL)

    new_k, new_v, new_bk, new_bv = [], [], [], []
    xn = None
    for l in range(DEPTH):
        mods_l = mods[l]
        g1 = norm1_g[l].reshape(1, D_MODEL)
        gq = jnp.tile(q_norm_g[l], A_HEADS).reshape(1, A_Q)
        gk = jnp.tile(k_norm_g[l], A_KV_HEADS).reshape(1, A_KV)
        aq, ak, av, bq, bk, bv, cz, dz = _in_proj(x, mods_l, g1, w_mix[l])
        ya_p, yb_p, kn = _prompt_attn(aq, ak, av, bq, bk, bv, gq, gk)
        new_k.append(kn.reshape(BATCH, SEQ, A_KV_HEADS, HEAD_DIM))
        new_v.append(av[:N_P].reshape(BATCH, SEQ, A_KV_HEADS, HEAD_DIM))
        new_bk.append(bk[:N_P].reshape(BATCH, SEQ, B_HEADS, HEAD_DIM))
        new_bv.append(bv[:N_P].reshape(BATCH, SEQ, B_HEADS, HEAD_DIM))
        ya_s = _sample_attn(aq, ak, av, cak, cav, cos_t, sin_t, gq, gk, l)
        yb_s = _na_attn(bq, bk, bv, cbk, cbv, _na_bias(na_rpb[l]), l)
        ya = jnp.concatenate([ya_p, ya_s], axis=0)
        yb = jnp.concatenate([yb_p, yb_s], axis=0)
        cw = conv_w[l]
        cb = conv_b[l].reshape(1, C_WIDTH)
        cg = conv_ln_g[l].reshape(1, C_WIDTH)
        cbb = conv_ln_b[l].reshape(1, C_WIDTH)
        yc = jnp.concatenate([_conv_call(cz, cw, cb, cg, cbb, SEQ, 0, BATCH),
                              _conv_call(cz, cw, cb, cg, cbb, DEC_SEQ, N_P // DEC_SEQ, DEC_BATCH)], axis=0)
        bs_full = jnp.repeat(sgu_b[l].T, SGU_GW, axis=1)
        yd = _sgu(dz, sgu_ln_g[l].reshape(1, SGU_WIDTH), sgu_ln_b[l].reshape(1, SGU_WIDTH),
                  sgu_w[l].astype(BF16), bs_full)
        xm, h2rows, logits = _merge(x, ya, yb, yc, yd, mods_l, g1, w_gate[l],
                                    b_gate[l].reshape(1, N_BRANCH * D_MODEL), w_br[l], w_o[l],
                                    norm2_g[l].reshape(1, D_MODEL), rw_hi[l], rw_lo[l],
                                    router_b[l].reshape(1, N_EXPERTS))
        e_idx, top_w, pos, counts = _route(logits)
        pad_start, blk_exp, nvalid = _block_tables(counts[0])
        dest = (pad_start[e_idx] + pos).reshape(N_ASSIGN)
        xs = _dispatch(dest, h2rows)
        y = _experts(blk_exp, nvalid, xs, exp_w1, exp_b1, exp_w2, exp_b2, l)
        x, xn = _combine(dest, y, top_w, xm, mods_l, final_g2)

    y_prompt = xn[:N_P].reshape(BATCH, SEQ, D_MODEL)
    y_sample = xn[N_P:].reshape(DEC_BATCH, DEC_SEQ, D_MODEL)
    return (y_prompt, y_sample, jnp.stack(new_k, axis=1), jnp.stack(new_v, axis=1),
            jnp.stack(new_bk, axis=1), jnp.stack(new_bv, axis=1))
```

```python
import functools

import numpy as np
import jax
import jax.numpy as jnp
from jax import lax
from jax.experimental import pallas as pl
from jax.experimental.pallas import tpu as pltpu

D_MODEL = 1024
BATCH = 32
SEQ = 256
DEPTH = 2
DEC_BATCH = 2
DEC_SEQ = 1024
PAST_LEN = 512
GRID_W = 64
HEAD_DIM = 64
A_HEADS = 4
A_KV_HEADS = 2
B_HEADS = 4
NA_ROWS = 8
NA_COLS = 16
C_WIDTH = 256
CONV_WIDTH = 31
SGU_WIDTH = 256
SGU_GROUPS = 4
SGU_CHUNK = 128
N_BRANCH = 4
BRANCH_W = 256
N_EXPERTS = 32
TOP_K = 4
D_FF = 1024
SWIGLU_ALPHA = 1.702
SWIGLU_LIMIT = 7.0
MOE_BLOCK = 128
ROPE_THETA = 10000.0
ROPE_PAIRS = HEAD_DIM // 4
N_MOD = 6
EPS = 1e-6
NEG_INF = -1e30

A_Q = A_HEADS * HEAD_DIM
A_KV = A_KV_HEADS * HEAD_DIM
B_W = B_HEADS * HEAD_DIM
MIX_SIZES = (A_Q, A_KV, A_KV, B_W, B_W, B_W, 2 * C_WIDTH, 2 * SGU_WIDTH)
MIX_COLS = sum(MIX_SIZES)

N_P = BATCH * SEQ
N_S = DEC_BATCH * DEC_SEQ
N_TOK = N_P + N_S
N_ASSIGN = N_TOK * TOP_K
EXP_BLOCK = 256
N_BLOCKS = N_ASSIGN // EXP_BLOCK + N_EXPERTS
N_ROWS = N_BLOCKS * EXP_BLOCK
GRID_ROWS = DEC_SEQ // GRID_W
NA_WR = min(NA_ROWS, GRID_ROWS)
N_LOC = NA_WR * GRID_W

SUBLANES = 8
LANES = 128
ROW_TILES = D_MODEL // LANES
VMEM_LIMIT = 56 * 1024 * 1024

F32 = jnp.float32
BF16 = jnp.bfloat16


def _params(n_axes, vmem=None):
    return pltpu.CompilerParams(
        dimension_semantics=("arbitrary",) * n_axes,
        vmem_limit_bytes=vmem if vmem is not None else VMEM_LIMIT)


def _mod_row(start):
    return jnp.where(start < N_P, 0, 1 + (start - N_P) // DEC_SEQ)


def _bdot(a, b):
    return jnp.dot(a.astype(BF16), b.astype(BF16), preferred_element_type=F32)


def _bdot_nt(a, b):
    return lax.dot_general(a.astype(BF16), b.astype(BF16), (((1,), (1,)), ((), ())),
                           preferred_element_type=F32)


def _sigmoid(x):
    return 1.0 / (1.0 + jnp.exp(-x))


MOD_TN = 1536


def _mod_kernel(c_ref, w_ref, b_ref, o_ref):
    c = c_ref[...]
    s = c * _sigmoid(c)
    o_ref[...] = _bdot(s, w_ref[...]) + b_ref[...]


def _modulation(cvec, w_mod, b_mod):
    n_col = N_MOD * D_MODEL
    return pl.pallas_call(
        _mod_kernel,
        grid=(DEPTH, n_col // MOD_TN),
        in_specs=[
            pl.BlockSpec((SUBLANES, D_MODEL), lambda l, j: (0, 0)),
            pl.BlockSpec((None, D_MODEL, MOD_TN), lambda l, j: (l, 0, j)),
            pl.BlockSpec((None, 1, MOD_TN), lambda l, j: (l, 0, j)),
        ],
        out_specs=pl.BlockSpec((None, SUBLANES, MOD_TN), lambda l, j: (l, 0, j)),
        out_shape=jax.ShapeDtypeStruct((DEPTH, SUBLANES, n_col), F32),
        compiler_params=_params(2),
        name="modulation",
    )(cvec, w_mod, b_mod.reshape(DEPTH, 1, n_col))


IN_TM = 512


def _norm_mod(x, g, shift, scale):
    y = x * lax.rsqrt(jnp.mean(x * x, axis=-1, keepdims=True) + EPS) * g
    return y * (1.0 + scale) + shift


def _in_kernel(x_ref, mod_ref, g_ref, w_ref, *out_refs):
    m = mod_ref[...]
    h = _norm_mod(x_ref[...], g_ref[...], m[0:1], m[1:2])
    z = jnp.dot(h.astype(BF16), w_ref[...], preferred_element_type=F32)
    off = 0
    for o_ref, sz in zip(out_refs, MIX_SIZES):
        o_ref[...] = z[:, off:off + sz]
        off += sz


def _in_proj(x, mods_l, g1, w_mix):
    return pl.pallas_call(
        _in_kernel,
        grid=(N_TOK // IN_TM,),
        in_specs=[
            pl.BlockSpec((IN_TM, D_MODEL), lambda i: (i, 0)),
            pl.BlockSpec((None, N_MOD, D_MODEL), lambda i: (_mod_row(i * IN_TM), 0, 0)),
            pl.BlockSpec((1, D_MODEL), lambda i: (0, 0)),
            pl.BlockSpec((D_MODEL, MIX_COLS), lambda i: (0, 0)),
        ],
        out_specs=[pl.BlockSpec((IN_TM, sz), lambda i: (i, 0)) for sz in MIX_SIZES],
        out_shape=[jax.ShapeDtypeStruct((N_TOK, sz), F32) for sz in MIX_SIZES],
        compiler_params=_params(1),
        name="in_proj",
    )(x, mods_l, g1, w_mix)


def _head_rms(x, g):
    n_heads = x.shape[-1] // HEAD_DIM
    seg = lax.broadcasted_iota(jnp.int32, x.shape, 1) // HEAD_DIM
    xx = x * x
    inv = jnp.zeros_like(x)
    for h in range(n_heads):
        ms = jnp.sum(jnp.where(seg == h, xx, 0.0), axis=-1, keepdims=True) * (1.0 / HEAD_DIM)
        inv = jnp.where(seg == h, lax.rsqrt(ms + EPS), inv)
    return x * inv * g


def _softmax_pv(score_parts, value_parts):
    m = score_parts[0].max(axis=-1, keepdims=True)
    for s in score_parts[1:]:
        m = jnp.maximum(m, s.max(axis=-1, keepdims=True))
    den = None
    acc = None
    for s, v in zip(score_parts, value_parts):
        e = jnp.exp(s - m)
        d = e.sum(axis=-1, keepdims=True)
        a = _bdot(e, v)
        den = d if den is None else den + d
        acc = a if acc is None else acc + a
    return acc / den


def _head(x, h):
    return x[:, h * HEAD_DIM:(h + 1) * HEAD_DIM]


SCALE = HEAD_DIM ** -0.5


def _prompt_attn_kernel(aq_ref, ak_ref, av_ref, bq_ref, bk_ref, bv_ref, gq_ref, gk_ref,
                        ya_ref, yb_ref, kn_ref):
    aq = _head_rms(aq_ref[...], gq_ref[...])
    ak = _head_rms(ak_ref[...], gk_ref[...])
    kn_ref[...] = ak
    av = av_ref[...]
    grp = A_HEADS // A_KV_HEADS
    outs = []
    for h in range(A_HEADS):
        s = _bdot_nt(_head(aq, h), _head(ak, h // grp)) * SCALE
        outs.append(_softmax_pv([s], [_head(av, h // grp)]))
    ya_ref[...] = jnp.concatenate(outs, axis=-1)
    bq = bq_ref[...]
    bk = bk_ref[...]
    bv = bv_ref[...]
    outs = []
    for h in range(B_HEADS):
        s = _bdot_nt(_head(bq, h), _head(bk, h)) * SCALE
        outs.append(_softmax_pv([s], [_head(bv, h)]))
    yb_ref[...] = jnp.concatenate(outs, axis=-1)


def _prompt_attn(aq, ak, av, bq, bk, bv, gq, gk):
    def spec(w):
        return pl.BlockSpec((SEQ, w), lambda b: (b, 0))

    return pl.pallas_call(
        _prompt_attn_kernel,
        grid=(BATCH,),
        in_specs=[spec(A_Q), spec(A_KV), spec(A_KV), spec(B_W), spec(B_W), spec(B_W),
                  pl.BlockSpec((1, A_Q), lambda b: (0, 0)),
                  pl.BlockSpec((1, A_KV), lambda b: (0, 0))],
        out_specs=[spec(A_Q), spec(B_W), spec(A_KV)],
        out_shape=[jax.ShapeDtypeStruct((N_P, A_Q), F32),
                   jax.ShapeDtypeStruct((N_P, B_W), F32),
                   jax.ShapeDtypeStruct((N_P, A_KV), F32)],
        compiler_params=_params(1),
        name="prompt_attn",
    )(aq, ak, av, bq, bk, bv, gq, gk)


QB = 128


def _rope(x, cos, sin_signed):
    n = x.shape[-1]
    nxt = pltpu.roll(x, n - 1, 1)
    prv = pltpu.roll(x, 1, 1)
    even = (lax.broadcasted_iota(jnp.int32, x.shape, 1) % 2) == 0
    return x * cos + jnp.where(even, nxt, prv) * sin_signed


def _sample_attn_kernel(q_ref, k_ref, v_ref, ck_ref, cv_ref, cosq_ref, sinq_ref, cosk_ref, sink_ref,
                        gq_ref, gk_ref, o_ref):
    q = _rope(_head_rms(q_ref[...], gq_ref[...]), cosq_ref[...], sinq_ref[...])
    k = _rope(_head_rms(k_ref[...], gk_ref[...]), cosk_ref[...], sink_ref[...])
    v = v_ref[...]
    ck = ck_ref[...]
    cv = cv_ref[...]
    grp = A_HEADS // A_KV_HEADS
    outs = []
    for h in range(A_HEADS):
        j = h // grp
        qh = _head(q, h)
        s1 = _bdot_nt(qh, _head(k, j)) * SCALE
        s2 = _bdot_nt(qh, _head(ck, j)) * SCALE
        outs.append(_softmax_pv([s1, s2], [_head(v, j), _head(cv, j)]))
    o_ref[...] = jnp.concatenate(outs, axis=-1)


def _sample_attn(aq, ak, av, cache_k, cache_v, cos_t, sin_t, gq, gk, layer):
    nqb = DEC_SEQ // QB
    q0 = N_P // QB
    k0 = N_P // DEC_SEQ
    return pl.pallas_call(
        _sample_attn_kernel,
        grid=(DEC_BATCH, nqb),
        in_specs=[
            pl.BlockSpec((QB, A_Q), lambda b, i: (q0 + b * nqb + i, 0)),
            pl.BlockSpec((DEC_SEQ, A_KV), lambda b, i: (k0 + b, 0)),
            pl.BlockSpec((DEC_SEQ, A_KV), lambda b, i: (k0 + b, 0)),
            pl.BlockSpec((None, None, PAST_LEN, A_KV), lambda b, i: (b, layer, 0, 0)),
            pl.BlockSpec((None, None, PAST_LEN, A_KV), lambda b, i: (b, layer, 0, 0)),
            pl.BlockSpec((QB, A_Q), lambda b, i: (i, 0)),
            pl.BlockSpec((QB, A_Q), lambda b, i: (i, 0)),
            pl.BlockSpec((DEC_SEQ, A_KV), lambda b, i: (0, 0)),
            pl.BlockSpec((DEC_SEQ, A_KV), lambda b, i: (0, 0)),
            pl.BlockSpec((1, A_Q), lambda b, i: (0, 0)),
            pl.BlockSpec((1, A_KV), lambda b, i: (0, 0)),
        ],
        out_specs=pl.BlockSpec((QB, A_Q), lambda b, i: (b * nqb + i, 0)),
        out_shape=jax.ShapeDtypeStruct((N_S, A_Q), F32),
        compiler_params=_params(2),
        name="sample_attn",
    )(aq, ak, av, cache_k, cache_v, cos_t, sin_t, cos_t, sin_t, gq, gk)


def _na_kernel(q_ref, k_ref, v_ref, ck_ref, cv_ref, bias_ref, o_ref):
    r = pl.program_id(1)
    r_start = jnp.clip(r - NA_WR // 2, 0, GRID_ROWS - NA_WR)
    base = pl.multiple_of(r_start * GRID_W, GRID_W)
    q = q_ref[...]
    kb = k_ref[pl.ds(base, N_LOC), :]
    vb = v_ref[pl.ds(base, N_LOC), :]
    ck = ck_ref[...]
    cv = cv_ref[...]
    outs = []
    for h in range(B_HEADS):
        qh = _head(q, h)
        s1 = _bdot_nt(qh, _head(kb, h)) * SCALE + bias_ref[h]
        s2 = _bdot_nt(qh, _head(ck, h)) * SCALE
        outs.append(_softmax_pv([s1, s2], [_head(vb, h), _head(cv, h)]))
    o_ref[...] = jnp.concatenate(outs, axis=-1)


def _na_bias(rpb):
    r = np.arange(GRID_ROWS)
    r_start = np.clip(r - NA_WR // 2, 0, GRID_ROWS - NA_WR)
    band_rows = r_start[:, None] + np.arange(NA_WR)[None, :]
    cidx = np.arange(GRID_W)
    c_start = np.clip(cidx - NA_COLS // 2, 0, GRID_W - NA_COLS)
    col_in = (cidx[None, :] >= c_start[:, None]) & (cidx[None, :] < c_start[:, None] + NA_COLS)
    row_off = band_rows - r[:, None] + NA_ROWS - 1
    col_off = np.clip(cidx[None, :] - cidx[:, None] + NA_COLS - 1, 0, 2 * NA_COLS - 2)
    bias = rpb.astype(F32)[:, row_off][..., col_off]
    bias = jnp.where(col_in, bias, NEG_INF).transpose(1, 0, 3, 2, 4)
    return bias.reshape(GRID_ROWS, B_HEADS, GRID_W, N_LOC)


def _na_attn(bq, bk, bv, cache_k, cache_v, bias, layer):
    q0 = N_P // GRID_W
    k0 = N_P // DEC_SEQ
    return pl.pallas_call(
        _na_kernel,
        grid=(DEC_BATCH, GRID_ROWS),
        in_specs=[
            pl.BlockSpec((GRID_W, B_W), lambda b, r: (q0 + b * GRID_ROWS + r, 0)),
            pl.BlockSpec((DEC_SEQ, B_W), lambda b, r: (k0 + b, 0)),
            pl.BlockSpec((DEC_SEQ, B_W), lambda b, r: (k0 + b, 0)),
            pl.BlockSpec((None, None, PAST_LEN, B_W), lambda b, r: (b, layer, 0, 0)),
            pl.BlockSpec((None, None, PAST_LEN, B_W), lambda b, r: (b, layer, 0, 0)),
            pl.BlockSpec((None, B_HEADS, GRID_W, N_LOC), lambda b, r: (r, 0, 0, 0)),
        ],
        out_specs=pl.BlockSpec((GRID_W, B_W), lambda b, r: (b * GRID_ROWS + r, 0)),
        out_shape=jax.ShapeDtypeStruct((N_S, B_W), F32),
        compiler_params=_params(2),
        name="na_attn",
    )(bq, bk, bv, cache_k, cache_v, bias)


CONV_PAD = 16
CONV_CHUNK = 128


def _layer_norm(x, g, b):
    mu = jnp.mean(x, axis=-1, keepdims=True)
    xc = x - mu
    var = jnp.mean(xc * xc, axis=-1, keepdims=True)
    return xc * lax.rsqrt(var + EPS) * g + b


def _conv_kernel(z_ref, w_ref, cb_ref, g_ref, b_ref, o_ref, pad_ref, *, s_len):
    z = z_ref[...]
    u = z[:, :C_WIDTH] * _sigmoid(z[:, C_WIDTH:])
    pad_ref[pl.ds(0, CONV_PAD), :] = jnp.zeros((CONV_PAD, C_WIDTH), F32)
    pad_ref[pl.ds(CONV_PAD + s_len, CONV_PAD), :] = jnp.zeros((CONV_PAD, C_WIDTH), F32)
    pad_ref[pl.ds(CONV_PAD, s_len), :] = u
    w = w_ref[...]
    shift = CONV_PAD - CONV_WIDTH // 2
    for c in range(s_len // CONV_CHUNK):
        acc = jnp.zeros((CONV_CHUNK, C_WIDTH), F32)
        for k in range(CONV_WIDTH):
            acc = acc + pad_ref[pl.ds(c * CONV_CHUNK + k + shift, CONV_CHUNK), :] * w[k:k + 1]
        y = _layer_norm(acc + cb_ref[...], g_ref[...], b_ref[...])
        o_ref[pl.ds(c * CONV_CHUNK, CONV_CHUNK), :] = y * _sigmoid(y)


def _conv_call(cz, w, cb, g, b, s_len, first_blk, n_seq):
    vec = pl.BlockSpec((1, C_WIDTH), lambda i: (0, 0))
    return pl.pallas_call(
        functools.partial(_conv_kernel, s_len=s_len),
        grid=(n_seq,),
        in_specs=[pl.BlockSpec((s_len, 2 * C_WIDTH), lambda i: (first_blk + i, 0)),
                  pl.BlockSpec((CONV_WIDTH, C_WIDTH), lambda i: (0, 0)), vec, vec, vec],
        out_specs=pl.BlockSpec((s_len, C_WIDTH), lambda i: (i, 0)),
        out_shape=jax.ShapeDtypeStruct((n_seq * s_len, C_WIDTH), F32),
        scratch_shapes=[pltpu.VMEM((s_len + 2 * CONV_PAD, C_WIDTH), F32)],
        compiler_params=_params(1),
        name="conformer_conv_%d" % s_len,
    )(cz, w, cb, g, b)


SGU_TM = 512
SGU_GW = SGU_WIDTH // SGU_GROUPS


def _sgu_kernel(z_ref, g_ref, b_ref, ws_ref, bs_ref, o_ref):
    z = z_ref[...]
    z = 0.5 * z * (1.0 + lax.erf(z * (2.0 ** -0.5)))
    u = z[:, :SGU_WIDTH]
    v = _layer_norm(z[:, SGU_WIDTH:], g_ref[...], b_ref[...])
    for c in range(SGU_TM // SGU_CHUNK):
        vc = v[c * SGU_CHUNK:(c + 1) * SGU_CHUNK]
        parts = [_bdot(ws_ref[g], vc[:, g * SGU_GW:(g + 1) * SGU_GW]) for g in range(SGU_GROUPS)]
        mixed = jnp.concatenate(parts, axis=-1) + bs_ref[...]
        o_ref[pl.ds(c * SGU_CHUNK, SGU_CHUNK), :] = u[c * SGU_CHUNK:(c + 1) * SGU_CHUNK] * mixed


def _sgu(dz, g, b, ws, bs_full):
    vec = pl.BlockSpec((1, SGU_WIDTH), lambda i: (0, 0))
    return pl.pallas_call(
        _sgu_kernel,
        grid=(N_TOK // SGU_TM,),
        in_specs=[pl.BlockSpec((SGU_TM, 2 * SGU_WIDTH), lambda i: (i, 0)), vec, vec,
                  pl.BlockSpec((SGU_GROUPS, SGU_CHUNK, SGU_CHUNK), lambda i: (0, 0, 0)),
                  pl.BlockSpec((SGU_CHUNK, SGU_WIDTH), lambda i: (0, 0))],
        out_specs=pl.BlockSpec((SGU_TM, SGU_WIDTH), lambda i: (i, 0)),
        out_shape=jax.ShapeDtypeStruct((N_TOK, SGU_WIDTH), F32),
        compiler_params=_params(1),
        name="chunk_sgu",
    )(dz, g, b, ws, bs_full)


MERGE_TM = 256


def _merge_kernel(x_ref, ya_ref, yb_ref, yc_ref, yd_ref, mod_ref, g1_ref, wg_ref, bg_ref, wb_ref, wo_ref,
                  g2_ref, rwh_ref, rwl_ref, rb_ref, xm_ref, h2_ref, lg_ref):
    m = mod_ref[...]
    x = x_ref[...]
    h = _norm_mod(x, g1_ref[...], m[0:1], m[1:2]).astype(BF16)
    merged = None
    for i, y_ref in enumerate((ya_ref, yb_ref, yc_ref, yd_ref)):
        logit = jnp.dot(h, wg_ref[:, i * D_MODEL:(i + 1) * D_MODEL], preferred_element_type=F32)
        gate = _sigmoid(logit + bg_ref[:, i * D_MODEL:(i + 1) * D_MODEL])
        term = gate * jnp.dot(y_ref[...].astype(BF16), wb_ref[i], preferred_element_type=F32)
        merged = term if merged is None else merged + term
    out = jnp.dot(merged.astype(BF16), wo_ref[...], preferred_element_type=F32)
    xm = x + m[2:3] * out
    xm_ref[...] = xm
    h2 = _norm_mod(xm, g2_ref[...], m[3:4], m[4:5])
    h2_hi = h2.astype(BF16)
    h2_lo = (h2 - h2_hi.astype(F32)).astype(BF16)
    lg = jnp.dot(h2_hi, rwh_ref[...], preferred_element_type=F32)
    lg = lg + jnp.dot(h2_hi, rwl_ref[...], preferred_element_type=F32)
    lg = lg + jnp.dot(h2_lo, rwh_ref[...], preferred_element_type=F32)
    lg_ref[...] = lg + rb_ref[...]
    h2_ref[...] = h2_hi


def _merge(x, ya, yb, yc, yd, mods_l, g1, wg, bg, wb, wo, g2, rwh, rwl, rb):
    def tok(w):
        return pl.BlockSpec((MERGE_TM, w), lambda i: (i, 0))

    def full(*shape):
        return pl.BlockSpec(shape, lambda i: (0,) * len(shape))

    return pl.pallas_call(
        _merge_kernel,
        grid=(N_TOK // MERGE_TM,),
        in_specs=[tok(D_MODEL), tok(BRANCH_W), tok(BRANCH_W), tok(BRANCH_W), tok(BRANCH_W),
                  pl.BlockSpec((None, N_MOD, D_MODEL), lambda i: (_mod_row(i * MERGE_TM), 0, 0)),
                  full(1, D_MODEL), full(D_MODEL, N_BRANCH * D_MODEL), full(1, N_BRANCH * D_MODEL),
                  full(N_BRANCH, BRANCH_W, D_MODEL), full(D_MODEL, D_MODEL), full(1, D_MODEL),
                  full(D_MODEL, N_EXPERTS), full(D_MODEL, N_EXPERTS), full(1, N_EXPERTS)],
        out_specs=[tok(D_MODEL), tok(D_MODEL), tok(N_EXPERTS)],
        out_shape=[jax.ShapeDtypeStruct((N_TOK, D_MODEL), F32),
                   jax.ShapeDtypeStruct((N_TOK, D_MODEL), BF16),
                   jax.ShapeDtypeStruct((N_TOK, N_EXPERTS), F32)],
        compiler_params=_params(1),
        name="merge",
    )(x, ya, yb, yc, yd, mods_l, g1, wg, bg, wb, wo, g2, rwh, rwl, rb)


ROUTE_TM = 256
TILE_ROWS = ROUTE_TM * TOP_K
N_TILES = N_TOK // ROUTE_TM


def _route_kernel(lg_ref, w_ref, q_ref, cnt_ref, off_ref):
    lg = lg_ref[...]
    lane = lax.broadcasted_iota(jnp.int32, lg.shape, 1)
    sels, vals = [], []
    for _ in range(TOP_K):
        mx = lg.max(axis=-1, keepdims=True)
        idx = jnp.where(lg == mx, lane, N_EXPERTS).min(axis=-1, keepdims=True)
        sel = lane == idx
        sels.append(sel)
        vals.append(mx)
        lg = jnp.where(sel, -jnp.inf, lg)
    exps = [jnp.exp(v - vals[0]) for v in vals]
    den = exps[0] + exps[1] + exps[2] + exps[3]
    onehot = jnp.zeros(lg.shape, F32)
    for sel in sels:
        onehot = onehot + sel.astype(F32)
    row = lax.broadcasted_iota(jnp.int32, (ROUTE_TM, ROUTE_TM), 0)
    col = lax.broadcasted_iota(jnp.int32, (ROUTE_TM, ROUTE_TM), 1)
    tri = jnp.where(col < row, 1.0, 0.0).astype(BF16)
    rank = jnp.dot(tri, onehot.astype(BF16), preferred_element_type=F32)
    cnt = jnp.sum(onehot, axis=0, keepdims=True)
    erow = lax.broadcasted_iota(jnp.int32, (N_EXPERTS, N_EXPERTS), 0)
    ecol = lax.broadcasted_iota(jnp.int32, (N_EXPERTS, N_EXPERTS), 1)
    upper = jnp.where(erow < ecol, 1.0, 0.0).astype(BF16)
    off = jnp.dot(jnp.broadcast_to(cnt, (SUBLANES, N_EXPERTS)).astype(BF16), upper,
                  preferred_element_type=F32)[0:1]
    slot = rank + off
    k_lane = lax.broadcasted_iota(jnp.int32, (ROUTE_TM, TOP_K), 1)
    w_out = jnp.zeros((ROUTE_TM, TOP_K), F32)
    q_out = jnp.zeros((ROUTE_TM, TOP_K), F32)
    for k in range(TOP_K):
        w_out = jnp.where(k_lane == k, exps[k] / den, w_out)
        qk = jnp.sum(jnp.where(sels[k], slot, 0.0), axis=-1, keepdims=True)
        q_out = jnp.where(k_lane == k, qk, q_out)
    w_ref[...] = w_out
    q_ref[...] = q_out.astype(jnp.int32)
    cnt_ref[...] = cnt.astype(jnp.int32)
    off_ref[...] = off.astype(jnp.int32)


def _route(logits):
    def tok(w):
        return pl.BlockSpec((ROUTE_TM, w), lambda i: (i, 0))

    tile_row = pl.BlockSpec((None, 1, N_EXPERTS), lambda i: (i, 0, 0))
    return pl.pallas_call(
        _route_kernel,
        grid=(N_TILES,),
        in_specs=[tok(N_EXPERTS)],
        out_specs=[tok(TOP_K), tok(TOP_K), tile_row, tile_row],
        out_shape=[jax.ShapeDtypeStruct((N_TOK, TOP_K), F32),
                   jax.ShapeDtypeStruct((N_TOK, TOP_K), jnp.int32),
                   jax.ShapeDtypeStruct((N_TILES, 1, N_EXPERTS), jnp.int32),
                   jax.ShapeDtypeStruct((N_TILES, 1, N_EXPERTS), jnp.int32)],
        compiler_params=_params(1),
        name="route",
    )(logits)


SLAB_CHUNK = 16


def _slab_pieces(tile, cnt_ref, off_ref, row_ref, fn):
    def per_expert(e, carry):
        n = cnt_ref[tile * N_EXPERTS + e]
        src = off_ref[tile * N_EXPERTS + e]
        dst = row_ref[tile * N_EXPERTS + e]
        n_full = n // SLAB_CHUNK

        def full(j, c):
            fn(src + j * SLAB_CHUNK, dst + j * SLAB_CHUNK, SLAB_CHUNK)
            return c

        lax.fori_loop(0, n_full, full, 0)
        rem = n - n_full * SLAB_CHUNK
        bit = SLAB_CHUNK // 2
        while bit >= 1:
            start = n_full * SLAB_CHUNK + (rem & ~(2 * bit - 1))

            @pl.when((rem & bit) != 0)
            def _(start=start, bit=bit):
                fn(src + start, dst + start, bit)

            bit //= 2
        return carry

    lax.fori_loop(0, N_EXPERTS, per_expert, 0)


def _rows(ref, row, n_rows):
    return ref.at[pl.ds(pl.multiple_of(row * ROW_TILES, ROW_TILES), n_rows * ROW_TILES)]


def _onehot_rows(q, values=None):
    lane = lax.broadcasted_iota(jnp.int32, (ROUTE_TM, TILE_ROWS), 1)
    s = jnp.zeros((ROUTE_TM, TILE_ROWS), F32)
    for k in range(TOP_K):
        v = 1.0 if values is None else values[:, k:k + 1]
        s = jnp.where(lane == q[:, k:k + 1], v, s)
    return s


def _dispatch_kernel(cnt_ref, off_ref, row_ref, q_ref, h2_ref, xs_ref, buf_ref, sem):
    tile = pl.program_id(0)
    sel = _onehot_rows(q_ref[...]).astype(BF16)
    xg = lax.dot_general(sel, h2_ref[...], (((0,), (0,)), ((), ())), preferred_element_type=F32)
    for c in range(ROW_TILES):
        buf_ref[pl.ds(c, TILE_ROWS, stride=ROW_TILES), :] = xg[:, c * LANES:(c + 1) * LANES]

    def copy(local_row, global_row, n_rows):
        return pltpu.make_async_copy(_rows(buf_ref, local_row, n_rows), _rows(xs_ref, global_row, n_rows), sem)

    _slab_pieces(tile, cnt_ref, off_ref, row_ref, lambda a, b, n: copy(a, b, n).start())
    _slab_pieces(tile, cnt_ref, off_ref, row_ref, lambda a, b, n: copy(a, b, n).wait())


def _dispatch(cnt, off, rowstart, q, h2):
    grid_spec = pltpu.PrefetchScalarGridSpec(
        num_scalar_prefetch=3,
        grid=(N_TILES,),
        in_specs=[pl.BlockSpec((ROUTE_TM, TOP_K), lambda i, *_: (i, 0)),
                  pl.BlockSpec((ROUTE_TM, D_MODEL), lambda i, *_: (i, 0))],
        out_specs=pl.BlockSpec(memory_space=pl.ANY),
        scratch_shapes=[pltpu.VMEM((TILE_ROWS * ROW_TILES, LANES), F32), pltpu.SemaphoreType.DMA(())],
    )
    return pl.pallas_call(
        _dispatch_kernel,
        grid_spec=grid_spec,
        out_shape=jax.ShapeDtypeStruct((N_ROWS * ROW_TILES, LANES), F32),
        compiler_params=_params(1),
        name="dispatch",
    )(cnt, off, rowstart, q, h2)


CAST_ROWS = 128


def _expert_kernel(blk_exp_ref, nvalid_ref, first_ref, head_ref, slot_ref, next_ref,
                   xs_ref, w1_hbm, b1_ref, w2_hbm, b2_ref, y_ref,
                   w1f_ref, w2f_ref, w1b_ref, w2b_ref, sem, *, layer):
    b = pl.program_id(0)
    e = blk_exp_ref[b]
    nvalid = nvalid_ref[b]
    slot = slot_ref[b]

    def weight_copies(expert, s):
        return (pltpu.make_async_copy(w1_hbm.at[layer, expert], w1f_ref.at[s], sem.at[0, s]),
                pltpu.make_async_copy(w2_hbm.at[layer, expert], w2f_ref.at[s], sem.at[1, s]))

    @pl.when(first_ref[b] == 1)
    def _():
        @pl.when(head_ref[b] == 1)
        def _():
            for cp in weight_copies(e, slot):
                cp.start()

        for cp in weight_copies(e, slot):
            cp.wait()

        @pl.when(next_ref[b] >= 0)
        def _():
            for cp in weight_copies(next_ref[b], 1 - slot):
                cp.start()

        def cast1(i, carry):
            r = pl.multiple_of(i * CAST_ROWS, CAST_ROWS)
            w1b_ref[pl.ds(r, CAST_ROWS), :] = w1f_ref[slot, pl.ds(r, CAST_ROWS), :].astype(BF16)
            return carry

        def cast2(i, carry):
            r = pl.multiple_of(i * CAST_ROWS, CAST_ROWS)
            w2b_ref[pl.ds(r, CAST_ROWS), :] = w2f_ref[slot, pl.ds(r, CAST_ROWS), :].astype(BF16)
            return carry

        lax.fori_loop(0, D_MODEL // CAST_ROWS, cast1, 0)
        lax.fori_loop(0, D_FF // CAST_ROWS, cast2, 0)

    @pl.when(nvalid > 0)
    def _():
        valid = lax.broadcasted_iota(jnp.int32, (EXP_BLOCK, LANES), 0) < nvalid
        chunks = [jnp.where(valid, xs_ref[pl.ds(c, EXP_BLOCK, stride=ROW_TILES), :], 0.0).astype(BF16)
                  for c in range(ROW_TILES)]
        xb = jnp.concatenate(chunks, axis=-1)
        hid = jnp.dot(xb, w1b_ref[...], preferred_element_type=F32) + b1_ref[...]
        glu = jnp.minimum(hid[:, :D_FF], SWIGLU_LIMIT)
        lin = jnp.clip(hid[:, D_FF:], -SWIGLU_LIMIT, SWIGLU_LIMIT)
        act = glu * _sigmoid(SWIGLU_ALPHA * glu) * (lin + 1.0)
        y = jnp.dot(act.astype(BF16), w2b_ref[...], preferred_element_type=F32) + b2_ref[...]
        for c in range(ROW_TILES):
            y_ref[pl.ds(c, EXP_BLOCK, stride=ROW_TILES), :] = y[:, c * LANES:(c + 1) * LANES]

    @pl.when(nvalid == 0)
    def _():
        y_ref[...] = jnp.zeros_like(y_ref)


def _experts(tables, xs, w1, b1, w2, b2, layer):
    def blk(b, *_):
        return (b, 0)

    def bias(b, be, *_):
        return (layer, be[b], 0, 0)

    grid_spec = pltpu.PrefetchScalarGridSpec(
        num_scalar_prefetch=6,
        grid=(N_BLOCKS,),
        in_specs=[
            pl.BlockSpec((EXP_BLOCK * ROW_TILES, LANES), blk),
            pl.BlockSpec(memory_space=pl.ANY),
            pl.BlockSpec((None, None, 1, 2 * D_FF), bias),
            pl.BlockSpec(memory_space=pl.ANY),
            pl.BlockSpec((None, None, 1, D_MODEL), bias),
        ],
        out_specs=pl.BlockSpec((EXP_BLOCK * ROW_TILES, LANES), blk),
        scratch_shapes=[pltpu.VMEM((2, D_MODEL, 2 * D_FF), F32), pltpu.VMEM((2, D_FF, D_MODEL), F32),
                        pltpu.VMEM((D_MODEL, 2 * D_FF), BF16), pltpu.VMEM((D_FF, D_MODEL), BF16),
                        pltpu.SemaphoreType.DMA((2, 2))],
    )
    return pl.pallas_call(
        functools.partial(_expert_kernel, layer=layer),
        grid_spec=grid_spec,
        out_shape=jax.ShapeDtypeStruct((N_ROWS * ROW_TILES, LANES), F32),
        compiler_params=_params(1),
        name="experts",
    )(*tables, xs, w1, b1.reshape(DEPTH, N_EXPERTS, 1, 2 * D_FF), w2,
      b2.reshape(DEPTH, N_EXPERTS, 1, D_MODEL))


def _split_bf16(x):
    hi = x.astype(BF16)
    return hi, (x - hi.astype(F32)).astype(BF16)


def _combine_kernel(cnt_ref, off_ref, row_ref, q_ref, w_ref, y_ref, xm_ref, mod_ref, fg_ref,
                    x_ref, xn_ref, buf_ref, sem):
    tile = pl.program_id(0)

    def copy(local_row, global_row, n_rows):
        return pltpu.make_async_copy(_rows(y_ref, global_row, n_rows), _rows(buf_ref, local_row, n_rows), sem)

    _slab_pieces(tile, cnt_ref, off_ref, row_ref, lambda a, b, n: copy(a, b, n).start())
    s_hi, s_lo = _split_bf16(_onehot_rows(q_ref[...], w_ref[...]))
    _slab_pieces(tile, cnt_ref, off_ref, row_ref, lambda a, b, n: copy(a, b, n).wait())
    rows = jnp.concatenate([buf_ref[pl.ds(c, TILE_ROWS, stride=ROW_TILES), :] for c in range(ROW_TILES)],
                           axis=-1)
    r_hi, r_lo = _split_bf16(rows)
    moe = jnp.dot(s_hi, r_hi, preferred_element_type=F32)
    moe = moe + jnp.dot(s_lo, r_hi, preferred_element_type=F32)
    moe = moe + jnp.dot(s_hi, r_lo, preferred_element_type=F32)
    m = mod_ref[...]
    x = xm_ref[...] + m[5:6] * moe
    x_ref[...] = x
    xn_ref[...] = x * lax.rsqrt(jnp.mean(x * x, axis=-1, keepdims=True) + EPS) * fg_ref[...]


def _combine(cnt, off, rowstart, q, topw, y, xm, mods_l, final_g):
    def tok(w):
        return pl.BlockSpec((ROUTE_TM, w), lambda i, *_: (i, 0))

    grid_spec = pltpu.PrefetchScalarGridSpec(
        num_scalar_prefetch=3,
        grid=(N_TILES,),
        in_specs=[tok(TOP_K), tok(TOP_K),
                  pl.BlockSpec(memory_space=pl.ANY),
                  tok(D_MODEL),
                  pl.BlockSpec((None, N_MOD, D_MODEL), lambda i, *_: (_mod_row(i * ROUTE_TM), 0, 0)),
                  pl.BlockSpec((1, D_MODEL), lambda i, *_: (0, 0))],
        out_specs=[tok(D_MODEL), tok(D_MODEL)],
        scratch_shapes=[pltpu.VMEM((TILE_ROWS * ROW_TILES, LANES), F32), pltpu.SemaphoreType.DMA(())],
    )
    return pl.pallas_call(
        _combine_kernel,
        grid_spec=grid_spec,
        out_shape=[jax.ShapeDtypeStruct((N_TOK, D_MODEL), F32),
                   jax.ShapeDtypeStruct((N_TOK, D_MODEL), F32)],
        compiler_params=_params(1),
        name="combine",
    )(cnt, off, rowstart, q, topw, y, xm, mods_l, final_g)


def _rope_tables():
    t = np.arange(DEC_SEQ)
    row = (t // GRID_W).astype(np.float32)
    col = (t % GRID_W).astype(np.float32)
    inv = jnp.asarray(ROPE_THETA, F32) ** (-jnp.arange(ROPE_PAIRS, dtype=F32) / ROPE_PAIRS)
    ang = jnp.concatenate([jnp.asarray(row)[:, None] * inv, jnp.asarray(col)[:, None] * inv], axis=-1)
    cos = jnp.repeat(jnp.cos(ang), 2, axis=-1)
    sin = jnp.repeat(jnp.sin(ang), 2, axis=-1)
    sign = jnp.asarray(np.tile(np.array([-1.0, 1.0], np.float32), HEAD_DIM // 2))
    return jnp.tile(cos, (1, A_HEADS)), jnp.tile(sin * sign, (1, A_HEADS))


def _routing_tables(tile_cnt):
    i32 = jnp.int32
    carry = jnp.cumsum(tile_cnt, axis=0) - tile_cnt
    counts = jnp.sum(tile_cnt, axis=0)
    padded = (counts + EXP_BLOCK - 1) // EXP_BLOCK * EXP_BLOCK
    pad_end = jnp.cumsum(padded)
    pad_start = pad_end - padded
    rowstart = (pad_start[None, :] + carry).astype(i32)
    blk_row = jnp.arange(N_BLOCKS, dtype=i32) * EXP_BLOCK
    blk_exp = jnp.sum((blk_row[:, None] >= pad_end[None, :]).astype(i32), axis=1)
    blk_exp = jnp.minimum(blk_exp, N_EXPERTS - 1)
    nvalid = jnp.clip(counts[blk_exp] - (blk_row - pad_start[blk_exp]), 0, EXP_BLOCK).astype(i32)
    first = jnp.logical_and(blk_row == pad_start[blk_exp], nvalid > 0)
    active = counts > 0
    act_rank = jnp.cumsum(active.astype(i32)) - 1
    eid = jnp.arange(N_EXPERTS, dtype=i32)
    later = jnp.logical_and(active[None, :], eid[None, :] > eid[:, None])
    nxt = jnp.min(jnp.where(later, eid[None, :], N_EXPERTS), axis=1)
    nxt = jnp.where(nxt == N_EXPERTS, -1, nxt)
    head = jnp.logical_and(first, act_rank[blk_exp] == 0)
    tables = (blk_exp, nvalid, first.astype(i32), head.astype(i32), (act_rank[blk_exp] % 2).astype(i32),
              nxt[blk_exp].astype(i32))
    return rowstart.reshape(-1), tables


def kernel(x_prompt, x_sample, cache_attn_k, cache_attn_v, cache_na_k, cache_na_v, c, c_ctx, w_mod, b_mod, norm1_g, norm2_g, w_in, b_gate, q_norm_g, k_norm_g, na_rpb, conv_w, conv_b, conv_ln_g, conv_ln_b, sgu_ln_g, sgu_ln_b, sgu_w, sgu_b, w_branch, w_out, router_w, router_b, exp_w1, exp_b1, exp_w2, exp_b2, final_g):
    x = jnp.concatenate([x_prompt.reshape(N_P, D_MODEL), x_sample.reshape(N_S, D_MODEL)], axis=0)
    cvec = jnp.zeros((SUBLANES, D_MODEL), F32).at[0].set(c_ctx).at[1:1 + DEC_BATCH].set(c)
    mods = _modulation(cvec, w_mod, b_mod).reshape(DEPTH, SUBLANES, N_MOD, D_MODEL)
    cos_t, sin_t = _rope_tables()
    cak = cache_attn_k.reshape(DEC_BATCH, DEPTH, PAST_LEN, A_KV)
    cav = cache_attn_v.reshape(DEC_BATCH, DEPTH, PAST_LEN, A_KV)
    cbk = cache_na_k.reshape(DEC_BATCH, DEPTH, PAST_LEN, B_W)
    cbv = cache_na_v.reshape(DEC_BATCH, DEPTH, PAST_LEN, B_W)
    w_mix = w_in[:, :, :MIX_COLS].astype(BF16)
    w_gate = w_in[:, :, MIX_COLS:].astype(BF16)
    w_br = w_branch.astype(BF16)
    w_o = w_out.astype(BF16)
    rw_hi = router_w.astype(BF16)
    rw_lo = (router_w - rw_hi.astype(F32)).astype(BF16)
    final_g2 = final_g.reshape(1, D_MODEL)

    new_k, new_v, new_bk, new_bv = [], [], [], []
    xn = None
    for l in range(DEPTH):
        mods_l = mods[l]
        g1 = norm1_g[l].reshape(1, D_MODEL)
        gq = jnp.tile(q_norm_g[l], A_HEADS).reshape(1, A_Q)
        gk = jnp.tile(k_norm_g[l], A_KV_HEADS).reshape(1, A_KV)
        aq, ak, av, bq, bk, bv, cz, dz = _in_proj(x, mods_l, g1, w_mix[l])
        ya_p, yb_p, kn = _prompt_attn(aq, ak, av, bq, bk, bv, gq, gk)
        new_k.append(kn.reshape(BATCH, SEQ, A_KV_HEADS, HEAD_DIM))
        new_v.append(av[:N_P].reshape(BATCH, SEQ, A_KV_HEADS, HEAD_DIM))
        new_bk.append(bk[:N_P].reshape(BATCH, SEQ, B_HEADS, HEAD_DIM))
        new_bv.append(bv[:N_P].reshape(BATCH, SEQ, B_HEADS, HEAD_DIM))
        ya_s = _sample_attn(aq, ak, av, cak, cav, cos_t, sin_t, gq, gk, l)
        yb_s = _na_attn(bq, bk, bv, cbk, cbv, _na_bias(na_rpb[l]), l)
        ya = jnp.concatenate([ya_p, ya_s], axis=0)
        yb = jnp.concatenate([yb_p, yb_s], axis=0)
        cw = conv_w[l]
        cb = conv_b[l].reshape(1, C_WIDTH)
        cg = conv_ln_g[l].reshape(1, C_WIDTH)
        cbb = conv_ln_b[l].reshape(1, C_WIDTH)
        yc = jnp.concatenate([_conv_call(cz, cw, cb, cg, cbb, SEQ, 0, BATCH),
                              _conv_call(cz, cw, cb, cg, cbb, DEC_SEQ, N_P // DEC_SEQ, DEC_BATCH)], axis=0)
        bs_full = jnp.repeat(sgu_b[l].T, SGU_GW, axis=1)
        yd = _sgu(dz, sgu_ln_g[l].reshape(1, SGU_WIDTH), sgu_ln_b[l].reshape(1, SGU_WIDTH),
                  sgu_w[l].astype(BF16), bs_full)
        xm, h2, logits = _merge(x, ya, yb, yc, yd, mods_l, g1, w_gate[l],
                                b_gate[l].reshape(1, N_BRANCH * D_MODEL), w_br[l], w_o[l],
                                norm2_g[l].reshape(1, D_MODEL), rw_hi[l], rw_lo[l],
                                router_b[l].reshape(1, N_EXPERTS))
        top_w, q, tile_cnt, tile_off = _route(logits)
        rowstart, tables = _routing_tables(tile_cnt.reshape(N_TILES, N_EXPERTS))
        tile_cnt = tile_cnt.reshape(-1)
        tile_off = tile_off.reshape(-1)
        xs = _dispatch(tile_cnt, tile_off, rowstart, q, h2)
        y = _experts(tables, xs, exp_w1, exp_b1, exp_w2, exp_b2, l)
        x, xn = _combine(tile_cnt, tile_off, rowstart, q, top_w, y, xm, mods_l, final_g2)

    y_prompt = xn[:N_P].reshape(BATCH, SEQ, D_MODEL)
    y_sample = xn[N_P:].reshape(DEC_BATCH, DEC_SEQ, D_MODEL)
    return (y_prompt, y_sample, jnp.stack(new_k, axis=1), jnp.stack(new_v, axis=1),
            jnp.stack(new_bk, axis=1), jnp.stack(new_bv, axis=1))
```

```python
import functools

import numpy as np
import jax
import jax.numpy as jnp
from jax import lax
from jax.experimental import pallas as pl
from jax.experimental.pallas import tpu as pltpu

D_MODEL = 1024
BATCH = 32
SEQ = 256
DEPTH = 2
DEC_BATCH = 2
DEC_SEQ = 1024
PAST_LEN = 512
GRID_W = 64
HEAD_DIM = 64
A_HEADS = 4
A_KV_HEADS = 2
B_HEADS = 4
NA_ROWS = 8
NA_COLS = 16
C_WIDTH = 256
CONV_WIDTH = 31
SGU_WIDTH = 256
SGU_GROUPS = 4
SGU_CHUNK = 128
N_BRANCH = 4
BRANCH_W = 256
N_EXPERTS = 32
TOP_K = 4
D_FF = 1024
SWIGLU_ALPHA = 1.702
SWIGLU_LIMIT = 7.0
MOE_BLOCK = 128
ROPE_THETA = 10000.0
ROPE_PAIRS = HEAD_DIM // 4
N_MOD = 6
EPS = 1e-6
NEG_INF = -1e30

A_Q = A_HEADS * HEAD_DIM
A_KV = A_KV_HEADS * HEAD_DIM
B_W = B_HEADS * HEAD_DIM
MIX_SIZES = (A_Q, A_KV, A_KV, B_W, B_W, B_W, 2 * C_WIDTH, 2 * SGU_WIDTH)
MIX_COLS = sum(MIX_SIZES)

N_P = BATCH * SEQ
N_S = DEC_BATCH * DEC_SEQ
N_TOK = N_P + N_S
N_ASSIGN = N_TOK * TOP_K
EXP_BLOCK = 256
N_BLOCKS = N_ASSIGN // EXP_BLOCK + N_EXPERTS
N_ROWS = N_BLOCKS * EXP_BLOCK
GRID_ROWS = DEC_SEQ // GRID_W
NA_WR = min(NA_ROWS, GRID_ROWS)
N_LOC = NA_WR * GRID_W

SUBLANES = 8
LANES = 128
ROW_TILES = D_MODEL // LANES
VMEM_LIMIT = 56 * 1024 * 1024

F32 = jnp.float32
BF16 = jnp.bfloat16


def _params(n_axes, vmem=None):
    return pltpu.CompilerParams(
        dimension_semantics=("arbitrary",) * n_axes,
        vmem_limit_bytes=vmem if vmem is not None else VMEM_LIMIT)


def _mod_row(start):
    return jnp.where(start < N_P, 0, 1 + (start - N_P) // DEC_SEQ)


def _bdot(a, b):
    return jnp.dot(a.astype(BF16), b.astype(BF16), preferred_element_type=F32)


def _bdot_nt(a, b):
    return lax.dot_general(a.astype(BF16), b.astype(BF16), (((1,), (1,)), ((), ())),
                           preferred_element_type=F32)


def _sigmoid(x):
    return 1.0 / (1.0 + jnp.exp(-x))


MOD_TN = 1536


def _mod_kernel(c_ref, w_ref, b_ref, o_ref):
    c = c_ref[...]
    s = c * _sigmoid(c)
    o_ref[...] = _bdot(s, w_ref[...]) + b_ref[...]


def _modulation(cvec, w_mod, b_mod):
    n_col = N_MOD * D_MODEL
    return pl.pallas_call(
        _mod_kernel,
        grid=(DEPTH, n_col // MOD_TN),
        in_specs=[
            pl.BlockSpec((SUBLANES, D_MODEL), lambda l, j: (0, 0)),
            pl.BlockSpec((None, D_MODEL, MOD_TN), lambda l, j: (l, 0, j)),
            pl.BlockSpec((None, 1, MOD_TN), lambda l, j: (l, 0, j)),
        ],
        out_specs=pl.BlockSpec((None, SUBLANES, MOD_TN), lambda l, j: (l, 0, j)),
        out_shape=jax.ShapeDtypeStruct((DEPTH, SUBLANES, n_col), F32),
        compiler_params=_params(2),
        name="modulation",
    )(cvec, w_mod, b_mod.reshape(DEPTH, 1, n_col))


IN_TM = 512


def _norm_mod(x, g, shift, scale):
    y = x * lax.rsqrt(jnp.mean(x * x, axis=-1, keepdims=True) + EPS) * g
    return y * (1.0 + scale) + shift


def _in_kernel(x_ref, mod_ref, g_ref, w_ref, *out_refs):
    m = mod_ref[...]
    h = _norm_mod(x_ref[...], g_ref[...], m[0:1], m[1:2])
    z = jnp.dot(h.astype(BF16), w_ref[...], preferred_element_type=F32)
    off = 0
    for o_ref, sz in zip(out_refs, MIX_SIZES):
        o_ref[...] = z[:, off:off + sz]
        off += sz


def _in_proj(x, mods_l, g1, w_in_b, layer):
    return pl.pallas_call(
        _in_kernel,
        grid=(N_TOK // IN_TM,),
        in_specs=[
            pl.BlockSpec((IN_TM, D_MODEL), lambda i: (i, 0)),
            pl.BlockSpec((None, N_MOD, D_MODEL), lambda i: (_mod_row(i * IN_TM), 0, 0)),
            pl.BlockSpec((1, D_MODEL), lambda i: (0, 0)),
            pl.BlockSpec((None, D_MODEL, MIX_COLS), lambda i: (layer, 0, 0)),
        ],
        out_specs=[pl.BlockSpec((IN_TM, sz), lambda i: (i, 0)) for sz in MIX_SIZES],
        out_shape=[jax.ShapeDtypeStruct((N_TOK, sz), F32) for sz in MIX_SIZES],
        compiler_params=_params(1),
        name="in_proj",
    )(x, mods_l, g1, w_in_b)


def _head_rms(x, g):
    n_heads = x.shape[-1] // HEAD_DIM
    seg = lax.broadcasted_iota(jnp.int32, x.shape, 1) // HEAD_DIM
    xx = x * x
    inv = jnp.zeros_like(x)
    for h in range(n_heads):
        ms = jnp.sum(jnp.where(seg == h, xx, 0.0), axis=-1, keepdims=True) * (1.0 / HEAD_DIM)
        inv = jnp.where(seg == h, lax.rsqrt(ms + EPS), inv)
    return x * inv * g


def _softmax_pv(score_parts, value_parts):
    m = score_parts[0].max(axis=-1, keepdims=True)
    for s in score_parts[1:]:
        m = jnp.maximum(m, s.max(axis=-1, keepdims=True))
    den = None
    acc = None
    for s, v in zip(score_parts, value_parts):
        e = jnp.exp(s - m)
        d = e.sum(axis=-1, keepdims=True)
        a = _bdot(e, v)
        den = d if den is None else den + d
        acc = a if acc is None else acc + a
    return acc / den


def _head(x, h):
    return x[:, h * HEAD_DIM:(h + 1) * HEAD_DIM]


SCALE = HEAD_DIM ** -0.5


def _prompt_attn_kernel(aq_ref, ak_ref, av_ref, bq_ref, bk_ref, bv_ref, gq_ref, gk_ref, *refs):
    ya_ref, yb_ref, nk_ref, nv_ref, nbk_ref, nbv_ref = refs[-6:]
    aq = _head_rms(aq_ref[...], gq_ref[...])
    ak = _head_rms(ak_ref[...], gk_ref[...])
    av = av_ref[...]
    nk_ref[...] = ak
    nv_ref[...] = av
    nbk_ref[...] = bk_ref[...]
    nbv_ref[...] = bv_ref[...]
    grp = A_HEADS // A_KV_HEADS
    outs = []
    for h in range(A_HEADS):
        s = _bdot_nt(_head(aq, h), _head(ak, h // grp)) * SCALE
        outs.append(_softmax_pv([s], [_head(av, h // grp)]))
    ya_ref[...] = jnp.concatenate(outs, axis=-1)
    bq = bq_ref[...]
    bk = bk_ref[...]
    bv = bv_ref[...]
    outs = []
    for h in range(B_HEADS):
        s = _bdot_nt(_head(bq, h), _head(bk, h)) * SCALE
        outs.append(_softmax_pv([s], [_head(bv, h)]))
    yb_ref[...] = jnp.concatenate(outs, axis=-1)


def _prompt_attn(aq, ak, av, bq, bk, bv, gq, gk, caches, layer):
    def spec(w):
        return pl.BlockSpec((SEQ, w), lambda b: (b, 0))

    def cache_spec(w):
        return pl.BlockSpec((SEQ, w), lambda b: (b * DEPTH + layer, 0))

    cache_widths = (A_KV, A_KV, B_W, B_W)
    n_in = 8
    return pl.pallas_call(
        _prompt_attn_kernel,
        grid=(BATCH,),
        in_specs=[spec(A_Q), spec(A_KV), spec(A_KV), spec(B_W), spec(B_W), spec(B_W),
                  pl.BlockSpec((1, A_Q), lambda b: (0, 0)),
                  pl.BlockSpec((1, A_KV), lambda b: (0, 0))]
        + [pl.BlockSpec(memory_space=pl.ANY) for _ in caches],
        out_specs=[spec(A_Q), spec(B_W)] + [cache_spec(w) for w in cache_widths],
        out_shape=[jax.ShapeDtypeStruct((N_TOK, A_Q), F32),
                   jax.ShapeDtypeStruct((N_TOK, B_W), F32)]
        + [jax.ShapeDtypeStruct((BATCH * DEPTH * SEQ, w), F32) for w in cache_widths],
        input_output_aliases={n_in + j: 2 + j for j in range(len(caches))},
        compiler_params=_params(1),
        name="prompt_attn",
    )(aq, ak, av, bq, bk, bv, gq, gk, *caches)


QB = 128


def _rope(x, cos, sin_signed):
    n = x.shape[-1]
    nxt = pltpu.roll(x, n - 1, 1)
    prv = pltpu.roll(x, 1, 1)
    even = (lax.broadcasted_iota(jnp.int32, x.shape, 1) % 2) == 0
    return x * cos + jnp.where(even, nxt, prv) * sin_signed


def _sample_attn_kernel(q_ref, k_ref, v_ref, ck_ref, cv_ref, cosq_ref, sinq_ref, cosk_ref, sink_ref,
                        gq_ref, gk_ref, ya_prompt_ref, o_ref):
    del ya_prompt_ref
    q = _rope(_head_rms(q_ref[...], gq_ref[...]), cosq_ref[...], sinq_ref[...])
    k = _rope(_head_rms(k_ref[...], gk_ref[...]), cosk_ref[...], sink_ref[...])
    v = v_ref[...]
    ck = ck_ref[...]
    cv = cv_ref[...]
    grp = A_HEADS // A_KV_HEADS
    outs = []
    for h in range(A_HEADS):
        j = h // grp
        qh = _head(q, h)
        s1 = _bdot_nt(qh, _head(k, j)) * SCALE
        s2 = _bdot_nt(qh, _head(ck, j)) * SCALE
        outs.append(_softmax_pv([s1, s2], [_head(v, j), _head(cv, j)]))
    o_ref[...] = jnp.concatenate(outs, axis=-1)


def _sample_attn(aq, ak, av, cache_k, cache_v, cos_t, sin_t, gq, gk, ya, layer):
    nqb = DEC_SEQ // QB
    q0 = N_P // QB
    k0 = N_P // DEC_SEQ
    return pl.pallas_call(
        _sample_attn_kernel,
        grid=(DEC_BATCH, nqb),
        in_specs=[
            pl.BlockSpec((QB, A_Q), lambda b, i: (q0 + b * nqb + i, 0)),
            pl.BlockSpec((DEC_SEQ, A_KV), lambda b, i: (k0 + b, 0)),
            pl.BlockSpec((DEC_SEQ, A_KV), lambda b, i: (k0 + b, 0)),
            pl.BlockSpec((None, None, PAST_LEN, A_KV), lambda b, i: (b, layer, 0, 0)),
            pl.BlockSpec((None, None, PAST_LEN, A_KV), lambda b, i: (b, layer, 0, 0)),
            pl.BlockSpec((QB, A_Q), lambda b, i: (i, 0)),
            pl.BlockSpec((QB, A_Q), lambda b, i: (i, 0)),
            pl.BlockSpec((DEC_SEQ, A_KV), lambda b, i: (0, 0)),
            pl.BlockSpec((DEC_SEQ, A_KV), lambda b, i: (0, 0)),
            pl.BlockSpec((1, A_Q), lambda b, i: (0, 0)),
            pl.BlockSpec((1, A_KV), lambda b, i: (0, 0)),
            pl.BlockSpec(memory_space=pl.ANY),
        ],
        out_specs=pl.BlockSpec((QB, A_Q), lambda b, i: (q0 + b * nqb + i, 0)),
        out_shape=jax.ShapeDtypeStruct((N_TOK, A_Q), F32),
        input_output_aliases={11: 0},
        compiler_params=_params(2),
        name="sample_attn",
    )(aq, ak, av, cache_k, cache_v, cos_t, sin_t, cos_t, sin_t, gq, gk, ya)


N_ROW_OFF = 2 * NA_ROWS - 1
N_COL_OFF = 2 * NA_COLS - 1
NA_PAIRS = N_ROW_OFF - 1
assert NA_WR == NA_ROWS and NA_WR % 2 == 0 and 2 * GRID_W == LANES


def _na_bias_kernel(rpb_ref, o_ref):
    h = pl.program_id(0)
    qc = lax.broadcasted_iota(jnp.int32, (GRID_W, LANES), 0)
    lane = lax.broadcasted_iota(jnp.int32, (GRID_W, LANES), 1)
    right = lane >= GRID_W
    kc = jnp.where(right, lane - GRID_W, lane)
    c_start = jnp.clip(qc - NA_COLS // 2, 0, GRID_W - NA_COLS)
    col_in = jnp.logical_and(kc >= c_start, kc < c_start + NA_COLS)
    col_off = jnp.clip(kc - qc + NA_COLS - 1, 0, N_COL_OFF - 1)
    for p in range(NA_PAIRS):
        acc = jnp.zeros((GRID_W, LANES), F32)
        for o in range(N_COL_OFF):
            left_v = rpb_ref[(h * N_ROW_OFF + p) * N_COL_OFF + o]
            right_v = rpb_ref[(h * N_ROW_OFF + p + 1) * N_COL_OFF + o]
            acc = jnp.where(col_off == o, jnp.where(right, right_v, left_v), acc)
        o_ref[p] = jnp.where(col_in, acc, NEG_INF)


def _na_bias(rpb):
    return pl.pallas_call(
        _na_bias_kernel,
        grid_spec=pltpu.PrefetchScalarGridSpec(
            num_scalar_prefetch=1,
            grid=(B_HEADS,),
            in_specs=[],
            out_specs=pl.BlockSpec((None, NA_PAIRS, GRID_W, LANES), lambda h, *_: (h, 0, 0, 0)),
        ),
        out_shape=jax.ShapeDtypeStruct((B_HEADS, NA_PAIRS, GRID_W, LANES), F32),
        compiler_params=_params(1),
        name="na_bias",
    )(rpb.reshape(-1))


def _na_kernel(q_ref, k_ref, v_ref, ck_ref, cv_ref, bias_ref, yb_prompt_ref, o_ref):
    del yb_prompt_ref
    r = pl.program_id(1)
    r_start = jnp.clip(r - NA_WR // 2, 0, GRID_ROWS - NA_WR)
    base = pl.multiple_of(r_start * GRID_W, GRID_W)
    row_off0 = r_start - r + NA_ROWS - 1
    q = q_ref[...]
    kb = k_ref[pl.ds(base, N_LOC), :]
    vb = v_ref[pl.ds(base, N_LOC), :]
    ck = ck_ref[...]
    cv = cv_ref[...]
    outs = []
    for h in range(B_HEADS):
        qh = _head(q, h)
        bias = jnp.concatenate([bias_ref[h, row_off0 + 2 * j] for j in range(NA_WR // 2)], axis=-1)
        s1 = _bdot_nt(qh, _head(kb, h)) * SCALE + bias
        s2 = _bdot_nt(qh, _head(ck, h)) * SCALE
        outs.append(_softmax_pv([s1, s2], [_head(vb, h), _head(cv, h)]))
    o_ref[...] = jnp.concatenate(outs, axis=-1)


def _na_attn(bq, bk, bv, cache_k, cache_v, bias, yb, layer):
    q0 = N_P // GRID_W
    k0 = N_P // DEC_SEQ
    return pl.pallas_call(
        _na_kernel,
        grid=(DEC_BATCH, GRID_ROWS),
        in_specs=[
            pl.BlockSpec((GRID_W, B_W), lambda b, r: (q0 + b * GRID_ROWS + r, 0)),
            pl.BlockSpec((DEC_SEQ, B_W), lambda b, r: (k0 + b, 0)),
            pl.BlockSpec((DEC_SEQ, B_W), lambda b, r: (k0 + b, 0)),
            pl.BlockSpec((None, None, PAST_LEN, B_W), lambda b, r: (b, layer, 0, 0)),
            pl.BlockSpec((None, None, PAST_LEN, B_W), lambda b, r: (b, layer, 0, 0)),
            pl.BlockSpec((B_HEADS, NA_PAIRS, GRID_W, LANES), lambda b, r: (0, 0, 0, 0)),
            pl.BlockSpec(memory_space=pl.ANY),
        ],
        out_specs=pl.BlockSpec((GRID_W, B_W), lambda b, r: (q0 + b * GRID_ROWS + r, 0)),
        out_shape=jax.ShapeDtypeStruct((N_TOK, B_W), F32),
        input_output_aliases={6: 0},
        compiler_params=_params(2),
        name="na_attn",
    )(bq, bk, bv, cache_k, cache_v, bias, yb)


CONV_PAD = 16
CONV_CHUNK = 128


def _layer_norm(x, g, b):
    mu = jnp.mean(x, axis=-1, keepdims=True)
    xc = x - mu
    var = jnp.mean(xc * xc, axis=-1, keepdims=True)
    return xc * lax.rsqrt(var + EPS) * g + b


def _conv_kernel(z_ref, w_ref, cb_ref, g_ref, b_ref, *refs, s_len):
    o_ref, pad_ref = refs[-2:]
    z = z_ref[...]
    u = z[:, :C_WIDTH] * _sigmoid(z[:, C_WIDTH:])
    pad_ref[pl.ds(0, CONV_PAD), :] = jnp.zeros((CONV_PAD, C_WIDTH), F32)
    pad_ref[pl.ds(CONV_PAD + s_len, CONV_PAD), :] = jnp.zeros((CONV_PAD, C_WIDTH), F32)
    pad_ref[pl.ds(CONV_PAD, s_len), :] = u
    w = w_ref[...]
    shift = CONV_PAD - CONV_WIDTH // 2
    for c in range(s_len // CONV_CHUNK):
        acc = jnp.zeros((CONV_CHUNK, C_WIDTH), F32)
        for k in range(CONV_WIDTH):
            acc = acc + pad_ref[pl.ds(c * CONV_CHUNK + k + shift, CONV_CHUNK), :] * w[k:k + 1]
        y = _layer_norm(acc + cb_ref[...], g_ref[...], b_ref[...])
        o_ref[pl.ds(c * CONV_CHUNK, CONV_CHUNK), :] = y * _sigmoid(y)


def _conv_call(cz, w, cb, g, b, s_len, first_blk, n_seq, partial_out=None):
    vec = pl.BlockSpec((1, C_WIDTH), lambda i: (0, 0))
    extra = [] if partial_out is None else [partial_out]
    return pl.pallas_call(
        functools.partial(_conv_kernel, s_len=s_len),
        grid=(n_seq,),
        in_specs=[pl.BlockSpec((s_len, 2 * C_WIDTH), lambda i: (first_blk + i, 0)),
                  pl.BlockSpec((CONV_WIDTH, C_WIDTH), lambda i: (0, 0)), vec, vec, vec]
        + [pl.BlockSpec(memory_space=pl.ANY) for _ in extra],
        out_specs=pl.BlockSpec((s_len, C_WIDTH), lambda i: (first_blk + i, 0)),
        out_shape=jax.ShapeDtypeStruct((N_TOK, C_WIDTH), F32),
        input_output_aliases={5: 0} if extra else {},
        scratch_shapes=[pltpu.VMEM((s_len + 2 * CONV_PAD, C_WIDTH), F32)],
        compiler_params=_params(1),
        name="conformer_conv_%d" % s_len,
    )(cz, w, cb, g, b, *extra)


SGU_TM = 512
SGU_GW = SGU_WIDTH // SGU_GROUPS


def _sgu_kernel(z_ref, g_ref, b_ref, ws_ref, bs_ref, o_ref):
    z = z_ref[...]
    z = 0.5 * z * (1.0 + lax.erf(z * (2.0 ** -0.5)))
    u = z[:, :SGU_WIDTH]
    v = _layer_norm(z[:, SGU_WIDTH:], g_ref[...], b_ref[...])
    for c in range(SGU_TM // SGU_CHUNK):
        vc = v[c * SGU_CHUNK:(c + 1) * SGU_CHUNK]
        parts = [_bdot(ws_ref[g], vc[:, g * SGU_GW:(g + 1) * SGU_GW]) for g in range(SGU_GROUPS)]
        mixed = jnp.concatenate(parts, axis=-1) + bs_ref[...]
        o_ref[pl.ds(c * SGU_CHUNK, SGU_CHUNK), :] = u[c * SGU_CHUNK:(c + 1) * SGU_CHUNK] * mixed


def _sgu(dz, g, b, ws, bs_full):
    vec = pl.BlockSpec((1, SGU_WIDTH), lambda i: (0, 0))
    return pl.pallas_call(
        _sgu_kernel,
        grid=(N_TOK // SGU_TM,),
        in_specs=[pl.BlockSpec((SGU_TM, 2 * SGU_WIDTH), lambda i: (i, 0)), vec, vec,
                  pl.BlockSpec((SGU_GROUPS, SGU_CHUNK, SGU_CHUNK), lambda i: (0, 0, 0)),
                  pl.BlockSpec((SGU_CHUNK, SGU_WIDTH), lambda i: (0, 0))],
        out_specs=pl.BlockSpec((SGU_TM, SGU_WIDTH), lambda i: (i, 0)),
        out_shape=jax.ShapeDtypeStruct((N_TOK, SGU_WIDTH), F32),
        compiler_params=_params(1),
        name="chunk_sgu",
    )(dz, g, b, ws, bs_full)


MERGE_TM = 512


def _merge_kernel(x_ref, ya_ref, yb_ref, yc_ref, yd_ref, mod_ref, g1_ref, w_in_hbm, bg_ref, wb_ref, wo_ref,
                  g2_ref, rwh_ref, rwl_ref, rb_ref, xm_ref, h2_ref, lg_ref, wg_ref, sem, *, layer):
    @pl.when(pl.program_id(0) == 0)
    def _():
        cp = pltpu.make_async_copy(w_in_hbm.at[layer, :, pl.ds(MIX_COLS, N_BRANCH * D_MODEL)], wg_ref, sem)
        cp.start()
        cp.wait()

    m = mod_ref[...]
    x = x_ref[...]
    h = _norm_mod(x, g1_ref[...], m[0:1], m[1:2]).astype(BF16)
    merged = None
    for i, y_ref in enumerate((ya_ref, yb_ref, yc_ref, yd_ref)):
        logit = jnp.dot(h, wg_ref[:, i * D_MODEL:(i + 1) * D_MODEL], preferred_element_type=F32)
        gate = _sigmoid(logit + bg_ref[:, i * D_MODEL:(i + 1) * D_MODEL])
        term = gate * jnp.dot(y_ref[...].astype(BF16), wb_ref[i], preferred_element_type=F32)
        merged = term if merged is None else merged + term
    out = jnp.dot(merged.astype(BF16), wo_ref[...], preferred_element_type=F32)
    xm = x + m[2:3] * out
    xm_ref[...] = xm
    h2 = _norm_mod(xm, g2_ref[...], m[3:4], m[4:5])
    h2_hi = h2.astype(BF16)
    h2_lo = (h2 - h2_hi.astype(F32)).astype(BF16)
    lg = jnp.dot(h2_hi, rwh_ref[...], preferred_element_type=F32)
    lg = lg + jnp.dot(h2_hi, rwl_ref[...], preferred_element_type=F32)
    lg = lg + jnp.dot(h2_lo, rwh_ref[...], preferred_element_type=F32)
    lg_ref[...] = lg + rb_ref[...]
    h2_ref[...] = h2_hi


def _merge(x, ya, yb, yc, yd, mods_l, g1, w_in_b, bg, wb, wo, g2, rwh, rwl, rb, layer):
    def tok(w):
        return pl.BlockSpec((MERGE_TM, w), lambda i: (i, 0))

    def full(*shape):
        return pl.BlockSpec(shape, lambda i: (0,) * len(shape))

    return pl.pallas_call(
        functools.partial(_merge_kernel, layer=layer),
        grid=(N_TOK // MERGE_TM,),
        in_specs=[tok(D_MODEL), tok(BRANCH_W), tok(BRANCH_W), tok(BRANCH_W), tok(BRANCH_W),
                  pl.BlockSpec((None, N_MOD, D_MODEL), lambda i: (_mod_row(i * MERGE_TM), 0, 0)),
                  full(1, D_MODEL), pl.BlockSpec(memory_space=pl.ANY), full(1, N_BRANCH * D_MODEL),
                  full(N_BRANCH, BRANCH_W, D_MODEL), full(D_MODEL, D_MODEL), full(1, D_MODEL),
                  full(D_MODEL, N_EXPERTS), full(D_MODEL, N_EXPERTS), full(1, N_EXPERTS)],
        out_specs=[tok(D_MODEL), tok(D_MODEL), tok(N_EXPERTS)],
        out_shape=[jax.ShapeDtypeStruct((N_TOK, D_MODEL), F32),
                   jax.ShapeDtypeStruct((N_TOK, D_MODEL), BF16),
                   jax.ShapeDtypeStruct((N_TOK, N_EXPERTS), F32)],
        scratch_shapes=[pltpu.VMEM((D_MODEL, N_BRANCH * D_MODEL), BF16), pltpu.SemaphoreType.DMA(())],
        compiler_params=_params(1),
        name="merge",
    )(x, ya, yb, yc, yd, mods_l, g1, w_in_b, bg, wb, wo, g2, rwh, rwl, rb)


ROUTE_TM = 256
TILE_ROWS = ROUTE_TM * TOP_K
N_TILES = N_TOK // ROUTE_TM


def _route_kernel(lg_ref, w_ref, q_ref, cnt_ref, off_ref):
    lg = lg_ref[...]
    lane = lax.broadcasted_iota(jnp.int32, lg.shape, 1)
    sels, vals = [], []
    for _ in range(TOP_K):
        mx = lg.max(axis=-1, keepdims=True)
        idx = jnp.where(lg == mx, lane, N_EXPERTS).min(axis=-1, keepdims=True)
        sel = lane == idx
        sels.append(sel)
        vals.append(mx)
        lg = jnp.where(sel, -jnp.inf, lg)
    exps = [jnp.exp(v - vals[0]) for v in vals]
    den = exps[0] + exps[1] + exps[2] + exps[3]
    onehot = jnp.zeros(lg.shape, F32)
    for sel in sels:
        onehot = onehot + sel.astype(F32)
    row = lax.broadcasted_iota(jnp.int32, (ROUTE_TM, ROUTE_TM), 0)
    col = lax.broadcasted_iota(jnp.int32, (ROUTE_TM, ROUTE_TM), 1)
    tri = jnp.where(col < row, 1.0, 0.0).astype(BF16)
    rank = jnp.dot(tri, onehot.astype(BF16), preferred_element_type=F32)
    cnt = jnp.sum(onehot, axis=0, keepdims=True)
    erow = lax.broadcasted_iota(jnp.int32, (N_EXPERTS, N_EXPERTS), 0)
    ecol = lax.broadcasted_iota(jnp.int32, (N_EXPERTS, N_EXPERTS), 1)
    upper = jnp.where(erow < ecol, 1.0, 0.0).astype(BF16)
    off = jnp.dot(jnp.broadcast_to(cnt, (SUBLANES, N_EXPERTS)).astype(BF16), upper,
                  preferred_element_type=F32)[0:1]
    slot = rank + off
    k_lane = lax.broadcasted_iota(jnp.int32, (ROUTE_TM, TOP_K), 1)
    w_out = jnp.zeros((ROUTE_TM, TOP_K), F32)
    q_out = jnp.zeros((ROUTE_TM, TOP_K), F32)
    for k in range(TOP_K):
        w_out = jnp.where(k_lane == k, exps[k] / den, w_out)
        qk = jnp.sum(jnp.where(sels[k], slot, 0.0), axis=-1, keepdims=True)
        q_out = jnp.where(k_lane == k, qk, q_out)
    w_ref[...] = w_out
    q_ref[...] = q_out.astype(jnp.int32)
    cnt_ref[...] = cnt.astype(jnp.int32)
    off_ref[...] = off.astype(jnp.int32)


def _route(logits):
    def tok(w):
        return pl.BlockSpec((ROUTE_TM, w), lambda i: (i, 0))

    tile_row = pl.BlockSpec((None, 1, N_EXPERTS), lambda i: (i, 0, 0))
    return pl.pallas_call(
        _route_kernel,
        grid=(N_TILES,),
        in_specs=[tok(N_EXPERTS)],
        out_specs=[tok(TOP_K), tok(TOP_K), tile_row, tile_row],
        out_shape=[jax.ShapeDtypeStruct((N_TOK, TOP_K), F32),
                   jax.ShapeDtypeStruct((N_TOK, TOP_K), jnp.int32),
                   jax.ShapeDtypeStruct((N_TILES, 1, N_EXPERTS), jnp.int32),
                   jax.ShapeDtypeStruct((N_TILES, 1, N_EXPERTS), jnp.int32)],
        compiler_params=_params(1),
        name="route",
    )(logits)


SLAB_CHUNK = 16


def _slab_pieces(tile, cnt_ref, off_ref, row_ref, fn):
    def per_expert(e, carry):
        n = cnt_ref[tile * N_EXPERTS + e]
        src = off_ref[tile * N_EXPERTS + e]
        dst = row_ref[tile * N_EXPERTS + e]
        n_full = n // SLAB_CHUNK

        def full(j, c):
            fn(src + j * SLAB_CHUNK, dst + j * SLAB_CHUNK, SLAB_CHUNK)
            return c

        lax.fori_loop(0, n_full, full, 0)
        rem = n - n_full * SLAB_CHUNK
        bit = SLAB_CHUNK // 2
        while bit >= 1:
            start = n_full * SLAB_CHUNK + (rem & ~(2 * bit - 1))

            @pl.when((rem & bit) != 0)
            def _(start=start, bit=bit):
                fn(src + start, dst + start, bit)

            bit //= 2
        return carry

    lax.fori_loop(0, N_EXPERTS, per_expert, 0)


def _rows(ref, row, n_rows):
    start = row * ROW_TILES
    if not isinstance(row, int):
        start = pl.multiple_of(start, ROW_TILES)
    return ref.at[pl.ds(start, n_rows * ROW_TILES)]


def _onehot_rows(q, values=None):
    lane = lax.broadcasted_iota(jnp.int32, (ROUTE_TM, TILE_ROWS), 1)
    s = jnp.zeros((ROUTE_TM, TILE_ROWS), F32)
    for k in range(TOP_K):
        v = 1.0 if values is None else values[:, k:k + 1]
        s = jnp.where(lane == q[:, k:k + 1], v, s)
    return s


def _wait_tile_rows(hbm_ref, buf_slot_ref, sem_slot):
    pltpu.make_async_copy(_rows(hbm_ref, 0, TILE_ROWS), buf_slot_ref, sem_slot).wait()


def _dispatch_kernel(cnt_ref, off_ref, row_ref, q_ref, h2_ref, xs_ref, buf_ref, sem):
    tile = pl.program_id(0)
    slot = tile % 2
    buf = buf_ref.at[slot]

    @pl.when(tile >= 2)
    def _():
        _wait_tile_rows(xs_ref, buf, sem.at[slot])

    sel = _onehot_rows(q_ref[...]).astype(BF16)
    xg = lax.dot_general(sel, h2_ref[...], (((0,), (0,)), ((), ())), preferred_element_type=F32)
    for c in range(ROW_TILES):
        buf[pl.ds(c, TILE_ROWS, stride=ROW_TILES), :] = xg[:, c * LANES:(c + 1) * LANES]

    def start(local_row, global_row, n_rows):
        pltpu.make_async_copy(_rows(buf, local_row, n_rows), _rows(xs_ref, global_row, n_rows),
                              sem.at[slot]).start()

    _slab_pieces(tile, cnt_ref, off_ref, row_ref, start)

    @pl.when(tile == N_TILES - 1)
    def _():
        _wait_tile_rows(xs_ref, buf, sem.at[slot])
        _wait_tile_rows(xs_ref, buf_ref.at[1 - slot], sem.at[1 - slot])


def _dispatch(cnt, off, rowstart, q, h2):
    grid_spec = pltpu.PrefetchScalarGridSpec(
        num_scalar_prefetch=3,
        grid=(N_TILES,),
        in_specs=[pl.BlockSpec((ROUTE_TM, TOP_K), lambda i, *_: (i, 0)),
                  pl.BlockSpec((ROUTE_TM, D_MODEL), lambda i, *_: (i, 0))],
        out_specs=pl.BlockSpec(memory_space=pl.ANY),
        scratch_shapes=[pltpu.VMEM((2, TILE_ROWS * ROW_TILES, LANES), F32), pltpu.SemaphoreType.DMA((2,))],
    )
    return pl.pallas_call(
        _dispatch_kernel,
        grid_spec=grid_spec,
        out_shape=jax.ShapeDtypeStruct((N_ROWS * ROW_TILES, LANES), F32),
        compiler_params=_params(1),
        name="dispatch",
    )(cnt, off, rowstart, q, h2)


CAST_ROWS = 128


def _expert_kernel(blk_exp_ref, nvalid_ref, first_ref, head_ref, slot_ref, next_ref,
                   xs_ref, w1_hbm, b1_ref, w2_hbm, b2_ref, y_ref,
                   w1f_ref, w2f_ref, w1b_ref, w2b_ref, sem, *, layer):
    b = pl.program_id(0)
    e = blk_exp_ref[b]
    nvalid = nvalid_ref[b]
    slot = slot_ref[b]

    def weight_copies(expert, s):
        return (pltpu.make_async_copy(w1_hbm.at[layer, expert], w1f_ref.at[s], sem.at[0, s]),
                pltpu.make_async_copy(w2_hbm.at[layer, expert], w2f_ref.at[s], sem.at[1, s]))

    @pl.when(first_ref[b] == 1)
    def _():
        @pl.when(head_ref[b] == 1)
        def _():
            for cp in weight_copies(e, slot):
                cp.start()

        for cp in weight_copies(e, slot):
            cp.wait()

        @pl.when(next_ref[b] >= 0)
        def _():
            for cp in weight_copies(next_ref[b], 1 - slot):
                cp.start()

        def cast1(i, carry):
            r = pl.multiple_of(i * CAST_ROWS, CAST_ROWS)
            w1b_ref[pl.ds(r, CAST_ROWS), :] = w1f_ref[slot, pl.ds(r, CAST_ROWS), :].astype(BF16)
            return carry

        def cast2(i, carry):
            r = pl.multiple_of(i * CAST_ROWS, CAST_ROWS)
            w2b_ref[pl.ds(r, CAST_ROWS), :] = w2f_ref[slot, pl.ds(r, CAST_ROWS), :].astype(BF16)
            return carry

        lax.fori_loop(0, D_MODEL // CAST_ROWS, cast1, 0)
        lax.fori_loop(0, D_FF // CAST_ROWS, cast2, 0)

    @pl.when(nvalid > 0)
    def _():
        valid = lax.broadcasted_iota(jnp.int32, (EXP_BLOCK, LANES), 0) < nvalid
        chunks = [jnp.where(valid, xs_ref[pl.ds(c, EXP_BLOCK, stride=ROW_TILES), :], 0.0).astype(BF16)
                  for c in range(ROW_TILES)]
        xb = jnp.concatenate(chunks, axis=-1)
        hid = jnp.dot(xb, w1b_ref[...], preferred_element_type=F32) + b1_ref[...]
        glu = jnp.minimum(hid[:, :D_FF], SWIGLU_LIMIT)
        lin = jnp.clip(hid[:, D_FF:], -SWIGLU_LIMIT, SWIGLU_LIMIT)
        act = glu * _sigmoid(SWIGLU_ALPHA * glu) * (lin + 1.0)
        y = jnp.dot(act.astype(BF16), w2b_ref[...], preferred_element_type=F32) + b2_ref[...]
        for c in range(ROW_TILES):
            y_ref[pl.ds(c, EXP_BLOCK, stride=ROW_TILES), :] = y[:, c * LANES:(c + 1) * LANES]

    @pl.when(nvalid == 0)
    def _():
        y_ref[...] = jnp.zeros_like(y_ref)


def _experts(tables, xs, w1, b1, w2, b2, layer):
    def blk(b, *_):
        return (b, 0)

    def bias(b, be, *_):
        return (layer, be[b], 0, 0)

    grid_spec = pltpu.PrefetchScalarGridSpec(
        num_scalar_prefetch=6,
        grid=(N_BLOCKS,),
        in_specs=[
            pl.BlockSpec((EXP_BLOCK * ROW_TILES, LANES), blk),
            pl.BlockSpec(memory_space=pl.ANY),
            pl.BlockSpec((None, None, 1, 2 * D_FF), bias),
            pl.BlockSpec(memory_space=pl.ANY),
            pl.BlockSpec((None, None, 1, D_MODEL), bias),
        ],
        out_specs=pl.BlockSpec((EXP_BLOCK * ROW_TILES, LANES), blk),
        scratch_shapes=[pltpu.VMEM((2, D_MODEL, 2 * D_FF), F32), pltpu.VMEM((2, D_FF, D_MODEL), F32),
                        pltpu.VMEM((D_MODEL, 2 * D_FF), BF16), pltpu.VMEM((D_FF, D_MODEL), BF16),
                        pltpu.SemaphoreType.DMA((2, 2))],
    )
    return pl.pallas_call(
        functools.partial(_expert_kernel, layer=layer),
        grid_spec=grid_spec,
        out_shape=jax.ShapeDtypeStruct((N_ROWS * ROW_TILES, LANES), F32),
        compiler_params=_params(1),
        name="experts",
    )(*tables, xs, w1, b1.reshape(DEPTH, N_EXPERTS, 1, 2 * D_FF), w2,
      b2.reshape(DEPTH, N_EXPERTS, 1, D_MODEL))


def _split_bf16(x):
    hi = x.astype(BF16)
    return hi, (x - hi.astype(F32)).astype(BF16)


P_TILES = N_P // ROUTE_TM


def _combine_kernel(cnt_ref, off_ref, row_ref, q_ref, w_ref, y_ref, xm_ref, mod_ref, fg_ref,
                    *refs, final):
    buf_ref, sem = refs[-2:]
    tile = pl.program_id(0)
    slot = tile % 2
    buf = buf_ref.at[slot]

    def fetch(t, s):
        def start(local_row, global_row, n_rows):
            pltpu.make_async_copy(_rows(y_ref, global_row, n_rows), _rows(buf_ref.at[s], local_row, n_rows),
                                  sem.at[s]).start()

        _slab_pieces(t, cnt_ref, off_ref, row_ref, start)

    @pl.when(tile == 0)
    def _():
        fetch(tile, slot)

    @pl.when(tile + 1 < N_TILES)
    def _():
        fetch(tile + 1, 1 - slot)

    s_hi, s_lo = _split_bf16(_onehot_rows(q_ref[...], w_ref[...]))
    _wait_tile_rows(y_ref, buf, sem.at[slot])
    rows = jnp.concatenate([buf[pl.ds(c, TILE_ROWS, stride=ROW_TILES), :] for c in range(ROW_TILES)],
                           axis=-1)
    r_hi, r_lo = _split_bf16(rows)
    moe = jnp.dot(s_hi, r_hi, preferred_element_type=F32)
    moe = moe + jnp.dot(s_lo, r_hi, preferred_element_type=F32)
    moe = moe + jnp.dot(s_hi, r_lo, preferred_element_type=F32)
    m = mod_ref[...]
    x = xm_ref[...] + m[5:6] * moe
    if not final:
        refs[0][...] = x
        return
    xn = x * lax.rsqrt(jnp.mean(x * x, axis=-1, keepdims=True) + EPS) * fg_ref[...]
    yp_ref, ys_ref = refs[:2]

    @pl.when(tile < P_TILES)
    def _():
        yp_ref[...] = xn

    @pl.when(tile >= P_TILES)
    def _():
        ys_ref[...] = xn


def _combine(cnt, off, rowstart, q, topw, y, xm, mods_l, final_g, final):
    def tok(w):
        return pl.BlockSpec((ROUTE_TM, w), lambda i, *_: (i, 0))

    if final:
        out_specs = [pl.BlockSpec((ROUTE_TM, D_MODEL), lambda i, *_: (jnp.minimum(i, P_TILES - 1), 0)),
                     pl.BlockSpec((ROUTE_TM, D_MODEL), lambda i, *_: (jnp.maximum(i - P_TILES, 0), 0))]
        out_shape = [jax.ShapeDtypeStruct((N_P, D_MODEL), F32), jax.ShapeDtypeStruct((N_S, D_MODEL), F32)]
    else:
        out_specs = [tok(D_MODEL)]
        out_shape = [jax.ShapeDtypeStruct((N_TOK, D_MODEL), F32)]
    grid_spec = pltpu.PrefetchScalarGridSpec(
        num_scalar_prefetch=3,
        grid=(N_TILES,),
        in_specs=[tok(TOP_K), tok(TOP_K),
                  pl.BlockSpec(memory_space=pl.ANY),
                  tok(D_MODEL),
                  pl.BlockSpec((None, N_MOD, D_MODEL), lambda i, *_: (_mod_row(i * ROUTE_TM), 0, 0)),
                  pl.BlockSpec((1, D_MODEL), lambda i, *_: (0, 0))],
        out_specs=out_specs,
        scratch_shapes=[pltpu.VMEM((2, TILE_ROWS * ROW_TILES, LANES), F32), pltpu.SemaphoreType.DMA((2,))],
    )
    return pl.pallas_call(
        functools.partial(_combine_kernel, final=final),
        grid_spec=grid_spec,
        out_shape=out_shape,
        compiler_params=_params(1),
        name="combine_final" if final else "combine",
    )(cnt, off, rowstart, q, topw, y, xm, mods_l, final_g)


def _rope_tables():
    t = np.arange(DEC_SEQ)
    row = (t // GRID_W).astype(np.float32)
    col = (t % GRID_W).astype(np.float32)
    inv = jnp.asarray(ROPE_THETA, F32) ** (-jnp.arange(ROPE_PAIRS, dtype=F32) / ROPE_PAIRS)
    ang = jnp.concatenate([jnp.asarray(row)[:, None] * inv, jnp.asarray(col)[:, None] * inv], axis=-1)
    cos = jnp.repeat(jnp.cos(ang), 2, axis=-1)
    sin = jnp.repeat(jnp.sin(ang), 2, axis=-1)
    sign = jnp.asarray(np.tile(np.array([-1.0, 1.0], np.float32), HEAD_DIM // 2))
    return jnp.tile(cos, (1, A_HEADS)), jnp.tile(sin * sign, (1, A_HEADS))


def _routing_tables(tile_cnt):
    i32 = jnp.int32
    carry = jnp.cumsum(tile_cnt, axis=0) - tile_cnt
    counts = jnp.sum(tile_cnt, axis=0)
    padded = (counts + EXP_BLOCK - 1) // EXP_BLOCK * EXP_BLOCK
    pad_end = jnp.cumsum(padded)
    pad_start = pad_end - padded
    rowstart = (pad_start[None, :] + carry).astype(i32)
    blk_row = jnp.arange(N_BLOCKS, dtype=i32) * EXP_BLOCK
    blk_exp = jnp.sum((blk_row[:, None] >= pad_end[None, :]).astype(i32), axis=1)
    blk_exp = jnp.minimum(blk_exp, N_EXPERTS - 1)
    nvalid = jnp.clip(counts[blk_exp] - (blk_row - pad_start[blk_exp]), 0, EXP_BLOCK).astype(i32)
    first = jnp.logical_and(blk_row == pad_start[blk_exp], nvalid > 0)
    active = counts > 0
    act_rank = jnp.cumsum(active.astype(i32)) - 1
    eid = jnp.arange(N_EXPERTS, dtype=i32)
    later = jnp.logical_and(active[None, :], eid[None, :] > eid[:, None])
    nxt = jnp.min(jnp.where(later, eid[None, :], N_EXPERTS), axis=1)
    nxt = jnp.where(nxt == N_EXPERTS, -1, nxt)
    head = jnp.logical_and(first, act_rank[blk_exp] == 0)
    tables = (blk_exp, nvalid, first.astype(i32), head.astype(i32), (act_rank[blk_exp] % 2).astype(i32),
              nxt[blk_exp].astype(i32))
    return rowstart.reshape(-1), tables


def kernel(x_prompt, x_sample, cache_attn_k, cache_attn_v, cache_na_k, cache_na_v, c, c_ctx, w_mod, b_mod, norm1_g, norm2_g, w_in, b_gate, q_norm_g, k_norm_g, na_rpb, conv_w, conv_b, conv_ln_g, conv_ln_b, sgu_ln_g, sgu_ln_b, sgu_w, sgu_b, w_branch, w_out, router_w, router_b, exp_w1, exp_b1, exp_w2, exp_b2, final_g):
    x = jnp.concatenate([x_prompt.reshape(N_P, D_MODEL), x_sample.reshape(N_S, D_MODEL)], axis=0)
    cvec = jnp.zeros((SUBLANES, D_MODEL), F32).at[0].set(c_ctx).at[1:1 + DEC_BATCH].set(c)
    mods = _modulation(cvec, w_mod, b_mod).reshape(DEPTH, SUBLANES, N_MOD, D_MODEL)
    cos_t, sin_t = _rope_tables()
    cak = cache_attn_k.reshape(DEC_BATCH, DEPTH, PAST_LEN, A_KV)
    cav = cache_attn_v.reshape(DEC_BATCH, DEPTH, PAST_LEN, A_KV)
    cbk = cache_na_k.reshape(DEC_BATCH, DEPTH, PAST_LEN, B_W)
    cbv = cache_na_v.reshape(DEC_BATCH, DEPTH, PAST_LEN, B_W)
    w_in_b = w_in.astype(BF16)
    w_br = w_branch.astype(BF16)
    w_o = w_out.astype(BF16)
    rw_hi = router_w.astype(BF16)
    rw_lo = (router_w - rw_hi.astype(F32)).astype(BF16)
    final_g2 = final_g.reshape(1, D_MODEL)

    caches = ()
    outs = None
    for l in range(DEPTH):
        mods_l = mods[l]
        g1 = norm1_g[l].reshape(1, D_MODEL)
        gq = jnp.tile(q_norm_g[l], A_HEADS).reshape(1, A_Q)
        gk = jnp.tile(k_norm_g[l], A_KV_HEADS).reshape(1, A_KV)
        aq, ak, av, bq, bk, bv, cz, dz = _in_proj(x, mods_l, g1, w_in_b, l)
        ya, yb, *caches = _prompt_attn(aq, ak, av, bq, bk, bv, gq, gk, caches, l)
        ya = _sample_attn(aq, ak, av, cak, cav, cos_t, sin_t, gq, gk, ya, l)
        yb = _na_attn(bq, bk, bv, cbk, cbv, _na_bias(na_rpb[l]), yb, l)
        cw = conv_w[l]
        cb = conv_b[l].reshape(1, C_WIDTH)
        cg = conv_ln_g[l].reshape(1, C_WIDTH)
        cbb = conv_ln_b[l].reshape(1, C_WIDTH)
        yc = _conv_call(cz, cw, cb, cg, cbb, SEQ, 0, BATCH)
        yc = _conv_call(cz, cw, cb, cg, cbb, DEC_SEQ, N_P // DEC_SEQ, DEC_BATCH, partial_out=yc)
        bs_full = jnp.repeat(sgu_b[l].T, SGU_GW, axis=1)
        yd = _sgu(dz, sgu_ln_g[l].reshape(1, SGU_WIDTH), sgu_ln_b[l].reshape(1, SGU_WIDTH),
                  sgu_w[l].astype(BF16), bs_full)
        xm, h2, logits = _merge(x, ya, yb, yc, yd, mods_l, g1, w_in_b,
                                b_gate[l].reshape(1, N_BRANCH * D_MODEL), w_br[l], w_o[l],
                                norm2_g[l].reshape(1, D_MODEL), rw_hi[l], rw_lo[l],
                                router_b[l].reshape(1, N_EXPERTS), l)
        top_w, q, tile_cnt, tile_off = _route(logits)
        rowstart, tables = _routing_tables(tile_cnt.reshape(N_TILES, N_EXPERTS))
        tile_cnt = tile_cnt.reshape(-1)
        tile_off = tile_off.reshape(-1)
        xs = _dispatch(tile_cnt, tile_off, rowstart, q, h2)
        y = _experts(tables, xs, exp_w1, exp_b1, exp_w2, exp_b2, l)
        outs = _combine(tile_cnt, tile_off, rowstart, q, top_w, y, xm, mods_l, final_g2, l == DEPTH - 1)
        x = outs[0]

    new_k, new_v, new_bk, new_bv = caches
    return (outs[0].reshape(BATCH, SEQ, D_MODEL), outs[1].reshape(DEC_BATCH, DEC_SEQ, D_MODEL),
            new_k.reshape(BATCH, DEPTH, SEQ, A_KV_HEADS, HEAD_DIM),
            new_v.reshape(BATCH, DEPTH, SEQ, A_KV_HEADS, HEAD_DIM),
            new_bk.reshape(BATCH, DEPTH, SEQ, B_HEADS, HEAD_DIM),
            new_bv.reshape(BATCH, DEPTH, SEQ, B_HEADS, HEAD_DIM))
```

```python
import functools

import numpy as np
import jax
import jax.numpy as jnp
from jax import lax
from jax.experimental import pallas as pl
from jax.experimental.pallas import tpu as pltpu

D_MODEL = 1024
BATCH = 32
SEQ = 256
DEPTH = 2
DEC_BATCH = 2
DEC_SEQ = 1024
PAST_LEN = 512
GRID_W = 64
HEAD_DIM = 64
A_HEADS = 4
A_KV_HEADS = 2
B_HEADS = 4
NA_ROWS = 8
NA_COLS = 16
C_WIDTH = 256
CONV_WIDTH = 31
SGU_WIDTH = 256
SGU_GROUPS = 4
SGU_CHUNK = 128
N_BRANCH = 4
BRANCH_W = 256
N_EXPERTS = 32
TOP_K = 4
D_FF = 1024
SWIGLU_ALPHA = 1.702
SWIGLU_LIMIT = 7.0
MOE_BLOCK = 128
ROPE_THETA = 10000.0
ROPE_PAIRS = HEAD_DIM // 4
N_MOD = 6
EPS = 1e-6
NEG_INF = -1e30

A_Q = A_HEADS * HEAD_DIM
A_KV = A_KV_HEADS * HEAD_DIM
B_W = B_HEADS * HEAD_DIM
MIX_SIZES = (A_Q, A_KV, A_KV, B_W, B_W, B_W, 2 * C_WIDTH, 2 * SGU_WIDTH)
MIX_COLS = sum(MIX_SIZES)

N_P = BATCH * SEQ
N_S = DEC_BATCH * DEC_SEQ
N_TOK = N_P + N_S
N_ASSIGN = N_TOK * TOP_K
EXP_BLOCK = 256
N_BLOCKS = N_ASSIGN // EXP_BLOCK + N_EXPERTS
N_ROWS = N_BLOCKS * EXP_BLOCK
GRID_ROWS = DEC_SEQ // GRID_W
NA_WR = min(NA_ROWS, GRID_ROWS)
N_LOC = NA_WR * GRID_W

SUBLANES = 8
LANES = 128
ROW_TILES = D_MODEL // LANES
VMEM_LIMIT = 56 * 1024 * 1024

F32 = jnp.float32
BF16 = jnp.bfloat16


def _params(n_axes, vmem=None):
    return pltpu.CompilerParams(
        dimension_semantics=("arbitrary",) * n_axes,
        vmem_limit_bytes=vmem if vmem is not None else VMEM_LIMIT)


def _mod_row(start):
    return jnp.where(start < N_P, 0, 1 + (start - N_P) // DEC_SEQ)


def _bdot(a, b):
    return jnp.dot(a.astype(BF16), b.astype(BF16), preferred_element_type=F32)


def _bdot_nt(a, b):
    return lax.dot_general(a.astype(BF16), b.astype(BF16), (((1,), (1,)), ((), ())),
                           preferred_element_type=F32)


def _sigmoid(x):
    return 1.0 / (1.0 + jnp.exp(-x))


MOD_TN = 1536


def _mod_kernel(c_ref, w_ref, b_ref, o_ref):
    c = c_ref[...]
    s = c * _sigmoid(c)
    o_ref[...] = _bdot(s, w_ref[...]) + b_ref[...]


def _modulation(cvec, w_mod, b_mod):
    n_col = N_MOD * D_MODEL
    return pl.pallas_call(
        _mod_kernel,
        grid=(DEPTH, n_col // MOD_TN),
        in_specs=[
            pl.BlockSpec((SUBLANES, D_MODEL), lambda l, j: (0, 0)),
            pl.BlockSpec((None, D_MODEL, MOD_TN), lambda l, j: (l, 0, j)),
            pl.BlockSpec((None, 1, MOD_TN), lambda l, j: (l, 0, j)),
        ],
        out_specs=pl.BlockSpec((None, SUBLANES, MOD_TN), lambda l, j: (l, 0, j)),
        out_shape=jax.ShapeDtypeStruct((DEPTH, SUBLANES, n_col), F32),
        compiler_params=_params(2),
        name="modulation",
    )(cvec, w_mod, b_mod.reshape(DEPTH, 1, n_col))


IN_TM = 512


def _norm_mod(x, g, shift, scale):
    y = x * lax.rsqrt(jnp.mean(x * x, axis=-1, keepdims=True) + EPS) * g
    return y * (1.0 + scale) + shift


def _in_kernel(x_ref, mod_ref, g_ref, w_ref, *out_refs):
    m = mod_ref[...]
    h = _norm_mod(x_ref[...], g_ref[...], m[0:1], m[1:2])
    z = jnp.dot(h.astype(BF16), w_ref[...], preferred_element_type=F32)
    off = 0
    for o_ref, sz in zip(out_refs, MIX_SIZES):
        o_ref[...] = z[:, off:off + sz]
        off += sz


def _in_proj(x, mods_l, g1, w_in_b, layer):
    return pl.pallas_call(
        _in_kernel,
        grid=(N_TOK // IN_TM,),
        in_specs=[
            pl.BlockSpec((IN_TM, D_MODEL), lambda i: (i, 0)),
            pl.BlockSpec((None, N_MOD, D_MODEL), lambda i: (_mod_row(i * IN_TM), 0, 0)),
            pl.BlockSpec((1, D_MODEL), lambda i: (0, 0)),
            pl.BlockSpec((None, D_MODEL, MIX_COLS), lambda i: (layer, 0, 0)),
        ],
        out_specs=[pl.BlockSpec((IN_TM, sz), lambda i: (i, 0)) for sz in MIX_SIZES],
        out_shape=[jax.ShapeDtypeStruct((N_TOK, sz), F32) for sz in MIX_SIZES],
        compiler_params=_params(1),
        name="in_proj",
    )(x, mods_l, g1, w_in_b)


def _head_rms(x, g):
    n_heads = x.shape[-1] // HEAD_DIM
    seg = lax.broadcasted_iota(jnp.int32, x.shape, 1) // HEAD_DIM
    xx = x * x
    inv = jnp.zeros_like(x)
    for h in range(n_heads):
        ms = jnp.sum(jnp.where(seg == h, xx, 0.0), axis=-1, keepdims=True) * (1.0 / HEAD_DIM)
        inv = jnp.where(seg == h, lax.rsqrt(ms + EPS), inv)
    return x * inv * g


def _softmax_pv(score_parts, value_parts):
    m = score_parts[0].max(axis=-1, keepdims=True)
    for s in score_parts[1:]:
        m = jnp.maximum(m, s.max(axis=-1, keepdims=True))
    den = None
    acc = None
    for s, v in zip(score_parts, value_parts):
        e = jnp.exp(s - m)
        d = e.sum(axis=-1, keepdims=True)
        a = _bdot(e, v)
        den = d if den is None else den + d
        acc = a if acc is None else acc + a
    return acc / den


def _head(x, h):
    return x[:, h * HEAD_DIM:(h + 1) * HEAD_DIM]


SCALE = HEAD_DIM ** -0.5


def _prompt_attn_kernel(aq_ref, ak_ref, av_ref, bq_ref, bk_ref, bv_ref, gq_ref, gk_ref, *refs):
    ya_ref, yb_ref, nk_ref, nv_ref, nbk_ref, nbv_ref = refs[-6:]
    aq = _head_rms(aq_ref[...], gq_ref[...])
    ak = _head_rms(ak_ref[...], gk_ref[...])
    av = av_ref[...]
    nk_ref[...] = ak
    nv_ref[...] = av
    nbk_ref[...] = bk_ref[...]
    nbv_ref[...] = bv_ref[...]
    grp = A_HEADS // A_KV_HEADS
    outs = []
    for h in range(A_HEADS):
        s = _bdot_nt(_head(aq, h), _head(ak, h // grp)) * SCALE
        outs.append(_softmax_pv([s], [_head(av, h // grp)]))
    ya_ref[...] = jnp.concatenate(outs, axis=-1)
    bq = bq_ref[...]
    bk = bk_ref[...]
    bv = bv_ref[...]
    outs = []
    for h in range(B_HEADS):
        s = _bdot_nt(_head(bq, h), _head(bk, h)) * SCALE
        outs.append(_softmax_pv([s], [_head(bv, h)]))
    yb_ref[...] = jnp.concatenate(outs, axis=-1)


def _prompt_attn(aq, ak, av, bq, bk, bv, gq, gk, caches, layer):
    def spec(w):
        return pl.BlockSpec((SEQ, w), lambda b: (b, 0))

    def cache_spec(w):
        return pl.BlockSpec((SEQ, w), lambda b: (b * DEPTH + layer, 0))

    cache_widths = (A_KV, A_KV, B_W, B_W)
    n_in = 8
    return pl.pallas_call(
        _prompt_attn_kernel,
        grid=(BATCH,),
        in_specs=[spec(A_Q), spec(A_KV), spec(A_KV), spec(B_W), spec(B_W), spec(B_W),
                  pl.BlockSpec((1, A_Q), lambda b: (0, 0)),
                  pl.BlockSpec((1, A_KV), lambda b: (0, 0))]
        + [pl.BlockSpec(memory_space=pl.ANY) for _ in caches],
        out_specs=[spec(A_Q), spec(B_W)] + [cache_spec(w) for w in cache_widths],
        out_shape=[jax.ShapeDtypeStruct((N_TOK, A_Q), F32),
                   jax.ShapeDtypeStruct((N_TOK, B_W), F32)]
        + [jax.ShapeDtypeStruct((BATCH * DEPTH * SEQ, w), F32) for w in cache_widths],
        input_output_aliases={n_in + j: 2 + j for j in range(len(caches))},
        compiler_params=_params(1),
        name="prompt_attn",
    )(aq, ak, av, bq, bk, bv, gq, gk, *caches)


QB = 128


def _rope(x, cos, sin_signed):
    n = x.shape[-1]
    nxt = pltpu.roll(x, n - 1, 1)
    prv = pltpu.roll(x, 1, 1)
    even = (lax.broadcasted_iota(jnp.int32, x.shape, 1) % 2) == 0
    return x * cos + jnp.where(even, nxt, prv) * sin_signed


def _sample_attn_kernel(q_ref, k_ref, v_ref, ck_ref, cv_ref, cosq_ref, sinq_ref, cosk_ref, sink_ref,
                        gq_ref, gk_ref, ya_prompt_ref, o_ref):
    del ya_prompt_ref
    q = _rope(_head_rms(q_ref[...], gq_ref[...]), cosq_ref[...], sinq_ref[...])
    k = _rope(_head_rms(k_ref[...], gk_ref[...]), cosk_ref[...], sink_ref[...])
    v = v_ref[...]
    ck = ck_ref[...]
    cv = cv_ref[...]
    grp = A_HEADS // A_KV_HEADS
    outs = []
    for h in range(A_HEADS):
        j = h // grp
        qh = _head(q, h)
        s1 = _bdot_nt(qh, _head(k, j)) * SCALE
        s2 = _bdot_nt(qh, _head(ck, j)) * SCALE
        outs.append(_softmax_pv([s1, s2], [_head(v, j), _head(cv, j)]))
    o_ref[...] = jnp.concatenate(outs, axis=-1)


def _sample_attn(aq, ak, av, cache_k, cache_v, cos_t, sin_t, gq, gk, ya, layer):
    nqb = DEC_SEQ // QB
    q0 = N_P // QB
    k0 = N_P // DEC_SEQ
    return pl.pallas_call(
        _sample_attn_kernel,
        grid=(DEC_BATCH, nqb),
        in_specs=[
            pl.BlockSpec((QB, A_Q), lambda b, i: (q0 + b * nqb + i, 0)),
            pl.BlockSpec((DEC_SEQ, A_KV), lambda b, i: (k0 + b, 0)),
            pl.BlockSpec((DEC_SEQ, A_KV), lambda b, i: (k0 + b, 0)),
            pl.BlockSpec((None, None, PAST_LEN, A_KV), lambda b, i: (b, layer, 0, 0)),
            pl.BlockSpec((None, None, PAST_LEN, A_KV), lambda b, i: (b, layer, 0, 0)),
            pl.BlockSpec((QB, A_Q), lambda b, i: (i, 0)),
            pl.BlockSpec((QB, A_Q), lambda b, i: (i, 0)),
            pl.BlockSpec((DEC_SEQ, A_KV), lambda b, i: (0, 0)),
            pl.BlockSpec((DEC_SEQ, A_KV), lambda b, i: (0, 0)),
            pl.BlockSpec((1, A_Q), lambda b, i: (0, 0)),
            pl.BlockSpec((1, A_KV), lambda b, i: (0, 0)),
            pl.BlockSpec(memory_space=pl.ANY),
        ],
        out_specs=pl.BlockSpec((QB, A_Q), lambda b, i: (q0 + b * nqb + i, 0)),
        out_shape=jax.ShapeDtypeStruct((N_TOK, A_Q), F32),
        input_output_aliases={11: 0},
        compiler_params=_params(2),
        name="sample_attn",
    )(aq, ak, av, cache_k, cache_v, cos_t, sin_t, cos_t, sin_t, gq, gk, ya)


N_ROW_OFF = 2 * NA_ROWS - 1
N_COL_OFF = 2 * NA_COLS - 1
NA_PAIRS = N_ROW_OFF - 1
assert NA_WR == NA_ROWS and NA_WR % 2 == 0 and 2 * GRID_W == LANES


def _na_bias_kernel(rpb_ref, o_ref):
    h = pl.program_id(0)
    qc = lax.broadcasted_iota(jnp.int32, (GRID_W, LANES), 0)
    lane = lax.broadcasted_iota(jnp.int32, (GRID_W, LANES), 1)
    right = lane >= GRID_W
    kc = jnp.where(right, lane - GRID_W, lane)
    c_start = jnp.clip(qc - NA_COLS // 2, 0, GRID_W - NA_COLS)
    col_in = jnp.logical_and(kc >= c_start, kc < c_start + NA_COLS)
    col_off = jnp.clip(kc - qc + NA_COLS - 1, 0, N_COL_OFF - 1)
    for p in range(NA_PAIRS):
        acc = jnp.zeros((GRID_W, LANES), F32)
        for o in range(N_COL_OFF):
            left_v = rpb_ref[(h * N_ROW_OFF + p) * N_COL_OFF + o]
            right_v = rpb_ref[(h * N_ROW_OFF + p + 1) * N_COL_OFF + o]
            acc = jnp.where(col_off == o, jnp.where(right, right_v, left_v), acc)
        o_ref[p] = jnp.where(col_in, acc, NEG_INF)


def _na_bias(rpb):
    return pl.pallas_call(
        _na_bias_kernel,
        grid_spec=pltpu.PrefetchScalarGridSpec(
            num_scalar_prefetch=1,
            grid=(B_HEADS,),
            in_specs=[],
            out_specs=pl.BlockSpec((None, NA_PAIRS, GRID_W, LANES), lambda h, *_: (h, 0, 0, 0)),
        ),
        out_shape=jax.ShapeDtypeStruct((B_HEADS, NA_PAIRS, GRID_W, LANES), F32),
        compiler_params=_params(1),
        name="na_bias",
    )(rpb.reshape(-1))


def _na_kernel(q_ref, k_ref, v_ref, ck_ref, cv_ref, bias_ref, yb_prompt_ref, o_ref):
    del yb_prompt_ref
    r = pl.program_id(1)
    r_start = jnp.clip(r - NA_WR // 2, 0, GRID_ROWS - NA_WR)
    base = pl.multiple_of(r_start * GRID_W, GRID_W)
    row_off0 = r_start - r + NA_ROWS - 1
    q = q_ref[...]
    kb = k_ref[pl.ds(base, N_LOC), :]
    vb = v_ref[pl.ds(base, N_LOC), :]
    ck = ck_ref[...]
    cv = cv_ref[...]
    outs = []
    for h in range(B_HEADS):
        qh = _head(q, h)
        bias = jnp.concatenate([bias_ref[h, row_off0 + 2 * j] for j in range(NA_WR // 2)], axis=-1)
        s1 = _bdot_nt(qh, _head(kb, h)) * SCALE + bias
        s2 = _bdot_nt(qh, _head(ck, h)) * SCALE
        outs.append(_softmax_pv([s1, s2], [_head(vb, h), _head(cv, h)]))
    o_ref[...] = jnp.concatenate(outs, axis=-1)


def _na_attn(bq, bk, bv, cache_k, cache_v, bias, yb, layer):
    q0 = N_P // GRID_W
    k0 = N_P // DEC_SEQ
    return pl.pallas_call(
        _na_kernel,
        grid=(DEC_BATCH, GRID_ROWS),
        in_specs=[
            pl.BlockSpec((GRID_W, B_W), lambda b, r: (q0 + b * GRID_ROWS + r, 0)),
            pl.BlockSpec((DEC_SEQ, B_W), lambda b, r: (k0 + b, 0)),
            pl.BlockSpec((DEC_SEQ, B_W), lambda b, r: (k0 + b, 0)),
            pl.BlockSpec((None, None, PAST_LEN, B_W), lambda b, r: (b, layer, 0, 0)),
            pl.BlockSpec((None, None, PAST_LEN, B_W), lambda b, r: (b, layer, 0, 0)),
            pl.BlockSpec((B_HEADS, NA_PAIRS, GRID_W, LANES), lambda b, r: (0, 0, 0, 0)),
            pl.BlockSpec(memory_space=pl.ANY),
        ],
        out_specs=pl.BlockSpec((GRID_W, B_W), lambda b, r: (q0 + b * GRID_ROWS + r, 0)),
        out_shape=jax.ShapeDtypeStruct((N_TOK, B_W), F32),
        input_output_aliases={6: 0},
        compiler_params=_params(2),
        name="na_attn",
    )(bq, bk, bv, cache_k, cache_v, bias, yb)


CONV_PAD = 16
CONV_CHUNK = 64


def _layer_norm(x, g, b):
    mu = jnp.mean(x, axis=-1, keepdims=True)
    xc = x - mu
    var = jnp.mean(xc * xc, axis=-1, keepdims=True)
    return xc * lax.rsqrt(var + EPS) * g + b


def _conv_kernel(z_ref, w_ref, cb_ref, g_ref, b_ref, *refs, s_len):
    o_ref, pad_ref = refs[-2:]
    z = z_ref[...]
    u = z[:, :C_WIDTH] * _sigmoid(z[:, C_WIDTH:])
    pad_ref[pl.ds(0, CONV_PAD), :] = jnp.zeros((CONV_PAD, C_WIDTH), F32)
    pad_ref[pl.ds(CONV_PAD + s_len, CONV_PAD), :] = jnp.zeros((CONV_PAD, C_WIDTH), F32)
    pad_ref[pl.ds(CONV_PAD, s_len), :] = u
    w = w_ref[...]
    shift = CONV_PAD - CONV_WIDTH // 2

    def chunk(c, carry):
        base = pl.multiple_of(c * CONV_CHUNK, CONV_CHUNK)
        acc = jnp.zeros((CONV_CHUNK, C_WIDTH), F32)
        for r in range(SUBLANES):
            part = None
            for k in range(CONV_WIDTH):
                if (k + shift) % SUBLANES != r:
                    continue
                rows = pad_ref[pl.ds(base + (k + shift - r), CONV_CHUNK + SUBLANES), :]
                term = rows * w[k:k + 1]
                part = term if part is None else part + term
            if part is not None:
                acc = acc + part[r:r + CONV_CHUNK]
        y = _layer_norm(acc + cb_ref[...], g_ref[...], b_ref[...])
        o_ref[pl.ds(base, CONV_CHUNK), :] = y * _sigmoid(y)
        return carry

    lax.fori_loop(0, s_len // CONV_CHUNK, chunk, 0)


def _conv_call(cz, w, cb, g, b, s_len, first_blk, n_seq, partial_out=None):
    vec = pl.BlockSpec((1, C_WIDTH), lambda i: (0, 0))
    extra = [] if partial_out is None else [partial_out]
    return pl.pallas_call(
        functools.partial(_conv_kernel, s_len=s_len),
        grid=(n_seq,),
        in_specs=[pl.BlockSpec((s_len, 2 * C_WIDTH), lambda i: (first_blk + i, 0)),
                  pl.BlockSpec((CONV_WIDTH, C_WIDTH), lambda i: (0, 0)), vec, vec, vec]
        + [pl.BlockSpec(memory_space=pl.ANY) for _ in extra],
        out_specs=pl.BlockSpec((s_len, C_WIDTH), lambda i: (first_blk + i, 0)),
        out_shape=jax.ShapeDtypeStruct((N_TOK, C_WIDTH), F32),
        input_output_aliases={5: 0} if extra else {},
        scratch_shapes=[pltpu.VMEM((s_len + 2 * CONV_PAD, C_WIDTH), F32)],
        compiler_params=_params(1),
        name="conformer_conv_%d" % s_len,
    )(cz, w, cb, g, b, *extra)


SGU_TM = 512
SGU_GW = SGU_WIDTH // SGU_GROUPS


def _sgu_kernel(z_ref, g_ref, b_ref, ws_ref, bs_ref, o_ref):
    z = z_ref[...]
    z = 0.5 * z * (1.0 + lax.erf(z * (2.0 ** -0.5)))
    u = z[:, :SGU_WIDTH]
    v = _layer_norm(z[:, SGU_WIDTH:], g_ref[...], b_ref[...])
    for c in range(SGU_TM // SGU_CHUNK):
        vc = v[c * SGU_CHUNK:(c + 1) * SGU_CHUNK]
        parts = [_bdot(ws_ref[g], vc[:, g * SGU_GW:(g + 1) * SGU_GW]) for g in range(SGU_GROUPS)]
        mixed = jnp.concatenate(parts, axis=-1) + bs_ref[...]
        o_ref[pl.ds(c * SGU_CHUNK, SGU_CHUNK), :] = u[c * SGU_CHUNK:(c + 1) * SGU_CHUNK] * mixed


def _sgu(dz, g, b, ws, bs_full):
    vec = pl.BlockSpec((1, SGU_WIDTH), lambda i: (0, 0))
    return pl.pallas_call(
        _sgu_kernel,
        grid=(N_TOK // SGU_TM,),
        in_specs=[pl.BlockSpec((SGU_TM, 2 * SGU_WIDTH), lambda i: (i, 0)), vec, vec,
                  pl.BlockSpec((SGU_GROUPS, SGU_CHUNK, SGU_CHUNK), lambda i: (0, 0, 0)),
                  pl.BlockSpec((SGU_CHUNK, SGU_WIDTH), lambda i: (0, 0))],
        out_specs=pl.BlockSpec((SGU_TM, SGU_WIDTH), lambda i: (i, 0)),
        out_shape=jax.ShapeDtypeStruct((N_TOK, SGU_WIDTH), F32),
        compiler_params=_params(1),
        name="chunk_sgu",
    )(dz, g, b, ws, bs_full)


MERGE_TM = 512


def _merge_kernel(x_ref, ya_ref, yb_ref, yc_ref, yd_ref, mod_ref, g1_ref, w_in_hbm, bg_ref, wb_ref, wo_ref,
                  g2_ref, rwh_ref, rwl_ref, rb_ref, xm_ref, h2_ref, lg_ref, wg_ref, sem, *, layer):
    @pl.when(pl.program_id(0) == 0)
    def _():
        cp = pltpu.make_async_copy(w_in_hbm.at[layer, :, pl.ds(MIX_COLS, N_BRANCH * D_MODEL)], wg_ref, sem)
        cp.start()
        cp.wait()

    m = mod_ref[...]
    x = x_ref[...]
    h = _norm_mod(x, g1_ref[...], m[0:1], m[1:2]).astype(BF16)
    merged = None
    for i, y_ref in enumerate((ya_ref, yb_ref, yc_ref, yd_ref)):
        logit = jnp.dot(h, wg_ref[:, i * D_MODEL:(i + 1) * D_MODEL], preferred_element_type=F32)
        gate = _sigmoid(logit + bg_ref[:, i * D_MODEL:(i + 1) * D_MODEL])
        term = gate * jnp.dot(y_ref[...].astype(BF16), wb_ref[i], preferred_element_type=F32)
        merged = term if merged is None else merged + term
    out = jnp.dot(merged.astype(BF16), wo_ref[...], preferred_element_type=F32)
    xm = x + m[2:3] * out
    xm_ref[...] = xm
    h2 = _norm_mod(xm, g2_ref[...], m[3:4], m[4:5])
    h2_hi = h2.astype(BF16)
    h2_lo = (h2 - h2_hi.astype(F32)).astype(BF16)
    lg = jnp.dot(h2_hi, rwh_ref[...], preferred_element_type=F32)
    lg = lg + jnp.dot(h2_hi, rwl_ref[...], preferred_element_type=F32)
    lg = lg + jnp.dot(h2_lo, rwh_ref[...], preferred_element_type=F32)
    lg_ref[...] = lg + rb_ref[...]
    h2_ref[...] = h2_hi


def _merge(x, ya, yb, yc, yd, mods_l, g1, w_in_b, bg, wb, wo, g2, rwh, rwl, rb, layer):
    def tok(w):
        return pl.BlockSpec((MERGE_TM, w), lambda i: (i, 0))

    def full(*shape):
        return pl.BlockSpec(shape, lambda i: (0,) * len(shape))

    return pl.pallas_call(
        functools.partial(_merge_kernel, layer=layer),
        grid=(N_TOK // MERGE_TM,),
        in_specs=[tok(D_MODEL), tok(BRANCH_W), tok(BRANCH_W), tok(BRANCH_W), tok(BRANCH_W),
                  pl.BlockSpec((None, N_MOD, D_MODEL), lambda i: (_mod_row(i * MERGE_TM), 0, 0)),
                  full(1, D_MODEL), pl.BlockSpec(memory_space=pl.ANY), full(1, N_BRANCH * D_MODEL),
                  full(N_BRANCH, BRANCH_W, D_MODEL), full(D_MODEL, D_MODEL), full(1, D_MODEL),
                  full(D_MODEL, N_EXPERTS), full(D_MODEL, N_EXPERTS), full(1, N_EXPERTS)],
        out_specs=[tok(D_MODEL), tok(D_MODEL), tok(N_EXPERTS)],
        out_shape=[jax.ShapeDtypeStruct((N_TOK, D_MODEL), F32),
                   jax.ShapeDtypeStruct((N_TOK, D_MODEL), BF16),
                   jax.ShapeDtypeStruct((N_TOK, N_EXPERTS), F32)],
        scratch_shapes=[pltpu.VMEM((D_MODEL, N_BRANCH * D_MODEL), BF16), pltpu.SemaphoreType.DMA(())],
        compiler_params=_params(1),
        name="merge",
    )(x, ya, yb, yc, yd, mods_l, g1, w_in_b, bg, wb, wo, g2, rwh, rwl, rb)


ROUTE_TM = 256
TILE_ROWS = ROUTE_TM * TOP_K
N_TILES = N_TOK // ROUTE_TM


def _route_kernel(lg_ref, w_ref, q_ref, cnt_ref, off_ref):
    lg = lg_ref[...]
    lane = lax.broadcasted_iota(jnp.int32, lg.shape, 1)
    sels, vals = [], []
    for _ in range(TOP_K):
        mx = lg.max(axis=-1, keepdims=True)
        idx = jnp.where(lg == mx, lane, N_EXPERTS).min(axis=-1, keepdims=True)
        sel = lane == idx
        sels.append(sel)
        vals.append(mx)
        lg = jnp.where(sel, -jnp.inf, lg)
    exps = [jnp.exp(v - vals[0]) for v in vals]
    den = exps[0] + exps[1] + exps[2] + exps[3]
    onehot = jnp.zeros(lg.shape, F32)
    for sel in sels:
        onehot = onehot + sel.astype(F32)
    row = lax.broadcasted_iota(jnp.int32, (ROUTE_TM, ROUTE_TM), 0)
    col = lax.broadcasted_iota(jnp.int32, (ROUTE_TM, ROUTE_TM), 1)
    tri = jnp.where(col < row, 1.0, 0.0).astype(BF16)
    rank = jnp.dot(tri, onehot.astype(BF16), preferred_element_type=F32)
    cnt = jnp.sum(onehot, axis=0, keepdims=True)
    erow = lax.broadcasted_iota(jnp.int32, (N_EXPERTS, N_EXPERTS), 0)
    ecol = lax.broadcasted_iota(jnp.int32, (N_EXPERTS, N_EXPERTS), 1)
    upper = jnp.where(erow < ecol, 1.0, 0.0).astype(BF16)
    off = jnp.dot(jnp.broadcast_to(cnt, (SUBLANES, N_EXPERTS)).astype(BF16), upper,
                  preferred_element_type=F32)[0:1]
    slot = rank + off
    k_lane = lax.broadcasted_iota(jnp.int32, (ROUTE_TM, TOP_K), 1)
    w_out = jnp.zeros((ROUTE_TM, TOP_K), F32)
    q_out = jnp.zeros((ROUTE_TM, TOP_K), F32)
    for k in range(TOP_K):
        w_out = jnp.where(k_lane == k, exps[k] / den, w_out)
        qk = jnp.sum(jnp.where(sels[k], slot, 0.0), axis=-1, keepdims=True)
        q_out = jnp.where(k_lane == k, qk, q_out)
    w_ref[...] = w_out
    q_ref[...] = q_out.astype(jnp.int32)
    cnt_ref[...] = cnt.astype(jnp.int32)
    off_ref[...] = off.astype(jnp.int32)


def _route(logits):
    def tok(w):
        return pl.BlockSpec((ROUTE_TM, w), lambda i: (i, 0))

    tile_row = pl.BlockSpec((None, 1, N_EXPERTS), lambda i: (i, 0, 0))
    return pl.pallas_call(
        _route_kernel,
        grid=(N_TILES,),
        in_specs=[tok(N_EXPERTS)],
        out_specs=[tok(TOP_K), tok(TOP_K), tile_row, tile_row],
        out_shape=[jax.ShapeDtypeStruct((N_TOK, TOP_K), F32),
                   jax.ShapeDtypeStruct((N_TOK, TOP_K), jnp.int32),
                   jax.ShapeDtypeStruct((N_TILES, 1, N_EXPERTS), jnp.int32),
                   jax.ShapeDtypeStruct((N_TILES, 1, N_EXPERTS), jnp.int32)],
        compiler_params=_params(1),
        name="route",
    )(logits)


SLAB_CHUNK = 16


def _slab_pieces(tile, cnt_ref, off_ref, row_ref, fn):
    def per_expert(e, carry):
        n = cnt_ref[tile * N_EXPERTS + e]
        src = off_ref[tile * N_EXPERTS + e]
        dst = row_ref[tile * N_EXPERTS + e]
        n_full = n // SLAB_CHUNK

        def full(j, c):
            fn(src + j * SLAB_CHUNK, dst + j * SLAB_CHUNK, SLAB_CHUNK)
            return c

        lax.fori_loop(0, n_full, full, 0)
        rem = n - n_full * SLAB_CHUNK
        bit = SLAB_CHUNK // 2
        while bit >= 1:
            start = n_full * SLAB_CHUNK + (rem & ~(2 * bit - 1))

            @pl.when((rem & bit) != 0)
            def _(start=start, bit=bit):
                fn(src + start, dst + start, bit)

            bit //= 2
        return carry

    lax.fori_loop(0, N_EXPERTS, per_expert, 0)


def _rows(ref, row, n_rows):
    start = row * ROW_TILES
    if not isinstance(row, int):
        start = pl.multiple_of(start, ROW_TILES)
    return ref.at[pl.ds(start, n_rows * ROW_TILES)]


def _onehot_rows(q, values=None):
    lane = lax.broadcasted_iota(jnp.int32, (ROUTE_TM, TILE_ROWS), 1)
    s = jnp.zeros((ROUTE_TM, TILE_ROWS), F32)
    for k in range(TOP_K):
        v = 1.0 if values is None else values[:, k:k + 1]
        s = jnp.where(lane == q[:, k:k + 1], v, s)
    return s


def _wait_tile_rows(hbm_ref, buf_slot_ref, sem_slot):
    pltpu.make_async_copy(_rows(hbm_ref, 0, TILE_ROWS), buf_slot_ref, sem_slot).wait()


def _dispatch_kernel(cnt_ref, off_ref, row_ref, q_ref, h2_ref, xs_ref, buf_ref, sem):
    tile = pl.program_id(0)
    slot = tile % 2
    buf = buf_ref.at[slot]

    @pl.when(tile >= 2)
    def _():
        _wait_tile_rows(xs_ref, buf, sem.at[slot])

    sel = _onehot_rows(q_ref[...]).astype(BF16)
    xg = lax.dot_general(sel, h2_ref[...], (((0,), (0,)), ((), ())), preferred_element_type=F32)
    for c in range(ROW_TILES):
        buf[pl.ds(c, TILE_ROWS, stride=ROW_TILES), :] = xg[:, c * LANES:(c + 1) * LANES]

    def start(local_row, global_row, n_rows):
        pltpu.make_async_copy(_rows(buf, local_row, n_rows), _rows(xs_ref, global_row, n_rows),
                              sem.at[slot]).start()

    _slab_pieces(tile, cnt_ref, off_ref, row_ref, start)

    @pl.when(tile == N_TILES - 1)
    def _():
        _wait_tile_rows(xs_ref, buf, sem.at[slot])
        _wait_tile_rows(xs_ref, buf_ref.at[1 - slot], sem.at[1 - slot])


def _dispatch(cnt, off, rowstart, q, h2):
    grid_spec = pltpu.PrefetchScalarGridSpec(
        num_scalar_prefetch=3,
        grid=(N_TILES,),
        in_specs=[pl.BlockSpec((ROUTE_TM, TOP_K), lambda i, *_: (i, 0)),
                  pl.BlockSpec((ROUTE_TM, D_MODEL), lambda i, *_: (i, 0))],
        out_specs=pl.BlockSpec(memory_space=pl.ANY),
        scratch_shapes=[pltpu.VMEM((2, TILE_ROWS * ROW_TILES, LANES), F32), pltpu.SemaphoreType.DMA((2,))],
    )
    return pl.pallas_call(
        _dispatch_kernel,
        grid_spec=grid_spec,
        out_shape=jax.ShapeDtypeStruct((N_ROWS * ROW_TILES, LANES), F32),
        compiler_params=_params(1),
        name="dispatch",
    )(cnt, off, rowstart, q, h2)


CAST_ROWS = 128
W_SLOTS = 3


def _expert_kernel(blk_exp_ref, nvalid_ref, first_ref, head_ref, slot_ref, next_ref, next2_ref,
                   xs_ref, w1_hbm, b1_ref, w2_hbm, b2_ref, y_ref,
                   w1f_ref, w2f_ref, w1b_ref, w2b_ref, sem, *, layer):
    b = pl.program_id(0)
    e = blk_exp_ref[b]
    nvalid = nvalid_ref[b]
    slot = slot_ref[b]

    def start_weights(expert, ahead):
        s = (slot + ahead) % W_SLOTS
        pltpu.make_async_copy(w1_hbm.at[layer, expert], w1f_ref.at[s], sem.at[0, s]).start()
        pltpu.make_async_copy(w2_hbm.at[layer, expert], w2f_ref.at[s], sem.at[1, s]).start()

    @pl.when(first_ref[b] == 1)
    def _():
        @pl.when(head_ref[b] == 1)
        def _():
            start_weights(e, 0)

            @pl.when(next_ref[b] >= 0)
            def _():
                start_weights(next_ref[b], 1)

        pltpu.make_async_copy(w1_hbm.at[layer, e], w1f_ref.at[slot], sem.at[0, slot]).wait()
        pltpu.make_async_copy(w2_hbm.at[layer, e], w2f_ref.at[slot], sem.at[1, slot]).wait()

        @pl.when(next2_ref[b] >= 0)
        def _():
            start_weights(next2_ref[b], 2)

        def cast1(i, carry):
            r = pl.multiple_of(i * CAST_ROWS, CAST_ROWS)
            w1b_ref[pl.ds(r, CAST_ROWS), :] = w1f_ref[slot, pl.ds(r, CAST_ROWS), :].astype(BF16)
            return carry

        def cast2(i, carry):
            r = pl.multiple_of(i * CAST_ROWS, CAST_ROWS)
            w2b_ref[pl.ds(r, CAST_ROWS), :] = w2f_ref[slot, pl.ds(r, CAST_ROWS), :].astype(BF16)
            return carry

        lax.fori_loop(0, D_MODEL // CAST_ROWS, cast1, 0)
        lax.fori_loop(0, D_FF // CAST_ROWS, cast2, 0)

    @pl.when(nvalid > 0)
    def _():
        valid = lax.broadcasted_iota(jnp.int32, (EXP_BLOCK, LANES), 0) < nvalid
        chunks = [jnp.where(valid, xs_ref[pl.ds(c, EXP_BLOCK, stride=ROW_TILES), :], 0.0).astype(BF16)
                  for c in range(ROW_TILES)]
        xb = jnp.concatenate(chunks, axis=-1)
        hid = jnp.dot(xb, w1b_ref[...], preferred_element_type=F32) + b1_ref[...]
        glu = jnp.minimum(hid[:, :D_FF], SWIGLU_LIMIT)
        lin = jnp.clip(hid[:, D_FF:], -SWIGLU_LIMIT, SWIGLU_LIMIT)
        act = glu * _sigmoid(SWIGLU_ALPHA * glu) * (lin + 1.0)
        y = jnp.dot(act.astype(BF16), w2b_ref[...], preferred_element_type=F32) + b2_ref[...]
        for c in range(ROW_TILES):
            y_ref[pl.ds(c, EXP_BLOCK, stride=ROW_TILES), :] = y[:, c * LANES:(c + 1) * LANES]

    @pl.when(nvalid == 0)
    def _():
        y_ref[...] = jnp.zeros_like(y_ref)


def _experts(tables, xs, w1, b1, w2, b2, layer):
    def blk(b, *_):
        return (b, 0)

    def bias(b, be, *_):
        return (layer, be[b], 0, 0)

    grid_spec = pltpu.PrefetchScalarGridSpec(
        num_scalar_prefetch=7,
        grid=(N_BLOCKS,),
        in_specs=[
            pl.BlockSpec((EXP_BLOCK * ROW_TILES, LANES), blk),
            pl.BlockSpec(memory_space=pl.ANY),
            pl.BlockSpec((None, None, 1, 2 * D_FF), bias),
            pl.BlockSpec(memory_space=pl.ANY),
            pl.BlockSpec((None, None, 1, D_MODEL), bias),
        ],
        out_specs=pl.BlockSpec((EXP_BLOCK * ROW_TILES, LANES), blk),
        scratch_shapes=[pltpu.VMEM((W_SLOTS, D_MODEL, 2 * D_FF), F32), pltpu.VMEM((W_SLOTS, D_FF, D_MODEL), F32),
                        pltpu.VMEM((D_MODEL, 2 * D_FF), BF16), pltpu.VMEM((D_FF, D_MODEL), BF16),
                        pltpu.SemaphoreType.DMA((2, W_SLOTS))],
    )
    return pl.pallas_call(
        functools.partial(_expert_kernel, layer=layer),
        grid_spec=grid_spec,
        out_shape=jax.ShapeDtypeStruct((N_ROWS * ROW_TILES, LANES), F32),
        compiler_params=_params(1),
        name="experts",
    )(*tables, xs, w1, b1.reshape(DEPTH, N_EXPERTS, 1, 2 * D_FF), w2,
      b2.reshape(DEPTH, N_EXPERTS, 1, D_MODEL))


def _split_bf16(x):
    hi = x.astype(BF16)
    return hi, (x - hi.astype(F32)).astype(BF16)


P_TILES = N_P // ROUTE_TM


def _combine_kernel(cnt_ref, off_ref, row_ref, q_ref, w_ref, y_ref, xm_ref, mod_ref, fg_ref,
                    *refs, final):
    buf_ref, sem = refs[-2:]
    tile = pl.program_id(0)
    slot = tile % 2
    buf = buf_ref.at[slot]

    def fetch(t, s):
        def start(local_row, global_row, n_rows):
            pltpu.make_async_copy(_rows(y_ref, global_row, n_rows), _rows(buf_ref.at[s], local_row, n_rows),
                                  sem.at[s]).start()

        _slab_pieces(t, cnt_ref, off_ref, row_ref, start)

    @pl.when(tile == 0)
    def _():
        fetch(tile, slot)

    @pl.when(tile + 1 < N_TILES)
    def _():
        fetch(tile + 1, 1 - slot)

    s_hi, s_lo = _split_bf16(_onehot_rows(q_ref[...], w_ref[...]))
    _wait_tile_rows(y_ref, buf, sem.at[slot])
    rows = jnp.concatenate([buf[pl.ds(c, TILE_ROWS, stride=ROW_TILES), :] for c in range(ROW_TILES)],
                           axis=-1)
    r_hi, r_lo = _split_bf16(rows)
    moe = jnp.dot(s_hi, r_hi, preferred_element_type=F32)
    moe = moe + jnp.dot(s_lo, r_hi, preferred_element_type=F32)
    moe = moe + jnp.dot(s_hi, r_lo, preferred_element_type=F32)
    m = mod_ref[...]
    x = xm_ref[...] + m[5:6] * moe
    if not final:
        refs[0][...] = x
        return
    xn = x * lax.rsqrt(jnp.mean(x * x, axis=-1, keepdims=True) + EPS) * fg_ref[...]
    yp_ref, ys_ref = refs[:2]

    @pl.when(tile < P_TILES)
    def _():
        yp_ref[...] = xn

    @pl.when(tile >= P_TILES)
    def _():
        ys_ref[...] = xn


def _combine(cnt, off, rowstart, q, topw, y, xm, mods_l, final_g, final):
    def tok(w):
        return pl.BlockSpec((ROUTE_TM, w), lambda i, *_: (i, 0))

    if final:
        out_specs = [pl.BlockSpec((ROUTE_TM, D_MODEL), lambda i, *_: (jnp.minimum(i, P_TILES - 1), 0)),
                     pl.BlockSpec((ROUTE_TM, D_MODEL), lambda i, *_: (jnp.maximum(i - P_TILES, 0), 0))]
        out_shape = [jax.ShapeDtypeStruct((N_P, D_MODEL), F32), jax.ShapeDtypeStruct((N_S, D_MODEL), F32)]
    else:
        out_specs = [tok(D_MODEL)]
        out_shape = [jax.ShapeDtypeStruct((N_TOK, D_MODEL), F32)]
    grid_spec = pltpu.PrefetchScalarGridSpec(
        num_scalar_prefetch=3,
        grid=(N_TILES,),
        in_specs=[tok(TOP_K), tok(TOP_K),
                  pl.BlockSpec(memory_space=pl.ANY),
                  tok(D_MODEL),
                  pl.BlockSpec((None, N_MOD, D_MODEL), lambda i, *_: (_mod_row(i * ROUTE_TM), 0, 0)),
                  pl.BlockSpec((1, D_MODEL), lambda i, *_: (0, 0))],
        out_specs=out_specs,
        scratch_shapes=[pltpu.VMEM((2, TILE_ROWS * ROW_TILES, LANES), F32), pltpu.SemaphoreType.DMA((2,))],
    )
    return pl.pallas_call(
        functools.partial(_combine_kernel, final=final),
        grid_spec=grid_spec,
        out_shape=out_shape,
        compiler_params=_params(1),
        name="combine_final" if final else "combine",
    )(cnt, off, rowstart, q, topw, y, xm, mods_l, final_g)


def _rope_tables():
    t = np.arange(DEC_SEQ)
    row = (t // GRID_W).astype(np.float32)
    col = (t % GRID_W).astype(np.float32)
    inv = jnp.asarray(ROPE_THETA, F32) ** (-jnp.arange(ROPE_PAIRS, dtype=F32) / ROPE_PAIRS)
    ang = jnp.concatenate([jnp.asarray(row)[:, None] * inv, jnp.asarray(col)[:, None] * inv], axis=-1)
    cos = jnp.repeat(jnp.cos(ang), 2, axis=-1)
    sin = jnp.repeat(jnp.sin(ang), 2, axis=-1)
    sign = jnp.asarray(np.tile(np.array([-1.0, 1.0], np.float32), HEAD_DIM // 2))
    return jnp.tile(cos, (1, A_HEADS)), jnp.tile(sin * sign, (1, A_HEADS))


def _routing_tables(tile_cnt):
    i32 = jnp.int32
    carry = jnp.cumsum(tile_cnt, axis=0) - tile_cnt
    counts = jnp.sum(tile_cnt, axis=0)
    padded = (counts + EXP_BLOCK - 1) // EXP_BLOCK * EXP_BLOCK
    pad_end = jnp.cumsum(padded)
    pad_start = pad_end - padded
    rowstart = (pad_start[None, :] + carry).astype(i32)
    blk_row = jnp.arange(N_BLOCKS, dtype=i32) * EXP_BLOCK
    blk_exp = jnp.sum((blk_row[:, None] >= pad_end[None, :]).astype(i32), axis=1)
    blk_exp = jnp.minimum(blk_exp, N_EXPERTS - 1)
    eid = jnp.arange(N_EXPERTS, dtype=i32)

    def pick(table, idx):
        return jnp.sum(jnp.where(idx[:, None] == eid[None, :], table[None, :], 0), axis=1).astype(i32)

    blk_start = pick(pad_start, blk_exp)
    nvalid = jnp.clip(pick(counts, blk_exp) - (blk_row - blk_start), 0, EXP_BLOCK).astype(i32)
    first = jnp.logical_and(blk_row == blk_start, nvalid > 0)
    active = counts > 0
    act_rank = jnp.cumsum(active.astype(i32)) - 1
    later = jnp.logical_and(active[None, :], eid[None, :] > eid[:, None])
    nxt = jnp.min(jnp.where(later, eid[None, :], N_EXPERTS), axis=1)
    nxt = jnp.where(nxt == N_EXPERTS, -1, nxt).astype(i32)
    nxt2 = jnp.where(nxt >= 0, pick(nxt + 1, nxt) - 1, -1).astype(i32)
    blk_rank = pick(act_rank, blk_exp)
    head = jnp.logical_and(first, blk_rank == 0)
    tables = (blk_exp, nvalid, first.astype(i32), head.astype(i32), (blk_rank % W_SLOTS).astype(i32),
              pick(nxt + 1, blk_exp) - 1, pick(nxt2 + 1, blk_exp) - 1)
    return rowstart.reshape(-1), tables


def kernel(x_prompt, x_sample, cache_attn_k, cache_attn_v, cache_na_k, cache_na_v, c, c_ctx, w_mod, b_mod, norm1_g, norm2_g, w_in, b_gate, q_norm_g, k_norm_g, na_rpb, conv_w, conv_b, conv_ln_g, conv_ln_b, sgu_ln_g, sgu_ln_b, sgu_w, sgu_b, w_branch, w_out, router_w, router_b, exp_w1, exp_b1, exp_w2, exp_b2, final_g):
    x = jnp.concatenate([x_prompt.reshape(N_P, D_MODEL), x_sample.reshape(N_S, D_MODEL)], axis=0)
    cvec = jnp.zeros((SUBLANES, D_MODEL), F32).at[0].set(c_ctx).at[1:1 + DEC_BATCH].set(c)
    mods = _modulation(cvec, w_mod, b_mod).reshape(DEPTH, SUBLANES, N_MOD, D_MODEL)
    cos_t, sin_t = _rope_tables()
    cak = cache_attn_k.reshape(DEC_BATCH, DEPTH, PAST_LEN, A_KV)
    cav = cache_attn_v.reshape(DEC_BATCH, DEPTH, PAST_LEN, A_KV)
    cbk = cache_na_k.reshape(DEC_BATCH, DEPTH, PAST_LEN, B_W)
    cbv = cache_na_v.reshape(DEC_BATCH, DEPTH, PAST_LEN, B_W)
    w_in_b = w_in.astype(BF16)
    w_br = w_branch.astype(BF16)
    w_o = w_out.astype(BF16)
    rw_hi = router_w.astype(BF16)
    rw_lo = (router_w - rw_hi.astype(F32)).astype(BF16)
    final_g2 = final_g.reshape(1, D_MODEL)

    caches = ()
    outs = None
    for l in range(DEPTH):
        mods_l = mods[l]
        g1 = norm1_g[l].reshape(1, D_MODEL)
        gq = jnp.tile(q_norm_g[l], A_HEADS).reshape(1, A_Q)
        gk = jnp.tile(k_norm_g[l], A_KV_HEADS).reshape(1, A_KV)
        aq, ak, av, bq, bk, bv, cz, dz = _in_proj(x, mods_l, g1, w_in_b, l)
        ya, yb, *caches = _prompt_attn(aq, ak, av, bq, bk, bv, gq, gk, caches, l)
        ya = _sample_attn(aq, ak, av, cak, cav, cos_t, sin_t, gq, gk, ya, l)
        yb = _na_attn(bq, bk, bv, cbk, cbv, _na_bias(na_rpb[l]), yb, l)
        cw = conv_w[l]
        cb = conv_b[l].reshape(1, C_WIDTH)
        cg = conv_ln_g[l].reshape(1, C_WIDTH)
        cbb = conv_ln_b[l].reshape(1, C_WIDTH)
        yc = _conv_call(cz, cw, cb, cg, cbb, SEQ, 0, BATCH)
        yc = _conv_call(cz, cw, cb, cg, cbb, DEC_SEQ, N_P // DEC_SEQ, DEC_BATCH, partial_out=yc)
        bs_full = jnp.repeat(sgu_b[l].T, SGU_GW, axis=1)
        yd = _sgu(dz, sgu_ln_g[l].reshape(1, SGU_WIDTH), sgu_ln_b[l].reshape(1, SGU_WIDTH),
                  sgu_w[l].astype(BF16), bs_full)
        xm, h2, logits = _merge(x, ya, yb, yc, yd, mods_l, g1, w_in_b,
                                b_gate[l].reshape(1, N_BRANCH * D_MODEL), w_br[l], w_o[l],
                                norm2_g[l].reshape(1, D_MODEL), rw_hi[l], rw_lo[l],
                                router_b[l].reshape(1, N_EXPERTS), l)
        top_w, q, tile_cnt, tile_off = _route(logits)
        rowstart, tables = _routing_tables(tile_cnt.reshape(N_TILES, N_EXPERTS))
        tile_cnt = tile_cnt.reshape(-1)
        tile_off = tile_off.reshape(-1)
        xs = _dispatch(tile_cnt, tile_off, rowstart, q, h2)
        y = _experts(tables, xs, exp_w1, exp_b1, exp_w2, exp_b2, l)
        outs = _combine(tile_cnt, tile_off, rowstart, q, top_w, y, xm, mods_l, final_g2, l == DEPTH - 1)
        x = outs[0]

    new_k, new_v, new_bk, new_bv = caches
    return (outs[0].reshape(BATCH, SEQ, D_MODEL), outs[1].reshape(DEC_BATCH, DEC_SEQ, D_MODEL),
            new_k.reshape(BATCH, DEPTH, SEQ, A_KV_HEADS, HEAD_DIM),
            new_v.reshape(BATCH, DEPTH, SEQ, A_KV_HEADS, HEAD_DIM),
            new_bk.reshape(BATCH, DEPTH, SEQ, B_HEADS, HEAD_DIM),
            new_bv.reshape(BATCH, DEPTH, SEQ, B_HEADS, HEAD_DIM))
```

```python
import functools

import numpy as np
import jax
import jax.numpy as jnp
from jax import lax
from jax.experimental import pallas as pl
from jax.experimental.pallas import tpu as pltpu

D_MODEL = 1024
BATCH = 32
SEQ = 256
DEPTH = 2
DEC_BATCH = 2
DEC_SEQ = 1024
PAST_LEN = 512
GRID_W = 64
HEAD_DIM = 64
A_HEADS = 4
A_KV_HEADS = 2
B_HEADS = 4
NA_ROWS = 8
NA_COLS = 16
C_WIDTH = 256
CONV_WIDTH = 31
SGU_WIDTH = 256
SGU_GROUPS = 4
SGU_CHUNK = 128
N_BRANCH = 4
BRANCH_W = 256
N_EXPERTS = 32
TOP_K = 4
D_FF = 1024
SWIGLU_ALPHA = 1.702
SWIGLU_LIMIT = 7.0
MOE_BLOCK = 128
ROPE_THETA = 10000.0
ROPE_PAIRS = HEAD_DIM // 4
N_MOD = 6
EPS = 1e-6
NEG_INF = -1e30

A_Q = A_HEADS * HEAD_DIM
A_KV = A_KV_HEADS * HEAD_DIM
B_W = B_HEADS * HEAD_DIM
MIX_SIZES = (A_Q, A_KV, A_KV, B_W, B_W, B_W, 2 * C_WIDTH, 2 * SGU_WIDTH)
MIX_COLS = sum(MIX_SIZES)

N_P = BATCH * SEQ
N_S = DEC_BATCH * DEC_SEQ
N_TOK = N_P + N_S
N_ASSIGN = N_TOK * TOP_K
EXP_BLOCK = 256
N_BLOCKS = N_ASSIGN // EXP_BLOCK + N_EXPERTS
N_ROWS = N_BLOCKS * EXP_BLOCK
GRID_ROWS = DEC_SEQ // GRID_W
NA_WR = min(NA_ROWS, GRID_ROWS)
N_LOC = NA_WR * GRID_W

SUBLANES = 8
LANES = 128
ROW_TILES = D_MODEL // LANES
VMEM_LIMIT = 56 * 1024 * 1024

F32 = jnp.float32
BF16 = jnp.bfloat16


def _params(n_axes, vmem=None):
    return pltpu.CompilerParams(
        dimension_semantics=("arbitrary",) * n_axes,
        vmem_limit_bytes=vmem if vmem is not None else VMEM_LIMIT)


def _mod_row(start):
    return jnp.where(start < N_P, 0, 1 + (start - N_P) // DEC_SEQ)


def _bdot(a, b):
    return jnp.dot(a.astype(BF16), b.astype(BF16), preferred_element_type=F32)


def _bdot_nt(a, b):
    return lax.dot_general(a.astype(BF16), b.astype(BF16), (((1,), (1,)), ((), ())),
                           preferred_element_type=F32)


def _sigmoid(x):
    return 1.0 / (1.0 + jnp.exp(-x))


MOD_TN = 1536


def _mod_kernel(c_ref, w_ref, b_ref, o_ref):
    c = c_ref[...]
    s = c * _sigmoid(c)
    o_ref[...] = _bdot(s, w_ref[...]) + b_ref[...]


def _modulation(cvec, w_mod, b_mod):
    n_col = N_MOD * D_MODEL
    return pl.pallas_call(
        _mod_kernel,
        grid=(DEPTH, n_col // MOD_TN),
        in_specs=[
            pl.BlockSpec((SUBLANES, D_MODEL), lambda l, j: (0, 0)),
            pl.BlockSpec((None, D_MODEL, MOD_TN), lambda l, j: (l, 0, j)),
            pl.BlockSpec((None, 1, MOD_TN), lambda l, j: (l, 0, j)),
        ],
        out_specs=pl.BlockSpec((None, SUBLANES, MOD_TN), lambda l, j: (l, 0, j)),
        out_shape=jax.ShapeDtypeStruct((DEPTH, SUBLANES, n_col), F32),
        compiler_params=_params(2),
        name="modulation",
    )(cvec, w_mod, b_mod.reshape(DEPTH, 1, n_col))


IN_TM = 512


def _norm_mod(x, g, shift, scale):
    y = x * lax.rsqrt(jnp.mean(x * x, axis=-1, keepdims=True) + EPS) * g
    return y * (1.0 + scale) + shift


def _stream_specs(tm, stream):
    p_tiles = N_P // tm
    s_first = stream[2] // tm
    return [pl.BlockSpec((tm, D_MODEL), lambda i: (jnp.minimum(i, p_tiles - 1), 0)),
            pl.BlockSpec((tm, D_MODEL), lambda i: (jnp.maximum(i - p_tiles, 0) + s_first, 0))]


def _stream_tile(xp_ref, xs_ref, tm):
    return jnp.where(pl.program_id(0) < N_P // tm, xp_ref[...], xs_ref[...])


def _in_kernel(xp_ref, xs_ref, mod_ref, g_ref, w_ref, *out_refs):
    m = mod_ref[...]
    h = _norm_mod(_stream_tile(xp_ref, xs_ref, IN_TM), g_ref[...], m[0:1], m[1:2])
    z = jnp.dot(h.astype(BF16), w_ref[...], preferred_element_type=F32)
    off = 0
    for o_ref, sz in zip(out_refs, MIX_SIZES):
        o_ref[...] = z[:, off:off + sz]
        off += sz


def _in_proj(stream, mods_l, g1, w_in_b, layer):
    return pl.pallas_call(
        _in_kernel,
        grid=(N_TOK // IN_TM,),
        in_specs=_stream_specs(IN_TM, stream) + [
            pl.BlockSpec((None, N_MOD, D_MODEL), lambda i: (_mod_row(i * IN_TM), 0, 0)),
            pl.BlockSpec((1, D_MODEL), lambda i: (0, 0)),
            pl.BlockSpec((None, D_MODEL, MIX_COLS), lambda i: (layer, 0, 0)),
        ],
        out_specs=[pl.BlockSpec((IN_TM, sz), lambda i: (i, 0)) for sz in MIX_SIZES],
        out_shape=[jax.ShapeDtypeStruct((N_TOK, sz), F32) for sz in MIX_SIZES],
        compiler_params=_params(1),
        name="in_proj",
    )(stream[0], stream[1], mods_l, g1, w_in_b)


def _head_rms(x, g):
    n_heads = x.shape[-1] // HEAD_DIM
    seg = lax.broadcasted_iota(jnp.int32, x.shape, 1) // HEAD_DIM
    xx = x * x
    inv = jnp.zeros_like(x)
    for h in range(n_heads):
        ms = jnp.sum(jnp.where(seg == h, xx, 0.0), axis=-1, keepdims=True) * (1.0 / HEAD_DIM)
        inv = jnp.where(seg == h, lax.rsqrt(ms + EPS), inv)
    return x * inv * g


def _softmax_pv(score_parts, value_parts):
    m = score_parts[0].max(axis=-1, keepdims=True)
    for s in score_parts[1:]:
        m = jnp.maximum(m, s.max(axis=-1, keepdims=True))
    den = None
    acc = None
    for s, v in zip(score_parts, value_parts):
        e = jnp.exp(s - m)
        d = e.sum(axis=-1, keepdims=True)
        a = _bdot(e, v)
        den = d if den is None else den + d
        acc = a if acc is None else acc + a
    return acc / den


def _head(x, h):
    return x[:, h * HEAD_DIM:(h + 1) * HEAD_DIM]


SCALE = HEAD_DIM ** -0.5


def _prompt_attn_kernel(aq_ref, ak_ref, av_ref, bq_ref, bk_ref, bv_ref, gq_ref, gk_ref, *refs):
    ya_ref, yb_ref, nk_ref, nv_ref, nbk_ref, nbv_ref = refs[-6:]
    aq = _head_rms(aq_ref[...], gq_ref[...])
    ak = _head_rms(ak_ref[...], gk_ref[...])
    av = av_ref[...]
    for ref, val in ((nk_ref, ak), (nv_ref, av), (nbk_ref, bk_ref[...]), (nbv_ref, bv_ref[...])):
        n_heads = val.shape[-1] // HEAD_DIM
        for h in range(n_heads):
            ref[pl.ds(h, SEQ, stride=n_heads), :] = _head(val, h)
    grp = A_HEADS // A_KV_HEADS
    outs = []
    for h in range(A_HEADS):
        s = _bdot_nt(_head(aq, h), _head(ak, h // grp)) * SCALE
        outs.append(_softmax_pv([s], [_head(av, h // grp)]))
    ya_ref[...] = jnp.concatenate(outs, axis=-1)
    bq = bq_ref[...]
    bk = bk_ref[...]
    bv = bv_ref[...]
    outs = []
    for h in range(B_HEADS):
        s = _bdot_nt(_head(bq, h), _head(bk, h)) * SCALE
        outs.append(_softmax_pv([s], [_head(bv, h)]))
    yb_ref[...] = jnp.concatenate(outs, axis=-1)


def _prompt_attn(aq, ak, av, bq, bk, bv, gq, gk, caches, layer):
    def spec(w):
        return pl.BlockSpec((SEQ, w), lambda b: (b, 0))

    def cache_spec(n_heads):
        return pl.BlockSpec((SEQ * n_heads, HEAD_DIM), lambda b: (b * DEPTH + layer, 0))

    cache_heads = (A_KV_HEADS, A_KV_HEADS, B_HEADS, B_HEADS)
    n_in = 8
    return pl.pallas_call(
        _prompt_attn_kernel,
        grid=(BATCH,),
        in_specs=[spec(A_Q), spec(A_KV), spec(A_KV), spec(B_W), spec(B_W), spec(B_W),
                  pl.BlockSpec((1, A_Q), lambda b: (0, 0)),
                  pl.BlockSpec((1, A_KV), lambda b: (0, 0))]
        + [pl.BlockSpec(memory_space=pl.ANY) for _ in caches],
        out_specs=[spec(A_Q), spec(B_W)] + [cache_spec(n) for n in cache_heads],
        out_shape=[jax.ShapeDtypeStruct((N_TOK, A_Q), F32),
                   jax.ShapeDtypeStruct((N_TOK, B_W), F32)]
        + [jax.ShapeDtypeStruct((BATCH * DEPTH * SEQ * n, HEAD_DIM), F32) for n in cache_heads],
        input_output_aliases={n_in + j: 2 + j for j in range(len(caches))},
        compiler_params=_params(1),
        name="prompt_attn",
    )(aq, ak, av, bq, bk, bv, gq, gk, *caches)


QB = 128


def _rope(x, cos, sin_signed):
    n = x.shape[-1]
    nxt = pltpu.roll(x, n - 1, 1)
    prv = pltpu.roll(x, 1, 1)
    even = (lax.broadcasted_iota(jnp.int32, x.shape, 1) % 2) == 0
    return x * cos + jnp.where(even, nxt, prv) * sin_signed


def _sample_attn_kernel(q_ref, k_ref, v_ref, ck_ref, cv_ref, cosq_ref, sinq_ref, cosk_ref, sink_ref,
                        gq_ref, gk_ref, ya_prompt_ref, o_ref):
    del ya_prompt_ref
    q = _rope(_head_rms(q_ref[...], gq_ref[...]), cosq_ref[...], sinq_ref[...])
    k = _rope(_head_rms(k_ref[...], gk_ref[...]), cosk_ref[...], sink_ref[...])
    v = v_ref[...]
    ck = ck_ref[...]
    cv = cv_ref[...]
    grp = A_HEADS // A_KV_HEADS
    outs = []
    for h in range(A_HEADS):
        j = h // grp
        qh = _head(q, h)
        s1 = _bdot_nt(qh, _head(k, j)) * SCALE
        s2 = _bdot_nt(qh, _head(ck, j)) * SCALE
        outs.append(_softmax_pv([s1, s2], [_head(v, j), _head(cv, j)]))
    o_ref[...] = jnp.concatenate(outs, axis=-1)


def _sample_attn(aq, ak, av, cache_k, cache_v, cos_t, sin_t, gq, gk, ya, layer):
    nqb = DEC_SEQ // QB
    q0 = N_P // QB
    k0 = N_P // DEC_SEQ
    return pl.pallas_call(
        _sample_attn_kernel,
        grid=(DEC_BATCH, nqb),
        in_specs=[
            pl.BlockSpec((QB, A_Q), lambda b, i: (q0 + b * nqb + i, 0)),
            pl.BlockSpec((DEC_SEQ, A_KV), lambda b, i: (k0 + b, 0)),
            pl.BlockSpec((DEC_SEQ, A_KV), lambda b, i: (k0 + b, 0)),
            pl.BlockSpec((None, None, PAST_LEN, A_KV), lambda b, i: (b, layer, 0, 0)),
            pl.BlockSpec((None, None, PAST_LEN, A_KV), lambda b, i: (b, layer, 0, 0)),
            pl.BlockSpec((QB, A_Q), lambda b, i: (i, 0)),
            pl.BlockSpec((QB, A_Q), lambda b, i: (i, 0)),
            pl.BlockSpec((DEC_SEQ, A_KV), lambda b, i: (0, 0)),
            pl.BlockSpec((DEC_SEQ, A_KV), lambda b, i: (0, 0)),
            pl.BlockSpec((1, A_Q), lambda b, i: (0, 0)),
            pl.BlockSpec((1, A_KV), lambda b, i: (0, 0)),
            pl.BlockSpec(memory_space=pl.ANY),
        ],
        out_specs=pl.BlockSpec((QB, A_Q), lambda b, i: (q0 + b * nqb + i, 0)),
        out_shape=jax.ShapeDtypeStruct((N_TOK, A_Q), F32),
        input_output_aliases={11: 0},
        compiler_params=_params(2),
        name="sample_attn",
    )(aq, ak, av, cache_k, cache_v, cos_t, sin_t, cos_t, sin_t, gq, gk, ya)


N_ROW_OFF = 2 * NA_ROWS - 1
N_COL_OFF = 2 * NA_COLS - 1
NA_PAIRS = N_ROW_OFF - 1
assert NA_WR == NA_ROWS and NA_WR % 2 == 0 and 2 * GRID_W == LANES


def _na_bias_kernel(rpb_ref, o_ref):
    h = pl.program_id(0)
    qc = lax.broadcasted_iota(jnp.int32, (GRID_W, LANES), 0)
    lane = lax.broadcasted_iota(jnp.int32, (GRID_W, LANES), 1)
    right = lane >= GRID_W
    kc = jnp.where(right, lane - GRID_W, lane)
    c_start = jnp.clip(qc - NA_COLS // 2, 0, GRID_W - NA_COLS)
    col_in = jnp.logical_and(kc >= c_start, kc < c_start + NA_COLS)
    col_off = jnp.clip(kc - qc + NA_COLS - 1, 0, N_COL_OFF - 1)
    for p in range(NA_PAIRS):
        acc = jnp.zeros((GRID_W, LANES), F32)
        for o in range(N_COL_OFF):
            left_v = rpb_ref[(h * N_ROW_OFF + p) * N_COL_OFF + o]
            right_v = rpb_ref[(h * N_ROW_OFF + p + 1) * N_COL_OFF + o]
            acc = jnp.where(col_off == o, jnp.where(right, right_v, left_v), acc)
        o_ref[p] = jnp.where(col_in, acc, NEG_INF)


def _na_bias(rpb):
    return pl.pallas_call(
        _na_bias_kernel,
        grid_spec=pltpu.PrefetchScalarGridSpec(
            num_scalar_prefetch=1,
            grid=(B_HEADS,),
            in_specs=[],
            out_specs=pl.BlockSpec((None, NA_PAIRS, GRID_W, LANES), lambda h, *_: (h, 0, 0, 0)),
        ),
        out_shape=jax.ShapeDtypeStruct((B_HEADS, NA_PAIRS, GRID_W, LANES), F32),
        compiler_params=_params(1),
        name="na_bias",
    )(rpb.reshape(-1))


def _na_kernel(q_ref, k_ref, v_ref, ck_ref, cv_ref, bias_ref, yb_prompt_ref, o_ref):
    del yb_prompt_ref
    r = pl.program_id(1)
    r_start = jnp.clip(r - NA_WR // 2, 0, GRID_ROWS - NA_WR)
    base = pl.multiple_of(r_start * GRID_W, GRID_W)
    row_off0 = r_start - r + NA_ROWS - 1
    q = q_ref[...]
    kb = k_ref[pl.ds(base, N_LOC), :]
    vb = v_ref[pl.ds(base, N_LOC), :]
    ck = ck_ref[...]
    cv = cv_ref[...]
    outs = []
    for h in range(B_HEADS):
        qh = _head(q, h)
        bias = jnp.concatenate([bias_ref[h, row_off0 + 2 * j] for j in range(NA_WR // 2)], axis=-1)
        s1 = _bdot_nt(qh, _head(kb, h)) * SCALE + bias
        s2 = _bdot_nt(qh, _head(ck, h)) * SCALE
        outs.append(_softmax_pv([s1, s2], [_head(vb, h), _head(cv, h)]))
    o_ref[...] = jnp.concatenate(outs, axis=-1)


def _na_attn(bq, bk, bv, cache_k, cache_v, bias, yb, layer):
    q0 = N_P // GRID_W
    k0 = N_P // DEC_SEQ
    return pl.pallas_call(
        _na_kernel,
        grid=(DEC_BATCH, GRID_ROWS),
        in_specs=[
            pl.BlockSpec((GRID_W, B_W), lambda b, r: (q0 + b * GRID_ROWS + r, 0)),
            pl.BlockSpec((DEC_SEQ, B_W), lambda b, r: (k0 + b, 0)),
            pl.BlockSpec((DEC_SEQ, B_W), lambda b, r: (k0 + b, 0)),
            pl.BlockSpec((None, None, PAST_LEN, B_W), lambda b, r: (b, layer, 0, 0)),
            pl.BlockSpec((None, None, PAST_LEN, B_W), lambda b, r: (b, layer, 0, 0)),
            pl.BlockSpec((B_HEADS, NA_PAIRS, GRID_W, LANES), lambda b, r: (0, 0, 0, 0)),
            pl.BlockSpec(memory_space=pl.ANY),
        ],
        out_specs=pl.BlockSpec((GRID_W, B_W), lambda b, r: (q0 + b * GRID_ROWS + r, 0)),
        out_shape=jax.ShapeDtypeStruct((N_TOK, B_W), F32),
        input_output_aliases={6: 0},
        compiler_params=_params(2),
        name="na_attn",
    )(bq, bk, bv, cache_k, cache_v, bias, yb)


CONV_PAD = 16
CONV_CHUNK = 64


def _layer_norm(x, g, b):
    mu = jnp.mean(x, axis=-1, keepdims=True)
    xc = x - mu
    var = jnp.mean(xc * xc, axis=-1, keepdims=True)
    return xc * lax.rsqrt(var + EPS) * g + b


def _conv_kernel(z_ref, w_ref, cb_ref, g_ref, b_ref, *refs, s_len):
    o_ref, pad_ref = refs[-2:]
    z = z_ref[...]
    u = z[:, :C_WIDTH] * _sigmoid(z[:, C_WIDTH:])
    pad_ref[pl.ds(0, CONV_PAD), :] = jnp.zeros((CONV_PAD, C_WIDTH), F32)
    pad_ref[pl.ds(CONV_PAD + s_len, CONV_PAD), :] = jnp.zeros((CONV_PAD, C_WIDTH), F32)
    pad_ref[pl.ds(CONV_PAD, s_len), :] = u
    w = w_ref[...]
    shift = CONV_PAD - CONV_WIDTH // 2

    def chunk(c, carry):
        base = pl.multiple_of(c * CONV_CHUNK, CONV_CHUNK)
        acc = jnp.zeros((CONV_CHUNK, C_WIDTH), F32)
        for r in range(SUBLANES):
            part = None
            for k in range(CONV_WIDTH):
                if (k + shift) % SUBLANES != r:
                    continue
                rows = pad_ref[pl.ds(base + (k + shift - r), CONV_CHUNK + SUBLANES), :]
                term = rows * w[k:k + 1]
                part = term if part is None else part + term
            if part is not None:
                acc = acc + part[r:r + CONV_CHUNK]
        y = _layer_norm(acc + cb_ref[...], g_ref[...], b_ref[...])
        o_ref[pl.ds(base, CONV_CHUNK), :] = y * _sigmoid(y)
        return carry

    lax.fori_loop(0, s_len // CONV_CHUNK, chunk, 0)


def _conv_call(cz, w, cb, g, b, s_len, first_blk, n_seq, partial_out=None):
    vec = pl.BlockSpec((1, C_WIDTH), lambda i: (0, 0))
    extra = [] if partial_out is None else [partial_out]
    return pl.pallas_call(
        functools.partial(_conv_kernel, s_len=s_len),
        grid=(n_seq,),
        in_specs=[pl.BlockSpec((s_len, 2 * C_WIDTH), lambda i: (first_blk + i, 0)),
                  pl.BlockSpec((CONV_WIDTH, C_WIDTH), lambda i: (0, 0)), vec, vec, vec]
        + [pl.BlockSpec(memory_space=pl.ANY) for _ in extra],
        out_specs=pl.BlockSpec((s_len, C_WIDTH), lambda i: (first_blk + i, 0)),
        out_shape=jax.ShapeDtypeStruct((N_TOK, C_WIDTH), F32),
        input_output_aliases={5: 0} if extra else {},
        scratch_shapes=[pltpu.VMEM((s_len + 2 * CONV_PAD, C_WIDTH), F32)],
        compiler_params=_params(1),
        name="conformer_conv_%d" % s_len,
    )(cz, w, cb, g, b, *extra)


SGU_TM = 512
SGU_GW = SGU_WIDTH // SGU_GROUPS


def _sgu_kernel(z_ref, g_ref, b_ref, ws_ref, bs_ref, o_ref):
    z = z_ref[...]
    z = 0.5 * z * (1.0 + lax.erf(z * (2.0 ** -0.5)))
    u = z[:, :SGU_WIDTH]
    v = _layer_norm(z[:, SGU_WIDTH:], g_ref[...], b_ref[...])
    for c in range(SGU_TM // SGU_CHUNK):
        vc = v[c * SGU_CHUNK:(c + 1) * SGU_CHUNK]
        parts = [_bdot(ws_ref[g], vc[:, g * SGU_GW:(g + 1) * SGU_GW]) for g in range(SGU_GROUPS)]
        mixed = jnp.concatenate(parts, axis=-1) + bs_ref[...]
        o_ref[pl.ds(c * SGU_CHUNK, SGU_CHUNK), :] = u[c * SGU_CHUNK:(c + 1) * SGU_CHUNK] * mixed


def _sgu(dz, g, b, ws, bs_full):
    vec = pl.BlockSpec((1, SGU_WIDTH), lambda i: (0, 0))
    return pl.pallas_call(
        _sgu_kernel,
        grid=(N_TOK // SGU_TM,),
        in_specs=[pl.BlockSpec((SGU_TM, 2 * SGU_WIDTH), lambda i: (i, 0)), vec, vec,
                  pl.BlockSpec((SGU_GROUPS, SGU_CHUNK, SGU_CHUNK), lambda i: (0, 0, 0)),
                  pl.BlockSpec((SGU_CHUNK, SGU_WIDTH), lambda i: (0, 0))],
        out_specs=pl.BlockSpec((SGU_TM, SGU_WIDTH), lambda i: (i, 0)),
        out_shape=jax.ShapeDtypeStruct((N_TOK, SGU_WIDTH), F32),
        compiler_params=_params(1),
        name="chunk_sgu",
    )(dz, g, b, ws, bs_full)


MERGE_TM = 512


def _merge_kernel(xp_ref, xs_ref, ya_ref, yb_ref, yc_ref, yd_ref, mod_ref, g1_ref, w_in_hbm, bg_ref, wb_ref, wo_ref,
                  g2_ref, rwh_ref, rwl_ref, rb_ref, xm_ref, h2_ref, lg_ref, wg_ref, sem, *, layer):
    @pl.when(pl.program_id(0) == 0)
    def _():
        cp = pltpu.make_async_copy(w_in_hbm.at[layer, :, pl.ds(MIX_COLS, N_BRANCH * D_MODEL)], wg_ref, sem)
        cp.start()
        cp.wait()

    m = mod_ref[...]
    x = _stream_tile(xp_ref, xs_ref, MERGE_TM)
    h = _norm_mod(x, g1_ref[...], m[0:1], m[1:2]).astype(BF16)
    merged = None
    for i, y_ref in enumerate((ya_ref, yb_ref, yc_ref, yd_ref)):
        logit = jnp.dot(h, wg_ref[:, i * D_MODEL:(i + 1) * D_MODEL], preferred_element_type=F32)
        gate = _sigmoid(logit + bg_ref[:, i * D_MODEL:(i + 1) * D_MODEL])
        term = gate * jnp.dot(y_ref[...].astype(BF16), wb_ref[i], preferred_element_type=F32)
        merged = term if merged is None else merged + term
    out = jnp.dot(merged.astype(BF16), wo_ref[...], preferred_element_type=F32)
    xm = x + m[2:3] * out
    xm_ref[...] = xm
    h2 = _norm_mod(xm, g2_ref[...], m[3:4], m[4:5])
    h2_hi = h2.astype(BF16)
    h2_lo = (h2 - h2_hi.astype(F32)).astype(BF16)
    lg = jnp.dot(h2_hi, rwh_ref[...], preferred_element_type=F32)
    lg = lg + jnp.dot(h2_hi, rwl_ref[...], preferred_element_type=F32)
    lg = lg + jnp.dot(h2_lo, rwh_ref[...], preferred_element_type=F32)
    lg_ref[...] = lg + rb_ref[...]
    h2_ref[...] = h2_hi


def _merge(stream, ya, yb, yc, yd, mods_l, g1, w_in_b, bg, wb, wo, g2, rwh, rwl, rb, layer):
    def tok(w):
        return pl.BlockSpec((MERGE_TM, w), lambda i: (i, 0))

    def full(*shape):
        return pl.BlockSpec(shape, lambda i: (0,) * len(shape))

    return pl.pallas_call(
        functools.partial(_merge_kernel, layer=layer),
        grid=(N_TOK // MERGE_TM,),
        in_specs=_stream_specs(MERGE_TM, stream) + [
                  tok(BRANCH_W), tok(BRANCH_W), tok(BRANCH_W), tok(BRANCH_W),
                  pl.BlockSpec((None, N_MOD, D_MODEL), lambda i: (_mod_row(i * MERGE_TM), 0, 0)),
                  full(1, D_MODEL), pl.BlockSpec(memory_space=pl.ANY), full(1, N_BRANCH * D_MODEL),
                  full(N_BRANCH, BRANCH_W, D_MODEL), full(D_MODEL, D_MODEL), full(1, D_MODEL),
                  full(D_MODEL, N_EXPERTS), full(D_MODEL, N_EXPERTS), full(1, N_EXPERTS)],
        out_specs=[tok(D_MODEL), tok(D_MODEL), tok(N_EXPERTS)],
        out_shape=[jax.ShapeDtypeStruct((N_TOK, D_MODEL), F32),
                   jax.ShapeDtypeStruct((N_TOK, D_MODEL), BF16),
                   jax.ShapeDtypeStruct((N_TOK, N_EXPERTS), F32)],
        scratch_shapes=[pltpu.VMEM((D_MODEL, N_BRANCH * D_MODEL), BF16), pltpu.SemaphoreType.DMA(())],
        compiler_params=_params(1),
        name="merge",
    )(stream[0], stream[1], ya, yb, yc, yd, mods_l, g1, w_in_b, bg, wb, wo, g2, rwh, rwl, rb)


ROUTE_TM = 256
TILE_ROWS = ROUTE_TM * TOP_K
N_TILES = N_TOK // ROUTE_TM


def _route_kernel(lg_ref, w_ref, q_ref, cnt_ref, off_ref):
    lg = lg_ref[...]
    lane = lax.broadcasted_iota(jnp.int32, lg.shape, 1)
    sels, vals = [], []
    for _ in range(TOP_K):
        mx = lg.max(axis=-1, keepdims=True)
        idx = jnp.where(lg == mx, lane, N_EXPERTS).min(axis=-1, keepdims=True)
        sel = lane == idx
        sels.append(sel)
        vals.append(mx)
        lg = jnp.where(sel, -jnp.inf, lg)
    exps = [jnp.exp(v - vals[0]) for v in vals]
    den = exps[0] + exps[1] + exps[2] + exps[3]
    onehot = jnp.zeros(lg.shape, F32)
    for sel in sels:
        onehot = onehot + sel.astype(F32)
    row = lax.broadcasted_iota(jnp.int32, (ROUTE_TM, ROUTE_TM), 0)
    col = lax.broadcasted_iota(jnp.int32, (ROUTE_TM, ROUTE_TM), 1)
    tri = jnp.where(col < row, 1.0, 0.0).astype(BF16)
    rank = jnp.dot(tri, onehot.astype(BF16), preferred_element_type=F32)
    cnt = jnp.sum(onehot, axis=0, keepdims=True)
    erow = lax.broadcasted_iota(jnp.int32, (N_EXPERTS, N_EXPERTS), 0)
    ecol = lax.broadcasted_iota(jnp.int32, (N_EXPERTS, N_EXPERTS), 1)
    upper = jnp.where(erow < ecol, 1.0, 0.0).astype(BF16)
    off = jnp.dot(jnp.broadcast_to(cnt, (SUBLANES, N_EXPERTS)).astype(BF16), upper,
                  preferred_element_type=F32)[0:1]
    slot = rank + off
    k_lane = lax.broadcasted_iota(jnp.int32, (ROUTE_TM, TOP_K), 1)
    w_out = jnp.zeros((ROUTE_TM, TOP_K), F32)
    q_out = jnp.zeros((ROUTE_TM, TOP_K), F32)
    for k in range(TOP_K):
        w_out = jnp.where(k_lane == k, exps[k] / den, w_out)
        qk = jnp.sum(jnp.where(sels[k], slot, 0.0), axis=-1, keepdims=True)
        q_out = jnp.where(k_lane == k, qk, q_out)
    w_ref[...] = w_out
    q_ref[...] = q_out.astype(jnp.int32)
    cnt_ref[...] = cnt.astype(jnp.int32)
    off_ref[...] = off.astype(jnp.int32)


def _route(logits):
    def tok(w):
        return pl.BlockSpec((ROUTE_TM, w), lambda i: (i, 0))

    tile_row = pl.BlockSpec((None, 1, N_EXPERTS), lambda i: (i, 0, 0))
    return pl.pallas_call(
        _route_kernel,
        grid=(N_TILES,),
        in_specs=[tok(N_EXPERTS)],
        out_specs=[tok(TOP_K), tok(TOP_K), tile_row, tile_row],
        out_shape=[jax.ShapeDtypeStruct((N_TOK, TOP_K), F32),
                   jax.ShapeDtypeStruct((N_TOK, TOP_K), jnp.int32),
                   jax.ShapeDtypeStruct((N_TILES, 1, N_EXPERTS), jnp.int32),
                   jax.ShapeDtypeStruct((N_TILES, 1, N_EXPERTS), jnp.int32)],
        compiler_params=_params(1),
        name="route",
    )(logits)


SLAB_CHUNK = 16


def _slab_pieces(tile, cnt_ref, off_ref, row_ref, fn):
    def per_expert(e, carry):
        n = cnt_ref[tile * N_EXPERTS + e]
        src = off_ref[tile * N_EXPERTS + e]
        dst = row_ref[tile * N_EXPERTS + e]
        n_full = n // SLAB_CHUNK

        def full(j, c):
            fn(src + j * SLAB_CHUNK, dst + j * SLAB_CHUNK, SLAB_CHUNK)
            return c

        lax.fori_loop(0, n_full, full, 0)
        rem = n - n_full * SLAB_CHUNK
        bit = SLAB_CHUNK // 2
        while bit >= 1:
            start = n_full * SLAB_CHUNK + (rem & ~(2 * bit - 1))

            @pl.when((rem & bit) != 0)
            def _(start=start, bit=bit):
                fn(src + start, dst + start, bit)

            bit //= 2
        return carry

    lax.fori_loop(0, N_EXPERTS, per_expert, 0)


def _rows(ref, row, n_rows):
    start = row * ROW_TILES
    if not isinstance(row, int):
        start = pl.multiple_of(start, ROW_TILES)
    return ref.at[pl.ds(start, n_rows * ROW_TILES)]


def _onehot_rows(q, values=None):
    lane = lax.broadcasted_iota(jnp.int32, (ROUTE_TM, TILE_ROWS), 1)
    s = jnp.zeros((ROUTE_TM, TILE_ROWS), F32)
    for k in range(TOP_K):
        v = 1.0 if values is None else values[:, k:k + 1]
        s = jnp.where(lane == q[:, k:k + 1], v, s)
    return s


def _wait_tile_rows(hbm_ref, buf_slot_ref, sem_slot):
    pltpu.make_async_copy(_rows(hbm_ref, 0, TILE_ROWS), buf_slot_ref, sem_slot).wait()


def _dispatch_kernel(cnt_ref, off_ref, row_ref, q_ref, h2_ref, xs_ref, buf_ref, sem):
    tile = pl.program_id(0)
    slot = tile % 2
    buf = buf_ref.at[slot]

    @pl.when(tile >= 2)
    def _():
        _wait_tile_rows(xs_ref, buf, sem.at[slot])

    sel = _onehot_rows(q_ref[...]).astype(BF16)
    xg = lax.dot_general(sel, h2_ref[...], (((0,), (0,)), ((), ())), preferred_element_type=F32)
    for c in range(ROW_TILES):
        buf[pl.ds(c, TILE_ROWS, stride=ROW_TILES), :] = xg[:, c * LANES:(c + 1) * LANES]

    def start(local_row, global_row, n_rows):
        pltpu.make_async_copy(_rows(buf, local_row, n_rows), _rows(xs_ref, global_row, n_rows),
                              sem.at[slot]).start()

    _slab_pieces(tile, cnt_ref, off_ref, row_ref, start)

    @pl.when(tile == N_TILES - 1)
    def _():
        _wait_tile_rows(xs_ref, buf, sem.at[slot])
        _wait_tile_rows(xs_ref, buf_ref.at[1 - slot], sem.at[1 - slot])


def _dispatch(cnt, off, rowstart, q, h2):
    grid_spec = pltpu.PrefetchScalarGridSpec(
        num_scalar_prefetch=3,
        grid=(N_TILES,),
        in_specs=[pl.BlockSpec((ROUTE_TM, TOP_K), lambda i, *_: (i, 0)),
                  pl.BlockSpec((ROUTE_TM, D_MODEL), lambda i, *_: (i, 0))],
        out_specs=pl.BlockSpec(memory_space=pl.ANY),
        scratch_shapes=[pltpu.VMEM((2, TILE_ROWS * ROW_TILES, LANES), F32), pltpu.SemaphoreType.DMA((2,))],
    )
    return pl.pallas_call(
        _dispatch_kernel,
        grid_spec=grid_spec,
        out_shape=jax.ShapeDtypeStruct((N_ROWS * ROW_TILES, LANES), F32),
        compiler_params=_params(1),
        name="dispatch",
    )(cnt, off, rowstart, q, h2)


CAST_ROWS = 128
W_SLOTS = 3


def _expert_kernel(blk_exp_ref, nvalid_ref, first_ref, head_ref, slot_ref, next_ref, next2_ref,
                   xs_ref, w1_hbm, b1_ref, w2_hbm, b2_ref, y_ref,
                   w1f_ref, w2f_ref, w1b_ref, w2b_ref, sem, *, layer):
    b = pl.program_id(0)
    e = blk_exp_ref[b]
    nvalid = nvalid_ref[b]
    slot = slot_ref[b]

    def start_weights(expert, ahead):
        s = (slot + ahead) % W_SLOTS
        pltpu.make_async_copy(w1_hbm.at[layer, expert], w1f_ref.at[s], sem.at[0, s]).start()
        pltpu.make_async_copy(w2_hbm.at[layer, expert], w2f_ref.at[s], sem.at[1, s]).start()

    @pl.when(first_ref[b] == 1)
    def _():
        @pl.when(head_ref[b] == 1)
        def _():
            start_weights(e, 0)

            @pl.when(next_ref[b] >= 0)
            def _():
                start_weights(next_ref[b], 1)

        pltpu.make_async_copy(w1_hbm.at[layer, e], w1f_ref.at[slot], sem.at[0, slot]).wait()
        pltpu.make_async_copy(w2_hbm.at[layer, e], w2f_ref.at[slot], sem.at[1, slot]).wait()

        @pl.when(next2_ref[b] >= 0)
        def _():
            start_weights(next2_ref[b], 2)

        def cast1(i, carry):
            r = pl.multiple_of(i * CAST_ROWS, CAST_ROWS)
            w1b_ref[pl.ds(r, CAST_ROWS), :] = w1f_ref[slot, pl.ds(r, CAST_ROWS), :].astype(BF16)
            return carry

        def cast2(i, carry):
            r = pl.multiple_of(i * CAST_ROWS, CAST_ROWS)
            w2b_ref[pl.ds(r, CAST_ROWS), :] = w2f_ref[slot, pl.ds(r, CAST_ROWS), :].astype(BF16)
            return carry

        lax.fori_loop(0, D_MODEL // CAST_ROWS, cast1, 0)
        lax.fori_loop(0, D_FF // CAST_ROWS, cast2, 0)

    @pl.when(nvalid > 0)
    def _():
        valid = lax.broadcasted_iota(jnp.int32, (EXP_BLOCK, LANES), 0) < nvalid
        chunks = [jnp.where(valid, xs_ref[pl.ds(c, EXP_BLOCK, stride=ROW_TILES), :], 0.0).astype(BF16)
                  for c in range(ROW_TILES)]
        xb = jnp.concatenate(chunks, axis=-1)
        hid = jnp.dot(xb, w1b_ref[...], preferred_element_type=F32) + b1_ref[...]
        glu = jnp.minimum(hid[:, :D_FF], SWIGLU_LIMIT)
        lin = jnp.clip(hid[:, D_FF:], -SWIGLU_LIMIT, SWIGLU_LIMIT)
        act = glu * _sigmoid(SWIGLU_ALPHA * glu) * (lin + 1.0)
        y = jnp.dot(act.astype(BF16), w2b_ref[...], preferred_element_type=F32) + b2_ref[...]
        for c in range(ROW_TILES):
            y_ref[pl.ds(c, EXP_BLOCK, stride=ROW_TILES), :] = y[:, c * LANES:(c + 1) * LANES]

    @pl.when(nvalid == 0)
    def _():
        y_ref[...] = jnp.zeros_like(y_ref)


def _experts(tables, xs, w1, b1, w2, b2, layer):
    def blk(b, *_):
        return (b, 0)

    def bias(b, be, *_):
        return (layer, be[b], 0, 0)

    grid_spec = pltpu.PrefetchScalarGridSpec(
        num_scalar_prefetch=7,
        grid=(N_BLOCKS,),
        in_specs=[
            pl.BlockSpec((EXP_BLOCK * ROW_TILES, LANES), blk),
            pl.BlockSpec(memory_space=pl.ANY),
            pl.BlockSpec((None, None, 1, 2 * D_FF), bias),
            pl.BlockSpec(memory_space=pl.ANY),
            pl.BlockSpec((None, None, 1, D_MODEL), bias),
        ],
        out_specs=pl.BlockSpec((EXP_BLOCK * ROW_TILES, LANES), blk),
        scratch_shapes=[pltpu.VMEM((W_SLOTS, D_MODEL, 2 * D_FF), F32), pltpu.VMEM((W_SLOTS, D_FF, D_MODEL), F32),
                        pltpu.VMEM((D_MODEL, 2 * D_FF), BF16), pltpu.VMEM((D_FF, D_MODEL), BF16),
                        pltpu.SemaphoreType.DMA((2, W_SLOTS))],
    )
    return pl.pallas_call(
        functools.partial(_expert_kernel, layer=layer),
        grid_spec=grid_spec,
        out_shape=jax.ShapeDtypeStruct((N_ROWS * ROW_TILES, LANES), F32),
        compiler_params=_params(1),
        name="experts",
    )(*tables, xs, w1, b1.reshape(DEPTH, N_EXPERTS, 1, 2 * D_FF), w2,
      b2.reshape(DEPTH, N_EXPERTS, 1, D_MODEL))


def _split_bf16(x):
    hi = x.astype(BF16)
    return hi, (x - hi.astype(F32)).astype(BF16)


P_TILES = N_P // ROUTE_TM


def _combine_kernel(cnt_ref, off_ref, row_ref, q_ref, w_ref, y_ref, xm_ref, mod_ref, fg_ref,
                    *refs, final):
    buf_ref, sem = refs[-2:]
    tile = pl.program_id(0)
    slot = tile % 2
    buf = buf_ref.at[slot]

    def fetch(t, s):
        def start(local_row, global_row, n_rows):
            pltpu.make_async_copy(_rows(y_ref, global_row, n_rows), _rows(buf_ref.at[s], local_row, n_rows),
                                  sem.at[s]).start()

        _slab_pieces(t, cnt_ref, off_ref, row_ref, start)

    @pl.when(tile == 0)
    def _():
        fetch(tile, slot)

    @pl.when(tile + 1 < N_TILES)
    def _():
        fetch(tile + 1, 1 - slot)

    s_hi, s_lo = _split_bf16(_onehot_rows(q_ref[...], w_ref[...]))
    _wait_tile_rows(y_ref, buf, sem.at[slot])
    rows = jnp.concatenate([buf[pl.ds(c, TILE_ROWS, stride=ROW_TILES), :] for c in range(ROW_TILES)],
                           axis=-1)
    r_hi, r_lo = _split_bf16(rows)
    moe = jnp.dot(s_hi, r_hi, preferred_element_type=F32)
    moe = moe + jnp.dot(s_lo, r_hi, preferred_element_type=F32)
    moe = moe + jnp.dot(s_hi, r_lo, preferred_element_type=F32)
    m = mod_ref[...]
    x = xm_ref[...] + m[5:6] * moe
    if not final:
        refs[0][...] = x
        return
    xn = x * lax.rsqrt(jnp.mean(x * x, axis=-1, keepdims=True) + EPS) * fg_ref[...]
    yp_ref, ys_ref = refs[:2]

    @pl.when(tile < P_TILES)
    def _():
        yp_ref[...] = xn

    @pl.when(tile >= P_TILES)
    def _():
        ys_ref[...] = xn


def _combine(cnt, off, rowstart, q, topw, y, xm, mods_l, final_g, final):
    def tok(w):
        return pl.BlockSpec((ROUTE_TM, w), lambda i, *_: (i, 0))

    if final:
        out_specs = [pl.BlockSpec((ROUTE_TM, D_MODEL), lambda i, *_: (jnp.minimum(i, P_TILES - 1), 0)),
                     pl.BlockSpec((ROUTE_TM, D_MODEL), lambda i, *_: (jnp.maximum(i - P_TILES, 0), 0))]
        out_shape = [jax.ShapeDtypeStruct((N_P, D_MODEL), F32), jax.ShapeDtypeStruct((N_S, D_MODEL), F32)]
    else:
        out_specs = [tok(D_MODEL)]
        out_shape = [jax.ShapeDtypeStruct((N_TOK, D_MODEL), F32)]
    grid_spec = pltpu.PrefetchScalarGridSpec(
        num_scalar_prefetch=3,
        grid=(N_TILES,),
        in_specs=[tok(TOP_K), tok(TOP_K),
                  pl.BlockSpec(memory_space=pl.ANY),
                  tok(D_MODEL),
                  pl.BlockSpec((None, N_MOD, D_MODEL), lambda i, *_: (_mod_row(i * ROUTE_TM), 0, 0)),
                  pl.BlockSpec((1, D_MODEL), lambda i, *_: (0, 0))],
        out_specs=out_specs,
        scratch_shapes=[pltpu.VMEM((2, TILE_ROWS * ROW_TILES, LANES), F32), pltpu.SemaphoreType.DMA((2,))],
    )
    return pl.pallas_call(
        functools.partial(_combine_kernel, final=final),
        grid_spec=grid_spec,
        out_shape=out_shape,
        compiler_params=_params(1),
        name="combine_final" if final else "combine",
    )(cnt, off, rowstart, q, topw, y, xm, mods_l, final_g)


def _rope_tables():
    t = np.arange(DEC_SEQ)
    row = (t // GRID_W).astype(np.float32)
    col = (t % GRID_W).astype(np.float32)
    inv = jnp.asarray(ROPE_THETA, F32) ** (-jnp.arange(ROPE_PAIRS, dtype=F32) / ROPE_PAIRS)
    ang = jnp.concatenate([jnp.asarray(row)[:, None] * inv, jnp.asarray(col)[:, None] * inv], axis=-1)
    cos = jnp.repeat(jnp.cos(ang), 2, axis=-1)
    sin = jnp.repeat(jnp.sin(ang), 2, axis=-1)
    sign = jnp.asarray(np.tile(np.array([-1.0, 1.0], np.float32), HEAD_DIM // 2))
    return jnp.tile(cos, (1, A_HEADS)), jnp.tile(sin * sign, (1, A_HEADS))


def _routing_tables(tile_cnt):
    i32 = jnp.int32
    carry = jnp.cumsum(tile_cnt, axis=0) - tile_cnt
    counts = jnp.sum(tile_cnt, axis=0)
    padded = (counts + EXP_BLOCK - 1) // EXP_BLOCK * EXP_BLOCK
    pad_end = jnp.cumsum(padded)
    pad_start = pad_end - padded
    rowstart = (pad_start[None, :] + carry).astype(i32)
    blk_row = jnp.arange(N_BLOCKS, dtype=i32) * EXP_BLOCK
    blk_exp = jnp.sum((blk_row[:, None] >= pad_end[None, :]).astype(i32), axis=1)
    blk_exp = jnp.minimum(blk_exp, N_EXPERTS - 1)
    eid = jnp.arange(N_EXPERTS, dtype=i32)

    def pick(table, idx):
        return jnp.sum(jnp.where(idx[:, None] == eid[None, :], table[None, :], 0), axis=1).astype(i32)

    blk_start = pick(pad_start, blk_exp)
    nvalid = jnp.clip(pick(counts, blk_exp) - (blk_row - blk_start), 0, EXP_BLOCK).astype(i32)
    first = jnp.logical_and(blk_row == blk_start, nvalid > 0)
    active = counts > 0
    act_rank = jnp.cumsum(active.astype(i32)) - 1
    later = jnp.logical_and(active[None, :], eid[None, :] > eid[:, None])
    nxt = jnp.min(jnp.where(later, eid[None, :], N_EXPERTS), axis=1)
    nxt = jnp.where(nxt == N_EXPERTS, -1, nxt).astype(i32)
    nxt2 = jnp.where(nxt >= 0, pick(nxt + 1, nxt) - 1, -1).astype(i32)
    blk_rank = pick(act_rank, blk_exp)
    head = jnp.logical_and(first, blk_rank == 0)
    tables = (blk_exp, nvalid, first.astype(i32), head.astype(i32), (blk_rank % W_SLOTS).astype(i32),
              pick(nxt + 1, blk_exp) - 1, pick(nxt2 + 1, blk_exp) - 1)
    return rowstart.reshape(-1), tables


def kernel(x_prompt, x_sample, cache_attn_k, cache_attn_v, cache_na_k, cache_na_v, c, c_ctx, w_mod, b_mod, norm1_g, norm2_g, w_in, b_gate, q_norm_g, k_norm_g, na_rpb, conv_w, conv_b, conv_ln_g, conv_ln_b, sgu_ln_g, sgu_ln_b, sgu_w, sgu_b, w_branch, w_out, router_w, router_b, exp_w1, exp_b1, exp_w2, exp_b2, final_g):
    stream = (x_prompt.reshape(N_P, D_MODEL), x_sample.reshape(N_S, D_MODEL), 0)
    cvec =jnp.zeros((SUBLANES, D_MODEL), F32).at[0].set(c_ctx).at[1:1 + DEC_BATCH].set(c)
    mods = _modulation(cvec, w_mod, b_mod).reshape(DEPTH, SUBLANES, N_MOD, D_MODEL)
    cos_t, sin_t = _rope_tables()
    cak = cache_attn_k.reshape(DEC_BATCH, DEPTH, PAST_LEN, A_KV)
    cav = cache_attn_v.reshape(DEC_BATCH, DEPTH, PAST_LEN, A_KV)
    cbk = cache_na_k.reshape(DEC_BATCH, DEPTH, PAST_LEN, B_W)
    cbv = cache_na_v.reshape(DEC_BATCH, DEPTH, PAST_LEN, B_W)
    w_in_b = w_in.astype(BF16)
    w_br = w_branch.astype(BF16)
    w_o = w_out.astype(BF16)
    rw_hi = router_w.astype(BF16)
    rw_lo = (router_w - rw_hi.astype(F32)).astype(BF16)
    final_g2 = final_g.reshape(1, D_MODEL)

    caches = ()
    outs = None
    for l in range(DEPTH):
        mods_l = mods[l]
        g1 = norm1_g[l].reshape(1, D_MODEL)
        gq = jnp.tile(q_norm_g[l], A_HEADS).reshape(1, A_Q)
        gk = jnp.tile(k_norm_g[l], A_KV_HEADS).reshape(1, A_KV)
        aq, ak, av, bq, bk, bv, cz, dz = _in_proj(stream, mods_l, g1, w_in_b, l)
        ya, yb, *caches = _prompt_attn(aq, ak, av, bq, bk, bv, gq, gk, caches, l)
        ya = _sample_attn(aq, ak, av, cak, cav, cos_t, sin_t, gq, gk, ya, l)
        yb = _na_attn(bq, bk, bv, cbk, cbv, _na_bias(na_rpb[l]), yb, l)
        cw = conv_w[l]
        cb = conv_b[l].reshape(1, C_WIDTH)
        cg = conv_ln_g[l].reshape(1, C_WIDTH)
        cbb = conv_ln_b[l].reshape(1, C_WIDTH)
        yc = _conv_call(cz, cw, cb, cg, cbb, SEQ, 0, BATCH)
        yc = _conv_call(cz, cw, cb, cg, cbb, DEC_SEQ, N_P // DEC_SEQ, DEC_BATCH, partial_out=yc)
        bs_full = jnp.repeat(sgu_b[l].T, SGU_GW, axis=1)
        yd = _sgu(dz, sgu_ln_g[l].reshape(1, SGU_WIDTH), sgu_ln_b[l].reshape(1, SGU_WIDTH),
                  sgu_w[l].astype(BF16), bs_full)
        xm, h2, logits = _merge(stream, ya, yb, yc, yd, mods_l, g1, w_in_b,
                                b_gate[l].reshape(1, N_BRANCH * D_MODEL), w_br[l], w_o[l],
                                norm2_g[l].reshape(1, D_MODEL), rw_hi[l], rw_lo[l],
                                router_b[l].reshape(1, N_EXPERTS), l)
        top_w, q, tile_cnt, tile_off = _route(logits)
        rowstart, tables = _routing_tables(tile_cnt.reshape(N_TILES, N_EXPERTS))
        tile_cnt = tile_cnt.reshape(-1)
        tile_off = tile_off.reshape(-1)
        xs = _dispatch(tile_cnt, tile_off, rowstart, q, h2)
        y = _experts(tables, xs, exp_w1, exp_b1, exp_w2, exp_b2, l)
        outs = _combine(tile_cnt, tile_off, rowstart, q, top_w, y, xm, mods_l, final_g2, l == DEPTH - 1)
        stream = (outs[0], outs[0], N_P)

    new_k, new_v, new_bk, new_bv = caches
    return (outs[0].reshape(BATCH, SEQ, D_MODEL), outs[1].reshape(DEC_BATCH, DEC_SEQ, D_MODEL),
            new_k.reshape(BATCH, DEPTH, SEQ, A_KV_HEADS, HEAD_DIM),
            new_v.reshape(BATCH, DEPTH, SEQ, A_KV_HEADS, HEAD_DIM),
            new_bk.reshape(BATCH, DEPTH, SEQ, B_HEADS, HEAD_DIM),
            new_bv.reshape(BATCH, DEPTH, SEQ, B_HEADS, HEAD_DIM))
```

```python
import functools

import numpy as np
import jax
import jax.numpy as jnp
from jax import lax
from jax.experimental import pallas as pl
from jax.experimental.pallas import tpu as pltpu

D_MODEL = 1024
BATCH = 32
SEQ = 256
DEPTH = 2
DEC_BATCH = 2
DEC_SEQ = 1024
PAST_LEN = 512
GRID_W = 64
HEAD_DIM = 64
A_HEADS = 4
A_KV_HEADS = 2
B_HEADS = 4
NA_ROWS = 8
NA_COLS = 16
C_WIDTH = 256
CONV_WIDTH = 31
SGU_WIDTH = 256
SGU_GROUPS = 4
SGU_CHUNK = 128
N_BRANCH = 4
BRANCH_W = 256
N_EXPERTS = 32
TOP_K = 4
D_FF = 1024
SWIGLU_ALPHA = 1.702
SWIGLU_LIMIT = 7.0
MOE_BLOCK = 128
ROPE_THETA = 10000.0
ROPE_PAIRS = HEAD_DIM // 4
N_MOD = 6
EPS = 1e-6
NEG_INF = -1e30

A_Q = A_HEADS * HEAD_DIM
A_KV = A_KV_HEADS * HEAD_DIM
B_W = B_HEADS * HEAD_DIM
MIX_SIZES = (A_Q, A_KV, A_KV, B_W, B_W, B_W, 2 * C_WIDTH, 2 * SGU_WIDTH)
MIX_COLS = sum(MIX_SIZES)

N_P = BATCH * SEQ
N_S = DEC_BATCH * DEC_SEQ
N_TOK = N_P + N_S
N_ASSIGN = N_TOK * TOP_K
EXP_BLOCK = 512
N_BLOCKS = N_ASSIGN // EXP_BLOCK + N_EXPERTS
N_ROWS = N_BLOCKS * EXP_BLOCK
GRID_ROWS = DEC_SEQ // GRID_W
NA_WR = min(NA_ROWS, GRID_ROWS)
N_LOC = NA_WR * GRID_W

SUBLANES = 8
LANES = 128
ROW_TILES = D_MODEL // LANES
VMEM_LIMIT = 56 * 1024 * 1024

F32 = jnp.float32
BF16 = jnp.bfloat16


def _params(n_axes, vmem=None):
    return pltpu.CompilerParams(
        dimension_semantics=("arbitrary",) * n_axes,
        vmem_limit_bytes=vmem if vmem is not None else VMEM_LIMIT)


def _mod_row(start):
    return jnp.where(start < N_P, 0, 1 + (start - N_P) // DEC_SEQ)


def _bdot(a, b):
    return jnp.dot(a.astype(BF16), b.astype(BF16), preferred_element_type=F32)


def _bdot_nt(a, b):
    return lax.dot_general(a.astype(BF16), b.astype(BF16), (((1,), (1,)), ((), ())),
                           preferred_element_type=F32)


def _sigmoid(x):
    return 1.0 / (1.0 + jnp.exp(-x))


MOD_TN = 1536


def _mod_kernel(c_ref, w_ref, b_ref, o_ref):
    c = c_ref[...]
    s = c * _sigmoid(c)
    o_ref[...] = _bdot(s, w_ref[...]) + b_ref[...]


def _modulation(cvec, w_mod, b_mod):
    n_col = N_MOD * D_MODEL
    return pl.pallas_call(
        _mod_kernel,
        grid=(DEPTH, n_col // MOD_TN),
        in_specs=[
            pl.BlockSpec((SUBLANES, D_MODEL), lambda l, j: (0, 0)),
            pl.BlockSpec((None, D_MODEL, MOD_TN), lambda l, j: (l, 0, j)),
            pl.BlockSpec((None, 1, MOD_TN), lambda l, j: (l, 0, j)),
        ],
        out_specs=pl.BlockSpec((None, SUBLANES, MOD_TN), lambda l, j: (l, 0, j)),
        out_shape=jax.ShapeDtypeStruct((DEPTH, SUBLANES, n_col), F32),
        compiler_params=_params(2),
        name="modulation",
    )(cvec, w_mod, b_mod.reshape(DEPTH, 1, n_col))


IN_TM = 512


def _norm_mod(x, g, shift, scale):
    y = x * lax.rsqrt(jnp.mean(x * x, axis=-1, keepdims=True) + EPS) * g
    return y * (1.0 + scale) + shift


def _stream_specs(tm, stream):
    p_tiles = N_P // tm
    s_first = stream[2] // tm
    return [pl.BlockSpec((tm, D_MODEL), lambda i: (jnp.minimum(i, p_tiles - 1), 0)),
            pl.BlockSpec((tm, D_MODEL), lambda i: (jnp.maximum(i - p_tiles, 0) + s_first, 0))]


def _stream_tile(xp_ref, xs_ref, tm):
    return jnp.where(pl.program_id(0) < N_P // tm, xp_ref[...], xs_ref[...])


def _in_kernel(xp_ref, xs_ref, mod_ref, g_ref, w_ref, *out_refs):
    m = mod_ref[...]
    h = _norm_mod(_stream_tile(xp_ref, xs_ref, IN_TM), g_ref[...], m[0:1], m[1:2])
    z = jnp.dot(h.astype(BF16), w_ref[...], preferred_element_type=F32)
    off = 0
    for o_ref, sz in zip(out_refs, MIX_SIZES):
        o_ref[...] = z[:, off:off + sz]
        off += sz


def _in_proj(stream, mods_l, g1, w_in_b, layer):
    return pl.pallas_call(
        _in_kernel,
        grid=(N_TOK // IN_TM,),
        in_specs=_stream_specs(IN_TM, stream) + [
            pl.BlockSpec((None, N_MOD, D_MODEL), lambda i: (_mod_row(i * IN_TM), 0, 0)),
            pl.BlockSpec((1, D_MODEL), lambda i: (0, 0)),
            pl.BlockSpec((None, D_MODEL, MIX_COLS), lambda i: (layer, 0, 0)),
        ],
        out_specs=[pl.BlockSpec((IN_TM, sz), lambda i: (i, 0)) for sz in MIX_SIZES],
        out_shape=[jax.ShapeDtypeStruct((N_TOK, sz), F32) for sz in MIX_SIZES],
        compiler_params=_params(1),
        name="in_proj",
    )(stream[0], stream[1], mods_l, g1, w_in_b)


def _head_rms(x, g):
    n_heads = x.shape[-1] // HEAD_DIM
    seg = lax.broadcasted_iota(jnp.int32, x.shape, 1) // HEAD_DIM
    xx = x * x
    inv = jnp.zeros_like(x)
    for h in range(n_heads):
        ms = jnp.sum(jnp.where(seg == h, xx, 0.0), axis=-1, keepdims=True) * (1.0 / HEAD_DIM)
        inv = jnp.where(seg == h, lax.rsqrt(ms + EPS), inv)
    return x * inv * g


def _softmax_pv(score_parts, value_parts):
    m = score_parts[0].max(axis=-1, keepdims=True)
    for s in score_parts[1:]:
        m = jnp.maximum(m, s.max(axis=-1, keepdims=True))
    den = None
    acc = None
    for s, v in zip(score_parts, value_parts):
        e = jnp.exp(s - m)
        d = e.sum(axis=-1, keepdims=True)
        a = _bdot(e, v)
        den = d if den is None else den + d
        acc = a if acc is None else acc + a
    return acc / den


def _head(x, h):
    return x[:, h * HEAD_DIM:(h + 1) * HEAD_DIM]


SCALE = HEAD_DIM ** -0.5


def _prompt_attn_kernel(aq_ref, ak_ref, av_ref, bq_ref, bk_ref, bv_ref, gq_ref, gk_ref, *refs):
    ya_ref, yb_ref, nk_ref, nv_ref, nbk_ref, nbv_ref = refs[-6:]
    aq = _head_rms(aq_ref[...], gq_ref[...])
    ak = _head_rms(ak_ref[...], gk_ref[...])
    av = av_ref[...]
    for ref, val in ((nk_ref, ak), (nv_ref, av), (nbk_ref, bk_ref[...]), (nbv_ref, bv_ref[...])):
        n_heads = val.shape[-1] // HEAD_DIM
        for h in range(n_heads):
            ref[pl.ds(h, SEQ, stride=n_heads), :] = _head(val, h)
    grp = A_HEADS // A_KV_HEADS
    outs = []
    for h in range(A_HEADS):
        s = _bdot_nt(_head(aq, h), _head(ak, h // grp)) * SCALE
        outs.append(_softmax_pv([s], [_head(av, h // grp)]))
    ya_ref[...] = jnp.concatenate(outs, axis=-1)
    bq = bq_ref[...]
    bk = bk_ref[...]
    bv = bv_ref[...]
    outs = []
    for h in range(B_HEADS):
        s = _bdot_nt(_head(bq, h), _head(bk, h)) * SCALE
        outs.append(_softmax_pv([s], [_head(bv, h)]))
    yb_ref[...] = jnp.concatenate(outs, axis=-1)


def _prompt_attn(aq, ak, av, bq, bk, bv, gq, gk, caches, layer):
    def spec(w):
        return pl.BlockSpec((SEQ, w), lambda b: (b, 0))

    def cache_spec(n_heads):
        return pl.BlockSpec((SEQ * n_heads, HEAD_DIM), lambda b: (b * DEPTH + layer, 0))

    cache_heads = (A_KV_HEADS, A_KV_HEADS, B_HEADS, B_HEADS)
    n_in = 8
    return pl.pallas_call(
        _prompt_attn_kernel,
        grid=(BATCH,),
        in_specs=[spec(A_Q), spec(A_KV), spec(A_KV), spec(B_W), spec(B_W), spec(B_W),
                  pl.BlockSpec((1, A_Q), lambda b: (0, 0)),
                  pl.BlockSpec((1, A_KV), lambda b: (0, 0))]
        + [pl.BlockSpec(memory_space=pl.ANY) for _ in caches],
        out_specs=[spec(A_Q), spec(B_W)] + [cache_spec(n) for n in cache_heads],
        out_shape=[jax.ShapeDtypeStruct((N_TOK, A_Q), F32),
                   jax.ShapeDtypeStruct((N_TOK, B_W), F32)]
        + [jax.ShapeDtypeStruct((BATCH * DEPTH * SEQ * n, HEAD_DIM), F32) for n in cache_heads],
        input_output_aliases={n_in + j: 2 + j for j in range(len(caches))},
        compiler_params=_params(1),
        name="prompt_attn",
    )(aq, ak, av, bq, bk, bv, gq, gk, *caches)


QB = 128


def _rope(x, cos, sin_signed):
    n = x.shape[-1]
    nxt = pltpu.roll(x, n - 1, 1)
    prv = pltpu.roll(x, 1, 1)
    even = (lax.broadcasted_iota(jnp.int32, x.shape, 1) % 2) == 0
    return x * cos + jnp.where(even, nxt, prv) * sin_signed


def _sample_attn_kernel(q_ref, k_ref, v_ref, ck_ref, cv_ref, cosq_ref, sinq_ref, cosk_ref, sink_ref,
                        gq_ref, gk_ref, ya_prompt_ref, o_ref):
    del ya_prompt_ref
    q = _rope(_head_rms(q_ref[...], gq_ref[...]), cosq_ref[...], sinq_ref[...])
    k = _rope(_head_rms(k_ref[...], gk_ref[...]), cosk_ref[...], sink_ref[...])
    v = v_ref[...]
    ck = ck_ref[...]
    cv = cv_ref[...]
    grp = A_HEADS // A_KV_HEADS
    outs = []
    for h in range(A_HEADS):
        j = h // grp
        qh = _head(q, h)
        s1 = _bdot_nt(qh, _head(k, j)) * SCALE
        s2 = _bdot_nt(qh, _head(ck, j)) * SCALE
        outs.append(_softmax_pv([s1, s2], [_head(v, j), _head(cv, j)]))
    o_ref[...] = jnp.concatenate(outs, axis=-1)


def _sample_attn(aq, ak, av, cache_k, cache_v, cos_t, sin_t, gq, gk, ya, layer):
    nqb = DEC_SEQ // QB
    q0 = N_P // QB
    k0 = N_P // DEC_SEQ
    return pl.pallas_call(
        _sample_attn_kernel,
        grid=(DEC_BATCH, nqb),
        in_specs=[
            pl.BlockSpec((QB, A_Q), lambda b, i: (q0 + b * nqb + i, 0)),
            pl.BlockSpec((DEC_SEQ, A_KV), lambda b, i: (k0 + b, 0)),
            pl.BlockSpec((DEC_SEQ, A_KV), lambda b, i: (k0 + b, 0)),
            pl.BlockSpec((None, None, PAST_LEN, A_KV), lambda b, i: (b, layer, 0, 0)),
            pl.BlockSpec((None, None, PAST_LEN, A_KV), lambda b, i: (b, layer, 0, 0)),
            pl.BlockSpec((QB, A_Q), lambda b, i: (i, 0)),
            pl.BlockSpec((QB, A_Q), lambda b, i: (i, 0)),
            pl.BlockSpec((DEC_SEQ, A_KV), lambda b, i: (0, 0)),
            pl.BlockSpec((DEC_SEQ, A_KV), lambda b, i: (0, 0)),
            pl.BlockSpec((1, A_Q), lambda b, i: (0, 0)),
            pl.BlockSpec((1, A_KV), lambda b, i: (0, 0)),
            pl.BlockSpec(memory_space=pl.ANY),
        ],
        out_specs=pl.BlockSpec((QB, A_Q), lambda b, i: (q0 + b * nqb + i, 0)),
        out_shape=jax.ShapeDtypeStruct((N_TOK, A_Q), F32),
        input_output_aliases={11: 0},
        compiler_params=_params(2),
        name="sample_attn",
    )(aq, ak, av, cache_k, cache_v, cos_t, sin_t, cos_t, sin_t, gq, gk, ya)


N_ROW_OFF = 2 * NA_ROWS - 1
N_COL_OFF = 2 * NA_COLS - 1
NA_PAIRS = N_ROW_OFF - 1
assert NA_WR == NA_ROWS and NA_WR % 2 == 0 and 2 * GRID_W == LANES


def _na_bias_kernel(rpb_ref, o_ref):
    h = pl.program_id(0)
    qc = lax.broadcasted_iota(jnp.int32, (GRID_W, LANES), 0)
    lane = lax.broadcasted_iota(jnp.int32, (GRID_W, LANES), 1)
    right = lane >= GRID_W
    kc = jnp.where(right, lane - GRID_W, lane)
    c_start = jnp.clip(qc - NA_COLS // 2, 0, GRID_W - NA_COLS)
    col_in = jnp.logical_and(kc >= c_start, kc < c_start + NA_COLS)
    col_off = jnp.clip(kc - qc + NA_COLS - 1, 0, N_COL_OFF - 1)
    for p in range(NA_PAIRS):
        acc = jnp.zeros((GRID_W, LANES), F32)
        for o in range(N_COL_OFF):
            left_v = rpb_ref[(h * N_ROW_OFF + p) * N_COL_OFF + o]
            right_v = rpb_ref[(h * N_ROW_OFF + p + 1) * N_COL_OFF + o]
            acc = jnp.where(col_off == o, jnp.where(right, right_v, left_v), acc)
        o_ref[p] = jnp.where(col_in, acc, NEG_INF)


def _na_bias(rpb):
    return pl.pallas_call(
        _na_bias_kernel,
        grid_spec=pltpu.PrefetchScalarGridSpec(
            num_scalar_prefetch=1,
            grid=(B_HEADS,),
            in_specs=[],
            out_specs=pl.BlockSpec((None, NA_PAIRS, GRID_W, LANES), lambda h, *_: (h, 0, 0, 0)),
        ),
        out_shape=jax.ShapeDtypeStruct((B_HEADS, NA_PAIRS, GRID_W, LANES), F32),
        compiler_params=_params(1),
        name="na_bias",
    )(rpb.reshape(-1))


def _na_kernel(q_ref, k_ref, v_ref, ck_ref, cv_ref, bias_ref, yb_prompt_ref, o_ref):
    del yb_prompt_ref
    r = pl.program_id(1)
    r_start = jnp.clip(r - NA_WR // 2, 0, GRID_ROWS - NA_WR)
    base = pl.multiple_of(r_start * GRID_W, GRID_W)
    row_off0 = r_start - r + NA_ROWS - 1
    q = q_ref[...]
    kb = k_ref[pl.ds(base, N_LOC), :]
    vb = v_ref[pl.ds(base, N_LOC), :]
    ck = ck_ref[...]
    cv = cv_ref[...]
    outs = []
    for h in range(B_HEADS):
        qh = _head(q, h)
        bias = jnp.concatenate([bias_ref[h, row_off0 + 2 * j] for j in range(NA_WR // 2)], axis=-1)
        s1 = _bdot_nt(qh, _head(kb, h)) * SCALE + bias
        s2 = _bdot_nt(qh, _head(ck, h)) * SCALE
        outs.append(_softmax_pv([s1, s2], [_head(vb, h), _head(cv, h)]))
    o_ref[...] = jnp.concatenate(outs, axis=-1)


def _na_attn(bq, bk, bv, cache_k, cache_v, bias, yb, layer):
    q0 = N_P // GRID_W
    k0 = N_P // DEC_SEQ
    return pl.pallas_call(
        _na_kernel,
        grid=(DEC_BATCH, GRID_ROWS),
        in_specs=[
            pl.BlockSpec((GRID_W, B_W), lambda b, r: (q0 + b * GRID_ROWS + r, 0)),
            pl.BlockSpec((DEC_SEQ, B_W), lambda b, r: (k0 + b, 0)),
            pl.BlockSpec((DEC_SEQ, B_W), lambda b, r: (k0 + b, 0)),
            pl.BlockSpec((None, None, PAST_LEN, B_W), lambda b, r: (b, layer, 0, 0)),
            pl.BlockSpec((None, None, PAST_LEN, B_W), lambda b, r: (b, layer, 0, 0)),
            pl.BlockSpec((B_HEADS, NA_PAIRS, GRID_W, LANES), lambda b, r: (0, 0, 0, 0)),
            pl.BlockSpec(memory_space=pl.ANY),
        ],
        out_specs=pl.BlockSpec((GRID_W, B_W), lambda b, r: (q0 + b * GRID_ROWS + r, 0)),
        out_shape=jax.ShapeDtypeStruct((N_TOK, B_W), F32),
        input_output_aliases={6: 0},
        compiler_params=_params(2),
        name="na_attn",
    )(bq, bk, bv, cache_k, cache_v, bias, yb)


CONV_PAD = 16
CONV_CHUNK = 64


def _layer_norm(x, g, b):
    mu = jnp.mean(x, axis=-1, keepdims=True)
    xc = x - mu
    var = jnp.mean(xc * xc, axis=-1, keepdims=True)
    return xc * lax.rsqrt(var + EPS) * g + b


def _conv_kernel(z_ref, w_ref, cb_ref, g_ref, b_ref, *refs, s_len):
    o_ref, pad_ref = refs[-2:]
    z = z_ref[...]
    u = z[:, :C_WIDTH] * _sigmoid(z[:, C_WIDTH:])
    pad_ref[pl.ds(0, CONV_PAD), :] = jnp.zeros((CONV_PAD, C_WIDTH), F32)
    pad_ref[pl.ds(CONV_PAD + s_len, CONV_PAD), :] = jnp.zeros((CONV_PAD, C_WIDTH), F32)
    pad_ref[pl.ds(CONV_PAD, s_len), :] = u
    w = w_ref[...]
    shift = CONV_PAD - CONV_WIDTH // 2

    def chunk(c, carry):
        base = pl.multiple_of(c * CONV_CHUNK, CONV_CHUNK)
        acc = jnp.zeros((CONV_CHUNK, C_WIDTH), F32)
        for r in range(SUBLANES):
            part = None
            for k in range(CONV_WIDTH):
                if (k + shift) % SUBLANES != r:
                    continue
                rows = pad_ref[pl.ds(base + (k + shift - r), CONV_CHUNK + SUBLANES), :]
                term = rows * w[k:k + 1]
                part = term if part is None else part + term
            if part is not None:
                acc = acc + part[r:r + CONV_CHUNK]
        y = _layer_norm(acc + cb_ref[...], g_ref[...], b_ref[...])
        o_ref[pl.ds(base, CONV_CHUNK), :] = y * _sigmoid(y)
        return carry

    lax.fori_loop(0, s_len // CONV_CHUNK, chunk, 0)


def _conv_call(cz, w, cb, g, b, s_len, first_blk, n_seq, partial_out=None):
    vec = pl.BlockSpec((1, C_WIDTH), lambda i: (0, 0))
    extra = [] if partial_out is None else [partial_out]
    return pl.pallas_call(
        functools.partial(_conv_kernel, s_len=s_len),
        grid=(n_seq,),
        in_specs=[pl.BlockSpec((s_len, 2 * C_WIDTH), lambda i: (first_blk + i, 0)),
                  pl.BlockSpec((CONV_WIDTH, C_WIDTH), lambda i: (0, 0)), vec, vec, vec]
        + [pl.BlockSpec(memory_space=pl.ANY) for _ in extra],
        out_specs=pl.BlockSpec((s_len, C_WIDTH), lambda i: (first_blk + i, 0)),
        out_shape=jax.ShapeDtypeStruct((N_TOK, C_WIDTH), F32),
        input_output_aliases={5: 0} if extra else {},
        scratch_shapes=[pltpu.VMEM((s_len + 2 * CONV_PAD, C_WIDTH), F32)],
        compiler_params=_params(1),
        name="conformer_conv_%d" % s_len,
    )(cz, w, cb, g, b, *extra)


SGU_TM = 512
SGU_GW = SGU_WIDTH // SGU_GROUPS


def _sgu_kernel(z_ref, g_ref, b_ref, ws_ref, bs_ref, o_ref):
    z = z_ref[...]
    z = 0.5 * z * (1.0 + lax.erf(z * (2.0 ** -0.5)))
    u = z[:, :SGU_WIDTH]
    v = _layer_norm(z[:, SGU_WIDTH:], g_ref[...], b_ref[...])
    for c in range(SGU_TM // SGU_CHUNK):
        vc = v[c * SGU_CHUNK:(c + 1) * SGU_CHUNK]
        parts = [_bdot(ws_ref[g], vc[:, g * SGU_GW:(g + 1) * SGU_GW]) for g in range(SGU_GROUPS)]
        mixed = jnp.concatenate(parts, axis=-1) + bs_ref[...]
        o_ref[pl.ds(c * SGU_CHUNK, SGU_CHUNK), :] = u[c * SGU_CHUNK:(c + 1) * SGU_CHUNK] * mixed


def _sgu(dz, g, b, ws, bs_full):
    vec = pl.BlockSpec((1, SGU_WIDTH), lambda i: (0, 0))
    return pl.pallas_call(
        _sgu_kernel,
        grid=(N_TOK // SGU_TM,),
        in_specs=[pl.BlockSpec((SGU_TM, 2 * SGU_WIDTH), lambda i: (i, 0)), vec, vec,
                  pl.BlockSpec((SGU_GROUPS, SGU_CHUNK, SGU_CHUNK), lambda i: (0, 0, 0)),
                  pl.BlockSpec((SGU_CHUNK, SGU_WIDTH), lambda i: (0, 0))],
        out_specs=pl.BlockSpec((SGU_TM, SGU_WIDTH), lambda i: (i, 0)),
        out_shape=jax.ShapeDtypeStruct((N_TOK, SGU_WIDTH), F32),
        compiler_params=_params(1),
        name="chunk_sgu",
    )(dz, g, b, ws, bs_full)


MERGE_TM = 512


def _merge_kernel(xp_ref, xs_ref, ya_ref, yb_ref, yc_ref, yd_ref, mod_ref, g1_ref, w_in_hbm, bg_ref, wb_ref, wo_ref,
                  g2_ref, rwh_ref, rwl_ref, rb_ref, xm_ref, h2_ref, w_ref, q_ref, cnt_ref, off_ref,
                  wg_ref, sem, *, layer):
    @pl.when(pl.program_id(0) == 0)
    def _():
        cp = pltpu.make_async_copy(w_in_hbm.at[layer, :, pl.ds(MIX_COLS, N_BRANCH * D_MODEL)], wg_ref, sem)
        cp.start()
        cp.wait()

    m = mod_ref[...]
    x = _stream_tile(xp_ref, xs_ref, MERGE_TM)
    h = _norm_mod(x, g1_ref[...], m[0:1], m[1:2]).astype(BF16)
    merged = None
    for i, y_ref in enumerate((ya_ref, yb_ref, yc_ref, yd_ref)):
        logit = jnp.dot(h, wg_ref[:, i * D_MODEL:(i + 1) * D_MODEL], preferred_element_type=F32)
        gate = _sigmoid(logit + bg_ref[:, i * D_MODEL:(i + 1) * D_MODEL])
        term = gate * jnp.dot(y_ref[...].astype(BF16), wb_ref[i], preferred_element_type=F32)
        merged = term if merged is None else merged + term
    out = jnp.dot(merged.astype(BF16), wo_ref[...], preferred_element_type=F32)
    xm = x + m[2:3] * out
    xm_ref[...] = xm
    h2 = _norm_mod(xm, g2_ref[...], m[3:4], m[4:5])
    h2_hi = h2.astype(BF16)
    h2_lo = (h2 - h2_hi.astype(F32)).astype(BF16)
    lg = jnp.dot(h2_hi, rwh_ref[...], preferred_element_type=F32)
    lg = lg + jnp.dot(h2_hi, rwl_ref[...], preferred_element_type=F32)
    lg = lg + jnp.dot(h2_lo, rwh_ref[...], preferred_element_type=F32)
    lg = lg + rb_ref[...]
    h2_ref[...] = h2_hi
    for j in range(MERGE_TM // ROUTE_TM):
        rows = slice(j * ROUTE_TM, (j + 1) * ROUTE_TM)
        w_out, q_out, cnt, off = _route_tile(lg[rows])
        w_ref[rows, :] = w_out
        q_ref[rows, :] = q_out
        cnt_ref[j] = cnt
        off_ref[j] = off


def _merge(stream, ya, yb, yc, yd, mods_l, g1, w_in_b, bg, wb, wo, g2, rwh, rwl, rb, layer):
    def tok(w):
        return pl.BlockSpec((MERGE_TM, w), lambda i: (i, 0))

    def full(*shape):
        return pl.BlockSpec(shape, lambda i: (0,) * len(shape))

    tile_rows = pl.BlockSpec((MERGE_TM // ROUTE_TM, 1, N_EXPERTS), lambda i: (i, 0, 0))
    return pl.pallas_call(
        functools.partial(_merge_kernel, layer=layer),
        grid=(N_TOK // MERGE_TM,),
        in_specs=_stream_specs(MERGE_TM, stream) + [
                  tok(BRANCH_W), tok(BRANCH_W), tok(BRANCH_W), tok(BRANCH_W),
                  pl.BlockSpec((None, N_MOD, D_MODEL), lambda i: (_mod_row(i * MERGE_TM), 0, 0)),
                  full(1, D_MODEL), pl.BlockSpec(memory_space=pl.ANY), full(1, N_BRANCH * D_MODEL),
                  full(N_BRANCH, BRANCH_W, D_MODEL), full(D_MODEL, D_MODEL), full(1, D_MODEL),
                  full(D_MODEL, N_EXPERTS), full(D_MODEL, N_EXPERTS), full(1, N_EXPERTS)],
        out_specs=[tok(D_MODEL), tok(D_MODEL), tok(TOP_K), tok(TOP_K), tile_rows, tile_rows],
        out_shape=[jax.ShapeDtypeStruct((N_TOK, D_MODEL), F32),
                   jax.ShapeDtypeStruct((N_TOK, D_MODEL), BF16),
                   jax.ShapeDtypeStruct((N_TOK, TOP_K), F32),
                   jax.ShapeDtypeStruct((N_TOK, TOP_K), jnp.int32),
                   jax.ShapeDtypeStruct((N_TILES, 1, N_EXPERTS), jnp.int32),
                   jax.ShapeDtypeStruct((N_TILES, 1, N_EXPERTS), jnp.int32)],
        scratch_shapes=[pltpu.VMEM((D_MODEL, N_BRANCH * D_MODEL), BF16), pltpu.SemaphoreType.DMA(())],
        compiler_params=_params(1),
        name="merge",
    )(stream[0], stream[1], ya, yb, yc, yd, mods_l, g1, w_in_b, bg, wb, wo, g2, rwh, rwl, rb)


ROUTE_TM = 256
TILE_ROWS = ROUTE_TM * TOP_K
N_TILES = N_TOK // ROUTE_TM


def _route_tile(lg):
    lane = lax.broadcasted_iota(jnp.int32, lg.shape, 1)
    sels, vals = [], []
    for _ in range(TOP_K):
        mx = lg.max(axis=-1, keepdims=True)
        idx = jnp.where(lg == mx, lane, N_EXPERTS).min(axis=-1, keepdims=True)
        sel = lane == idx
        sels.append(sel)
        vals.append(mx)
        lg = jnp.where(sel, -jnp.inf, lg)
    exps = [jnp.exp(v - vals[0]) for v in vals]
    den = exps[0] + exps[1] + exps[2] + exps[3]
    onehot = jnp.zeros(lg.shape, F32)
    for sel in sels:
        onehot = onehot + sel.astype(F32)
    row = lax.broadcasted_iota(jnp.int32, (ROUTE_TM, ROUTE_TM), 0)
    col = lax.broadcasted_iota(jnp.int32, (ROUTE_TM, ROUTE_TM), 1)
    tri = jnp.where(col < row, 1.0, 0.0).astype(BF16)
    rank = jnp.dot(tri, onehot.astype(BF16), preferred_element_type=F32)
    cnt = jnp.sum(onehot, axis=0, keepdims=True)
    erow = lax.broadcasted_iota(jnp.int32, (N_EXPERTS, N_EXPERTS), 0)
    ecol = lax.broadcasted_iota(jnp.int32, (N_EXPERTS, N_EXPERTS), 1)
    upper = jnp.where(erow < ecol, 1.0, 0.0).astype(BF16)
    off = jnp.dot(jnp.broadcast_to(cnt, (SUBLANES, N_EXPERTS)).astype(BF16), upper,
                  preferred_element_type=F32)[0:1]
    slot = rank + off
    k_lane = lax.broadcasted_iota(jnp.int32, (ROUTE_TM, TOP_K), 1)
    w_out = jnp.zeros((ROUTE_TM, TOP_K), F32)
    q_out = jnp.zeros((ROUTE_TM, TOP_K), F32)
    for k in range(TOP_K):
        w_out = jnp.where(k_lane == k, exps[k] / den, w_out)
        qk = jnp.sum(jnp.where(sels[k], slot, 0.0), axis=-1, keepdims=True)
        q_out = jnp.where(k_lane == k, qk, q_out)
    return w_out, q_out.astype(jnp.int32), cnt.astype(jnp.int32), off.astype(jnp.int32)


SLAB_CHUNK = 16


def _slab_pieces(tile, cnt_ref, off_ref, row_ref, fn):
    def per_expert(e, carry):
        n = cnt_ref[tile * N_EXPERTS + e]
        src = off_ref[tile * N_EXPERTS + e]
        dst = row_ref[tile * N_EXPERTS + e]
        n_full = n // SLAB_CHUNK

        def full(j, c):
            fn(src + j * SLAB_CHUNK, dst + j * SLAB_CHUNK, SLAB_CHUNK)
            return c

        lax.fori_loop(0, n_full, full, 0)
        rem = n - n_full * SLAB_CHUNK
        bit = SLAB_CHUNK // 2
        while bit >= 1:
            start = n_full * SLAB_CHUNK + (rem & ~(2 * bit - 1))

            @pl.when((rem & bit) != 0)
            def _(start=start, bit=bit):
                fn(src + start, dst + start, bit)

            bit //= 2
        return carry

    lax.fori_loop(0, N_EXPERTS, per_expert, 0)


def _rows(ref, row, n_rows):
    start = row * ROW_TILES
    if not isinstance(row, int):
        start = pl.multiple_of(start, ROW_TILES)
    return ref.at[pl.ds(start, n_rows * ROW_TILES)]


def _onehot_rows(q, values=None):
    lane = lax.broadcasted_iota(jnp.int32, (ROUTE_TM, TILE_ROWS), 1)
    s = jnp.zeros((ROUTE_TM, TILE_ROWS), F32)
    for k in range(TOP_K):
        v = 1.0 if values is None else values[:, k:k + 1]
        s = jnp.where(lane == q[:, k:k + 1], v, s)
    return s


def _wait_tile_rows(hbm_ref, buf_slot_ref, sem_slot):
    pltpu.make_async_copy(_rows(hbm_ref, 0, TILE_ROWS), buf_slot_ref, sem_slot).wait()


def _dispatch_kernel(cnt_ref, off_ref, row_ref, q_ref, h2_ref, xs_ref, buf_ref, sem):
    tile = pl.program_id(0)
    slot = tile % 2
    buf = buf_ref.at[slot]

    @pl.when(tile >= 2)
    def _():
        _wait_tile_rows(xs_ref, buf, sem.at[slot])

    sel = _onehot_rows(q_ref[...]).astype(BF16)
    xg = lax.dot_general(sel, h2_ref[...], (((0,), (0,)), ((), ())), preferred_element_type=F32)
    for c in range(ROW_TILES):
        buf[pl.ds(c, TILE_ROWS, stride=ROW_TILES), :] = xg[:, c * LANES:(c + 1) * LANES]

    def start(local_row, global_row, n_rows):
        pltpu.make_async_copy(_rows(buf, local_row, n_rows), _rows(xs_ref, global_row, n_rows),
                              sem.at[slot]).start()

    _slab_pieces(tile, cnt_ref, off_ref, row_ref, start)

    @pl.when(tile == N_TILES - 1)
    def _():
        _wait_tile_rows(xs_ref, buf, sem.at[slot])
        _wait_tile_rows(xs_ref, buf_ref.at[1 - slot], sem.at[1 - slot])


def _dispatch(cnt, off, rowstart, q, h2):
    grid_spec = pltpu.PrefetchScalarGridSpec(
        num_scalar_prefetch=3,
        grid=(N_TILES,),
        in_specs=[pl.BlockSpec((ROUTE_TM, TOP_K), lambda i, *_: (i, 0)),
                  pl.BlockSpec((ROUTE_TM, D_MODEL), lambda i, *_: (i, 0))],
        out_specs=pl.BlockSpec(memory_space=pl.ANY),
        scratch_shapes=[pltpu.VMEM((2, TILE_ROWS * ROW_TILES, LANES), F32), pltpu.SemaphoreType.DMA((2,))],
    )
    return pl.pallas_call(
        _dispatch_kernel,
        grid_spec=grid_spec,
        out_shape=jax.ShapeDtypeStruct((N_ROWS * ROW_TILES, LANES), F32),
        compiler_params=_params(1),
        name="dispatch",
    )(cnt, off, rowstart, q, h2)


CAST_ROWS = 128
W_SLOTS = 2


def _expert_kernel(blk_exp_ref, nvalid_ref, first_ref, head_ref, slot_ref, next_ref,
                   xs_ref, w1_hbm, b1_ref, w2_hbm, b2_ref, y_ref,
                   w1f_ref, w2f_ref, w1b_ref, w2b_ref, sem, *, layer):
    b = pl.program_id(0)
    e = blk_exp_ref[b]
    nvalid = nvalid_ref[b]
    slot = slot_ref[b]

    def start_weights(expert, ahead):
        s = (slot + ahead) % W_SLOTS
        pltpu.make_async_copy(w1_hbm.at[layer, expert], w1f_ref.at[s], sem.at[0, s]).start()
        pltpu.make_async_copy(w2_hbm.at[layer, expert], w2f_ref.at[s], sem.at[1, s]).start()

    @pl.when(first_ref[b] == 1)
    def _():
        @pl.when(head_ref[b] == 1)
        def _():
            start_weights(e, 0)

        pltpu.make_async_copy(w1_hbm.at[layer, e], w1f_ref.at[slot], sem.at[0, slot]).wait()
        pltpu.make_async_copy(w2_hbm.at[layer, e], w2f_ref.at[slot], sem.at[1, slot]).wait()

        @pl.when(next_ref[b] >= 0)
        def _():
            start_weights(next_ref[b], 1)

        def cast1(i, carry):
            r = pl.multiple_of(i * CAST_ROWS, CAST_ROWS)
            w1b_ref[pl.ds(r, CAST_ROWS), :] = w1f_ref[slot, pl.ds(r, CAST_ROWS), :].astype(BF16)
            return carry

        def cast2(i, carry):
            r = pl.multiple_of(i * CAST_ROWS, CAST_ROWS)
            w2b_ref[pl.ds(r, CAST_ROWS), :] = w2f_ref[slot, pl.ds(r, CAST_ROWS), :].astype(BF16)
            return carry

        lax.fori_loop(0, D_MODEL // CAST_ROWS, cast1, 0)
        lax.fori_loop(0, D_FF // CAST_ROWS, cast2, 0)

    def run_rows(n_rows):
        valid = lax.broadcasted_iota(jnp.int32, (n_rows, LANES), 0) < nvalid
        chunks = [jnp.where(valid, xs_ref[pl.ds(c, n_rows, stride=ROW_TILES), :], 0.0).astype(BF16)
                  for c in range(ROW_TILES)]
        xb = jnp.concatenate(chunks, axis=-1)
        hid = jnp.dot(xb, w1b_ref[...], preferred_element_type=F32) + b1_ref[...]
        glu = jnp.minimum(hid[:, :D_FF], SWIGLU_LIMIT)
        lin = jnp.clip(hid[:, D_FF:], -SWIGLU_LIMIT, SWIGLU_LIMIT)
        act = glu * _sigmoid(SWIGLU_ALPHA * glu) * (lin + 1.0)
        y = jnp.dot(act.astype(BF16), w2b_ref[...], preferred_element_type=F32) + b2_ref[...]
        for c in range(ROW_TILES):
            y_ref[pl.ds(c, n_rows, stride=ROW_TILES), :] = y[:, c * LANES:(c + 1) * LANES]
        if n_rows < EXP_BLOCK:
            rest = (EXP_BLOCK - n_rows) * ROW_TILES
            y_ref[pl.ds(n_rows * ROW_TILES, rest), :] = jnp.zeros((rest, LANES), F32)

    @pl.when(nvalid > EXP_BLOCK // 2)
    def _():
        run_rows(EXP_BLOCK)

    @pl.when(jnp.logical_and(nvalid > 0, nvalid <= EXP_BLOCK // 2))
    def _():
        run_rows(EXP_BLOCK // 2)

    @pl.when(nvalid == 0)
    def _():
        y_ref[...] = jnp.zeros_like(y_ref)


def _experts(tables, xs, w1, b1, w2, b2, layer):
    def blk(b, *_):
        return (b, 0)

    def bias(b, be, *_):
        return (layer, be[b], 0, 0)

    grid_spec = pltpu.PrefetchScalarGridSpec(
        num_scalar_prefetch=6,
        grid=(N_BLOCKS,),
        in_specs=[
            pl.BlockSpec((EXP_BLOCK * ROW_TILES, LANES), blk),
            pl.BlockSpec(memory_space=pl.ANY),
            pl.BlockSpec((None, None, 1, 2 * D_FF), bias),
            pl.BlockSpec(memory_space=pl.ANY),
            pl.BlockSpec((None, None, 1, D_MODEL), bias),
        ],
        out_specs=pl.BlockSpec((EXP_BLOCK * ROW_TILES, LANES), blk),
        scratch_shapes=[pltpu.VMEM((W_SLOTS, D_MODEL, 2 * D_FF), F32), pltpu.VMEM((W_SLOTS, D_FF, D_MODEL), F32),
                        pltpu.VMEM((D_MODEL, 2 * D_FF), BF16), pltpu.VMEM((D_FF, D_MODEL), BF16),
                        pltpu.SemaphoreType.DMA((2, W_SLOTS))],
    )
    return pl.pallas_call(
        functools.partial(_expert_kernel, layer=layer),
        grid_spec=grid_spec,
        out_shape=jax.ShapeDtypeStruct((N_ROWS * ROW_TILES, LANES), F32),
        compiler_params=_params(1),
        name="experts",
    )(*tables, xs, w1, b1.reshape(DEPTH, N_EXPERTS, 1, 2 * D_FF), w2,
      b2.reshape(DEPTH, N_EXPERTS, 1, D_MODEL))


def _split_bf16(x):
    hi = x.astype(BF16)
    return hi, (x - hi.astype(F32)).astype(BF16)


P_TILES = N_P // ROUTE_TM


def _combine_kernel(cnt_ref, off_ref, row_ref, q_ref, w_ref, y_ref, xm_ref, mod_ref, fg_ref,
                    *refs, final):
    buf_ref, sem = refs[-2:]
    tile = pl.program_id(0)
    slot = tile % 2
    buf = buf_ref.at[slot]

    def fetch(t, s):
        def start(local_row, global_row, n_rows):
            pltpu.make_async_copy(_rows(y_ref, global_row, n_rows), _rows(buf_ref.at[s], local_row, n_rows),
                                  sem.at[s]).start()

        _slab_pieces(t, cnt_ref, off_ref, row_ref, start)

    @pl.when(tile == 0)
    def _():
        fetch(tile, slot)

    @pl.when(tile + 1 < N_TILES)
    def _():
        fetch(tile + 1, 1 - slot)

    s_hi, s_lo = _split_bf16(_onehot_rows(q_ref[...], w_ref[...]))
    _wait_tile_rows(y_ref, buf, sem.at[slot])
    rows = jnp.concatenate([buf[pl.ds(c, TILE_ROWS, stride=ROW_TILES), :] for c in range(ROW_TILES)],
                           axis=-1)
    r_hi, r_lo = _split_bf16(rows)
    moe = jnp.dot(s_hi, r_hi, preferred_element_type=F32)
    moe = moe + jnp.dot(s_lo, r_hi, preferred_element_type=F32)
    moe = moe + jnp.dot(s_hi, r_lo, preferred_element_type=F32)
    m = mod_ref[...]
    x = xm_ref[...] + m[5:6] * moe
    if not final:
        refs[0][...] = x
        return
    xn = x * lax.rsqrt(jnp.mean(x * x, axis=-1, keepdims=True) + EPS) * fg_ref[...]
    yp_ref, ys_ref = refs[:2]

    @pl.when(tile < P_TILES)
    def _():
        yp_ref[...] = xn

    @pl.when(tile >= P_TILES)
    def _():
        ys_ref[...] = xn


def _combine(cnt, off, rowstart, q, topw, y, xm, mods_l, final_g, final):
    def tok(w):
        return pl.BlockSpec((ROUTE_TM, w), lambda i, *_: (i, 0))

    if final:
        out_specs = [pl.BlockSpec((ROUTE_TM, D_MODEL), lambda i, *_: (jnp.minimum(i, P_TILES - 1), 0)),
                     pl.BlockSpec((ROUTE_TM, D_MODEL), lambda i, *_: (jnp.maximum(i - P_TILES, 0), 0))]
        out_shape = [jax.ShapeDtypeStruct((N_P, D_MODEL), F32), jax.ShapeDtypeStruct((N_S, D_MODEL), F32)]
    else:
        out_specs = [tok(D_MODEL)]
        out_shape = [jax.ShapeDtypeStruct((N_TOK, D_MODEL), F32)]
    grid_spec = pltpu.PrefetchScalarGridSpec(
        num_scalar_prefetch=3,
        grid=(N_TILES,),
        in_specs=[tok(TOP_K), tok(TOP_K),
                  pl.BlockSpec(memory_space=pl.ANY),
                  tok(D_MODEL),
                  pl.BlockSpec((None, N_MOD, D_MODEL), lambda i, *_: (_mod_row(i * ROUTE_TM), 0, 0)),
                  pl.BlockSpec((1, D_MODEL), lambda i, *_: (0, 0))],
        out_specs=out_specs,
        scratch_shapes=[pltpu.VMEM((2, TILE_ROWS * ROW_TILES, LANES), F32), pltpu.SemaphoreType.DMA((2,))],
    )
    return pl.pallas_call(
        functools.partial(_combine_kernel, final=final),
        grid_spec=grid_spec,
        out_shape=out_shape,
        compiler_params=_params(1),
        name="combine_final" if final else "combine",
    )(cnt, off, rowstart, q, topw, y, xm, mods_l, final_g)


def _rope_tables():
    t = np.arange(DEC_SEQ)
    row = (t // GRID_W).astype(np.float32)
    col = (t % GRID_W).astype(np.float32)
    inv = jnp.asarray(ROPE_THETA, F32) ** (-jnp.arange(ROPE_PAIRS, dtype=F32) / ROPE_PAIRS)
    ang = jnp.concatenate([jnp.asarray(row)[:, None] * inv, jnp.asarray(col)[:, None] * inv], axis=-1)
    cos = jnp.repeat(jnp.cos(ang), 2, axis=-1)
    sin = jnp.repeat(jnp.sin(ang), 2, axis=-1)
    sign = jnp.asarray(np.tile(np.array([-1.0, 1.0], np.float32), HEAD_DIM // 2))
    return jnp.tile(cos, (1, A_HEADS)), jnp.tile(sin * sign, (1, A_HEADS))


def _routing_tables(tile_cnt):
    i32 = jnp.int32
    carry = jnp.cumsum(tile_cnt, axis=0) - tile_cnt
    counts = jnp.sum(tile_cnt, axis=0)
    padded = (counts + EXP_BLOCK - 1) // EXP_BLOCK * EXP_BLOCK
    pad_end = jnp.cumsum(padded)
    pad_start = pad_end - padded
    rowstart = (pad_start[None, :] + carry).astype(i32)
    blk_row = jnp.arange(N_BLOCKS, dtype=i32) * EXP_BLOCK
    blk_exp = jnp.sum((blk_row[:, None] >= pad_end[None, :]).astype(i32), axis=1)
    blk_exp = jnp.minimum(blk_exp, N_EXPERTS - 1)
    eid = jnp.arange(N_EXPERTS, dtype=i32)

    def pick(table, idx):
        return jnp.sum(jnp.where(idx[:, None] == eid[None, :], table[None, :], 0), axis=1).astype(i32)

    blk_start = pick(pad_start, blk_exp)
    nvalid = jnp.clip(pick(counts, blk_exp) - (blk_row - blk_start), 0, EXP_BLOCK).astype(i32)
    first = jnp.logical_and(blk_row == blk_start, nvalid > 0)
    active = counts > 0
    act_rank = jnp.cumsum(active.astype(i32)) - 1
    later = jnp.logical_and(active[None, :], eid[None, :] > eid[:, None])
    nxt = jnp.min(jnp.where(later, eid[None, :], N_EXPERTS), axis=1)
    nxt = jnp.where(nxt == N_EXPERTS, -1, nxt).astype(i32)
    blk_rank = pick(act_rank, blk_exp)
    head = jnp.logical_and(first, blk_rank == 0)
    tables = (blk_exp, nvalid, first.astype(i32), head.astype(i32), (blk_rank % W_SLOTS).astype(i32),
              pick(nxt + 1, blk_exp) - 1)
    return rowstart.reshape(-1), tables


def kernel(x_prompt, x_sample, cache_attn_k, cache_attn_v, cache_na_k, cache_na_v, c, c_ctx, w_mod, b_mod, norm1_g, norm2_g, w_in, b_gate, q_norm_g, k_norm_g, na_rpb, conv_w, conv_b, conv_ln_g, conv_ln_b, sgu_ln_g, sgu_ln_b, sgu_w, sgu_b, w_branch, w_out, router_w, router_b, exp_w1, exp_b1, exp_w2, exp_b2, final_g):
    stream = (x_prompt.reshape(N_P, D_MODEL), x_sample.reshape(N_S, D_MODEL), 0)
    cvec =jnp.zeros((SUBLANES, D_MODEL), F32).at[0].set(c_ctx).at[1:1 + DEC_BATCH].set(c)
    mods = _modulation(cvec, w_mod, b_mod).reshape(DEPTH, SUBLANES, N_MOD, D_MODEL)
    cos_t, sin_t = _rope_tables()
    cak = cache_attn_k.reshape(DEC_BATCH, DEPTH, PAST_LEN, A_KV)
    cav = cache_attn_v.reshape(DEC_BATCH, DEPTH, PAST_LEN, A_KV)
    cbk = cache_na_k.reshape(DEC_BATCH, DEPTH, PAST_LEN, B_W)
    cbv = cache_na_v.reshape(DEC_BATCH, DEPTH, PAST_LEN, B_W)
    w_in_b = w_in.astype(BF16)
    w_br = w_branch.astype(BF16)
    w_o = w_out.astype(BF16)
    rw_hi = router_w.astype(BF16)
    rw_lo = (router_w - rw_hi.astype(F32)).astype(BF16)
    final_g2 = final_g.reshape(1, D_MODEL)

    caches = ()
    outs = None
    for l in range(DEPTH):
        mods_l = mods[l]
        g1 = norm1_g[l].reshape(1, D_MODEL)
        gq = jnp.tile(q_norm_g[l], A_HEADS).reshape(1, A_Q)
        gk = jnp.tile(k_norm_g[l], A_KV_HEADS).reshape(1, A_KV)
        aq, ak, av, bq, bk, bv, cz, dz = _in_proj(stream, mods_l, g1, w_in_b, l)
        ya, yb, *caches = _prompt_attn(aq, ak, av, bq, bk, bv, gq, gk, caches, l)
        ya = _sample_attn(aq, ak, av, cak, cav, cos_t, sin_t, gq, gk, ya, l)
        yb = _na_attn(bq, bk, bv, cbk, cbv, _na_bias(na_rpb[l]), yb, l)
        cw = conv_w[l]
        cb = conv_b[l].reshape(1, C_WIDTH)
        cg = conv_ln_g[l].reshape(1, C_WIDTH)
        cbb = conv_ln_b[l].reshape(1, C_WIDTH)
        yc = _conv_call(cz, cw, cb, cg, cbb, SEQ, 0, BATCH)
        yc = _conv_call(cz, cw, cb, cg, cbb, DEC_SEQ, N_P // DEC_SEQ, DEC_BATCH, partial_out=yc)
        bs_full = jnp.repeat(sgu_b[l].T, SGU_GW, axis=1)
        yd = _sgu(dz, sgu_ln_g[l].reshape(1, SGU_WIDTH), sgu_ln_b[l].reshape(1, SGU_WIDTH),
                  sgu_w[l].astype(BF16), bs_full)
        xm, h2, top_w, q, tile_cnt, tile_off = _merge(
            stream, ya, yb, yc, yd, mods_l, g1, w_in_b, b_gate[l].reshape(1, N_BRANCH * D_MODEL), w_br[l], w_o[l],
            norm2_g[l].reshape(1, D_MODEL), rw_hi[l], rw_lo[l], router_b[l].reshape(1, N_EXPERTS), l)
        rowstart, tables = _routing_tables(tile_cnt.reshape(N_TILES, N_EXPERTS))
        tile_cnt = tile_cnt.reshape(-1)
        tile_off = tile_off.reshape(-1)
        xs = _dispatch(tile_cnt, tile_off, rowstart, q, h2)
        y = _experts(tables, xs, exp_w1, exp_b1, exp_w2, exp_b2, l)
        outs = _combine(tile_cnt, tile_off, rowstart, q, top_w, y, xm, mods_l, final_g2, l == DEPTH - 1)
        stream = (outs[0], outs[0], N_P)

    new_k, new_v, new_bk, new_bv = caches
    return (outs[0].reshape(BATCH, SEQ, D_MODEL), outs[1].reshape(DEC_BATCH, DEC_SEQ, D_MODEL),
            new_k.reshape(BATCH, DEPTH, SEQ, A_KV_HEADS, HEAD_DIM),
            new_v.reshape(BATCH, DEPTH, SEQ, A_KV_HEADS, HEAD_DIM),
            new_bk.reshape(BATCH, DEPTH, SEQ, B_HEADS, HEAD_DIM),
            new_bv.reshape(BATCH, DEPTH, SEQ, B_HEADS, HEAD_DIM))
```

```python
import functools

import numpy as np
import jax
import jax.numpy as jnp
from jax import lax
from jax.experimental import pallas as pl
from jax.experimental.pallas import tpu as pltpu

D_MODEL = 1024
BATCH = 32
SEQ = 256
DEPTH = 2
DEC_BATCH = 2
DEC_SEQ = 1024
PAST_LEN = 512
GRID_W = 64
HEAD_DIM = 64
A_HEADS = 4
A_KV_HEADS = 2
B_HEADS = 4
NA_ROWS = 8
NA_COLS = 16
C_WIDTH = 256
CONV_WIDTH = 31
SGU_WIDTH = 256
SGU_GROUPS = 4
SGU_CHUNK = 128
N_BRANCH = 4
BRANCH_W = 256
N_EXPERTS = 32
TOP_K = 4
D_FF = 1024
SWIGLU_ALPHA = 1.702
SWIGLU_LIMIT = 7.0
MOE_BLOCK = 128
ROPE_THETA = 10000.0
ROPE_PAIRS = HEAD_DIM // 4
N_MOD = 6
EPS = 1e-6
NEG_INF = -1e30

A_Q = A_HEADS * HEAD_DIM
A_KV = A_KV_HEADS * HEAD_DIM
B_W = B_HEADS * HEAD_DIM
MIX_SIZES = (A_Q, A_KV, A_KV, B_W, B_W, B_W, 2 * C_WIDTH, 2 * SGU_WIDTH)
MIX_COLS = sum(MIX_SIZES)

N_P = BATCH * SEQ
N_S = DEC_BATCH * DEC_SEQ
N_TOK = N_P + N_S
N_ASSIGN = N_TOK * TOP_K
EXP_BLOCK = 512
N_BLOCKS = N_ASSIGN // EXP_BLOCK + N_EXPERTS
N_ROWS = N_BLOCKS * EXP_BLOCK
GRID_ROWS = DEC_SEQ // GRID_W
NA_WR = min(NA_ROWS, GRID_ROWS)
N_LOC = NA_WR * GRID_W

SUBLANES = 8
LANES = 128
ROW_TILES = D_MODEL // LANES
VMEM_LIMIT = 56 * 1024 * 1024

F32 = jnp.float32
BF16 = jnp.bfloat16


def _params(n_axes, vmem=None):
    return pltpu.CompilerParams(
        dimension_semantics=("arbitrary",) * n_axes,
        vmem_limit_bytes=vmem if vmem is not None else VMEM_LIMIT)


def _mod_row(start):
    return jnp.where(start < N_P, 0, 1 + (start - N_P) // DEC_SEQ)


def _bdot(a, b):
    return jnp.dot(a.astype(BF16), b.astype(BF16), preferred_element_type=F32)


def _bdot_nt(a, b):
    return lax.dot_general(a.astype(BF16), b.astype(BF16), (((1,), (1,)), ((), ())),
                           preferred_element_type=F32)


def _sigmoid(x):
    return 1.0 / (1.0 + jnp.exp(-x))


MOD_TN = 1536


def _mod_kernel(c_ref, w_ref, b_ref, o_ref):
    c = c_ref[...]
    s = c * _sigmoid(c)
    o_ref[...] = _bdot(s, w_ref[...]) + b_ref[...]


def _modulation(cvec, w_mod, b_mod):
    n_col = N_MOD * D_MODEL
    return pl.pallas_call(
        _mod_kernel,
        grid=(DEPTH, n_col // MOD_TN),
        in_specs=[
            pl.BlockSpec((SUBLANES, D_MODEL), lambda l, j: (0, 0)),
            pl.BlockSpec((None, D_MODEL, MOD_TN), lambda l, j: (l, 0, j)),
            pl.BlockSpec((None, 1, MOD_TN), lambda l, j: (l, 0, j)),
        ],
        out_specs=pl.BlockSpec((None, SUBLANES, MOD_TN), lambda l, j: (l, 0, j)),
        out_shape=jax.ShapeDtypeStruct((DEPTH, SUBLANES, n_col), F32),
        compiler_params=_params(2),
        name="modulation",
    )(cvec, w_mod, b_mod.reshape(DEPTH, 1, n_col))


IN_TM = 512


def _norm_mod(x, g, shift, scale):
    y = x * lax.rsqrt(jnp.mean(x * x, axis=-1, keepdims=True) + EPS) * g
    return y * (1.0 + scale) + shift


def _stream_specs(tm, stream):
    p_tiles = N_P // tm
    s_first = stream[2] // tm
    return [pl.BlockSpec((tm, D_MODEL), lambda i: (jnp.minimum(i, p_tiles - 1), 0)),
            pl.BlockSpec((tm, D_MODEL), lambda i: (jnp.maximum(i - p_tiles, 0) + s_first, 0))]


def _stream_tile(xp_ref, xs_ref, tm):
    return jnp.where(pl.program_id(0) < N_P // tm, xp_ref[...], xs_ref[...])


def _in_kernel(xp_ref, xs_ref, mod_ref, g_ref, w_ref, *out_refs):
    m = mod_ref[...]
    h = _norm_mod(_stream_tile(xp_ref, xs_ref, IN_TM), g_ref[...], m[0:1], m[1:2])
    z = jnp.dot(h.astype(BF16), w_ref[...], preferred_element_type=F32)
    off = 0
    for o_ref, sz in zip(out_refs, MIX_SIZES):
        o_ref[...] = z[:, off:off + sz]
        off += sz


def _in_proj(stream, mods_l, g1, w_in_b, layer):
    return pl.pallas_call(
        _in_kernel,
        grid=(N_TOK // IN_TM,),
        in_specs=_stream_specs(IN_TM, stream) + [
            pl.BlockSpec((None, N_MOD, D_MODEL), lambda i: (_mod_row(i * IN_TM), 0, 0)),
            pl.BlockSpec((1, D_MODEL), lambda i: (0, 0)),
            pl.BlockSpec((None, D_MODEL, MIX_COLS), lambda i: (layer, 0, 0)),
        ],
        out_specs=[pl.BlockSpec((IN_TM, sz), lambda i: (i, 0)) for sz in MIX_SIZES],
        out_shape=[jax.ShapeDtypeStruct((N_TOK, sz), F32) for sz in MIX_SIZES],
        compiler_params=_params(1),
        name="in_proj",
    )(stream[0], stream[1], mods_l, g1, w_in_b)


def _head_rms(x, g):
    n_heads = x.shape[-1] // HEAD_DIM
    seg = lax.broadcasted_iota(jnp.int32, x.shape, 1) // HEAD_DIM
    xx = x * x
    inv = jnp.zeros_like(x)
    for h in range(n_heads):
        ms = jnp.sum(jnp.where(seg == h, xx, 0.0), axis=-1, keepdims=True) * (1.0 / HEAD_DIM)
        inv = jnp.where(seg == h, lax.rsqrt(ms + EPS), inv)
    return x * inv * g


def _softmax_pv(score_parts, value_parts):
    m = score_parts[0].max(axis=-1, keepdims=True)
    for s in score_parts[1:]:
        m = jnp.maximum(m, s.max(axis=-1, keepdims=True))
    den = None
    acc = None
    for s, v in zip(score_parts, value_parts):
        e = jnp.exp(s - m)
        d = e.sum(axis=-1, keepdims=True)
        a = _bdot(e, v)
        den = d if den is None else den + d
        acc = a if acc is None else acc + a
    return acc / den


def _head(x, h):
    return x[:, h * HEAD_DIM:(h + 1) * HEAD_DIM]


SCALE = HEAD_DIM ** -0.5


def _prompt_attn_kernel(aq_ref, ak_ref, av_ref, bq_ref, bk_ref, bv_ref, gq_ref, gk_ref, *refs):
    ya_ref, yb_ref, nk_ref, nv_ref, nbk_ref, nbv_ref = refs[-6:]
    aq = _head_rms(aq_ref[...], gq_ref[...])
    ak = _head_rms(ak_ref[...], gk_ref[...])
    av = av_ref[...]
    for ref, val in ((nk_ref, ak), (nv_ref, av), (nbk_ref, bk_ref[...]), (nbv_ref, bv_ref[...])):
        n_heads = val.shape[-1] // HEAD_DIM
        for h in range(n_heads):
            ref[pl.ds(h, SEQ, stride=n_heads), :] = _head(val, h)
    grp = A_HEADS // A_KV_HEADS
    outs = []
    for h in range(A_HEADS):
        s = _bdot_nt(_head(aq, h), _head(ak, h // grp)) * SCALE
        outs.append(_softmax_pv([s], [_head(av, h // grp)]))
    ya_ref[...] = jnp.concatenate(outs, axis=-1)
    bq = bq_ref[...]
    bk = bk_ref[...]
    bv = bv_ref[...]
    outs = []
    for h in range(B_HEADS):
        s = _bdot_nt(_head(bq, h), _head(bk, h)) * SCALE
        outs.append(_softmax_pv([s], [_head(bv, h)]))
    yb_ref[...] = jnp.concatenate(outs, axis=-1)


def _prompt_attn(aq, ak, av, bq, bk, bv, gq, gk, caches, layer):
    def spec(w):
        return pl.BlockSpec((SEQ, w), lambda b: (b, 0))

    def cache_spec(n_heads):
        return pl.BlockSpec((SEQ * n_heads, HEAD_DIM), lambda b: (b * DEPTH + layer, 0))

    cache_heads = (A_KV_HEADS, A_KV_HEADS, B_HEADS, B_HEADS)
    n_in = 8
    return pl.pallas_call(
        _prompt_attn_kernel,
        grid=(BATCH,),
        in_specs=[spec(A_Q), spec(A_KV), spec(A_KV), spec(B_W), spec(B_W), spec(B_W),
                  pl.BlockSpec((1, A_Q), lambda b: (0, 0)),
                  pl.BlockSpec((1, A_KV), lambda b: (0, 0))]
        + [pl.BlockSpec(memory_space=pl.ANY) for _ in caches],
        out_specs=[spec(A_Q), spec(B_W)] + [cache_spec(n) for n in cache_heads],
        out_shape=[jax.ShapeDtypeStruct((N_TOK, A_Q), F32),
                   jax.ShapeDtypeStruct((N_TOK, B_W), F32)]
        + [jax.ShapeDtypeStruct((BATCH * DEPTH * SEQ * n, HEAD_DIM), F32) for n in cache_heads],
        input_output_aliases={n_in + j: 2 + j for j in range(len(caches))},
        compiler_params=_params(1),
        name="prompt_attn",
    )(aq, ak, av, bq, bk, bv, gq, gk, *caches)


QB = 128


def _rope(x, cos, sin_signed):
    n = x.shape[-1]
    nxt = pltpu.roll(x, n - 1, 1)
    prv = pltpu.roll(x, 1, 1)
    even = (lax.broadcasted_iota(jnp.int32, x.shape, 1) % 2) == 0
    return x * cos + jnp.where(even, nxt, prv) * sin_signed


def _sample_attn_kernel(q_ref, k_ref, v_ref, ck_ref, cv_ref, cosq_ref, sinq_ref, cosk_ref, sink_ref,
                        gq_ref, gk_ref, ya_prompt_ref, o_ref):
    del ya_prompt_ref
    q = _rope(_head_rms(q_ref[...], gq_ref[...]), cosq_ref[...], sinq_ref[...])
    k = _rope(_head_rms(k_ref[...], gk_ref[...]), cosk_ref[...], sink_ref[...])
    v = v_ref[...]
    ck = ck_ref[...]
    cv = cv_ref[...]
    grp = A_HEADS // A_KV_HEADS
    outs = []
    for h in range(A_HEADS):
        j = h // grp
        qh = _head(q, h)
        s1 = _bdot_nt(qh, _head(k, j)) * SCALE
        s2 = _bdot_nt(qh, _head(ck, j)) * SCALE
        outs.append(_softmax_pv([s1, s2], [_head(v, j), _head(cv, j)]))
    o_ref[...] = jnp.concatenate(outs, axis=-1)


def _sample_attn(aq, ak, av, cache_k, cache_v, cos_t, sin_t, gq, gk, ya, layer):
    nqb = DEC_SEQ // QB
    q0 = N_P // QB
    k0 = N_P // DEC_SEQ
    return pl.pallas_call(
        _sample_attn_kernel,
        grid=(DEC_BATCH, nqb),
        in_specs=[
            pl.BlockSpec((QB, A_Q), lambda b, i: (q0 + b * nqb + i, 0)),
            pl.BlockSpec((DEC_SEQ, A_KV), lambda b, i: (k0 + b, 0)),
            pl.BlockSpec((DEC_SEQ, A_KV), lambda b, i: (k0 + b, 0)),
            pl.BlockSpec((None, None, PAST_LEN, A_KV), lambda b, i: (b, layer, 0, 0)),
            pl.BlockSpec((None, None, PAST_LEN, A_KV), lambda b, i: (b, layer, 0, 0)),
            pl.BlockSpec((QB, A_Q), lambda b, i: (i, 0)),
            pl.BlockSpec((QB, A_Q), lambda b, i: (i, 0)),
            pl.BlockSpec((DEC_SEQ, A_KV), lambda b, i: (0, 0)),
            pl.BlockSpec((DEC_SEQ, A_KV), lambda b, i: (0, 0)),
            pl.BlockSpec((1, A_Q), lambda b, i: (0, 0)),
            pl.BlockSpec((1, A_KV), lambda b, i: (0, 0)),
            pl.BlockSpec(memory_space=pl.ANY),
        ],
        out_specs=pl.BlockSpec((QB, A_Q), lambda b, i: (q0 + b * nqb + i, 0)),
        out_shape=jax.ShapeDtypeStruct((N_TOK, A_Q), F32),
        input_output_aliases={11: 0},
        compiler_params=_params(2),
        name="sample_attn",
    )(aq, ak, av, cache_k, cache_v, cos_t, sin_t, cos_t, sin_t, gq, gk, ya)


N_ROW_OFF = 2 * NA_ROWS - 1
N_COL_OFF = 2 * NA_COLS - 1
NA_PAIRS = N_ROW_OFF - 1
assert NA_WR == NA_ROWS and NA_WR % 2 == 0 and 2 * GRID_W == LANES


def _na_bias_kernel(rpb_ref, o_ref):
    h = pl.program_id(0)
    qc = lax.broadcasted_iota(jnp.int32, (GRID_W, LANES), 0)
    lane = lax.broadcasted_iota(jnp.int32, (GRID_W, LANES), 1)
    right = lane >= GRID_W
    kc = jnp.where(right, lane - GRID_W, lane)
    c_start = jnp.clip(qc - NA_COLS // 2, 0, GRID_W - NA_COLS)
    col_in = jnp.logical_and(kc >= c_start, kc < c_start + NA_COLS)
    col_off = jnp.clip(kc - qc + NA_COLS - 1, 0, N_COL_OFF - 1)
    for p in range(NA_PAIRS):
        acc = jnp.zeros((GRID_W, LANES), F32)
        for o in range(N_COL_OFF):
            left_v = rpb_ref[(h * N_ROW_OFF + p) * N_COL_OFF + o]
            right_v = rpb_ref[(h * N_ROW_OFF + p + 1) * N_COL_OFF + o]
            acc = jnp.where(col_off == o, jnp.where(right, right_v, left_v), acc)
        o_ref[p] = jnp.where(col_in, acc, NEG_INF)


def _na_bias(rpb):
    return pl.pallas_call(
        _na_bias_kernel,
        grid_spec=pltpu.PrefetchScalarGridSpec(
            num_scalar_prefetch=1,
            grid=(B_HEADS,),
            in_specs=[],
            out_specs=pl.BlockSpec((None, NA_PAIRS, GRID_W, LANES), lambda h, *_: (h, 0, 0, 0)),
        ),
        out_shape=jax.ShapeDtypeStruct((B_HEADS, NA_PAIRS, GRID_W, LANES), F32),
        compiler_params=_params(1),
        name="na_bias",
    )(rpb.reshape(-1))


def _na_kernel(q_ref, k_ref, v_ref, ck_ref, cv_ref, bias_ref, yb_prompt_ref, o_ref):
    del yb_prompt_ref
    r = pl.program_id(1)
    r_start = jnp.clip(r - NA_WR // 2, 0, GRID_ROWS - NA_WR)
    base = pl.multiple_of(r_start * GRID_W, GRID_W)
    row_off0 = r_start - r + NA_ROWS - 1
    q = q_ref[...]
    kb = k_ref[pl.ds(base, N_LOC), :]
    vb = v_ref[pl.ds(base, N_LOC), :]
    ck = ck_ref[...]
    cv = cv_ref[...]
    outs = []
    for h in range(B_HEADS):
        qh = _head(q, h)
        bias = jnp.concatenate([bias_ref[h, row_off0 + 2 * j] for j in range(NA_WR // 2)], axis=-1)
        s1 = _bdot_nt(qh, _head(kb, h)) * SCALE + bias
        s2 = _bdot_nt(qh, _head(ck, h)) * SCALE
        outs.append(_softmax_pv([s1, s2], [_head(vb, h), _head(cv, h)]))
    o_ref[...] = jnp.concatenate(outs, axis=-1)


def _na_attn(bq, bk, bv, cache_k, cache_v, bias, yb, layer):
    q0 = N_P // GRID_W
    k0 = N_P // DEC_SEQ
    return pl.pallas_call(
        _na_kernel,
        grid=(DEC_BATCH, GRID_ROWS),
        in_specs=[
            pl.BlockSpec((GRID_W, B_W), lambda b, r: (q0 + b * GRID_ROWS + r, 0)),
            pl.BlockSpec((DEC_SEQ, B_W), lambda b, r: (k0 + b, 0)),
            pl.BlockSpec((DEC_SEQ, B_W), lambda b, r: (k0 + b, 0)),
            pl.BlockSpec((None, None, PAST_LEN, B_W), lambda b, r: (b, layer, 0, 0)),
            pl.BlockSpec((None, None, PAST_LEN, B_W), lambda b, r: (b, layer, 0, 0)),
            pl.BlockSpec((B_HEADS, NA_PAIRS, GRID_W, LANES), lambda b, r: (0, 0, 0, 0)),
            pl.BlockSpec(memory_space=pl.ANY),
        ],
        out_specs=pl.BlockSpec((GRID_W, B_W), lambda b, r: (q0 + b * GRID_ROWS + r, 0)),
        out_shape=jax.ShapeDtypeStruct((N_TOK, B_W), F32),
        input_output_aliases={6: 0},
        compiler_params=_params(2),
        name="na_attn",
    )(bq, bk, bv, cache_k, cache_v, bias, yb)


CONV_PAD = 16
CONV_CHUNK = 64


def _layer_norm(x, g, b):
    mu = jnp.mean(x, axis=-1, keepdims=True)
    xc = x - mu
    var = jnp.mean(xc * xc, axis=-1, keepdims=True)
    return xc * lax.rsqrt(var + EPS) * g + b


def _conv_kernel(z_ref, w_ref, cb_ref, g_ref, b_ref, *refs, s_len):
    o_ref, pad_ref = refs[-2:]
    z = z_ref[...]
    u = z[:, :C_WIDTH] * _sigmoid(z[:, C_WIDTH:])
    pad_ref[pl.ds(0, CONV_PAD), :] = jnp.zeros((CONV_PAD, C_WIDTH), F32)
    pad_ref[pl.ds(CONV_PAD + s_len, CONV_PAD), :] = jnp.zeros((CONV_PAD, C_WIDTH), F32)
    pad_ref[pl.ds(CONV_PAD, s_len), :] = u
    w = w_ref[...]
    shift = CONV_PAD - CONV_WIDTH // 2

    def chunk(c, carry):
        base = pl.multiple_of(c * CONV_CHUNK, CONV_CHUNK)
        acc = jnp.zeros((CONV_CHUNK, C_WIDTH), F32)
        for r in range(SUBLANES):
            part = None
            for k in range(CONV_WIDTH):
                if (k + shift) % SUBLANES != r:
                    continue
                rows = pad_ref[pl.ds(base + (k + shift - r), CONV_CHUNK + SUBLANES), :]
                term = rows * w[k:k + 1]
                part = term if part is None else part + term
            if part is not None:
                acc = acc + part[r:r + CONV_CHUNK]
        y = _layer_norm(acc + cb_ref[...], g_ref[...], b_ref[...])
        o_ref[pl.ds(base, CONV_CHUNK), :] = y * _sigmoid(y)
        return carry

    lax.fori_loop(0, s_len // CONV_CHUNK, chunk, 0)


def _conv_call(cz, w, cb, g, b, s_len, first_blk, n_seq, partial_out=None):
    vec = pl.BlockSpec((1, C_WIDTH), lambda i: (0, 0))
    extra = [] if partial_out is None else [partial_out]
    return pl.pallas_call(
        functools.partial(_conv_kernel, s_len=s_len),
        grid=(n_seq,),
        in_specs=[pl.BlockSpec((s_len, 2 * C_WIDTH), lambda i: (first_blk + i, 0)),
                  pl.BlockSpec((CONV_WIDTH, C_WIDTH), lambda i: (0, 0)), vec, vec, vec]
        + [pl.BlockSpec(memory_space=pl.ANY) for _ in extra],
        out_specs=pl.BlockSpec((s_len, C_WIDTH), lambda i: (first_blk + i, 0)),
        out_shape=jax.ShapeDtypeStruct((N_TOK, C_WIDTH), F32),
        input_output_aliases={5: 0} if extra else {},
        scratch_shapes=[pltpu.VMEM((s_len + 2 * CONV_PAD, C_WIDTH), F32)],
        compiler_params=_params(1),
        name="conformer_conv_%d" % s_len,
    )(cz, w, cb, g, b, *extra)


SGU_TM = 512
SGU_GW = SGU_WIDTH // SGU_GROUPS


def _sgu_kernel(z_ref, g_ref, b_ref, ws_ref, bs_ref, o_ref):
    z = z_ref[...]
    z = 0.5 * z * (1.0 + lax.erf(z * (2.0 ** -0.5)))
    u = z[:, :SGU_WIDTH]
    v = _layer_norm(z[:, SGU_WIDTH:], g_ref[...], b_ref[...])
    for c in range(SGU_TM // SGU_CHUNK):
        vc = v[c * SGU_CHUNK:(c + 1) * SGU_CHUNK]
        parts = [_bdot(ws_ref[g], vc[:, g * SGU_GW:(g + 1) * SGU_GW]) for g in range(SGU_GROUPS)]
        mixed = jnp.concatenate(parts, axis=-1) + bs_ref[...]
        o_ref[pl.ds(c * SGU_CHUNK, SGU_CHUNK), :] = u[c * SGU_CHUNK:(c + 1) * SGU_CHUNK] * mixed


def _sgu(dz, g, b, ws, bs_full):
    vec = pl.BlockSpec((1, SGU_WIDTH), lambda i: (0, 0))
    return pl.pallas_call(
        _sgu_kernel,
        grid=(N_TOK // SGU_TM,),
        in_specs=[pl.BlockSpec((SGU_TM, 2 * SGU_WIDTH), lambda i: (i, 0)), vec, vec,
                  pl.BlockSpec((SGU_GROUPS, SGU_CHUNK, SGU_CHUNK), lambda i: (0, 0, 0)),
                  pl.BlockSpec((SGU_CHUNK, SGU_WIDTH), lambda i: (0, 0))],
        out_specs=pl.BlockSpec((SGU_TM, SGU_WIDTH), lambda i: (i, 0)),
        out_shape=jax.ShapeDtypeStruct((N_TOK, SGU_WIDTH), F32),
        compiler_params=_params(1),
        name="chunk_sgu",
    )(dz, g, b, ws, bs_full)


MERGE_TM = 512


def _merge_kernel(xp_ref, xs_ref, ya_ref, yb_ref, yc_ref, yd_ref, mod_ref, g1_ref, w_in_hbm, bg_ref, wb_ref, wo_ref,
                  g2_ref, rwh_ref, rwl_ref, rb_ref, xm_ref, h2_ref, w_ref, q_ref, cnt_ref, off_ref,
                  wg_ref, sem, *, layer):
    @pl.when(pl.program_id(0) == 0)
    def _():
        cp = pltpu.make_async_copy(w_in_hbm.at[layer, :, pl.ds(MIX_COLS, N_BRANCH * D_MODEL)], wg_ref, sem)
        cp.start()
        cp.wait()

    m = mod_ref[...]
    x = _stream_tile(xp_ref, xs_ref, MERGE_TM)
    h = _norm_mod(x, g1_ref[...], m[0:1], m[1:2]).astype(BF16)
    merged = None
    for i, y_ref in enumerate((ya_ref, yb_ref, yc_ref, yd_ref)):
        logit = jnp.dot(h, wg_ref[:, i * D_MODEL:(i + 1) * D_MODEL], preferred_element_type=F32)
        gate = _sigmoid(logit + bg_ref[:, i * D_MODEL:(i + 1) * D_MODEL])
        term = gate * jnp.dot(y_ref[...].astype(BF16), wb_ref[i], preferred_element_type=F32)
        merged = term if merged is None else merged + term
    out = jnp.dot(merged.astype(BF16), wo_ref[...], preferred_element_type=F32)
    xm = x + m[2:3] * out
    xm_ref[...] = xm
    h2 = _norm_mod(xm, g2_ref[...], m[3:4], m[4:5])
    h2_hi = h2.astype(BF16)
    h2_lo = (h2 - h2_hi.astype(F32)).astype(BF16)
    lg = jnp.dot(h2_hi, rwh_ref[...], preferred_element_type=F32)
    lg = lg + jnp.dot(h2_hi, rwl_ref[...], preferred_element_type=F32)
    lg = lg + jnp.dot(h2_lo, rwh_ref[...], preferred_element_type=F32)
    lg = lg + rb_ref[...]
    h2_ref[...] = h2_hi
    for j in range(MERGE_TM // ROUTE_TM):
        rows = slice(j * ROUTE_TM, (j + 1) * ROUTE_TM)
        w_out, q_out, cnt, off = _route_tile(lg[rows])
        w_ref[rows, :] = w_out
        q_ref[rows, :] = q_out
        cnt_ref[j] = cnt
        off_ref[j] = off


def _merge(stream, ya, yb, yc, yd, mods_l, g1, w_in_b, bg, wb, wo, g2, rwh, rwl, rb, layer):
    def tok(w):
        return pl.BlockSpec((MERGE_TM, w), lambda i: (i, 0))

    def full(*shape):
        return pl.BlockSpec(shape, lambda i: (0,) * len(shape))

    tile_rows = pl.BlockSpec((MERGE_TM // ROUTE_TM, 1, N_EXPERTS), lambda i: (i, 0, 0))
    return pl.pallas_call(
        functools.partial(_merge_kernel, layer=layer),
        grid=(N_TOK // MERGE_TM,),
        in_specs=_stream_specs(MERGE_TM, stream) + [
                  tok(BRANCH_W), tok(BRANCH_W), tok(BRANCH_W), tok(BRANCH_W),
                  pl.BlockSpec((None, N_MOD, D_MODEL), lambda i: (_mod_row(i * MERGE_TM), 0, 0)),
                  full(1, D_MODEL), pl.BlockSpec(memory_space=pl.ANY), full(1, N_BRANCH * D_MODEL),
                  full(N_BRANCH, BRANCH_W, D_MODEL), full(D_MODEL, D_MODEL), full(1, D_MODEL),
                  full(D_MODEL, N_EXPERTS), full(D_MODEL, N_EXPERTS), full(1, N_EXPERTS)],
        out_specs=[tok(D_MODEL), tok(D_MODEL), tok(TOP_K), tok(TOP_K), tile_rows, tile_rows],
        out_shape=[jax.ShapeDtypeStruct((N_TOK, D_MODEL), F32),
                   jax.ShapeDtypeStruct((N_TOK, D_MODEL), BF16),
                   jax.ShapeDtypeStruct((N_TOK, TOP_K), F32),
                   jax.ShapeDtypeStruct((N_TOK, TOP_K), jnp.int32),
                   jax.ShapeDtypeStruct((N_TILES, 1, N_EXPERTS), jnp.int32),
                   jax.ShapeDtypeStruct((N_TILES, 1, N_EXPERTS), jnp.int32)],
        scratch_shapes=[pltpu.VMEM((D_MODEL, N_BRANCH * D_MODEL), BF16), pltpu.SemaphoreType.DMA(())],
        compiler_params=_params(1),
        name="merge",
    )(stream[0], stream[1], ya, yb, yc, yd, mods_l, g1, w_in_b, bg, wb, wo, g2, rwh, rwl, rb)


ROUTE_TM = 256
TILE_ROWS = ROUTE_TM * TOP_K
N_TILES = N_TOK // ROUTE_TM


def _route_tile(lg):
    lane = lax.broadcasted_iota(jnp.int32, lg.shape, 1)
    sels, vals = [], []
    for _ in range(TOP_K):
        mx = lg.max(axis=-1, keepdims=True)
        idx = jnp.where(lg == mx, lane, N_EXPERTS).min(axis=-1, keepdims=True)
        sel = lane == idx
        sels.append(sel)
        vals.append(mx)
        lg = jnp.where(sel, -jnp.inf, lg)
    exps = [jnp.exp(v - vals[0]) for v in vals]
    den = exps[0] + exps[1] + exps[2] + exps[3]
    onehot = jnp.zeros(lg.shape, F32)
    for sel in sels:
        onehot = onehot + sel.astype(F32)
    row = lax.broadcasted_iota(jnp.int32, (ROUTE_TM, ROUTE_TM), 0)
    col = lax.broadcasted_iota(jnp.int32, (ROUTE_TM, ROUTE_TM), 1)
    tri = jnp.where(col < row, 1.0, 0.0).astype(BF16)
    rank = jnp.dot(tri, onehot.astype(BF16), preferred_element_type=F32)
    cnt = jnp.sum(onehot, axis=0, keepdims=True)
    erow = lax.broadcasted_iota(jnp.int32, (N_EXPERTS, N_EXPERTS), 0)
    ecol = lax.broadcasted_iota(jnp.int32, (N_EXPERTS, N_EXPERTS), 1)
    upper = jnp.where(erow < ecol, 1.0, 0.0).astype(BF16)
    off = jnp.dot(jnp.broadcast_to(cnt, (SUBLANES, N_EXPERTS)).astype(BF16), upper,
                  preferred_element_type=F32)[0:1]
    slot = rank + off
    k_lane = lax.broadcasted_iota(jnp.int32, (ROUTE_TM, TOP_K), 1)
    w_out = jnp.zeros((ROUTE_TM, TOP_K), F32)
    q_out = jnp.zeros((ROUTE_TM, TOP_K), F32)
    for k in range(TOP_K):
        w_out = jnp.where(k_lane == k, exps[k] / den, w_out)
        qk = jnp.sum(jnp.where(sels[k], slot, 0.0), axis=-1, keepdims=True)
        q_out = jnp.where(k_lane == k, qk, q_out)
    return w_out, q_out.astype(jnp.int32), cnt.astype(jnp.int32), off.astype(jnp.int32)


PIECE_SIZES = (32, 16, 8, 4, 2, 1)
PIECE_SLOTS = TILE_ROWS // PIECE_SIZES[0]
assert PIECE_SLOTS >= N_EXPERTS


def _compact(valid, *values):
    pos = jnp.cumsum(valid.astype(jnp.int32), axis=1) - 1
    slot = jnp.arange(PIECE_SLOTS, dtype=jnp.int32)
    hit = jnp.logical_and(valid[:, :, None], pos[:, :, None] == slot[None, None, :])
    packed = [jnp.sum(jnp.where(hit, v[:, :, None], 0), axis=1).astype(jnp.int32) for v in values]
    return packed, jnp.sum(valid.astype(jnp.int32), axis=1)


def _piece_lists(cnt, off, row):
    big = PIECE_SIZES[0]
    n_big = cnt // big
    p = jnp.arange(ROUTE_TM // big, dtype=jnp.int32)
    valid = (p[None, None, :] < n_big[:, :, None]).reshape(N_TILES, -1)
    src = (off[:, :, None] + big * p).reshape(N_TILES, -1)
    dst = (row[:, :, None] + big * p).reshape(N_TILES, -1)
    lists = [_compact(valid, src, dst)]
    rem = cnt - n_big * big
    for size in PIECE_SIZES[1:]:
        start = n_big * big + (rem & ~(2 * size - 1))
        lists.append(_compact((rem & size) != 0, off + start, row + start))
    counts = jnp.stack([n for _, n in lists], axis=1).reshape(-1)
    local_rows = jnp.stack([v[0] for v, _ in lists], axis=1).reshape(-1)
    global_rows = jnp.stack([v[1] for v, _ in lists], axis=1).reshape(-1)
    return counts, local_rows, global_rows


def _slab_pieces(tile, count_ref, local_ref, global_ref, fn):
    for k, size in enumerate(PIECE_SIZES):
        lst = tile * len(PIECE_SIZES) + k

        def body(j, carry, lst=lst, size=size):
            fn(local_ref[lst * PIECE_SLOTS + j], global_ref[lst * PIECE_SLOTS + j], size)
            return carry

        lax.fori_loop(0, count_ref[lst], body, 0)


def _rows(ref, row, n_rows):
    start = row * ROW_TILES
    if not isinstance(row, int):
        start = pl.multiple_of(start, ROW_TILES)
    return ref.at[pl.ds(start, n_rows * ROW_TILES)]


def _onehot_rows(q, values=None):
    lane = lax.broadcasted_iota(jnp.int32, (ROUTE_TM, TILE_ROWS), 1)
    s = jnp.zeros((ROUTE_TM, TILE_ROWS), F32)
    for k in range(TOP_K):
        v = 1.0 if values is None else values[:, k:k + 1]
        s = jnp.where(lane == q[:, k:k + 1], v, s)
    return s


def _wait_tile_rows(hbm_ref, buf_slot_ref, sem_slot):
    pltpu.make_async_copy(_rows(hbm_ref, 0, TILE_ROWS), buf_slot_ref, sem_slot).wait()


def _dispatch_kernel(count_ref, local_ref, global_ref, q_ref, h2_ref, xs_ref, buf_ref, sem):
    tile = pl.program_id(0)
    slot = tile % 2
    buf = buf_ref.at[slot]

    @pl.when(tile >= 2)
    def _():
        _wait_tile_rows(xs_ref, buf, sem.at[slot])

    sel = _onehot_rows(q_ref[...]).astype(BF16)
    xg = lax.dot_general(sel, h2_ref[...], (((0,), (0,)), ((), ())), preferred_element_type=F32)
    for c in range(ROW_TILES):
        buf[pl.ds(c, TILE_ROWS, stride=ROW_TILES), :] = xg[:, c * LANES:(c + 1) * LANES]

    def start(local_row, global_row, n_rows):
        pltpu.make_async_copy(_rows(buf, local_row, n_rows), _rows(xs_ref, global_row, n_rows),
                              sem.at[slot]).start()

    _slab_pieces(tile, count_ref, local_ref, global_ref, start)

    @pl.when(tile == N_TILES - 1)
    def _():
        _wait_tile_rows(xs_ref, buf, sem.at[slot])
        _wait_tile_rows(xs_ref, buf_ref.at[1 - slot], sem.at[1 - slot])


def _dispatch(pieces, q, h2):
    grid_spec = pltpu.PrefetchScalarGridSpec(
        num_scalar_prefetch=3,
        grid=(N_TILES,),
        in_specs=[pl.BlockSpec((ROUTE_TM, TOP_K), lambda i, *_: (i, 0)),
                  pl.BlockSpec((ROUTE_TM, D_MODEL), lambda i, *_: (i, 0))],
        out_specs=pl.BlockSpec(memory_space=pl.ANY),
        scratch_shapes=[pltpu.VMEM((2, TILE_ROWS * ROW_TILES, LANES), F32), pltpu.SemaphoreType.DMA((2,))],
    )
    return pl.pallas_call(
        _dispatch_kernel,
        grid_spec=grid_spec,
        out_shape=jax.ShapeDtypeStruct((N_ROWS * ROW_TILES, LANES), F32),
        compiler_params=_params(1),
        name="dispatch",
    )(*pieces, q, h2)


CAST_ROWS = 128
W_SLOTS = 2


def _expert_kernel(blk_exp_ref, nvalid_ref, first_ref, head_ref, slot_ref, next_ref,
                   xs_ref, w1_hbm, b1_ref, w2_hbm, b2_ref, y_ref,
                   w1f_ref, w2f_ref, w1b_ref, w2b_ref, sem, *, layer):
    b = pl.program_id(0)
    e = blk_exp_ref[b]
    nvalid = nvalid_ref[b]
    slot = slot_ref[b]

    def start_weights(expert, ahead):
        s = (slot + ahead) % W_SLOTS
        pltpu.make_async_copy(w1_hbm.at[layer, expert], w1f_ref.at[s], sem.at[0, s]).start()
        pltpu.make_async_copy(w2_hbm.at[layer, expert], w2f_ref.at[s], sem.at[1, s]).start()

    @pl.when(first_ref[b] == 1)
    def _():
        @pl.when(head_ref[b] == 1)
        def _():
            start_weights(e, 0)

        pltpu.make_async_copy(w1_hbm.at[layer, e], w1f_ref.at[slot], sem.at[0, slot]).wait()
        pltpu.make_async_copy(w2_hbm.at[layer, e], w2f_ref.at[slot], sem.at[1, slot]).wait()

        @pl.when(next_ref[b] >= 0)
        def _():
            start_weights(next_ref[b], 1)

        def cast1(i, carry):
            r = pl.multiple_of(i * CAST_ROWS, CAST_ROWS)
            w1b_ref[pl.ds(r, CAST_ROWS), :] = w1f_ref[slot, pl.ds(r, CAST_ROWS), :].astype(BF16)
            return carry

        def cast2(i, carry):
            r = pl.multiple_of(i * CAST_ROWS, CAST_ROWS)
            w2b_ref[pl.ds(r, CAST_ROWS), :] = w2f_ref[slot, pl.ds(r, CAST_ROWS), :].astype(BF16)
            return carry

        lax.fori_loop(0, D_MODEL // CAST_ROWS, cast1, 0)
        lax.fori_loop(0, D_FF // CAST_ROWS, cast2, 0)

    def run_rows(n_rows):
        valid = lax.broadcasted_iota(jnp.int32, (n_rows, LANES), 0) < nvalid
        chunks = [jnp.where(valid, xs_ref[pl.ds(c, n_rows, stride=ROW_TILES), :], 0.0).astype(BF16)
                  for c in range(ROW_TILES)]
        xb = jnp.concatenate(chunks, axis=-1)
        hid = jnp.dot(xb, w1b_ref[...], preferred_element_type=F32) + b1_ref[...]
        glu = jnp.minimum(hid[:, :D_FF], SWIGLU_LIMIT)
        lin = jnp.clip(hid[:, D_FF:], -SWIGLU_LIMIT, SWIGLU_LIMIT)
        act = glu * _sigmoid(SWIGLU_ALPHA * glu) * (lin + 1.0)
        y = jnp.dot(act.astype(BF16), w2b_ref[...], preferred_element_type=F32) + b2_ref[...]
        for c in range(ROW_TILES):
            y_ref[pl.ds(c, n_rows, stride=ROW_TILES), :] = y[:, c * LANES:(c + 1) * LANES]
        if n_rows < EXP_BLOCK:
            rest = (EXP_BLOCK - n_rows) * ROW_TILES
            y_ref[pl.ds(n_rows * ROW_TILES, rest), :] = jnp.zeros((rest, LANES), F32)

    @pl.when(nvalid > EXP_BLOCK // 2)
    def _():
        run_rows(EXP_BLOCK)

    @pl.when(jnp.logical_and(nvalid > 0, nvalid <= EXP_BLOCK // 2))
    def _():
        run_rows(EXP_BLOCK // 2)

    @pl.when(nvalid == 0)
    def _():
        y_ref[...] = jnp.zeros_like(y_ref)


def _experts(tables, xs, w1, b1, w2, b2, layer):
    def blk(b, *_):
        return (b, 0)

    def bias(b, be, *_):
        return (layer, be[b], 0, 0)

    grid_spec = pltpu.PrefetchScalarGridSpec(
        num_scalar_prefetch=6,
        grid=(N_BLOCKS,),
        in_specs=[
            pl.BlockSpec((EXP_BLOCK * ROW_TILES, LANES), blk),
            pl.BlockSpec(memory_space=pl.ANY),
            pl.BlockSpec((None, None, 1, 2 * D_FF), bias),
            pl.BlockSpec(memory_space=pl.ANY),
            pl.BlockSpec((None, None, 1, D_MODEL), bias),
        ],
        out_specs=pl.BlockSpec((EXP_BLOCK * ROW_TILES, LANES), blk),
        scratch_shapes=[pltpu.VMEM((W_SLOTS, D_MODEL, 2 * D_FF), F32), pltpu.VMEM((W_SLOTS, D_FF, D_MODEL), F32),
                        pltpu.VMEM((D_MODEL, 2 * D_FF), BF16), pltpu.VMEM((D_FF, D_MODEL), BF16),
                        pltpu.SemaphoreType.DMA((2, W_SLOTS))],
    )
    return pl.pallas_call(
        functools.partial(_expert_kernel, layer=layer),
        grid_spec=grid_spec,
        out_shape=jax.ShapeDtypeStruct((N_ROWS * ROW_TILES, LANES), F32),
        compiler_params=_params(1),
        name="experts",
    )(*tables, xs, w1, b1.reshape(DEPTH, N_EXPERTS, 1, 2 * D_FF), w2,
      b2.reshape(DEPTH, N_EXPERTS, 1, D_MODEL))


def _split_bf16(x):
    hi = x.astype(BF16)
    return hi, (x - hi.astype(F32)).astype(BF16)


P_TILES = N_P // ROUTE_TM


def _combine_kernel(count_ref, local_ref, global_ref, q_ref, w_ref, y_ref, xm_ref, mod_ref, fg_ref,
                    *refs, final):
    buf_ref, sem = refs[-2:]
    tile = pl.program_id(0)
    slot = tile % 2
    buf = buf_ref.at[slot]

    def fetch(t, s):
        def start(local_row, global_row, n_rows):
            pltpu.make_async_copy(_rows(y_ref, global_row, n_rows), _rows(buf_ref.at[s], local_row, n_rows),
                                  sem.at[s]).start()

        _slab_pieces(t, count_ref, local_ref, global_ref, start)

    @pl.when(tile == 0)
    def _():
        fetch(tile, slot)

    @pl.when(tile + 1 < N_TILES)
    def _():
        fetch(tile + 1, 1 - slot)

    s_hi, s_lo = _split_bf16(_onehot_rows(q_ref[...], w_ref[...]))
    _wait_tile_rows(y_ref, buf, sem.at[slot])
    rows = jnp.concatenate([buf[pl.ds(c, TILE_ROWS, stride=ROW_TILES), :] for c in range(ROW_TILES)],
                           axis=-1)
    r_hi, r_lo = _split_bf16(rows)
    moe = jnp.dot(s_hi, r_hi, preferred_element_type=F32)
    moe = moe + jnp.dot(s_lo, r_hi, preferred_element_type=F32)
    moe = moe + jnp.dot(s_hi, r_lo, preferred_element_type=F32)
    m = mod_ref[...]
    x = xm_ref[...] + m[5:6] * moe
    if not final:
        refs[0][...] = x
        return
    xn = x * lax.rsqrt(jnp.mean(x * x, axis=-1, keepdims=True) + EPS) * fg_ref[...]
    yp_ref, ys_ref = refs[:2]

    @pl.when(tile < P_TILES)
    def _():
        yp_ref[...] = xn

    @pl.when(tile >= P_TILES)
    def _():
        ys_ref[...] = xn


def _combine(pieces, q, topw, y, xm, mods_l, final_g, final):
    def tok(w):
        return pl.BlockSpec((ROUTE_TM, w), lambda i, *_: (i, 0))

    if final:
        out_specs = [pl.BlockSpec((ROUTE_TM, D_MODEL), lambda i, *_: (jnp.minimum(i, P_TILES - 1), 0)),
                     pl.BlockSpec((ROUTE_TM, D_MODEL), lambda i, *_: (jnp.maximum(i - P_TILES, 0), 0))]
        out_shape = [jax.ShapeDtypeStruct((N_P, D_MODEL), F32), jax.ShapeDtypeStruct((N_S, D_MODEL), F32)]
    else:
        out_specs = [tok(D_MODEL)]
        out_shape = [jax.ShapeDtypeStruct((N_TOK, D_MODEL), F32)]
    grid_spec = pltpu.PrefetchScalarGridSpec(
        num_scalar_prefetch=3,
        grid=(N_TILES,),
        in_specs=[tok(TOP_K), tok(TOP_K),
                  pl.BlockSpec(memory_space=pl.ANY),
                  tok(D_MODEL),
                  pl.BlockSpec((None, N_MOD, D_MODEL), lambda i, *_: (_mod_row(i * ROUTE_TM), 0, 0)),
                  pl.BlockSpec((1, D_MODEL), lambda i, *_: (0, 0))],
        out_specs=out_specs,
        scratch_shapes=[pltpu.VMEM((2, TILE_ROWS * ROW_TILES, LANES), F32), pltpu.SemaphoreType.DMA((2,))],
    )
    return pl.pallas_call(
        functools.partial(_combine_kernel, final=final),
        grid_spec=grid_spec,
        out_shape=out_shape,
        compiler_params=_params(1),
        name="combine_final" if final else "combine",
    )(*pieces, q, topw, y, xm, mods_l, final_g)


def _rope_tables():
    t = np.arange(DEC_SEQ)
    row = (t // GRID_W).astype(np.float32)
    col = (t % GRID_W).astype(np.float32)
    inv = jnp.asarray(ROPE_THETA, F32) ** (-jnp.arange(ROPE_PAIRS, dtype=F32) / ROPE_PAIRS)
    ang = jnp.concatenate([jnp.asarray(row)[:, None] * inv, jnp.asarray(col)[:, None] * inv], axis=-1)
    cos = jnp.repeat(jnp.cos(ang), 2, axis=-1)
    sin = jnp.repeat(jnp.sin(ang), 2, axis=-1)
    sign = jnp.asarray(np.tile(np.array([-1.0, 1.0], np.float32), HEAD_DIM // 2))
    return jnp.tile(cos, (1, A_HEADS)), jnp.tile(sin * sign, (1, A_HEADS))


def _routing_tables(tile_cnt):
    i32 = jnp.int32
    carry = jnp.cumsum(tile_cnt, axis=0) - tile_cnt
    counts = jnp.sum(tile_cnt, axis=0)
    padded = (counts + EXP_BLOCK - 1) // EXP_BLOCK * EXP_BLOCK
    pad_end = jnp.cumsum(padded)
    pad_start = pad_end - padded
    rowstart = (pad_start[None, :] + carry).astype(i32)
    blk_row = jnp.arange(N_BLOCKS, dtype=i32) * EXP_BLOCK
    blk_exp = jnp.sum((blk_row[:, None] >= pad_end[None, :]).astype(i32), axis=1)
    blk_exp = jnp.minimum(blk_exp, N_EXPERTS - 1)
    eid = jnp.arange(N_EXPERTS, dtype=i32)

    def pick(table, idx):
        return jnp.sum(jnp.where(idx[:, None] == eid[None, :], table[None, :], 0), axis=1).astype(i32)

    blk_start = pick(pad_start, blk_exp)
    nvalid = jnp.clip(pick(counts, blk_exp) - (blk_row - blk_start), 0, EXP_BLOCK).astype(i32)
    first = jnp.logical_and(blk_row == blk_start, nvalid > 0)
    active = counts > 0
    act_rank = jnp.cumsum(active.astype(i32)) - 1
    later = jnp.logical_and(active[None, :], eid[None, :] > eid[:, None])
    nxt = jnp.min(jnp.where(later, eid[None, :], N_EXPERTS), axis=1)
    nxt = jnp.where(nxt == N_EXPERTS, -1, nxt).astype(i32)
    blk_rank = pick(act_rank, blk_exp)
    head = jnp.logical_and(first, blk_rank == 0)
    tables = (blk_exp, nvalid, first.astype(i32), head.astype(i32), (blk_rank % W_SLOTS).astype(i32),
              pick(nxt + 1, blk_exp) - 1)
    return rowstart, tables


def kernel(x_prompt, x_sample, cache_attn_k, cache_attn_v, cache_na_k, cache_na_v, c, c_ctx, w_mod, b_mod, norm1_g, norm2_g, w_in, b_gate, q_norm_g, k_norm_g, na_rpb, conv_w, conv_b, conv_ln_g, conv_ln_b, sgu_ln_g, sgu_ln_b, sgu_w, sgu_b, w_branch, w_out, router_w, router_b, exp_w1, exp_b1, exp_w2, exp_b2, final_g):
    stream = (x_prompt.reshape(N_P, D_MODEL), x_sample.reshape(N_S, D_MODEL), 0)
    cvec =jnp.zeros((SUBLANES, D_MODEL), F32).at[0].set(c_ctx).at[1:1 + DEC_BATCH].set(c)
    mods = _modulation(cvec, w_mod, b_mod).reshape(DEPTH, SUBLANES, N_MOD, D_MODEL)
    cos_t, sin_t = _rope_tables()
    cak = cache_attn_k.reshape(DEC_BATCH, DEPTH, PAST_LEN, A_KV)
    cav = cache_attn_v.reshape(DEC_BATCH, DEPTH, PAST_LEN, A_KV)
    cbk = cache_na_k.reshape(DEC_BATCH, DEPTH, PAST_LEN, B_W)
    cbv = cache_na_v.reshape(DEC_BATCH, DEPTH, PAST_LEN, B_W)
    w_in_b = w_in.astype(BF16)
    w_br = w_branch.astype(BF16)
    w_o = w_out.astype(BF16)
    rw_hi = router_w.astype(BF16)
    rw_lo = (router_w - rw_hi.astype(F32)).astype(BF16)
    final_g2 = final_g.reshape(1, D_MODEL)

    caches = ()
    outs = None
    for l in range(DEPTH):
        mods_l = mods[l]
        g1 = norm1_g[l].reshape(1, D_MODEL)
        gq = jnp.tile(q_norm_g[l], A_HEADS).reshape(1, A_Q)
        gk = jnp.tile(k_norm_g[l], A_KV_HEADS).reshape(1, A_KV)
        aq, ak, av, bq, bk, bv, cz, dz = _in_proj(stream, mods_l, g1, w_in_b, l)
        ya, yb, *caches = _prompt_attn(aq, ak, av, bq, bk, bv, gq, gk, caches, l)
        ya = _sample_attn(aq, ak, av, cak, cav, cos_t, sin_t, gq, gk, ya, l)
        yb = _na_attn(bq, bk, bv, cbk, cbv, _na_bias(na_rpb[l]), yb, l)
        cw = conv_w[l]
        cb = conv_b[l].reshape(1, C_WIDTH)
        cg = conv_ln_g[l].reshape(1, C_WIDTH)
        cbb = conv_ln_b[l].reshape(1, C_WIDTH)
        yc = _conv_call(cz, cw, cb, cg, cbb, SEQ, 0, BATCH)
        yc = _conv_call(cz, cw, cb, cg, cbb, DEC_SEQ, N_P // DEC_SEQ, DEC_BATCH, partial_out=yc)
        bs_full = jnp.repeat(sgu_b[l].T, SGU_GW, axis=1)
        yd = _sgu(dz, sgu_ln_g[l].reshape(1, SGU_WIDTH), sgu_ln_b[l].reshape(1, SGU_WIDTH),
                  sgu_w[l].astype(BF16), bs_full)
        xm, h2, top_w, q, tile_cnt, tile_off = _merge(
            stream, ya, yb, yc, yd, mods_l, g1, w_in_b, b_gate[l].reshape(1, N_BRANCH * D_MODEL), w_br[l], w_o[l],
            norm2_g[l].reshape(1, D_MODEL), rw_hi[l], rw_lo[l], router_b[l].reshape(1, N_EXPERTS), l)
        tile_cnt = tile_cnt.reshape(N_TILES, N_EXPERTS)
        rowstart, tables = _routing_tables(tile_cnt)
        pieces = _piece_lists(tile_cnt, tile_off.reshape(N_TILES, N_EXPERTS), rowstart)
        xs = _dispatch(pieces, q, h2)
        y = _experts(tables, xs, exp_w1, exp_b1, exp_w2, exp_b2, l)
        outs = _combine(pieces, q, top_w, y, xm, mods_l, final_g2, l == DEPTH - 1)
        stream = (outs[0], outs[0], N_P)

    new_k, new_v, new_bk, new_bv = caches
    return (outs[0].reshape(BATCH, SEQ, D_MODEL), outs[1].reshape(DEC_BATCH, DEC_SEQ, D_MODEL),
            new_k.reshape(BATCH, DEPTH, SEQ, A_KV_HEADS, HEAD_DIM),
            new_v.reshape(BATCH, DEPTH, SEQ, A_KV_HEADS, HEAD_DIM),
            new_bk.reshape(BATCH, DEPTH, SEQ, B_HEADS, HEAD_DIM),
            new_bv.reshape(BATCH, DEPTH, SEQ, B_HEADS, HEAD_DIM))
```

```python
import functools

import numpy as np
import jax
import jax.numpy as jnp
from jax import lax
from jax.experimental import pallas as pl
from jax.experimental.pallas import tpu as pltpu

D_MODEL = 1024
BATCH = 32
SEQ = 256
DEPTH = 2
DEC_BATCH = 2
DEC_SEQ = 1024
PAST_LEN = 512
GRID_W = 64
HEAD_DIM = 64
A_HEADS = 4
A_KV_HEADS = 2
B_HEADS = 4
NA_ROWS = 8
NA_COLS = 16
C_WIDTH = 256
CONV_WIDTH = 31
SGU_WIDTH = 256
SGU_GROUPS = 4
SGU_CHUNK = 128
N_BRANCH = 4
BRANCH_W = 256
N_EXPERTS = 32
TOP_K = 4
D_FF = 1024
SWIGLU_ALPHA = 1.702
SWIGLU_LIMIT = 7.0
MOE_BLOCK = 128
ROPE_THETA = 10000.0
ROPE_PAIRS = HEAD_DIM // 4
N_MOD = 6
EPS = 1e-6
NEG_INF = -1e30

A_Q = A_HEADS * HEAD_DIM
A_KV = A_KV_HEADS * HEAD_DIM
B_W = B_HEADS * HEAD_DIM
MIX_SIZES = (A_Q, A_KV, A_KV, B_W, B_W, B_W, 2 * C_WIDTH, 2 * SGU_WIDTH)
MIX_COLS = sum(MIX_SIZES)

N_P = BATCH * SEQ
N_S = DEC_BATCH * DEC_SEQ
N_TOK = N_P + N_S
N_ASSIGN = N_TOK * TOP_K
EXP_BLOCK = 512
N_BLOCKS = N_ASSIGN // EXP_BLOCK + N_EXPERTS
N_ROWS = N_BLOCKS * EXP_BLOCK
GRID_ROWS = DEC_SEQ // GRID_W
NA_WR = min(NA_ROWS, GRID_ROWS)
N_LOC = NA_WR * GRID_W

SUBLANES = 8
LANES = 128
ROW_TILES = D_MODEL // LANES
VMEM_LIMIT = 56 * 1024 * 1024

F32 = jnp.float32
BF16 = jnp.bfloat16


def _params(n_axes, vmem=None):
    return pltpu.CompilerParams(
        dimension_semantics=("arbitrary",) * n_axes,
        vmem_limit_bytes=vmem if vmem is not None else VMEM_LIMIT)


def _mod_row(start):
    return jnp.where(start < N_P, 0, 1 + (start - N_P) // DEC_SEQ)


def _bdot(a, b):
    return jnp.dot(a.astype(BF16), b.astype(BF16), preferred_element_type=F32)


def _bdot_nt(a, b):
    return lax.dot_general(a.astype(BF16), b.astype(BF16), (((1,), (1,)), ((), ())),
                           preferred_element_type=F32)


def _sigmoid(x):
    return 0.5 * jnp.tanh(0.5 * x) + 0.5


MOD_TN = 1536


def _mod_kernel(c_ref, w_ref, b_ref, o_ref):
    c = c_ref[...]
    s = c * _sigmoid(c)
    o_ref[...] = _bdot(s, w_ref[...]) + b_ref[...]


def _modulation(cvec, w_mod, b_mod):
    n_col = N_MOD * D_MODEL
    return pl.pallas_call(
        _mod_kernel,
        grid=(DEPTH, n_col // MOD_TN),
        in_specs=[
            pl.BlockSpec((SUBLANES, D_MODEL), lambda l, j: (0, 0)),
            pl.BlockSpec((None, D_MODEL, MOD_TN), lambda l, j: (l, 0, j)),
            pl.BlockSpec((None, 1, MOD_TN), lambda l, j: (l, 0, j)),
        ],
        out_specs=pl.BlockSpec((None, SUBLANES, MOD_TN), lambda l, j: (l, 0, j)),
        out_shape=jax.ShapeDtypeStruct((DEPTH, SUBLANES, n_col), F32),
        compiler_params=_params(2),
        name="modulation",
    )(cvec, w_mod, b_mod.reshape(DEPTH, 1, n_col))


IN_TM = 512


def _norm_mod(x, g, shift, scale):
    y = x * lax.rsqrt(jnp.mean(x * x, axis=-1, keepdims=True) + EPS) * g
    return y * (1.0 + scale) + shift


def _stream_specs(tm, stream):
    p_tiles = N_P // tm
    s_first = stream[2] // tm
    return [pl.BlockSpec((tm, D_MODEL), lambda i: (jnp.minimum(i, p_tiles - 1), 0)),
            pl.BlockSpec((tm, D_MODEL), lambda i: (jnp.maximum(i - p_tiles, 0) + s_first, 0))]


def _stream_tile(xp_ref, xs_ref, tm):
    return jnp.where(pl.program_id(0) < N_P // tm, xp_ref[...], xs_ref[...])


def _in_kernel(xp_ref, xs_ref, mod_ref, g_ref, w_ref, *out_refs):
    m = mod_ref[...]
    h = _norm_mod(_stream_tile(xp_ref, xs_ref, IN_TM), g_ref[...], m[0:1], m[1:2])
    z = jnp.dot(h.astype(BF16), w_ref[...], preferred_element_type=F32)
    off = 0
    for o_ref, sz in zip(out_refs, MIX_SIZES):
        o_ref[...] = z[:, off:off + sz]
        off += sz


def _in_proj(stream, mods_l, g1, w_in_b, layer):
    return pl.pallas_call(
        _in_kernel,
        grid=(N_TOK // IN_TM,),
        in_specs=_stream_specs(IN_TM, stream) + [
            pl.BlockSpec((None, N_MOD, D_MODEL), lambda i: (_mod_row(i * IN_TM), 0, 0)),
            pl.BlockSpec((1, D_MODEL), lambda i: (0, 0)),
            pl.BlockSpec((None, D_MODEL, MIX_COLS), lambda i: (layer, 0, 0)),
        ],
        out_specs=[pl.BlockSpec((IN_TM, sz), lambda i: (i, 0)) for sz in MIX_SIZES],
        out_shape=[jax.ShapeDtypeStruct((N_TOK, sz), F32) for sz in MIX_SIZES],
        compiler_params=_params(1),
        name="in_proj",
    )(stream[0], stream[1], mods_l, g1, w_in_b)


def _head_rms(x, g):
    n_heads = x.shape[-1] // HEAD_DIM
    seg = lax.broadcasted_iota(jnp.int32, x.shape, 1) // HEAD_DIM
    xx = x * x
    inv = jnp.zeros_like(x)
    for h in range(n_heads):
        ms = jnp.sum(jnp.where(seg == h, xx, 0.0), axis=-1, keepdims=True) * (1.0 / HEAD_DIM)
        inv = jnp.where(seg == h, lax.rsqrt(ms + EPS), inv)
    return x * inv * g


def _softmax_pv(score_parts, value_parts):
    m = score_parts[0].max(axis=-1, keepdims=True)
    for s in score_parts[1:]:
        m = jnp.maximum(m, s.max(axis=-1, keepdims=True))
    den = None
    acc = None
    for s, v in zip(score_parts, value_parts):
        e = jnp.exp(s - m)
        d = e.sum(axis=-1, keepdims=True)
        a = _bdot(e, v)
        den = d if den is None else den + d
        acc = a if acc is None else acc + a
    return acc / den


def _head(x, h):
    return x[:, h * HEAD_DIM:(h + 1) * HEAD_DIM]


SCALE = HEAD_DIM ** -0.5


def _prompt_attn_kernel(aq_ref, ak_ref, av_ref, bq_ref, bk_ref, bv_ref, gq_ref, gk_ref, *refs):
    ya_ref, yb_ref, nk_ref, nv_ref, nbk_ref, nbv_ref = refs[-6:]
    aq = _head_rms(aq_ref[...], gq_ref[...])
    ak = _head_rms(ak_ref[...], gk_ref[...])
    av = av_ref[...]
    for ref, val in ((nk_ref, ak), (nv_ref, av), (nbk_ref, bk_ref[...]), (nbv_ref, bv_ref[...])):
        n_heads = val.shape[-1] // HEAD_DIM
        for h in range(n_heads):
            ref[pl.ds(h, SEQ, stride=n_heads), :] = _head(val, h)
    grp = A_HEADS // A_KV_HEADS
    outs = []
    for h in range(A_HEADS):
        s = _bdot_nt(_head(aq, h), _head(ak, h // grp)) * SCALE
        outs.append(_softmax_pv([s], [_head(av, h // grp)]))
    ya_ref[...] = jnp.concatenate(outs, axis=-1)
    bq = bq_ref[...]
    bk = bk_ref[...]
    bv = bv_ref[...]
    outs = []
    for h in range(B_HEADS):
        s = _bdot_nt(_head(bq, h), _head(bk, h)) * SCALE
        outs.append(_softmax_pv([s], [_head(bv, h)]))
    yb_ref[...] = jnp.concatenate(outs, axis=-1)


def _prompt_attn(aq, ak, av, bq, bk, bv, gq, gk, caches, layer):
    def spec(w):
        return pl.BlockSpec((SEQ, w), lambda b: (b, 0))

    def cache_spec(n_heads):
        return pl.BlockSpec((SEQ * n_heads, HEAD_DIM), lambda b: (b * DEPTH + layer, 0))

    cache_heads = (A_KV_HEADS, A_KV_HEADS, B_HEADS, B_HEADS)
    n_in = 8
    return pl.pallas_call(
        _prompt_attn_kernel,
        grid=(BATCH,),
        in_specs=[spec(A_Q), spec(A_KV), spec(A_KV), spec(B_W), spec(B_W), spec(B_W),
                  pl.BlockSpec((1, A_Q), lambda b: (0, 0)),
                  pl.BlockSpec((1, A_KV), lambda b: (0, 0))]
        + [pl.BlockSpec(memory_space=pl.ANY) for _ in caches],
        out_specs=[spec(A_Q), spec(B_W)] + [cache_spec(n) for n in cache_heads],
        out_shape=[jax.ShapeDtypeStruct((N_TOK, A_Q), F32),
                   jax.ShapeDtypeStruct((N_TOK, B_W), F32)]
        + [jax.ShapeDtypeStruct((BATCH * DEPTH * SEQ * n, HEAD_DIM), F32) for n in cache_heads],
        input_output_aliases={n_in + j: 2 + j for j in range(len(caches))},
        compiler_params=_params(1),
        name="prompt_attn",
    )(aq, ak, av, bq, bk, bv, gq, gk, *caches)


QB = 128


def _rope(x, cos, sin_signed):
    n = x.shape[-1]
    nxt = pltpu.roll(x, n - 1, 1)
    prv = pltpu.roll(x, 1, 1)
    even = (lax.broadcasted_iota(jnp.int32, x.shape, 1) % 2) == 0
    return x * cos + jnp.where(even, nxt, prv) * sin_signed


def _sample_attn_kernel(q_ref, k_ref, v_ref, ck_ref, cv_ref, cosq_ref, sinq_ref, cosk_ref, sink_ref,
                        gq_ref, gk_ref, ya_prompt_ref, o_ref):
    del ya_prompt_ref
    q = _rope(_head_rms(q_ref[...], gq_ref[...]), cosq_ref[...], sinq_ref[...])
    k = _rope(_head_rms(k_ref[...], gk_ref[...]), cosk_ref[...], sink_ref[...])
    v = v_ref[...]
    ck = ck_ref[...]
    cv = cv_ref[...]
    grp = A_HEADS // A_KV_HEADS
    outs = []
    for h in range(A_HEADS):
        j = h // grp
        qh = _head(q, h)
        s1 = _bdot_nt(qh, _head(k, j)) * SCALE
        s2 = _bdot_nt(qh, _head(ck, j)) * SCALE
        outs.append(_softmax_pv([s1, s2], [_head(v, j), _head(cv, j)]))
    o_ref[...] = jnp.concatenate(outs, axis=-1)


def _sample_attn(aq, ak, av, cache_k, cache_v, cos_t, sin_t, gq, gk, ya, layer):
    nqb = DEC_SEQ // QB
    q0 = N_P // QB
    k0 = N_P // DEC_SEQ
    return pl.pallas_call(
        _sample_attn_kernel,
        grid=(DEC_BATCH, nqb),
        in_specs=[
            pl.BlockSpec((QB, A_Q), lambda b, i: (q0 + b * nqb + i, 0)),
            pl.BlockSpec((DEC_SEQ, A_KV), lambda b, i: (k0 + b, 0)),
            pl.BlockSpec((DEC_SEQ, A_KV), lambda b, i: (k0 + b, 0)),
            pl.BlockSpec((None, None, PAST_LEN, A_KV), lambda b, i: (b, layer, 0, 0)),
            pl.BlockSpec((None, None, PAST_LEN, A_KV), lambda b, i: (b, layer, 0, 0)),
            pl.BlockSpec((QB, A_Q), lambda b, i: (i, 0)),
            pl.BlockSpec((QB, A_Q), lambda b, i: (i, 0)),
            pl.BlockSpec((DEC_SEQ, A_KV), lambda b, i: (0, 0)),
            pl.BlockSpec((DEC_SEQ, A_KV), lambda b, i: (0, 0)),
            pl.BlockSpec((1, A_Q), lambda b, i: (0, 0)),
            pl.BlockSpec((1, A_KV), lambda b, i: (0, 0)),
            pl.BlockSpec(memory_space=pl.ANY),
        ],
        out_specs=pl.BlockSpec((QB, A_Q), lambda b, i: (q0 + b * nqb + i, 0)),
        out_shape=jax.ShapeDtypeStruct((N_TOK, A_Q), F32),
        input_output_aliases={11: 0},
        compiler_params=_params(2),
        name="sample_attn",
    )(aq, ak, av, cache_k, cache_v, cos_t, sin_t, cos_t, sin_t, gq, gk, ya)


N_ROW_OFF = 2 * NA_ROWS - 1
N_COL_OFF = 2 * NA_COLS - 1
NA_PAIRS = N_ROW_OFF - 1
assert NA_WR == NA_ROWS and NA_WR % 2 == 0 and 2 * GRID_W == LANES


def _na_bias_kernel(rpb_ref, o_ref):
    h = pl.program_id(0)
    qc = lax.broadcasted_iota(jnp.int32, (GRID_W, LANES), 0)
    lane = lax.broadcasted_iota(jnp.int32, (GRID_W, LANES), 1)
    right = lane >= GRID_W
    kc = jnp.where(right, lane - GRID_W, lane)
    c_start = jnp.clip(qc - NA_COLS // 2, 0, GRID_W - NA_COLS)
    col_in = jnp.logical_and(kc >= c_start, kc < c_start + NA_COLS)
    col_off = jnp.clip(kc - qc + NA_COLS - 1, 0, N_COL_OFF - 1)
    for p in range(NA_PAIRS):
        acc = jnp.zeros((GRID_W, LANES), F32)
        for o in range(N_COL_OFF):
            left_v = rpb_ref[(h * N_ROW_OFF + p) * N_COL_OFF + o]
            right_v = rpb_ref[(h * N_ROW_OFF + p + 1) * N_COL_OFF + o]
            acc = jnp.where(col_off == o, jnp.where(right, right_v, left_v), acc)
        o_ref[p] = jnp.where(col_in, acc, NEG_INF)


def _na_bias(rpb):
    return pl.pallas_call(
        _na_bias_kernel,
        grid_spec=pltpu.PrefetchScalarGridSpec(
            num_scalar_prefetch=1,
            grid=(B_HEADS,),
            in_specs=[],
            out_specs=pl.BlockSpec((None, NA_PAIRS, GRID_W, LANES), lambda h, *_: (h, 0, 0, 0)),
        ),
        out_shape=jax.ShapeDtypeStruct((B_HEADS, NA_PAIRS, GRID_W, LANES), F32),
        compiler_params=_params(1),
        name="na_bias",
    )(rpb.reshape(-1))


def _na_kernel(q_ref, k_ref, v_ref, ck_ref, cv_ref, bias_ref, yb_prompt_ref, o_ref):
    del yb_prompt_ref
    r = pl.program_id(1)
    r_start = jnp.clip(r - NA_WR // 2, 0, GRID_ROWS - NA_WR)
    base = pl.multiple_of(r_start * GRID_W, GRID_W)
    row_off0 = r_start - r + NA_ROWS - 1
    q = q_ref[...]
    kb = k_ref[pl.ds(base, N_LOC), :]
    vb = v_ref[pl.ds(base, N_LOC), :]
    ck = ck_ref[...]
    cv = cv_ref[...]
    outs = []
    for h in range(B_HEADS):
        qh = _head(q, h)
        bias = jnp.concatenate([bias_ref[h, row_off0 + 2 * j] for j in range(NA_WR // 2)], axis=-1)
        s1 = _bdot_nt(qh, _head(kb, h)) * SCALE + bias
        s2 = _bdot_nt(qh, _head(ck, h)) * SCALE
        outs.append(_softmax_pv([s1, s2], [_head(vb, h), _head(cv, h)]))
    o_ref[...] = jnp.concatenate(outs, axis=-1)


def _na_attn(bq, bk, bv, cache_k, cache_v, bias, yb, layer):
    q0 = N_P // GRID_W
    k0 = N_P // DEC_SEQ
    return pl.pallas_call(
        _na_kernel,
        grid=(DEC_BATCH, GRID_ROWS),
        in_specs=[
            pl.BlockSpec((GRID_W, B_W), lambda b, r: (q0 + b * GRID_ROWS + r, 0)),
            pl.BlockSpec((DEC_SEQ, B_W), lambda b, r: (k0 + b, 0)),
            pl.BlockSpec((DEC_SEQ, B_W), lambda b, r: (k0 + b, 0)),
            pl.BlockSpec((None, None, PAST_LEN, B_W), lambda b, r: (b, layer, 0, 0)),
            pl.BlockSpec((None, None, PAST_LEN, B_W), lambda b, r: (b, layer, 0, 0)),
            pl.BlockSpec((B_HEADS, NA_PAIRS, GRID_W, LANES), lambda b, r: (0, 0, 0, 0)),
            pl.BlockSpec(memory_space=pl.ANY),
        ],
        out_specs=pl.BlockSpec((GRID_W, B_W), lambda b, r: (q0 + b * GRID_ROWS + r, 0)),
        out_shape=jax.ShapeDtypeStruct((N_TOK, B_W), F32),
        input_output_aliases={6: 0},
        compiler_params=_params(2),
        name="na_attn",
    )(bq, bk, bv, cache_k, cache_v, bias, yb)


CONV_PAD = 16
CONV_CHUNK = 64


def _layer_norm(x, g, b):
    mu = jnp.mean(x, axis=-1, keepdims=True)
    xc = x - mu
    var = jnp.mean(xc * xc, axis=-1, keepdims=True)
    return xc * lax.rsqrt(var + EPS) * g + b


def _conv_kernel(z_ref, w_ref, cb_ref, g_ref, b_ref, *refs, s_len):
    o_ref, pad_ref = refs[-2:]
    z = z_ref[...]
    u = z[:, :C_WIDTH] * _sigmoid(z[:, C_WIDTH:])
    pad_ref[pl.ds(0, CONV_PAD), :] = jnp.zeros((CONV_PAD, C_WIDTH), F32)
    pad_ref[pl.ds(CONV_PAD + s_len, CONV_PAD), :] = jnp.zeros((CONV_PAD, C_WIDTH), F32)
    pad_ref[pl.ds(CONV_PAD, s_len), :] = u
    w = w_ref[...]
    shift = CONV_PAD - CONV_WIDTH // 2

    def chunk(c, carry):
        base = pl.multiple_of(c * CONV_CHUNK, CONV_CHUNK)
        acc = jnp.zeros((CONV_CHUNK, C_WIDTH), F32)
        for r in range(SUBLANES):
            part = None
            for k in range(CONV_WIDTH):
                if (k + shift) % SUBLANES != r:
                    continue
                rows = pad_ref[pl.ds(base + (k + shift - r), CONV_CHUNK + SUBLANES), :]
                term = rows * w[k:k + 1]
                part = term if part is None else part + term
            if part is not None:
                acc = acc + part[r:r + CONV_CHUNK]
        y = _layer_norm(acc + cb_ref[...], g_ref[...], b_ref[...])
        o_ref[pl.ds(base, CONV_CHUNK), :] = y * _sigmoid(y)
        return carry

    lax.fori_loop(0, s_len // CONV_CHUNK, chunk, 0)


def _conv_call(cz, w, cb, g, b, s_len, first_blk, n_seq, partial_out=None):
    vec = pl.BlockSpec((1, C_WIDTH), lambda i: (0, 0))
    extra = [] if partial_out is None else [partial_out]
    return pl.pallas_call(
        functools.partial(_conv_kernel, s_len=s_len),
        grid=(n_seq,),
        in_specs=[pl.BlockSpec((s_len, 2 * C_WIDTH), lambda i: (first_blk + i, 0)),
                  pl.BlockSpec((CONV_WIDTH, C_WIDTH), lambda i: (0, 0)), vec, vec, vec]
        + [pl.BlockSpec(memory_space=pl.ANY) for _ in extra],
        out_specs=pl.BlockSpec((s_len, C_WIDTH), lambda i: (first_blk + i, 0)),
        out_shape=jax.ShapeDtypeStruct((N_TOK, C_WIDTH), F32),
        input_output_aliases={5: 0} if extra else {},
        scratch_shapes=[pltpu.VMEM((s_len + 2 * CONV_PAD, C_WIDTH), F32)],
        compiler_params=_params(1),
        name="conformer_conv_%d" % s_len,
    )(cz, w, cb, g, b, *extra)


SGU_TM = 512
SGU_GW = SGU_WIDTH // SGU_GROUPS


def _sgu_kernel(z_ref, g_ref, b_ref, ws_ref, bs_ref, o_ref):
    z = z_ref[...]
    z = 0.5 * z * (1.0 + lax.erf(z * (2.0 ** -0.5)))
    u = z[:, :SGU_WIDTH]
    v = _layer_norm(z[:, SGU_WIDTH:], g_ref[...], b_ref[...])
    for c in range(SGU_TM // SGU_CHUNK):
        vc = v[c * SGU_CHUNK:(c + 1) * SGU_CHUNK]
        parts = [_bdot(ws_ref[g], vc[:, g * SGU_GW:(g + 1) * SGU_GW]) for g in range(SGU_GROUPS)]
        mixed = jnp.concatenate(parts, axis=-1) + bs_ref[...]
        o_ref[pl.ds(c * SGU_CHUNK, SGU_CHUNK), :] = u[c * SGU_CHUNK:(c + 1) * SGU_CHUNK] * mixed


def _sgu(dz, g, b, ws, bs_full):
    vec = pl.BlockSpec((1, SGU_WIDTH), lambda i: (0, 0))
    return pl.pallas_call(
        _sgu_kernel,
        grid=(N_TOK // SGU_TM,),
        in_specs=[pl.BlockSpec((SGU_TM, 2 * SGU_WIDTH), lambda i: (i, 0)), vec, vec,
                  pl.BlockSpec((SGU_GROUPS, SGU_CHUNK, SGU_CHUNK), lambda i: (0, 0, 0)),
                  pl.BlockSpec((SGU_CHUNK, SGU_WIDTH), lambda i: (0, 0))],
        out_specs=pl.BlockSpec((SGU_TM, SGU_WIDTH), lambda i: (i, 0)),
        out_shape=jax.ShapeDtypeStruct((N_TOK, SGU_WIDTH), F32),
        compiler_params=_params(1),
        name="chunk_sgu",
    )(dz, g, b, ws, bs_full)


MERGE_TM = 512


def _merge_kernel(xp_ref, xs_ref, ya_ref, yb_ref, yc_ref, yd_ref, mod_ref, g1_ref, w_in_hbm, bg_ref, wb_ref, wo_ref,
                  g2_ref, rwh_ref, rwl_ref, rb_ref, xm_ref, h2_ref, w_ref, q_ref, cnt_ref, off_ref,
                  wg_ref, sem, *, layer):
    @pl.when(pl.program_id(0) == 0)
    def _():
        cp = pltpu.make_async_copy(w_in_hbm.at[layer, :, pl.ds(MIX_COLS, N_BRANCH * D_MODEL)], wg_ref, sem)
        cp.start()
        cp.wait()

    m = mod_ref[...]
    x = _stream_tile(xp_ref, xs_ref, MERGE_TM)
    h = _norm_mod(x, g1_ref[...], m[0:1], m[1:2]).astype(BF16)
    merged = None
    for i, y_ref in enumerate((ya_ref, yb_ref, yc_ref, yd_ref)):
        logit = jnp.dot(h, wg_ref[:, i * D_MODEL:(i + 1) * D_MODEL], preferred_element_type=F32)
        gate = _sigmoid(logit + bg_ref[:, i * D_MODEL:(i + 1) * D_MODEL])
        term = gate * jnp.dot(y_ref[...].astype(BF16), wb_ref[i], preferred_element_type=F32)
        merged = term if merged is None else merged + term
    out = jnp.dot(merged.astype(BF16), wo_ref[...], preferred_element_type=F32)
    xm = x + m[2:3] * out
    xm_ref[...] = xm
    h2 = _norm_mod(xm, g2_ref[...], m[3:4], m[4:5])
    h2_hi = h2.astype(BF16)
    h2_lo = (h2 - h2_hi.astype(F32)).astype(BF16)
    lg = jnp.dot(h2_hi, rwh_ref[...], preferred_element_type=F32)
    lg = lg + jnp.dot(h2_hi, rwl_ref[...], preferred_element_type=F32)
    lg = lg + jnp.dot(h2_lo, rwh_ref[...], preferred_element_type=F32)
    lg = lg + rb_ref[...]
    h2_ref[...] = h2_hi
    for j in range(MERGE_TM // ROUTE_TM):
        rows = slice(j * ROUTE_TM, (j + 1) * ROUTE_TM)
        w_out, q_out, cnt, off = _route_tile(lg[rows])
        w_ref[rows, :] = w_out
        q_ref[rows, :] = q_out
        cnt_ref[j] = cnt
        off_ref[j] = off


def _merge(stream, ya, yb, yc, yd, mods_l, g1, w_in_b, bg, wb, wo, g2, rwh, rwl, rb, layer):
    def tok(w):
        return pl.BlockSpec((MERGE_TM, w), lambda i: (i, 0))

    def full(*shape):
        return pl.BlockSpec(shape, lambda i: (0,) * len(shape))

    tile_rows = pl.BlockSpec((MERGE_TM // ROUTE_TM, 1, N_EXPERTS), lambda i: (i, 0, 0))
    return pl.pallas_call(
        functools.partial(_merge_kernel, layer=layer),
        grid=(N_TOK // MERGE_TM,),
        in_specs=_stream_specs(MERGE_TM, stream) + [
                  tok(BRANCH_W), tok(BRANCH_W), tok(BRANCH_W), tok(BRANCH_W),
                  pl.BlockSpec((None, N_MOD, D_MODEL), lambda i: (_mod_row(i * MERGE_TM), 0, 0)),
                  full(1, D_MODEL), pl.BlockSpec(memory_space=pl.ANY), full(1, N_BRANCH * D_MODEL),
                  full(N_BRANCH, BRANCH_W, D_MODEL), full(D_MODEL, D_MODEL), full(1, D_MODEL),
                  full(D_MODEL, N_EXPERTS), full(D_MODEL, N_EXPERTS), full(1, N_EXPERTS)],
        out_specs=[tok(D_MODEL), tok(D_MODEL), tok(TOP_K), tok(TOP_K), tile_rows, tile_rows],
        out_shape=[jax.ShapeDtypeStruct((N_TOK, D_MODEL), F32),
                   jax.ShapeDtypeStruct((N_TOK, D_MODEL), BF16),
                   jax.ShapeDtypeStruct((N_TOK, TOP_K), F32),
                   jax.ShapeDtypeStruct((N_TOK, TOP_K), jnp.int32),
                   jax.ShapeDtypeStruct((N_TILES, 1, N_EXPERTS), jnp.int32),
                   jax.ShapeDtypeStruct((N_TILES, 1, N_EXPERTS), jnp.int32)],
        scratch_shapes=[pltpu.VMEM((D_MODEL, N_BRANCH * D_MODEL), BF16), pltpu.SemaphoreType.DMA(())],
        compiler_params=_params(1),
        name="merge",
    )(stream[0], stream[1], ya, yb, yc, yd, mods_l, g1, w_in_b, bg, wb, wo, g2, rwh, rwl, rb)


ROUTE_TM = 256
TILE_ROWS = ROUTE_TM * TOP_K
N_TILES = N_TOK // ROUTE_TM


def _route_tile(lg):
    lane = lax.broadcasted_iota(jnp.int32, lg.shape, 1)
    sels, vals = [], []
    for _ in range(TOP_K):
        mx = lg.max(axis=-1, keepdims=True)
        idx = jnp.where(lg == mx, lane, N_EXPERTS).min(axis=-1, keepdims=True)
        sel = lane == idx
        sels.append(sel)
        vals.append(mx)
        lg = jnp.where(sel, -jnp.inf, lg)
    exps = [jnp.exp(v - vals[0]) for v in vals]
    den = exps[0] + exps[1] + exps[2] + exps[3]
    onehot = jnp.zeros(lg.shape, F32)
    for sel in sels:
        onehot = onehot + sel.astype(F32)
    row = lax.broadcasted_iota(jnp.int32, (ROUTE_TM, ROUTE_TM), 0)
    col = lax.broadcasted_iota(jnp.int32, (ROUTE_TM, ROUTE_TM), 1)
    tri = jnp.where(col < row, 1.0, 0.0).astype(BF16)
    rank = jnp.dot(tri, onehot.astype(BF16), preferred_element_type=F32)
    cnt = jnp.sum(onehot, axis=0, keepdims=True)
    erow = lax.broadcasted_iota(jnp.int32, (N_EXPERTS, N_EXPERTS), 0)
    ecol = lax.broadcasted_iota(jnp.int32, (N_EXPERTS, N_EXPERTS), 1)
    upper = jnp.where(erow < ecol, 1.0, 0.0).astype(BF16)
    off = jnp.dot(jnp.broadcast_to(cnt, (SUBLANES, N_EXPERTS)).astype(BF16), upper,
                  preferred_element_type=F32)[0:1]
    slot = rank + off
    k_lane = lax.broadcasted_iota(jnp.int32, (ROUTE_TM, TOP_K), 1)
    w_out = jnp.zeros((ROUTE_TM, TOP_K), F32)
    q_out = jnp.zeros((ROUTE_TM, TOP_K), F32)
    for k in range(TOP_K):
        w_out = jnp.where(k_lane == k, exps[k] / den, w_out)
        qk = jnp.sum(jnp.where(sels[k], slot, 0.0), axis=-1, keepdims=True)
        q_out = jnp.where(k_lane == k, qk, q_out)
    return w_out, q_out.astype(jnp.int32), cnt.astype(jnp.int32), off.astype(jnp.int32)


PIECE_SIZES = (32, 16, 8, 4, 2, 1)
PIECE_SLOTS = TILE_ROWS // PIECE_SIZES[0]
assert PIECE_SLOTS >= N_EXPERTS


def _compact(valid, *values):
    pos = jnp.cumsum(valid.astype(jnp.int32), axis=1) - 1
    slot = jnp.arange(PIECE_SLOTS, dtype=jnp.int32)
    hit = jnp.logical_and(valid[:, :, None], pos[:, :, None] == slot[None, None, :])
    packed = [jnp.sum(jnp.where(hit, v[:, :, None], 0), axis=1).astype(jnp.int32) for v in values]
    return packed, jnp.sum(valid.astype(jnp.int32), axis=1)


def _piece_lists(cnt, off, row):
    big = PIECE_SIZES[0]
    n_big = cnt // big
    p = jnp.arange(ROUTE_TM // big, dtype=jnp.int32)
    valid = (p[None, None, :] < n_big[:, :, None]).reshape(N_TILES, -1)
    src = (off[:, :, None] + big * p).reshape(N_TILES, -1)
    dst = (row[:, :, None] + big * p).reshape(N_TILES, -1)
    lists = [_compact(valid, src, dst)]
    rem = cnt - n_big * big
    for size in PIECE_SIZES[1:]:
        start = n_big * big + (rem & ~(2 * size - 1))
        lists.append(_compact((rem & size) != 0, off + start, row + start))
    counts = jnp.stack([n for _, n in lists], axis=1).reshape(-1)
    local_rows = jnp.stack([v[0] for v, _ in lists], axis=1).reshape(-1)
    global_rows = jnp.stack([v[1] for v, _ in lists], axis=1).reshape(-1)
    return counts, local_rows, global_rows


def _slab_pieces(tile, count_ref, local_ref, global_ref, fn):
    for k, size in enumerate(PIECE_SIZES):
        lst = tile * len(PIECE_SIZES) + k

        def body(j, carry, lst=lst, size=size):
            fn(local_ref[lst * PIECE_SLOTS + j], global_ref[lst * PIECE_SLOTS + j], size)
            return carry

        lax.fori_loop(0, count_ref[lst], body, 0)


def _rows(ref, row, n_rows):
    start = row * ROW_TILES
    if not isinstance(row, int):
        start = pl.multiple_of(start, ROW_TILES)
    return ref.at[pl.ds(start, n_rows * ROW_TILES)]


def _onehot_rows(q, values=None):
    lane = lax.broadcasted_iota(jnp.int32, (ROUTE_TM, TILE_ROWS), 1)
    s = jnp.zeros((ROUTE_TM, TILE_ROWS), F32)
    for k in range(TOP_K):
        v = 1.0 if values is None else values[:, k:k + 1]
        s = jnp.where(lane == q[:, k:k + 1], v, s)
    return s


def _wait_tile_rows(hbm_ref, buf_slot_ref, sem_slot):
    pltpu.make_async_copy(_rows(hbm_ref, 0, TILE_ROWS), buf_slot_ref, sem_slot).wait()


def _dispatch_kernel(count_ref, local_ref, global_ref, q_ref, h2_ref, xs_ref, buf_ref, sem):
    tile = pl.program_id(0)
    slot = tile % 2
    buf = buf_ref.at[slot]

    @pl.when(tile >= 2)
    def _():
        _wait_tile_rows(xs_ref, buf, sem.at[slot])

    sel = _onehot_rows(q_ref[...]).astype(BF16)
    xg = lax.dot_general(sel, h2_ref[...], (((0,), (0,)), ((), ())), preferred_element_type=F32)
    for c in range(ROW_TILES):
        buf[pl.ds(c, TILE_ROWS, stride=ROW_TILES), :] = xg[:, c * LANES:(c + 1) * LANES]

    def start(local_row, global_row, n_rows):
        pltpu.make_async_copy(_rows(buf, local_row, n_rows), _rows(xs_ref, global_row, n_rows),
                              sem.at[slot]).start()

    _slab_pieces(tile, count_ref, local_ref, global_ref, start)

    @pl.when(tile == N_TILES - 1)
    def _():
        _wait_tile_rows(xs_ref, buf, sem.at[slot])
        _wait_tile_rows(xs_ref, buf_ref.at[1 - slot], sem.at[1 - slot])


def _dispatch(pieces, q, h2):
    grid_spec = pltpu.PrefetchScalarGridSpec(
        num_scalar_prefetch=3,
        grid=(N_TILES,),
        in_specs=[pl.BlockSpec((ROUTE_TM, TOP_K), lambda i, *_: (i, 0)),
                  pl.BlockSpec((ROUTE_TM, D_MODEL), lambda i, *_: (i, 0))],
        out_specs=pl.BlockSpec(memory_space=pl.ANY),
        scratch_shapes=[pltpu.VMEM((2, TILE_ROWS * ROW_TILES, LANES), F32), pltpu.SemaphoreType.DMA((2,))],
    )
    return pl.pallas_call(
        _dispatch_kernel,
        grid_spec=grid_spec,
        out_shape=jax.ShapeDtypeStruct((N_ROWS * ROW_TILES, LANES), F32),
        compiler_params=_params(1),
        name="dispatch",
    )(*pieces, q, h2)


CAST_ROWS = 128
W_SLOTS = 2


def _expert_kernel(blk_exp_ref, nvalid_ref, first_ref, head_ref, slot_ref, next_ref,
                   xs_ref, w1_hbm, b1_ref, w2_hbm, b2_ref, y_ref,
                   w1f_ref, w2f_ref, w1b_ref, w2b_ref, sem, *, layer):
    b = pl.program_id(0)
    e = blk_exp_ref[b]
    nvalid = nvalid_ref[b]
    slot = slot_ref[b]

    def start_weights(expert, ahead):
        s = (slot + ahead) % W_SLOTS
        pltpu.make_async_copy(w1_hbm.at[layer, expert], w1f_ref.at[s], sem.at[0, s]).start()
        pltpu.make_async_copy(w2_hbm.at[layer, expert], w2f_ref.at[s], sem.at[1, s]).start()

    @pl.when(first_ref[b] == 1)
    def _():
        @pl.when(head_ref[b] == 1)
        def _():
            start_weights(e, 0)

        pltpu.make_async_copy(w1_hbm.at[layer, e], w1f_ref.at[slot], sem.at[0, slot]).wait()
        pltpu.make_async_copy(w2_hbm.at[layer, e], w2f_ref.at[slot], sem.at[1, slot]).wait()

        @pl.when(next_ref[b] >= 0)
        def _():
            start_weights(next_ref[b], 1)

        def cast1(i, carry):
            r = pl.multiple_of(i * CAST_ROWS, CAST_ROWS)
            w1b_ref[pl.ds(r, CAST_ROWS), :] = w1f_ref[slot, pl.ds(r, CAST_ROWS), :].astype(BF16)
            return carry

        def cast2(i, carry):
            r = pl.multiple_of(i * CAST_ROWS, CAST_ROWS)
            w2b_ref[pl.ds(r, CAST_ROWS), :] = w2f_ref[slot, pl.ds(r, CAST_ROWS), :].astype(BF16)
            return carry

        lax.fori_loop(0, D_MODEL // CAST_ROWS, cast1, 0)
        lax.fori_loop(0, D_FF // CAST_ROWS, cast2, 0)

    def run_rows(n_rows):
        valid = lax.broadcasted_iota(jnp.int32, (n_rows, LANES), 0) < nvalid
        chunks = [jnp.where(valid, xs_ref[pl.ds(c, n_rows, stride=ROW_TILES), :], 0.0).astype(BF16)
                  for c in range(ROW_TILES)]
        xb = jnp.concatenate(chunks, axis=-1)
        hid = jnp.dot(xb, w1b_ref[...], preferred_element_type=F32) + b1_ref[...]
        glu = jnp.minimum(hid[:, :D_FF], SWIGLU_LIMIT)
        lin = jnp.clip(hid[:, D_FF:], -SWIGLU_LIMIT, SWIGLU_LIMIT)
        act = glu * _sigmoid(SWIGLU_ALPHA * glu) * (lin + 1.0)
        y = jnp.dot(act.astype(BF16), w2b_ref[...], preferred_element_type=F32) + b2_ref[...]
        for c in range(ROW_TILES):
            y_ref[pl.ds(c, n_rows, stride=ROW_TILES), :] = y[:, c * LANES:(c + 1) * LANES]
        if n_rows < EXP_BLOCK:
            rest = (EXP_BLOCK - n_rows) * ROW_TILES
            y_ref[pl.ds(n_rows * ROW_TILES, rest), :] = jnp.zeros((rest, LANES), F32)

    @pl.when(nvalid > EXP_BLOCK // 2)
    def _():
        run_rows(EXP_BLOCK)

    @pl.when(jnp.logical_and(nvalid > 0, nvalid <= EXP_BLOCK // 2))
    def _():
        run_rows(EXP_BLOCK // 2)

    @pl.when(nvalid == 0)
    def _():
        y_ref[...] = jnp.zeros_like(y_ref)


def _experts(tables, xs, w1, b1, w2, b2, layer):
    def blk(b, *_):
        return (b, 0)

    def bias(b, be, *_):
        return (layer, be[b], 0, 0)

    grid_spec = pltpu.PrefetchScalarGridSpec(
        num_scalar_prefetch=6,
        grid=(N_BLOCKS,),
        in_specs=[
            pl.BlockSpec((EXP_BLOCK * ROW_TILES, LANES), blk),
            pl.BlockSpec(memory_space=pl.ANY),
            pl.BlockSpec((None, None, 1, 2 * D_FF), bias),
            pl.BlockSpec(memory_space=pl.ANY),
            pl.BlockSpec((None, None, 1, D_MODEL), bias),
        ],
        out_specs=pl.BlockSpec((EXP_BLOCK * ROW_TILES, LANES), blk),
        scratch_shapes=[pltpu.VMEM((W_SLOTS, D_MODEL, 2 * D_FF), F32), pltpu.VMEM((W_SLOTS, D_FF, D_MODEL), F32),
                        pltpu.VMEM((D_MODEL, 2 * D_FF), BF16), pltpu.VMEM((D_FF, D_MODEL), BF16),
                        pltpu.SemaphoreType.DMA((2, W_SLOTS))],
    )
    return pl.pallas_call(
        functools.partial(_expert_kernel, layer=layer),
        grid_spec=grid_spec,
        out_shape=jax.ShapeDtypeStruct((N_ROWS * ROW_TILES, LANES), F32),
        compiler_params=_params(1),
        name="experts",
    )(*tables, xs, w1, b1.reshape(DEPTH, N_EXPERTS, 1, 2 * D_FF), w2,
      b2.reshape(DEPTH, N_EXPERTS, 1, D_MODEL))


def _split_bf16(x):
    hi = x.astype(BF16)
    return hi, (x - hi.astype(F32)).astype(BF16)


P_TILES = N_P // ROUTE_TM


def _combine_kernel(count_ref, local_ref, global_ref, q_ref, w_ref, y_ref, xm_ref, mod_ref, fg_ref,
                    *refs, final):
    buf_ref, sem = refs[-2:]
    tile = pl.program_id(0)
    slot = tile % 2
    buf = buf_ref.at[slot]

    def fetch(t, s):
        def start(local_row, global_row, n_rows):
            pltpu.make_async_copy(_rows(y_ref, global_row, n_rows), _rows(buf_ref.at[s], local_row, n_rows),
                                  sem.at[s]).start()

        _slab_pieces(t, count_ref, local_ref, global_ref, start)

    @pl.when(tile == 0)
    def _():
        fetch(tile, slot)

    @pl.when(tile + 1 < N_TILES)
    def _():
        fetch(tile + 1, 1 - slot)

    q = q_ref[...]
    pick = _onehot_rows(q).astype(BF16)
    row_w = jnp.sum(_onehot_rows(q, w_ref[...]), axis=0, keepdims=True)
    row_w = jnp.transpose(jnp.broadcast_to(row_w, (LANES, TILE_ROWS)))
    _wait_tile_rows(y_ref, buf, sem.at[slot])
    rows = jnp.concatenate([buf[pl.ds(c, TILE_ROWS, stride=ROW_TILES), :] * row_w for c in range(ROW_TILES)],
                           axis=-1)
    r_hi, r_lo = _split_bf16(rows)
    moe = jnp.dot(pick, r_hi, preferred_element_type=F32) + jnp.dot(pick, r_lo, preferred_element_type=F32)
    m = mod_ref[...]
    x = xm_ref[...] + m[5:6] * moe
    if not final:
        refs[0][...] = x
        return
    xn = x * lax.rsqrt(jnp.mean(x * x, axis=-1, keepdims=True) + EPS) * fg_ref[...]
    yp_ref, ys_ref = refs[:2]

    @pl.when(tile < P_TILES)
    def _():
        yp_ref[...] = xn

    @pl.when(tile >= P_TILES)
    def _():
        ys_ref[...] = xn


def _combine(pieces, q, topw, y, xm, mods_l, final_g, final):
    def tok(w):
        return pl.BlockSpec((ROUTE_TM, w), lambda i, *_: (i, 0))

    if final:
        out_specs = [pl.BlockSpec((ROUTE_TM, D_MODEL), lambda i, *_: (jnp.minimum(i, P_TILES - 1), 0)),
                     pl.BlockSpec((ROUTE_TM, D_MODEL), lambda i, *_: (jnp.maximum(i - P_TILES, 0), 0))]
        out_shape = [jax.ShapeDtypeStruct((N_P, D_MODEL), F32), jax.ShapeDtypeStruct((N_S, D_MODEL), F32)]
    else:
        out_specs = [tok(D_MODEL)]
        out_shape = [jax.ShapeDtypeStruct((N_TOK, D_MODEL), F32)]
    grid_spec = pltpu.PrefetchScalarGridSpec(
        num_scalar_prefetch=3,
        grid=(N_TILES,),
        in_specs=[tok(TOP_K), tok(TOP_K),
                  pl.BlockSpec(memory_space=pl.ANY),
                  tok(D_MODEL),
                  pl.BlockSpec((None, N_MOD, D_MODEL), lambda i, *_: (_mod_row(i * ROUTE_TM), 0, 0)),
                  pl.BlockSpec((1, D_MODEL), lambda i, *_: (0, 0))],
        out_specs=out_specs,
        scratch_shapes=[pltpu.VMEM((2, TILE_ROWS * ROW_TILES, LANES), F32), pltpu.SemaphoreType.DMA((2,))],
    )
    return pl.pallas_call(
        functools.partial(_combine_kernel, final=final),
        grid_spec=grid_spec,
        out_shape=out_shape,
        compiler_params=_params(1),
        name="combine_final" if final else "combine",
    )(*pieces, q, topw, y, xm, mods_l, final_g)


def _rope_tables():
    t = np.arange(DEC_SEQ)
    row = (t // GRID_W).astype(np.float32)
    col = (t % GRID_W).astype(np.float32)
    inv = jnp.asarray(ROPE_THETA, F32) ** (-jnp.arange(ROPE_PAIRS, dtype=F32) / ROPE_PAIRS)
    ang = jnp.concatenate([jnp.asarray(row)[:, None] * inv, jnp.asarray(col)[:, None] * inv], axis=-1)
    cos = jnp.repeat(jnp.cos(ang), 2, axis=-1)
    sin = jnp.repeat(jnp.sin(ang), 2, axis=-1)
    sign = jnp.asarray(np.tile(np.array([-1.0, 1.0], np.float32), HEAD_DIM // 2))
    return jnp.tile(cos, (1, A_HEADS)), jnp.tile(sin * sign, (1, A_HEADS))


def _routing_tables(tile_cnt):
    i32 = jnp.int32
    carry = jnp.cumsum(tile_cnt, axis=0) - tile_cnt
    counts = jnp.sum(tile_cnt, axis=0)
    padded = (counts + EXP_BLOCK - 1) // EXP_BLOCK * EXP_BLOCK
    pad_end = jnp.cumsum(padded)
    pad_start = pad_end - padded
    rowstart = (pad_start[None, :] + carry).astype(i32)
    blk_row = jnp.arange(N_BLOCKS, dtype=i32) * EXP_BLOCK
    blk_exp = jnp.sum((blk_row[:, None] >= pad_end[None, :]).astype(i32), axis=1)
    blk_exp = jnp.minimum(blk_exp, N_EXPERTS - 1)
    eid = jnp.arange(N_EXPERTS, dtype=i32)

    def pick(table, idx):
        return jnp.sum(jnp.where(idx[:, None] == eid[None, :], table[None, :], 0), axis=1).astype(i32)

    blk_start = pick(pad_start, blk_exp)
    nvalid = jnp.clip(pick(counts, blk_exp) - (blk_row - blk_start), 0, EXP_BLOCK).astype(i32)
    first = jnp.logical_and(blk_row == blk_start, nvalid > 0)
    active = counts > 0
    act_rank = jnp.cumsum(active.astype(i32)) - 1
    later = jnp.logical_and(active[None, :], eid[None, :] > eid[:, None])
    nxt = jnp.min(jnp.where(later, eid[None, :], N_EXPERTS), axis=1)
    nxt = jnp.where(nxt == N_EXPERTS, -1, nxt).astype(i32)
    blk_rank = pick(act_rank, blk_exp)
    head = jnp.logical_and(first, blk_rank == 0)
    tables = (blk_exp, nvalid, first.astype(i32), head.astype(i32), (blk_rank % W_SLOTS).astype(i32),
              pick(nxt + 1, blk_exp) - 1)
    return rowstart, tables


def kernel(x_prompt, x_sample, cache_attn_k, cache_attn_v, cache_na_k, cache_na_v, c, c_ctx, w_mod, b_mod, norm1_g, norm2_g, w_in, b_gate, q_norm_g, k_norm_g, na_rpb, conv_w, conv_b, conv_ln_g, conv_ln_b, sgu_ln_g, sgu_ln_b, sgu_w, sgu_b, w_branch, w_out, router_w, router_b, exp_w1, exp_b1, exp_w2, exp_b2, final_g):
    stream = (x_prompt.reshape(N_P, D_MODEL), x_sample.reshape(N_S, D_MODEL), 0)
    cvec =jnp.zeros((SUBLANES, D_MODEL), F32).at[0].set(c_ctx).at[1:1 + DEC_BATCH].set(c)
    mods = _modulation(cvec, w_mod, b_mod).reshape(DEPTH, SUBLANES, N_MOD, D_MODEL)
    cos_t, sin_t = _rope_tables()
    cak = cache_attn_k.reshape(DEC_BATCH, DEPTH, PAST_LEN, A_KV)
    cav = cache_attn_v.reshape(DEC_BATCH, DEPTH, PAST_LEN, A_KV)
    cbk = cache_na_k.reshape(DEC_BATCH, DEPTH, PAST_LEN, B_W)
    cbv = cache_na_v.reshape(DEC_BATCH, DEPTH, PAST_LEN, B_W)
    w_in_b = w_in.astype(BF16)
    w_br = w_branch.astype(BF16)
    w_o = w_out.astype(BF16)
    rw_hi = router_w.astype(BF16)
    rw_lo = (router_w - rw_hi.astype(F32)).astype(BF16)
    final_g2 = final_g.reshape(1, D_MODEL)

    caches = ()
    outs = None
    for l in range(DEPTH):
        mods_l = mods[l]
        g1 = norm1_g[l].reshape(1, D_MODEL)
        gq = jnp.tile(q_norm_g[l], A_HEADS).reshape(1, A_Q)
        gk = jnp.tile(k_norm_g[l], A_KV_HEADS).reshape(1, A_KV)
        aq, ak, av, bq, bk, bv, cz, dz = _in_proj(stream, mods_l, g1, w_in_b, l)
        ya, yb, *caches = _prompt_attn(aq, ak, av, bq, bk, bv, gq, gk, caches, l)
        ya = _sample_attn(aq, ak, av, cak, cav, cos_t, sin_t, gq, gk, ya, l)
        yb = _na_attn(bq, bk, bv, cbk, cbv, _na_bias(na_rpb[l]), yb, l)
        cw = conv_w[l]
        cb = conv_b[l].reshape(1, C_WIDTH)
        cg = conv_ln_g[l].reshape(1, C_WIDTH)
        cbb = conv_ln_b[l].reshape(1, C_WIDTH)
        yc = _conv_call(cz, cw, cb, cg, cbb, SEQ, 0, BATCH)
        yc = _conv_call(cz, cw, cb, cg, cbb, DEC_SEQ, N_P // DEC_SEQ, DEC_BATCH, partial_out=yc)
        bs_full = jnp.repeat(sgu_b[l].T, SGU_GW, axis=1)
        yd = _sgu(dz, sgu_ln_g[l].reshape(1, SGU_WIDTH), sgu_ln_b[l].reshape(1, SGU_WIDTH),
                  sgu_w[l].astype(BF16), bs_full)
        xm, h2, top_w, q, tile_cnt, tile_off = _merge(
            stream, ya, yb, yc, yd, mods_l, g1, w_in_b, b_gate[l].reshape(1, N_BRANCH * D_MODEL), w_br[l], w_o[l],
            norm2_g[l].reshape(1, D_MODEL), rw_hi[l], rw_lo[l], router_b[l].reshape(1, N_EXPERTS), l)
        tile_cnt = tile_cnt.reshape(N_TILES, N_EXPERTS)
        rowstart, tables = _routing_tables(tile_cnt)
        pieces = _piece_lists(tile_cnt, tile_off.reshape(N_TILES, N_EXPERTS), rowstart)
        xs = _dispatch(pieces, q, h2)
        y = _experts(tables, xs, exp_w1, exp_b1, exp_w2, exp_b2, l)
        outs = _combine(pieces, q, top_w, y, xm, mods_l, final_g2, l == DEPTH - 1)
        stream = (outs[0], outs[0], N_P)

    new_k, new_v, new_bk, new_bv = caches
    return (outs[0].reshape(BATCH, SEQ, D_MODEL), outs[1].reshape(DEC_BATCH, DEC_SEQ, D_MODEL),
            new_k.reshape(BATCH, DEPTH, SEQ, A_KV_HEADS, HEAD_DIM),
            new_v.reshape(BATCH, DEPTH, SEQ, A_KV_HEADS, HEAD_DIM),
            new_bk.reshape(BATCH, DEPTH, SEQ, B_HEADS, HEAD_DIM),
            new_bv.reshape(BATCH, DEPTH, SEQ, B_HEADS, HEAD_DIM))
```

```python
import functools

import numpy as np
import jax
import jax.numpy as jnp
from jax import lax
from jax.experimental import pallas as pl
from jax.experimental.pallas import tpu as pltpu

D_MODEL = 1024
BATCH = 32
SEQ = 256
DEPTH = 2
DEC_BATCH = 2
DEC_SEQ = 1024
PAST_LEN = 512
GRID_W = 64
HEAD_DIM = 64
A_HEADS = 4
A_KV_HEADS = 2
B_HEADS = 4
NA_ROWS = 8
NA_COLS = 16
C_WIDTH = 256
CONV_WIDTH = 31
SGU_WIDTH = 256
SGU_GROUPS = 4
SGU_CHUNK = 128
N_BRANCH = 4
BRANCH_W = 256
N_EXPERTS = 32
TOP_K = 4
D_FF = 1024
SWIGLU_ALPHA = 1.702
SWIGLU_LIMIT = 7.0
MOE_BLOCK = 128
ROPE_THETA = 10000.0
ROPE_PAIRS = HEAD_DIM // 4
N_MOD = 6
EPS = 1e-6
NEG_INF = -1e30

A_Q = A_HEADS * HEAD_DIM
A_KV = A_KV_HEADS * HEAD_DIM
B_W = B_HEADS * HEAD_DIM
MIX_SIZES = (A_Q, A_KV, A_KV, B_W, B_W, B_W, 2 * C_WIDTH, 2 * SGU_WIDTH)
MIX_COLS = sum(MIX_SIZES)

N_P = BATCH * SEQ
N_S = DEC_BATCH * DEC_SEQ
N_TOK = N_P + N_S
N_ASSIGN = N_TOK * TOP_K
EXP_BLOCK = 512
N_BLOCKS = N_ASSIGN // EXP_BLOCK + N_EXPERTS
N_ROWS = N_BLOCKS * EXP_BLOCK
GRID_ROWS = DEC_SEQ // GRID_W
NA_WR = min(NA_ROWS, GRID_ROWS)
N_LOC = NA_WR * GRID_W

SUBLANES = 8
LANES = 128
ROW_TILES = D_MODEL // LANES
VMEM_LIMIT = 56 * 1024 * 1024

F32 = jnp.float32
BF16 = jnp.bfloat16


def _params(n_axes, vmem=None):
    return pltpu.CompilerParams(
        dimension_semantics=("arbitrary",) * n_axes,
        vmem_limit_bytes=vmem if vmem is not None else VMEM_LIMIT)


def _mod_row(start):
    return jnp.where(start < N_P, 0, 1 + (start - N_P) // DEC_SEQ)


def _bdot(a, b):
    return jnp.dot(a.astype(BF16), b.astype(BF16), preferred_element_type=F32)


def _bdot_nt(a, b):
    return lax.dot_general(a.astype(BF16), b.astype(BF16), (((1,), (1,)), ((), ())),
                           preferred_element_type=F32)


def _sigmoid(x):
    return 0.5 * jnp.tanh(0.5 * x) + 0.5


MOD_TN = 1536


def _mod_kernel(c_ref, w_ref, b_ref, o_ref):
    c = c_ref[...]
    s = c * _sigmoid(c)
    o_ref[...] = _bdot(s, w_ref[...]) + b_ref[...]


def _modulation(cvec, w_mod, b_mod):
    n_col = N_MOD * D_MODEL
    return pl.pallas_call(
        _mod_kernel,
        grid=(DEPTH, n_col // MOD_TN),
        in_specs=[
            pl.BlockSpec((SUBLANES, D_MODEL), lambda l, j: (0, 0)),
            pl.BlockSpec((None, D_MODEL, MOD_TN), lambda l, j: (l, 0, j)),
            pl.BlockSpec((None, 1, MOD_TN), lambda l, j: (l, 0, j)),
        ],
        out_specs=pl.BlockSpec((None, SUBLANES, MOD_TN), lambda l, j: (l, 0, j)),
        out_shape=jax.ShapeDtypeStruct((DEPTH, SUBLANES, n_col), F32),
        compiler_params=_params(2),
        name="modulation",
    )(cvec, w_mod, b_mod.reshape(DEPTH, 1, n_col))


IN_TM = 512


def _norm_mod(x, g, shift, scale):
    y = x * lax.rsqrt(jnp.mean(x * x, axis=-1, keepdims=True) + EPS) * g
    return y * (1.0 + scale) + shift


def _stream_specs(tm, stream):
    p_tiles = N_P // tm
    s_first = stream[2] // tm
    return [pl.BlockSpec((tm, D_MODEL), lambda i: (jnp.minimum(i, p_tiles - 1), 0)),
            pl.BlockSpec((tm, D_MODEL), lambda i: (jnp.maximum(i - p_tiles, 0) + s_first, 0))]


def _stream_tile(xp_ref, xs_ref, tm):
    return jnp.where(pl.program_id(0) < N_P // tm, xp_ref[...], xs_ref[...])


def _in_kernel(xp_ref, xs_ref, mod_ref, g_ref, w_ref, *out_refs):
    m = mod_ref[...]
    h = _norm_mod(_stream_tile(xp_ref, xs_ref, IN_TM), g_ref[...], m[0:1], m[1:2])
    z = jnp.dot(h.astype(BF16), w_ref[...], preferred_element_type=F32)
    off = 0
    for o_ref, sz in zip(out_refs, MIX_SIZES):
        o_ref[...] = z[:, off:off + sz]
        off += sz


def _in_proj(stream, mods_l, g1, w_in_b, layer):
    return pl.pallas_call(
        _in_kernel,
        grid=(N_TOK // IN_TM,),
        in_specs=_stream_specs(IN_TM, stream) + [
            pl.BlockSpec((None, N_MOD, D_MODEL), lambda i: (_mod_row(i * IN_TM), 0, 0)),
            pl.BlockSpec((1, D_MODEL), lambda i: (0, 0)),
            pl.BlockSpec((None, D_MODEL, MIX_COLS), lambda i: (layer, 0, 0)),
        ],
        out_specs=[pl.BlockSpec((IN_TM, sz), lambda i: (i, 0)) for sz in MIX_SIZES],
        out_shape=[jax.ShapeDtypeStruct((N_TOK, sz), F32) for sz in MIX_SIZES],
        compiler_params=_params(1),
        name="in_proj",
    )(stream[0], stream[1], mods_l, g1, w_in_b)


def _head_rms(x, g):
    width = x.shape[-1]
    seg_r = lax.broadcasted_iota(jnp.int32, (width, width), 0) // HEAD_DIM
    seg_c = lax.broadcasted_iota(jnp.int32, (width, width), 1) // HEAD_DIM
    avg = jnp.where(seg_r == seg_c, 1.0 / HEAD_DIM, 0.0).astype(BF16)
    hi, lo = _split_bf16(x * x)
    ms = jnp.dot(hi, avg, preferred_element_type=F32) + jnp.dot(lo, avg, preferred_element_type=F32)
    return x * lax.rsqrt(ms + EPS) * g


def _softmax_pv(score_parts, value_parts):
    m = score_parts[0].max(axis=-1, keepdims=True)
    for s in score_parts[1:]:
        m = jnp.maximum(m, s.max(axis=-1, keepdims=True))
    den = None
    acc = None
    for s, v in zip(score_parts, value_parts):
        e = jnp.exp(s - m)
        d = e.sum(axis=-1, keepdims=True)
        a = _bdot(e, v)
        den = d if den is None else den + d
        acc = a if acc is None else acc + a
    return acc / den


def _head(x, h):
    return x[:, h * HEAD_DIM:(h + 1) * HEAD_DIM]


SCALE = HEAD_DIM ** -0.5


def _prompt_attn_kernel(aq_ref, ak_ref, av_ref, bq_ref, bk_ref, bv_ref, gq_ref, gk_ref, *refs):
    ya_ref, yb_ref, nk_ref, nv_ref, nbk_ref, nbv_ref = refs[-6:]
    aq = _head_rms(aq_ref[...], gq_ref[...]) * SCALE
    ak = _head_rms(ak_ref[...], gk_ref[...])
    av = av_ref[...]
    for ref, val in ((nk_ref, ak), (nv_ref, av), (nbk_ref, bk_ref[...]), (nbv_ref, bv_ref[...])):
        n_heads = val.shape[-1] // HEAD_DIM
        for h in range(n_heads):
            ref[pl.ds(h, SEQ, stride=n_heads), :] = _head(val, h)
    grp = A_HEADS // A_KV_HEADS
    outs = []
    for h in range(A_HEADS):
        s = _bdot_nt(_head(aq, h), _head(ak, h // grp))
        outs.append(_softmax_pv([s], [_head(av, h // grp)]))
    ya_ref[...] = jnp.concatenate(outs, axis=-1)
    bq = bq_ref[...] * SCALE
    bk = bk_ref[...]
    bv = bv_ref[...]
    outs = []
    for h in range(B_HEADS):
        s = _bdot_nt(_head(bq, h), _head(bk, h))
        outs.append(_softmax_pv([s], [_head(bv, h)]))
    yb_ref[...] = jnp.concatenate(outs, axis=-1)


def _prompt_attn(aq, ak, av, bq, bk, bv, gq, gk, caches, layer):
    def spec(w):
        return pl.BlockSpec((SEQ, w), lambda b: (b, 0))

    def cache_spec(n_heads):
        return pl.BlockSpec((SEQ * n_heads, HEAD_DIM), lambda b: (b * DEPTH + layer, 0))

    cache_heads = (A_KV_HEADS, A_KV_HEADS, B_HEADS, B_HEADS)
    n_in = 8
    return pl.pallas_call(
        _prompt_attn_kernel,
        grid=(BATCH,),
        in_specs=[spec(A_Q), spec(A_KV), spec(A_KV), spec(B_W), spec(B_W), spec(B_W),
                  pl.BlockSpec((1, A_Q), lambda b: (0, 0)),
                  pl.BlockSpec((1, A_KV), lambda b: (0, 0))]
        + [pl.BlockSpec(memory_space=pl.ANY) for _ in caches],
        out_specs=[spec(A_Q), spec(B_W)] + [cache_spec(n) for n in cache_heads],
        out_shape=[jax.ShapeDtypeStruct((N_TOK, A_Q), F32),
                   jax.ShapeDtypeStruct((N_TOK, B_W), F32)]
        + [jax.ShapeDtypeStruct((BATCH * DEPTH * SEQ * n, HEAD_DIM), F32) for n in cache_heads],
        input_output_aliases={n_in + j: 2 + j for j in range(len(caches))},
        compiler_params=_params(1),
        name="prompt_attn",
    )(aq, ak, av, bq, bk, bv, gq, gk, *caches)


QB = 128


def _rope(x, cos, sin_signed):
    n = x.shape[-1]
    nxt = pltpu.roll(x, n - 1, 1)
    prv = pltpu.roll(x, 1, 1)
    even = (lax.broadcasted_iota(jnp.int32, x.shape, 1) % 2) == 0
    return x * cos + jnp.where(even, nxt, prv) * sin_signed


def _sample_attn_kernel(q_ref, k_ref, v_ref, ck_ref, cv_ref, cosq_ref, sinq_ref, cosk_ref, sink_ref,
                        gq_ref, gk_ref, ya_prompt_ref, o_ref):
    del ya_prompt_ref
    q = _rope(_head_rms(q_ref[...], gq_ref[...]), cosq_ref[...], sinq_ref[...]) * SCALE
    k = _rope(_head_rms(k_ref[...], gk_ref[...]), cosk_ref[...], sink_ref[...])
    v = v_ref[...]
    ck = ck_ref[...]
    cv = cv_ref[...]
    grp = A_HEADS // A_KV_HEADS
    outs = []
    for h in range(A_HEADS):
        j = h // grp
        qh = _head(q, h)
        s1 = _bdot_nt(qh, _head(k, j))
        s2 = _bdot_nt(qh, _head(ck, j))
        outs.append(_softmax_pv([s1, s2], [_head(v, j), _head(cv, j)]))
    o_ref[...] = jnp.concatenate(outs, axis=-1)


def _sample_attn(aq, ak, av, cache_k, cache_v, cos_t, sin_t, gq, gk, ya, layer):
    nqb = DEC_SEQ // QB
    q0 = N_P // QB
    k0 = N_P // DEC_SEQ
    return pl.pallas_call(
        _sample_attn_kernel,
        grid=(DEC_BATCH, nqb),
        in_specs=[
            pl.BlockSpec((QB, A_Q), lambda b, i: (q0 + b * nqb + i, 0)),
            pl.BlockSpec((DEC_SEQ, A_KV), lambda b, i: (k0 + b, 0)),
            pl.BlockSpec((DEC_SEQ, A_KV), lambda b, i: (k0 + b, 0)),
            pl.BlockSpec((None, None, PAST_LEN, A_KV), lambda b, i: (b, layer, 0, 0)),
            pl.BlockSpec((None, None, PAST_LEN, A_KV), lambda b, i: (b, layer, 0, 0)),
            pl.BlockSpec((QB, A_Q), lambda b, i: (i, 0)),
            pl.BlockSpec((QB, A_Q), lambda b, i: (i, 0)),
            pl.BlockSpec((DEC_SEQ, A_KV), lambda b, i: (0, 0)),
            pl.BlockSpec((DEC_SEQ, A_KV), lambda b, i: (0, 0)),
            pl.BlockSpec((1, A_Q), lambda b, i: (0, 0)),
            pl.BlockSpec((1, A_KV), lambda b, i: (0, 0)),
            pl.BlockSpec(memory_space=pl.ANY),
        ],
        out_specs=pl.BlockSpec((QB, A_Q), lambda b, i: (q0 + b * nqb + i, 0)),
        out_shape=jax.ShapeDtypeStruct((N_TOK, A_Q), F32),
        input_output_aliases={11: 0},
        compiler_params=_params(2),
        name="sample_attn",
    )(aq, ak, av, cache_k, cache_v, cos_t, sin_t, cos_t, sin_t, gq, gk, ya)


N_ROW_OFF = 2 * NA_ROWS - 1
N_COL_OFF = 2 * NA_COLS - 1
NA_PAIRS = N_ROW_OFF - 1
assert NA_WR == NA_ROWS and NA_WR % 2 == 0 and 2 * GRID_W == LANES


def _na_bias_kernel(rpb_ref, o_ref):
    h = pl.program_id(0)
    qc = lax.broadcasted_iota(jnp.int32, (GRID_W, LANES), 0)
    lane = lax.broadcasted_iota(jnp.int32, (GRID_W, LANES), 1)
    right = lane >= GRID_W
    kc = jnp.where(right, lane - GRID_W, lane)
    c_start = jnp.clip(qc - NA_COLS // 2, 0, GRID_W - NA_COLS)
    col_in = jnp.logical_and(kc >= c_start, kc < c_start + NA_COLS)
    col_off = jnp.clip(kc - qc + NA_COLS - 1, 0, N_COL_OFF - 1)
    for p in range(NA_PAIRS):
        acc = jnp.zeros((GRID_W, LANES), F32)
        for o in range(N_COL_OFF):
            left_v = rpb_ref[(h * N_ROW_OFF + p) * N_COL_OFF + o]
            right_v = rpb_ref[(h * N_ROW_OFF + p + 1) * N_COL_OFF + o]
            acc = jnp.where(col_off == o, jnp.where(right, right_v, left_v), acc)
        o_ref[p] = jnp.where(col_in, acc, NEG_INF)


def _na_bias(rpb):
    return pl.pallas_call(
        _na_bias_kernel,
        grid_spec=pltpu.PrefetchScalarGridSpec(
            num_scalar_prefetch=1,
            grid=(B_HEADS,),
            in_specs=[],
            out_specs=pl.BlockSpec((None, NA_PAIRS, GRID_W, LANES), lambda h, *_: (h, 0, 0, 0)),
        ),
        out_shape=jax.ShapeDtypeStruct((B_HEADS, NA_PAIRS, GRID_W, LANES), F32),
        compiler_params=_params(1),
        name="na_bias",
    )(rpb.reshape(-1))


def _na_kernel(q_ref, k_ref, v_ref, ck_ref, cv_ref, bias_ref, yb_prompt_ref, o_ref):
    del yb_prompt_ref
    r = pl.program_id(1)
    r_start = jnp.clip(r - NA_WR // 2, 0, GRID_ROWS - NA_WR)
    base = pl.multiple_of(r_start * GRID_W, GRID_W)
    row_off0 = r_start - r + NA_ROWS - 1
    q = q_ref[...] * SCALE
    kb = k_ref[pl.ds(base, N_LOC), :]
    vb = v_ref[pl.ds(base, N_LOC), :]
    ck = ck_ref[...]
    cv = cv_ref[...]
    outs = []
    for h in range(B_HEADS):
        qh = _head(q, h)
        bias = jnp.concatenate([bias_ref[h, row_off0 + 2 * j] for j in range(NA_WR // 2)], axis=-1)
        s1 = _bdot_nt(qh, _head(kb, h)) + bias
        s2 = _bdot_nt(qh, _head(ck, h))
        outs.append(_softmax_pv([s1, s2], [_head(vb, h), _head(cv, h)]))
    o_ref[...] = jnp.concatenate(outs, axis=-1)


def _na_attn(bq, bk, bv, cache_k, cache_v, bias, yb, layer):
    q0 = N_P // GRID_W
    k0 = N_P // DEC_SEQ
    return pl.pallas_call(
        _na_kernel,
        grid=(DEC_BATCH, GRID_ROWS),
        in_specs=[
            pl.BlockSpec((GRID_W, B_W), lambda b, r: (q0 + b * GRID_ROWS + r, 0)),
            pl.BlockSpec((DEC_SEQ, B_W), lambda b, r: (k0 + b, 0)),
            pl.BlockSpec((DEC_SEQ, B_W), lambda b, r: (k0 + b, 0)),
            pl.BlockSpec((None, None, PAST_LEN, B_W), lambda b, r: (b, layer, 0, 0)),
            pl.BlockSpec((None, None, PAST_LEN, B_W), lambda b, r: (b, layer, 0, 0)),
            pl.BlockSpec((B_HEADS, NA_PAIRS, GRID_W, LANES), lambda b, r: (0, 0, 0, 0)),
            pl.BlockSpec(memory_space=pl.ANY),
        ],
        out_specs=pl.BlockSpec((GRID_W, B_W), lambda b, r: (q0 + b * GRID_ROWS + r, 0)),
        out_shape=jax.ShapeDtypeStruct((N_TOK, B_W), F32),
        input_output_aliases={6: 0},
        compiler_params=_params(2),
        name="na_attn",
    )(bq, bk, bv, cache_k, cache_v, bias, yb)


CONV_PAD = 16
CONV_CHUNK = 64


def _layer_norm(x, g, b):
    mu = jnp.mean(x, axis=-1, keepdims=True)
    xc = x - mu
    var = jnp.mean(xc * xc, axis=-1, keepdims=True)
    return xc * lax.rsqrt(var + EPS) * g + b


def _conv_kernel(z_ref, w_ref, cb_ref, g_ref, b_ref, *refs, s_len):
    o_ref, pad_ref = refs[-2:]
    z = z_ref[...]
    u = z[:, :C_WIDTH] * _sigmoid(z[:, C_WIDTH:])
    pad_ref[pl.ds(0, CONV_PAD), :] = jnp.zeros((CONV_PAD, C_WIDTH), F32)
    pad_ref[pl.ds(CONV_PAD + s_len, CONV_PAD), :] = jnp.zeros((CONV_PAD, C_WIDTH), F32)
    pad_ref[pl.ds(CONV_PAD, s_len), :] = u
    w = w_ref[...]
    shift = CONV_PAD - CONV_WIDTH // 2

    def chunk(c, carry):
        base = pl.multiple_of(c * CONV_CHUNK, CONV_CHUNK)
        acc = jnp.zeros((CONV_CHUNK, C_WIDTH), F32)
        for r in range(SUBLANES):
            part = None
            for k in range(CONV_WIDTH):
                if (k + shift) % SUBLANES != r:
                    continue
                rows = pad_ref[pl.ds(base + (k + shift - r), CONV_CHUNK + SUBLANES), :]
                term = rows * w[k:k + 1]
                part = term if part is None else part + term
            if part is not None:
                acc = acc + part[r:r + CONV_CHUNK]
        y = _layer_norm(acc + cb_ref[...], g_ref[...], b_ref[...])
        o_ref[pl.ds(base, CONV_CHUNK), :] = y * _sigmoid(y)
        return carry

    lax.fori_loop(0, s_len // CONV_CHUNK, chunk, 0)


def _conv_call(cz, w, cb, g, b, s_len, first_blk, n_seq, partial_out=None):
    vec = pl.BlockSpec((1, C_WIDTH), lambda i: (0, 0))
    extra = [] if partial_out is None else [partial_out]
    return pl.pallas_call(
        functools.partial(_conv_kernel, s_len=s_len),
        grid=(n_seq,),
        in_specs=[pl.BlockSpec((s_len, 2 * C_WIDTH), lambda i: (first_blk + i, 0)),
                  pl.BlockSpec((CONV_WIDTH, C_WIDTH), lambda i: (0, 0)), vec, vec, vec]
        + [pl.BlockSpec(memory_space=pl.ANY) for _ in extra],
        out_specs=pl.BlockSpec((s_len, C_WIDTH), lambda i: (first_blk + i, 0)),
        out_shape=jax.ShapeDtypeStruct((N_TOK, C_WIDTH), F32),
        input_output_aliases={5: 0} if extra else {},
        scratch_shapes=[pltpu.VMEM((s_len + 2 * CONV_PAD, C_WIDTH), F32)],
        compiler_params=_params(1),
        name="conformer_conv_%d" % s_len,
    )(cz, w, cb, g, b, *extra)


SGU_TM = 512
SGU_GW = SGU_WIDTH // SGU_GROUPS


def _sgu_kernel(z_ref, g_ref, b_ref, ws_ref, bs_ref, o_ref):
    z = z_ref[...]
    z = 0.5 * z * (1.0 + lax.erf(z * (2.0 ** -0.5)))
    u = z[:, :SGU_WIDTH]
    v = _layer_norm(z[:, SGU_WIDTH:], g_ref[...], b_ref[...])
    for c in range(SGU_TM // SGU_CHUNK):
        vc = v[c * SGU_CHUNK:(c + 1) * SGU_CHUNK]
        parts = [_bdot(ws_ref[g], vc[:, g * SGU_GW:(g + 1) * SGU_GW]) for g in range(SGU_GROUPS)]
        mixed = jnp.concatenate(parts, axis=-1) + bs_ref[...]
        o_ref[pl.ds(c * SGU_CHUNK, SGU_CHUNK), :] = u[c * SGU_CHUNK:(c + 1) * SGU_CHUNK] * mixed


def _sgu(dz, g, b, ws, bs_full):
    vec = pl.BlockSpec((1, SGU_WIDTH), lambda i: (0, 0))
    return pl.pallas_call(
        _sgu_kernel,
        grid=(N_TOK // SGU_TM,),
        in_specs=[pl.BlockSpec((SGU_TM, 2 * SGU_WIDTH), lambda i: (i, 0)), vec, vec,
                  pl.BlockSpec((SGU_GROUPS, SGU_CHUNK, SGU_CHUNK), lambda i: (0, 0, 0)),
                  pl.BlockSpec((SGU_CHUNK, SGU_WIDTH), lambda i: (0, 0))],
        out_specs=pl.BlockSpec((SGU_TM, SGU_WIDTH), lambda i: (i, 0)),
        out_shape=jax.ShapeDtypeStruct((N_TOK, SGU_WIDTH), F32),
        compiler_params=_params(1),
        name="chunk_sgu",
    )(dz, g, b, ws, bs_full)


MERGE_TM = 512


def _merge_kernel(xp_ref, xs_ref, ya_ref, yb_ref, yc_ref, yd_ref, mod_ref, g1_ref, w_in_hbm, bg_ref, wb_ref, wo_ref,
                  g2_ref, rwh_ref, rwl_ref, rb_ref, xm_ref, h2_ref, w_ref, q_ref, cnt_ref, off_ref,
                  wg_ref, sem, *, layer):
    @pl.when(pl.program_id(0) == 0)
    def _():
        cp = pltpu.make_async_copy(w_in_hbm.at[layer, :, pl.ds(MIX_COLS, N_BRANCH * D_MODEL)], wg_ref, sem)
        cp.start()
        cp.wait()

    m = mod_ref[...]
    x = _stream_tile(xp_ref, xs_ref, MERGE_TM)
    h = _norm_mod(x, g1_ref[...], m[0:1], m[1:2]).astype(BF16)
    merged = None
    for i, y_ref in enumerate((ya_ref, yb_ref, yc_ref, yd_ref)):
        logit = jnp.dot(h, wg_ref[:, i * D_MODEL:(i + 1) * D_MODEL], preferred_element_type=F32)
        gate = _sigmoid(logit + bg_ref[:, i * D_MODEL:(i + 1) * D_MODEL])
        term = gate * jnp.dot(y_ref[...].astype(BF16), wb_ref[i], preferred_element_type=F32)
        merged = term if merged is None else merged + term
    out = jnp.dot(merged.astype(BF16), wo_ref[...], preferred_element_type=F32)
    xm = x + m[2:3] * out
    xm_ref[...] = xm
    h2 = _norm_mod(xm, g2_ref[...], m[3:4], m[4:5])
    h2_hi = h2.astype(BF16)
    h2_lo = (h2 - h2_hi.astype(F32)).astype(BF16)
    lg = jnp.dot(h2_hi, rwh_ref[...], preferred_element_type=F32)
    lg = lg + jnp.dot(h2_hi, rwl_ref[...], preferred_element_type=F32)
    lg = lg + jnp.dot(h2_lo, rwh_ref[...], preferred_element_type=F32)
    lg = lg + rb_ref[...]
    h2_ref[...] = h2_hi
    for j in range(MERGE_TM // ROUTE_TM):
        rows = slice(j * ROUTE_TM, (j + 1) * ROUTE_TM)
        w_out, q_out, cnt, off = _route_tile(lg[rows])
        w_ref[rows, :] = w_out
        q_ref[rows, :] = q_out
        cnt_ref[j] = cnt
        off_ref[j] = off


def _merge(stream, ya, yb, yc, yd, mods_l, g1, w_in_b, bg, wb, wo, g2, rwh, rwl, rb, layer):
    def tok(w):
        return pl.BlockSpec((MERGE_TM, w), lambda i: (i, 0))

    def full(*shape):
        return pl.BlockSpec(shape, lambda i: (0,) * len(shape))

    tile_rows = pl.BlockSpec((MERGE_TM // ROUTE_TM, 1, N_EXPERTS), lambda i: (i, 0, 0))
    return pl.pallas_call(
        functools.partial(_merge_kernel, layer=layer),
        grid=(N_TOK // MERGE_TM,),
        in_specs=_stream_specs(MERGE_TM, stream) + [
                  tok(BRANCH_W), tok(BRANCH_W), tok(BRANCH_W), tok(BRANCH_W),
                  pl.BlockSpec((None, N_MOD, D_MODEL), lambda i: (_mod_row(i * MERGE_TM), 0, 0)),
                  full(1, D_MODEL), pl.BlockSpec(memory_space=pl.ANY), full(1, N_BRANCH * D_MODEL),
                  full(N_BRANCH, BRANCH_W, D_MODEL), full(D_MODEL, D_MODEL), full(1, D_MODEL),
                  full(D_MODEL, N_EXPERTS), full(D_MODEL, N_EXPERTS), full(1, N_EXPERTS)],
        out_specs=[tok(D_MODEL), tok(D_MODEL), tok(TOP_K), tok(TOP_K), tile_rows, tile_rows],
        out_shape=[jax.ShapeDtypeStruct((N_TOK, D_MODEL), F32),
                   jax.ShapeDtypeStruct((N_TOK, D_MODEL), BF16),
                   jax.ShapeDtypeStruct((N_TOK, TOP_K), F32),
                   jax.ShapeDtypeStruct((N_TOK, TOP_K), jnp.int32),
                   jax.ShapeDtypeStruct((N_TILES, 1, N_EXPERTS), jnp.int32),
                   jax.ShapeDtypeStruct((N_TILES, 1, N_EXPERTS), jnp.int32)],
        scratch_shapes=[pltpu.VMEM((D_MODEL, N_BRANCH * D_MODEL), BF16), pltpu.SemaphoreType.DMA(())],
        compiler_params=_params(1),
        name="merge",
    )(stream[0], stream[1], ya, yb, yc, yd, mods_l, g1, w_in_b, bg, wb, wo, g2, rwh, rwl, rb)


ROUTE_TM = 256
TILE_ROWS = ROUTE_TM * TOP_K
N_TILES = N_TOK // ROUTE_TM


def _route_tile(lg):
    lane = lax.broadcasted_iota(jnp.int32, lg.shape, 1)
    sels, vals = [], []
    for _ in range(TOP_K):
        mx = lg.max(axis=-1, keepdims=True)
        idx = jnp.where(lg == mx, lane, N_EXPERTS).min(axis=-1, keepdims=True)
        sel = lane == idx
        sels.append(sel)
        vals.append(mx)
        lg = jnp.where(sel, -jnp.inf, lg)
    exps = [jnp.exp(v - vals[0]) for v in vals]
    den = exps[0] + exps[1] + exps[2] + exps[3]
    onehot = jnp.zeros(lg.shape, F32)
    for sel in sels:
        onehot = onehot + sel.astype(F32)
    row = lax.broadcasted_iota(jnp.int32, (ROUTE_TM, ROUTE_TM), 0)
    col = lax.broadcasted_iota(jnp.int32, (ROUTE_TM, ROUTE_TM), 1)
    tri = jnp.where(col < row, 1.0, 0.0).astype(BF16)
    rank = jnp.dot(tri, onehot.astype(BF16), preferred_element_type=F32)
    cnt = jnp.sum(onehot, axis=0, keepdims=True)
    erow = lax.broadcasted_iota(jnp.int32, (N_EXPERTS, N_EXPERTS), 0)
    ecol = lax.broadcasted_iota(jnp.int32, (N_EXPERTS, N_EXPERTS), 1)
    upper = jnp.where(erow < ecol, 1.0, 0.0).astype(BF16)
    off = jnp.dot(jnp.broadcast_to(cnt, (SUBLANES, N_EXPERTS)).astype(BF16), upper,
                  preferred_element_type=F32)[0:1]
    slot = rank + off
    k_lane = lax.broadcasted_iota(jnp.int32, (ROUTE_TM, TOP_K), 1)
    w_out = jnp.zeros((ROUTE_TM, TOP_K), F32)
    q_out = jnp.zeros((ROUTE_TM, TOP_K), F32)
    for k in range(TOP_K):
        w_out = jnp.where(k_lane == k, exps[k] / den, w_out)
        qk = jnp.sum(jnp.where(sels[k], slot, 0.0), axis=-1, keepdims=True)
        q_out = jnp.where(k_lane == k, qk, q_out)
    return w_out, q_out.astype(jnp.int32), cnt.astype(jnp.int32), off.astype(jnp.int32)


PIECE_SIZES = (32, 16, 8, 4, 2, 1)
PIECE_SLOTS = TILE_ROWS // PIECE_SIZES[0]
assert PIECE_SLOTS >= N_EXPERTS


def _compact(valid, *values):
    pos = jnp.cumsum(valid.astype(jnp.int32), axis=1) - 1
    slot = jnp.arange(PIECE_SLOTS, dtype=jnp.int32)
    hit = jnp.logical_and(valid[:, :, None], pos[:, :, None] == slot[None, None, :])
    packed = [jnp.sum(jnp.where(hit, v[:, :, None], 0), axis=1).astype(jnp.int32) for v in values]
    return packed, jnp.sum(valid.astype(jnp.int32), axis=1)


def _piece_lists(cnt, off, row):
    big = PIECE_SIZES[0]
    n_big = cnt // big
    p = jnp.arange(ROUTE_TM // big, dtype=jnp.int32)
    valid = (p[None, None, :] < n_big[:, :, None]).reshape(N_TILES, -1)
    src = (off[:, :, None] + big * p).reshape(N_TILES, -1)
    dst = (row[:, :, None] + big * p).reshape(N_TILES, -1)
    lists = [_compact(valid, src, dst)]
    rem = cnt - n_big * big
    for size in PIECE_SIZES[1:]:
        start = n_big * big + (rem & ~(2 * size - 1))
        lists.append(_compact((rem & size) != 0, off + start, row + start))
    counts = jnp.stack([n for _, n in lists], axis=1).reshape(-1)
    local_rows = jnp.stack([v[0] for v, _ in lists], axis=1).reshape(-1)
    global_rows = jnp.stack([v[1] for v, _ in lists], axis=1).reshape(-1)
    return counts, local_rows, global_rows


def _slab_pieces(tile, count_ref, local_ref, global_ref, fn):
    for k, size in enumerate(PIECE_SIZES):
        lst = tile * len(PIECE_SIZES) + k

        def body(j, carry, lst=lst, size=size):
            fn(local_ref[lst * PIECE_SLOTS + j], global_ref[lst * PIECE_SLOTS + j], size)
            return carry

        lax.fori_loop(0, count_ref[lst], body, 0)


def _rows(ref, row, n_rows):
    start = row * ROW_TILES
    if not isinstance(row, int):
        start = pl.multiple_of(start, ROW_TILES)
    return ref.at[pl.ds(start, n_rows * ROW_TILES)]


def _onehot_rows(q, values=None):
    lane = lax.broadcasted_iota(jnp.int32, (ROUTE_TM, TILE_ROWS), 1)
    s = jnp.zeros((ROUTE_TM, TILE_ROWS), F32)
    for k in range(TOP_K):
        v = 1.0 if values is None else values[:, k:k + 1]
        s = jnp.where(lane == q[:, k:k + 1], v, s)
    return s


def _wait_tile_rows(hbm_ref, buf_slot_ref, sem_slot):
    pltpu.make_async_copy(_rows(hbm_ref, 0, TILE_ROWS), buf_slot_ref, sem_slot).wait()


def _dispatch_kernel(count_ref, local_ref, global_ref, q_ref, h2_ref, xs_ref, buf_ref, sem):
    tile = pl.program_id(0)
    slot = tile % 2
    buf = buf_ref.at[slot]

    @pl.when(tile >= 2)
    def _():
        _wait_tile_rows(xs_ref, buf, sem.at[slot])

    sel = _onehot_rows(q_ref[...]).astype(BF16)
    xg = lax.dot_general(sel, h2_ref[...], (((0,), (0,)), ((), ())), preferred_element_type=F32)
    for c in range(ROW_TILES):
        buf[pl.ds(c, TILE_ROWS, stride=ROW_TILES), :] = xg[:, c * LANES:(c + 1) * LANES]

    def start(local_row, global_row, n_rows):
        pltpu.make_async_copy(_rows(buf, local_row, n_rows), _rows(xs_ref, global_row, n_rows),
                              sem.at[slot]).start()

    _slab_pieces(tile, count_ref, local_ref, global_ref, start)

    @pl.when(tile == N_TILES - 1)
    def _():
        _wait_tile_rows(xs_ref, buf, sem.at[slot])
        _wait_tile_rows(xs_ref, buf_ref.at[1 - slot], sem.at[1 - slot])


def _dispatch(pieces, q, h2):
    grid_spec = pltpu.PrefetchScalarGridSpec(
        num_scalar_prefetch=3,
        grid=(N_TILES,),
        in_specs=[pl.BlockSpec((ROUTE_TM, TOP_K), lambda i, *_: (i, 0)),
                  pl.BlockSpec((ROUTE_TM, D_MODEL), lambda i, *_: (i, 0))],
        out_specs=pl.BlockSpec(memory_space=pl.ANY),
        scratch_shapes=[pltpu.VMEM((2, TILE_ROWS * ROW_TILES, LANES), F32), pltpu.SemaphoreType.DMA((2,))],
    )
    return pl.pallas_call(
        _dispatch_kernel,
        grid_spec=grid_spec,
        out_shape=jax.ShapeDtypeStruct((N_ROWS * ROW_TILES, LANES), F32),
        compiler_params=_params(1),
        name="dispatch",
    )(*pieces, q, h2)


CAST_ROWS = 128
W_SLOTS = 2


def _expert_kernel(blk_exp_ref, nvalid_ref, first_ref, head_ref, slot_ref, next_ref, io_blk_ref,
                   xs_ref, w1_hbm, b1_ref, w2_hbm, b2_ref, y_ref,
                   w1f_ref, w2f_ref, w1b_ref, w2b_ref, sem, *, layer):
    b = pl.program_id(0)
    e = blk_exp_ref[b]
    nvalid = nvalid_ref[b]
    slot = slot_ref[b]

    def start_weights(expert, ahead):
        s = (slot + ahead) % W_SLOTS
        pltpu.make_async_copy(w1_hbm.at[layer, expert], w1f_ref.at[s], sem.at[0, s]).start()
        pltpu.make_async_copy(w2_hbm.at[layer, expert], w2f_ref.at[s], sem.at[1, s]).start()

    @pl.when(first_ref[b] == 1)
    def _():
        @pl.when(head_ref[b] == 1)
        def _():
            start_weights(e, 0)

        pltpu.make_async_copy(w1_hbm.at[layer, e], w1f_ref.at[slot], sem.at[0, slot]).wait()
        pltpu.make_async_copy(w2_hbm.at[layer, e], w2f_ref.at[slot], sem.at[1, slot]).wait()

        @pl.when(next_ref[b] >= 0)
        def _():
            start_weights(next_ref[b], 1)

        def cast1(i, carry):
            r = pl.multiple_of(i * CAST_ROWS, CAST_ROWS)
            w1b_ref[pl.ds(r, CAST_ROWS), :] = w1f_ref[slot, pl.ds(r, CAST_ROWS), :].astype(BF16)
            return carry

        def cast2(i, carry):
            r = pl.multiple_of(i * CAST_ROWS, CAST_ROWS)
            w2b_ref[pl.ds(r, CAST_ROWS), :] = w2f_ref[slot, pl.ds(r, CAST_ROWS), :].astype(BF16)
            return carry

        lax.fori_loop(0, D_MODEL // CAST_ROWS, cast1, 0)
        lax.fori_loop(0, D_FF // CAST_ROWS, cast2, 0)

    def run_rows(n_rows):
        valid = lax.broadcasted_iota(jnp.int32, (n_rows, LANES), 0) < nvalid
        chunks = [jnp.where(valid, xs_ref[pl.ds(c, n_rows, stride=ROW_TILES), :], 0.0).astype(BF16)
                  for c in range(ROW_TILES)]
        xb = jnp.concatenate(chunks, axis=-1)
        hid = jnp.dot(xb, w1b_ref[...], preferred_element_type=F32) + b1_ref[...]
        glu = jnp.minimum(hid[:, :D_FF], SWIGLU_LIMIT)
        lin = jnp.clip(hid[:, D_FF:], -SWIGLU_LIMIT, SWIGLU_LIMIT)
        act = glu * _sigmoid(SWIGLU_ALPHA * glu) * (lin + 1.0)
        y = jnp.dot(act.astype(BF16), w2b_ref[...], preferred_element_type=F32) + b2_ref[...]
        for c in range(ROW_TILES):
            y_ref[pl.ds(c, n_rows, stride=ROW_TILES), :] = y[:, c * LANES:(c + 1) * LANES]
        if n_rows < EXP_BLOCK:
            rest = (EXP_BLOCK - n_rows) * ROW_TILES
            y_ref[pl.ds(n_rows * ROW_TILES, rest), :] = jnp.zeros((rest, LANES), F32)

    @pl.when(nvalid > EXP_BLOCK // 2)
    def _():
        run_rows(EXP_BLOCK)

    @pl.when(jnp.logical_and(nvalid > 0, nvalid <= EXP_BLOCK // 2))
    def _():
        run_rows(EXP_BLOCK // 2)


def _experts(tables, xs, w1, b1, w2, b2, layer):
    def blk(b, *tbl):
        return (tbl[-1][b], 0)

    def bias(b, be, *_):
        return (layer, be[b], 0, 0)

    grid_spec = pltpu.PrefetchScalarGridSpec(
        num_scalar_prefetch=7,
        grid=(N_BLOCKS,),
        in_specs=[
            pl.BlockSpec((EXP_BLOCK * ROW_TILES, LANES), blk),
            pl.BlockSpec(memory_space=pl.ANY),
            pl.BlockSpec((None, None, 1, 2 * D_FF), bias),
            pl.BlockSpec(memory_space=pl.ANY),
            pl.BlockSpec((None, None, 1, D_MODEL), bias),
        ],
        out_specs=pl.BlockSpec((EXP_BLOCK * ROW_TILES, LANES), blk),
        scratch_shapes=[pltpu.VMEM((W_SLOTS, D_MODEL, 2 * D_FF), F32), pltpu.VMEM((W_SLOTS, D_FF, D_MODEL), F32),
                        pltpu.VMEM((D_MODEL, 2 * D_FF), BF16), pltpu.VMEM((D_FF, D_MODEL), BF16),
                        pltpu.SemaphoreType.DMA((2, W_SLOTS))],
    )
    return pl.pallas_call(
        functools.partial(_expert_kernel, layer=layer),
        grid_spec=grid_spec,
        out_shape=jax.ShapeDtypeStruct((N_ROWS * ROW_TILES, LANES), F32),
        compiler_params=_params(1),
        name="experts",
    )(*tables, xs, w1, b1.reshape(DEPTH, N_EXPERTS, 1, 2 * D_FF), w2,
      b2.reshape(DEPTH, N_EXPERTS, 1, D_MODEL))


def _split_bf16(x):
    hi = x.astype(BF16)
    return hi, (x - hi.astype(F32)).astype(BF16)


P_TILES = N_P // ROUTE_TM


def _combine_kernel(count_ref, local_ref, global_ref, q_ref, w_ref, y_ref, xm_ref, mod_ref, fg_ref,
                    *refs, final):
    buf_ref, sem = refs[-2:]
    tile = pl.program_id(0)
    slot = tile % 2
    buf = buf_ref.at[slot]

    def fetch(t, s):
        def start(local_row, global_row, n_rows):
            pltpu.make_async_copy(_rows(y_ref, global_row, n_rows), _rows(buf_ref.at[s], local_row, n_rows),
                                  sem.at[s]).start()

        _slab_pieces(t, count_ref, local_ref, global_ref, start)

    @pl.when(tile == 0)
    def _():
        fetch(tile, slot)

    @pl.when(tile + 1 < N_TILES)
    def _():
        fetch(tile + 1, 1 - slot)

    q = q_ref[...]
    pick = _onehot_rows(q).astype(BF16)
    row_w = jnp.sum(_onehot_rows(q, w_ref[...]), axis=0, keepdims=True)
    row_w = jnp.transpose(jnp.broadcast_to(row_w, (LANES, TILE_ROWS)))
    _wait_tile_rows(y_ref, buf, sem.at[slot])
    rows = jnp.concatenate([buf[pl.ds(c, TILE_ROWS, stride=ROW_TILES), :] * row_w for c in range(ROW_TILES)],
                           axis=-1)
    r_hi, r_lo = _split_bf16(rows)
    moe = jnp.dot(pick, r_hi, preferred_element_type=F32) + jnp.dot(pick, r_lo, preferred_element_type=F32)
    m = mod_ref[...]
    x = xm_ref[...] + m[5:6] * moe
    if not final:
        refs[0][...] = x
        return
    xn = x * lax.rsqrt(jnp.mean(x * x, axis=-1, keepdims=True) + EPS) * fg_ref[...]
    yp_ref, ys_ref = refs[:2]

    @pl.when(tile < P_TILES)
    def _():
        yp_ref[...] = xn

    @pl.when(tile >= P_TILES)
    def _():
        ys_ref[...] = xn


def _combine(pieces, q, topw, y, xm, mods_l, final_g, final):
    def tok(w):
        return pl.BlockSpec((ROUTE_TM, w), lambda i, *_: (i, 0))

    if final:
        out_specs = [pl.BlockSpec((ROUTE_TM, D_MODEL), lambda i, *_: (jnp.minimum(i, P_TILES - 1), 0)),
                     pl.BlockSpec((ROUTE_TM, D_MODEL), lambda i, *_: (jnp.maximum(i - P_TILES, 0), 0))]
        out_shape = [jax.ShapeDtypeStruct((N_P, D_MODEL), F32), jax.ShapeDtypeStruct((N_S, D_MODEL), F32)]
    else:
        out_specs = [tok(D_MODEL)]
        out_shape = [jax.ShapeDtypeStruct((N_TOK, D_MODEL), F32)]
    grid_spec = pltpu.PrefetchScalarGridSpec(
        num_scalar_prefetch=3,
        grid=(N_TILES,),
        in_specs=[tok(TOP_K), tok(TOP_K),
                  pl.BlockSpec(memory_space=pl.ANY),
                  tok(D_MODEL),
                  pl.BlockSpec((None, N_MOD, D_MODEL), lambda i, *_: (_mod_row(i * ROUTE_TM), 0, 0)),
                  pl.BlockSpec((1, D_MODEL), lambda i, *_: (0, 0))],
        out_specs=out_specs,
        scratch_shapes=[pltpu.VMEM((2, TILE_ROWS * ROW_TILES, LANES), F32), pltpu.SemaphoreType.DMA((2,))],
    )
    return pl.pallas_call(
        functools.partial(_combine_kernel, final=final),
        grid_spec=grid_spec,
        out_shape=out_shape,
        compiler_params=_params(1),
        name="combine_final" if final else "combine",
    )(*pieces, q, topw, y, xm, mods_l, final_g)


def _rope_tables():
    t = np.arange(DEC_SEQ)
    row = (t // GRID_W).astype(np.float32)
    col = (t % GRID_W).astype(np.float32)
    inv = jnp.asarray(ROPE_THETA, F32) ** (-jnp.arange(ROPE_PAIRS, dtype=F32) / ROPE_PAIRS)
    ang = jnp.concatenate([jnp.asarray(row)[:, None] * inv, jnp.asarray(col)[:, None] * inv], axis=-1)
    cos = jnp.repeat(jnp.cos(ang), 2, axis=-1)
    sin = jnp.repeat(jnp.sin(ang), 2, axis=-1)
    sign = jnp.asarray(np.tile(np.array([-1.0, 1.0], np.float32), HEAD_DIM // 2))
    return jnp.tile(cos, (1, A_HEADS)), jnp.tile(sin * sign, (1, A_HEADS))


def _routing_tables(tile_cnt):
    i32 = jnp.int32
    carry = jnp.cumsum(tile_cnt, axis=0) - tile_cnt
    counts = jnp.sum(tile_cnt, axis=0)
    padded = (counts + EXP_BLOCK - 1) // EXP_BLOCK * EXP_BLOCK
    pad_end = jnp.cumsum(padded)
    pad_start = pad_end - padded
    rowstart = (pad_start[None, :] + carry).astype(i32)
    blk_row = jnp.arange(N_BLOCKS, dtype=i32) * EXP_BLOCK
    blk_exp = jnp.sum((blk_row[:, None] >= pad_end[None, :]).astype(i32), axis=1)
    blk_exp = jnp.minimum(blk_exp, N_EXPERTS - 1)
    eid = jnp.arange(N_EXPERTS, dtype=i32)

    def pick(table, idx):
        return jnp.sum(jnp.where(idx[:, None] == eid[None, :], table[None, :], 0), axis=1).astype(i32)

    blk_start = pick(pad_start, blk_exp)
    nvalid = jnp.clip(pick(counts, blk_exp) - (blk_row - blk_start), 0, EXP_BLOCK).astype(i32)
    first = jnp.logical_and(blk_row == blk_start, nvalid > 0)
    active = counts > 0
    act_rank = jnp.cumsum(active.astype(i32)) - 1
    later = jnp.logical_and(active[None, :], eid[None, :] > eid[:, None])
    nxt = jnp.min(jnp.where(later, eid[None, :], N_EXPERTS), axis=1)
    nxt = jnp.where(nxt == N_EXPERTS, -1, nxt).astype(i32)
    blk_rank = pick(act_rank, blk_exp)
    head = jnp.logical_and(first, blk_rank == 0)
    n_used = pad_end[-1] // EXP_BLOCK
    io_blk = jnp.minimum(jnp.arange(N_BLOCKS, dtype=i32), n_used - 1).astype(i32)
    tables = (blk_exp, nvalid, first.astype(i32), head.astype(i32), (blk_rank % W_SLOTS).astype(i32),
              pick(nxt + 1, blk_exp) - 1, io_blk)
    return rowstart, tables


def kernel(x_prompt, x_sample, cache_attn_k, cache_attn_v, cache_na_k, cache_na_v, c, c_ctx, w_mod, b_mod, norm1_g, norm2_g, w_in, b_gate, q_norm_g, k_norm_g, na_rpb, conv_w, conv_b, conv_ln_g, conv_ln_b, sgu_ln_g, sgu_ln_b, sgu_w, sgu_b, w_branch, w_out, router_w, router_b, exp_w1, exp_b1, exp_w2, exp_b2, final_g):
    stream = (x_prompt.reshape(N_P, D_MODEL), x_sample.reshape(N_S, D_MODEL), 0)
    cvec =jnp.zeros((SUBLANES, D_MODEL), F32).at[0].set(c_ctx).at[1:1 + DEC_BATCH].set(c)
    mods = _modulation(cvec, w_mod, b_mod).reshape(DEPTH, SUBLANES, N_MOD, D_MODEL)
    cos_t, sin_t = _rope_tables()
    cak = cache_attn_k.reshape(DEC_BATCH, DEPTH, PAST_LEN, A_KV)
    cav = cache_attn_v.reshape(DEC_BATCH, DEPTH, PAST_LEN, A_KV)
    cbk = cache_na_k.reshape(DEC_BATCH, DEPTH, PAST_LEN, B_W)
    cbv = cache_na_v.reshape(DEC_BATCH, DEPTH, PAST_LEN, B_W)
    w_in_b = w_in.astype(BF16)
    w_br = w_branch.astype(BF16)
    w_o = w_out.astype(BF16)
    rw_hi = router_w.astype(BF16)
    rw_lo = (router_w - rw_hi.astype(F32)).astype(BF16)
    final_g2 = final_g.reshape(1, D_MODEL)

    caches = ()
    outs = None
    for l in range(DEPTH):
        mods_l = mods[l]
        g1 = norm1_g[l].reshape(1, D_MODEL)
        gq = jnp.tile(q_norm_g[l], A_HEADS).reshape(1, A_Q)
        gk = jnp.tile(k_norm_g[l], A_KV_HEADS).reshape(1, A_KV)
        aq, ak, av, bq, bk, bv, cz, dz = _in_proj(stream, mods_l, g1, w_in_b, l)
        ya, yb, *caches = _prompt_attn(aq, ak, av, bq, bk, bv, gq, gk, caches, l)
        ya = _sample_attn(aq, ak, av, cak, cav, cos_t, sin_t, gq, gk, ya, l)
        yb = _na_attn(bq, bk, bv, cbk, cbv, _na_bias(na_rpb[l]), yb, l)
        cw = conv_w[l]
        cb = conv_b[l].reshape(1, C_WIDTH)
        cg = conv_ln_g[l].reshape(1, C_WIDTH)
        cbb = conv_ln_b[l].reshape(1, C_WIDTH)
        yc = _conv_call(cz, cw, cb, cg, cbb, SEQ, 0, BATCH)
        yc = _conv_call(cz, cw, cb, cg, cbb, DEC_SEQ, N_P // DEC_SEQ, DEC_BATCH, partial_out=yc)
        bs_full = jnp.repeat(sgu_b[l].T, SGU_GW, axis=1)
        yd = _sgu(dz, sgu_ln_g[l].reshape(1, SGU_WIDTH), sgu_ln_b[l].reshape(1, SGU_WIDTH),
                  sgu_w[l].astype(BF16), bs_full)
        xm, h2, top_w, q, tile_cnt, tile_off = _merge(
            stream, ya, yb, yc, yd, mods_l, g1, w_in_b, b_gate[l].reshape(1, N_BRANCH * D_MODEL), w_br[l], w_o[l],
            norm2_g[l].reshape(1, D_MODEL), rw_hi[l], rw_lo[l], router_b[l].reshape(1, N_EXPERTS), l)
        tile_cnt = tile_cnt.reshape(N_TILES, N_EXPERTS)
        rowstart, tables = _routing_tables(tile_cnt)
        pieces = _piece_lists(tile_cnt, tile_off.reshape(N_TILES, N_EXPERTS), rowstart)
        xs = _dispatch(pieces, q, h2)
        y = _experts(tables, xs, exp_w1, exp_b1, exp_w2, exp_b2, l)
        outs = _combine(pieces, q, top_w, y, xm, mods_l, final_g2, l == DEPTH - 1)
        stream = (outs[0], outs[0], N_P)

    new_k, new_v, new_bk, new_bv = caches
    return (outs[0].reshape(BATCH, SEQ, D_MODEL), outs[1].reshape(DEC_BATCH, DEC_SEQ, D_MODEL),
            new_k.reshape(BATCH, DEPTH, SEQ, A_KV_HEADS, HEAD_DIM),
            new_v.reshape(BATCH, DEPTH, SEQ, A_KV_HEADS, HEAD_DIM),
            new_bk.reshape(BATCH, DEPTH, SEQ, B_HEADS, HEAD_DIM),
            new_bv.reshape(BATCH, DEPTH, SEQ, B_HEADS, HEAD_DIM))
```

```python
import functools

import numpy as np
import jax
import jax.numpy as jnp
from jax import lax
from jax.experimental import pallas as pl
from jax.experimental.pallas import tpu as pltpu

D_MODEL = 1024
BATCH = 32
SEQ = 256
DEPTH = 2
DEC_BATCH = 2
DEC_SEQ = 1024
PAST_LEN = 512
GRID_W = 64
HEAD_DIM = 64
A_HEADS = 4
A_KV_HEADS = 2
B_HEADS = 4
NA_ROWS = 8
NA_COLS = 16
C_WIDTH = 256
CONV_WIDTH = 31
SGU_WIDTH = 256
SGU_GROUPS = 4
SGU_CHUNK = 128
N_BRANCH = 4
BRANCH_W = 256
N_EXPERTS = 32
TOP_K = 4
D_FF = 1024
SWIGLU_ALPHA = 1.702
SWIGLU_LIMIT = 7.0
MOE_BLOCK = 128
ROPE_THETA = 10000.0
ROPE_PAIRS = HEAD_DIM // 4
N_MOD = 6
EPS = 1e-6
NEG_INF = -1e30

A_Q = A_HEADS * HEAD_DIM
A_KV = A_KV_HEADS * HEAD_DIM
B_W = B_HEADS * HEAD_DIM
MIX_SIZES = (A_Q, A_KV, A_KV, B_W, B_W, B_W, 2 * C_WIDTH, 2 * SGU_WIDTH)
MIX_COLS = sum(MIX_SIZES)

N_P = BATCH * SEQ
N_S = DEC_BATCH * DEC_SEQ
N_TOK = N_P + N_S
N_ASSIGN = N_TOK * TOP_K
EXP_BLOCK = 512
N_BLOCKS = N_ASSIGN // EXP_BLOCK + N_EXPERTS
N_ROWS = N_BLOCKS * EXP_BLOCK
GRID_ROWS = DEC_SEQ // GRID_W
NA_WR = min(NA_ROWS, GRID_ROWS)
N_LOC = NA_WR * GRID_W

SUBLANES = 8
LANES = 128
ROW_TILES = D_MODEL // LANES
VMEM_LIMIT = 56 * 1024 * 1024

F32 = jnp.float32
BF16 = jnp.bfloat16


def _params(n_axes, vmem=None):
    return pltpu.CompilerParams(
        dimension_semantics=("arbitrary",) * n_axes,
        vmem_limit_bytes=vmem if vmem is not None else VMEM_LIMIT)


def _mod_row(start):
    return jnp.where(start < N_P, 0, 1 + (start - N_P) // DEC_SEQ)


def _bdot(a, b):
    return jnp.dot(a.astype(BF16), b.astype(BF16), preferred_element_type=F32)


def _bdot_nt(a, b):
    return lax.dot_general(a.astype(BF16), b.astype(BF16), (((1,), (1,)), ((), ())),
                           preferred_element_type=F32)


def _sigmoid(x):
    return 0.5 * jnp.tanh(0.5 * x) + 0.5


MOD_TN = 1536


def _mod_kernel(c_ref, w_ref, b_ref, o_ref):
    c = c_ref[...]
    s = c * _sigmoid(c)
    o_ref[...] = _bdot(s, w_ref[...]) + b_ref[...]


def _modulation(cvec, w_mod, b_mod):
    n_col = N_MOD * D_MODEL
    return pl.pallas_call(
        _mod_kernel,
        grid=(DEPTH, n_col // MOD_TN),
        in_specs=[
            pl.BlockSpec((SUBLANES, D_MODEL), lambda l, j: (0, 0)),
            pl.BlockSpec((None, D_MODEL, MOD_TN), lambda l, j: (l, 0, j)),
            pl.BlockSpec((None, 1, MOD_TN), lambda l, j: (l, 0, j)),
        ],
        out_specs=pl.BlockSpec((None, SUBLANES, MOD_TN), lambda l, j: (l, 0, j)),
        out_shape=jax.ShapeDtypeStruct((DEPTH, SUBLANES, n_col), F32),
        compiler_params=_params(2),
        name="modulation",
    )(cvec, w_mod, b_mod.reshape(DEPTH, 1, n_col))


IN_TM = 512


def _norm_mod(x, g, shift, scale):
    y = x * lax.rsqrt(jnp.mean(x * x, axis=-1, keepdims=True) + EPS) * g
    return y * (1.0 + scale) + shift


def _stream_specs(tm, stream):
    p_tiles = N_P // tm
    s_first = stream[2] // tm
    return [pl.BlockSpec((tm, D_MODEL), lambda i: (jnp.minimum(i, p_tiles - 1), 0)),
            pl.BlockSpec((tm, D_MODEL), lambda i: (jnp.maximum(i - p_tiles, 0) + s_first, 0))]


def _stream_tile(xp_ref, xs_ref, tm):
    return jnp.where(pl.program_id(0) < N_P // tm, xp_ref[...], xs_ref[...])


def _in_kernel(xp_ref, xs_ref, mod_ref, g_ref, w_ref, *out_refs):
    m = mod_ref[...]
    h = _norm_mod(_stream_tile(xp_ref, xs_ref, IN_TM), g_ref[...], m[0:1], m[1:2])
    z = jnp.dot(h.astype(BF16), w_ref[...], preferred_element_type=F32)
    off = 0
    for o_ref, sz in zip(out_refs, MIX_SIZES):
        o_ref[...] = z[:, off:off + sz]
        off += sz


def _in_proj(stream, mods_l, g1, w_in_b, layer):
    return pl.pallas_call(
        _in_kernel,
        grid=(N_TOK // IN_TM,),
        in_specs=_stream_specs(IN_TM, stream) + [
            pl.BlockSpec((None, N_MOD, D_MODEL), lambda i: (_mod_row(i * IN_TM), 0, 0)),
            pl.BlockSpec((1, D_MODEL), lambda i: (0, 0)),
            pl.BlockSpec((None, D_MODEL, MIX_COLS), lambda i: (layer, 0, 0)),
        ],
        out_specs=[pl.BlockSpec((IN_TM, sz), lambda i: (i, 0)) for sz in MIX_SIZES],
        out_shape=[jax.ShapeDtypeStruct((N_TOK, sz), F32) for sz in MIX_SIZES],
        compiler_params=_params(1),
        name="in_proj",
    )(stream[0], stream[1], mods_l, g1, w_in_b)


def _head_rms(x, g):
    width = x.shape[-1]
    seg_r = lax.broadcasted_iota(jnp.int32, (width, width), 0) // HEAD_DIM
    seg_c = lax.broadcasted_iota(jnp.int32, (width, width), 1) // HEAD_DIM
    avg = jnp.where(seg_r == seg_c, 1.0 / HEAD_DIM, 0.0).astype(BF16)
    hi, lo = _split_bf16(x * x)
    ms = jnp.dot(hi, avg, preferred_element_type=F32) + jnp.dot(lo, avg, preferred_element_type=F32)
    return x * lax.rsqrt(ms + EPS) * g


def _softmax_pv(score_parts, value_parts):
    m = score_parts[0].max(axis=-1, keepdims=True)
    for s in score_parts[1:]:
        m = jnp.maximum(m, s.max(axis=-1, keepdims=True))
    den = None
    acc = None
    for s, v in zip(score_parts, value_parts):
        e = jnp.exp(s - m)
        d = e.sum(axis=-1, keepdims=True)
        a = _bdot(e, v)
        den = d if den is None else den + d
        acc = a if acc is None else acc + a
    return acc / den


def _head(x, h):
    return x[:, h * HEAD_DIM:(h + 1) * HEAD_DIM]


SCALE = HEAD_DIM ** -0.5


def _prompt_attn_kernel(aq_ref, ak_ref, av_ref, bq_ref, bk_ref, bv_ref, gq_ref, gk_ref, *refs):
    ya_ref, yb_ref, nk_ref, nv_ref, nbk_ref, nbv_ref = refs[-6:]
    aq = _head_rms(aq_ref[...], gq_ref[...]) * SCALE
    ak = _head_rms(ak_ref[...], gk_ref[...])
    av = av_ref[...]
    for ref, val in ((nk_ref, ak), (nv_ref, av), (nbk_ref, bk_ref[...]), (nbv_ref, bv_ref[...])):
        n_heads = val.shape[-1] // HEAD_DIM
        for h in range(n_heads):
            ref[pl.ds(h, SEQ, stride=n_heads), :] = _head(val, h)
    grp = A_HEADS // A_KV_HEADS
    outs = []
    for h in range(A_HEADS):
        s = _bdot_nt(_head(aq, h), _head(ak, h // grp))
        outs.append(_softmax_pv([s], [_head(av, h // grp)]))
    ya_ref[...] = jnp.concatenate(outs, axis=-1)
    bq = bq_ref[...] * SCALE
    bk = bk_ref[...]
    bv = bv_ref[...]
    outs = []
    for h in range(B_HEADS):
        s = _bdot_nt(_head(bq, h), _head(bk, h))
        outs.append(_softmax_pv([s], [_head(bv, h)]))
    yb_ref[...] = jnp.concatenate(outs, axis=-1)


def _prompt_attn(aq, ak, av, bq, bk, bv, gq, gk, caches, layer):
    def spec(w):
        return pl.BlockSpec((SEQ, w), lambda b: (b, 0))

    def cache_spec(n_heads):
        return pl.BlockSpec((SEQ * n_heads, HEAD_DIM), lambda b: (b * DEPTH + layer, 0))

    cache_heads = (A_KV_HEADS, A_KV_HEADS, B_HEADS, B_HEADS)
    n_in = 8
    return pl.pallas_call(
        _prompt_attn_kernel,
        grid=(BATCH,),
        in_specs=[spec(A_Q), spec(A_KV), spec(A_KV), spec(B_W), spec(B_W), spec(B_W),
                  pl.BlockSpec((1, A_Q), lambda b: (0, 0)),
                  pl.BlockSpec((1, A_KV), lambda b: (0, 0))]
        + [pl.BlockSpec(memory_space=pl.ANY) for _ in caches],
        out_specs=[spec(A_Q), spec(B_W)] + [cache_spec(n) for n in cache_heads],
        out_shape=[jax.ShapeDtypeStruct((N_TOK, A_Q), F32),
                   jax.ShapeDtypeStruct((N_TOK, B_W), F32)]
        + [jax.ShapeDtypeStruct((BATCH * DEPTH * SEQ * n, HEAD_DIM), F32) for n in cache_heads],
        input_output_aliases={n_in + j: 2 + j for j in range(len(caches))},
        compiler_params=_params(1),
        name="prompt_attn",
    )(aq, ak, av, bq, bk, bv, gq, gk, *caches)


QB = 128


def _rope(x, cos, sin_signed):
    n = x.shape[-1]
    nxt = pltpu.roll(x, n - 1, 1)
    prv = pltpu.roll(x, 1, 1)
    even = (lax.broadcasted_iota(jnp.int32, x.shape, 1) % 2) == 0
    return x * cos + jnp.where(even, nxt, prv) * sin_signed


def _sample_attn_kernel(q_ref, k_ref, v_ref, ck_ref, cv_ref, cosq_ref, sinq_ref, cosk_ref, sink_ref,
                        gq_ref, gk_ref, ya_prompt_ref, o_ref, kr_ref):
    del ya_prompt_ref

    @pl.when(pl.program_id(1) == 0)
    def _():
        kr_ref[...] = _rope(_head_rms(k_ref[...], gk_ref[...]), cosk_ref[...], sink_ref[...]).astype(BF16)

    q = _rope(_head_rms(q_ref[...], gq_ref[...]), cosq_ref[...], sinq_ref[...]) * SCALE
    k = kr_ref[...]
    v = v_ref[...]
    ck = ck_ref[...]
    cv = cv_ref[...]
    grp = A_HEADS // A_KV_HEADS
    outs = []
    for h in range(A_HEADS):
        j = h // grp
        qh = _head(q, h)
        s1 = _bdot_nt(qh, _head(k, j))
        s2 = _bdot_nt(qh, _head(ck, j))
        outs.append(_softmax_pv([s1, s2], [_head(v, j), _head(cv, j)]))
    o_ref[...] = jnp.concatenate(outs, axis=-1)


def _sample_attn(aq, ak, av, cache_k, cache_v, cos_t, sin_t, gq, gk, ya, layer):
    nqb = DEC_SEQ // QB
    q0 = N_P // QB
    k0 = N_P // DEC_SEQ
    return pl.pallas_call(
        _sample_attn_kernel,
        grid=(DEC_BATCH, nqb),
        in_specs=[
            pl.BlockSpec((QB, A_Q), lambda b, i: (q0 + b * nqb + i, 0)),
            pl.BlockSpec((DEC_SEQ, A_KV), lambda b, i: (k0 + b, 0)),
            pl.BlockSpec((DEC_SEQ, A_KV), lambda b, i: (k0 + b, 0)),
            pl.BlockSpec((None, None, PAST_LEN, A_KV), lambda b, i: (b, layer, 0, 0)),
            pl.BlockSpec((None, None, PAST_LEN, A_KV), lambda b, i: (b, layer, 0, 0)),
            pl.BlockSpec((QB, A_Q), lambda b, i: (i, 0)),
            pl.BlockSpec((QB, A_Q), lambda b, i: (i, 0)),
            pl.BlockSpec((DEC_SEQ, A_KV), lambda b, i: (0, 0)),
            pl.BlockSpec((DEC_SEQ, A_KV), lambda b, i: (0, 0)),
            pl.BlockSpec((1, A_Q), lambda b, i: (0, 0)),
            pl.BlockSpec((1, A_KV), lambda b, i: (0, 0)),
            pl.BlockSpec(memory_space=pl.ANY),
        ],
        out_specs=pl.BlockSpec((QB, A_Q), lambda b, i: (q0 + b * nqb + i, 0)),
        out_shape=jax.ShapeDtypeStruct((N_TOK, A_Q), F32),
        input_output_aliases={11: 0},
        scratch_shapes=[pltpu.VMEM((DEC_SEQ, A_KV), BF16)],
        compiler_params=_params(2),
        name="sample_attn",
    )(aq, ak, av, cache_k, cache_v, cos_t, sin_t, cos_t, sin_t, gq, gk, ya)


N_ROW_OFF = 2 * NA_ROWS - 1
N_COL_OFF = 2 * NA_COLS - 1
NA_PAIRS = N_ROW_OFF - 1
assert NA_WR == NA_ROWS and NA_WR % 2 == 0 and 2 * GRID_W == LANES


def _na_bias_kernel(rpb_ref, o_ref):
    h = pl.program_id(0)
    qc = lax.broadcasted_iota(jnp.int32, (GRID_W, LANES), 0)
    lane = lax.broadcasted_iota(jnp.int32, (GRID_W, LANES), 1)
    right = lane >= GRID_W
    kc = jnp.where(right, lane - GRID_W, lane)
    c_start = jnp.clip(qc - NA_COLS // 2, 0, GRID_W - NA_COLS)
    col_in = jnp.logical_and(kc >= c_start, kc < c_start + NA_COLS)
    col_off = jnp.clip(kc - qc + NA_COLS - 1, 0, N_COL_OFF - 1)
    for p in range(NA_PAIRS):
        acc = jnp.zeros((GRID_W, LANES), F32)
        for o in range(N_COL_OFF):
            left_v = rpb_ref[(h * N_ROW_OFF + p) * N_COL_OFF + o]
            right_v = rpb_ref[(h * N_ROW_OFF + p + 1) * N_COL_OFF + o]
            acc = jnp.where(col_off == o, jnp.where(right, right_v, left_v), acc)
        o_ref[p] = jnp.where(col_in, acc, NEG_INF)


def _na_bias(rpb):
    return pl.pallas_call(
        _na_bias_kernel,
        grid_spec=pltpu.PrefetchScalarGridSpec(
            num_scalar_prefetch=1,
            grid=(B_HEADS,),
            in_specs=[],
            out_specs=pl.BlockSpec((None, NA_PAIRS, GRID_W, LANES), lambda h, *_: (h, 0, 0, 0)),
        ),
        out_shape=jax.ShapeDtypeStruct((B_HEADS, NA_PAIRS, GRID_W, LANES), F32),
        compiler_params=_params(1),
        name="na_bias",
    )(rpb.reshape(-1))


NA_STEP_ROWS = 2


def _na_kernel(q_ref, k_ref, v_ref, ck_ref, cv_ref, bias_ref, yb_prompt_ref, o_ref):
    del yb_prompt_ref
    q = q_ref[...] * SCALE
    ck = ck_ref[...]
    cv = cv_ref[...]
    ctx_scores = [_bdot_nt(_head(q, h), _head(ck, h)) for h in range(B_HEADS)]
    for i in range(NA_STEP_ROWS):
        r = pl.program_id(1) * NA_STEP_ROWS + i
        r_start = jnp.clip(r - NA_WR // 2, 0, GRID_ROWS - NA_WR)
        base = pl.multiple_of(r_start * GRID_W, GRID_W)
        row_off0 = r_start - r + NA_ROWS - 1
        rows = slice(i * GRID_W, (i + 1) * GRID_W)
        kb = k_ref[pl.ds(base, N_LOC), :]
        vb = v_ref[pl.ds(base, N_LOC), :]
        outs = []
        for h in range(B_HEADS):
            bias = jnp.concatenate([bias_ref[h, row_off0 + 2 * j] for j in range(NA_WR // 2)], axis=-1)
            s1 = _bdot_nt(_head(q[rows], h), _head(kb, h)) + bias
            outs.append(_softmax_pv([s1, ctx_scores[h][rows]], [_head(vb, h), _head(cv, h)]))
        o_ref[rows, :] = jnp.concatenate(outs, axis=-1)


def _na_attn(bq, bk, bv, cache_k, cache_v, bias, yb, layer):
    step_tok = NA_STEP_ROWS * GRID_W
    steps = GRID_ROWS // NA_STEP_ROWS
    q0 = N_P // step_tok
    k0 = N_P // DEC_SEQ
    return pl.pallas_call(
        _na_kernel,
        grid=(DEC_BATCH, steps),
        in_specs=[
            pl.BlockSpec((step_tok, B_W), lambda b, r: (q0 + b * steps + r, 0)),
            pl.BlockSpec((DEC_SEQ, B_W), lambda b, r: (k0 + b, 0)),
            pl.BlockSpec((DEC_SEQ, B_W), lambda b, r: (k0 + b, 0)),
            pl.BlockSpec((None, None, PAST_LEN, B_W), lambda b, r: (b, layer, 0, 0)),
            pl.BlockSpec((None, None, PAST_LEN, B_W), lambda b, r: (b, layer, 0, 0)),
            pl.BlockSpec((B_HEADS, NA_PAIRS, GRID_W, LANES), lambda b, r: (0, 0, 0, 0)),
            pl.BlockSpec(memory_space=pl.ANY),
        ],
        out_specs=pl.BlockSpec((step_tok, B_W), lambda b, r: (q0 + b * steps + r, 0)),
        out_shape=jax.ShapeDtypeStruct((N_TOK, B_W), F32),
        input_output_aliases={6: 0},
        compiler_params=_params(2),
        name="na_attn",
    )(bq, bk, bv, cache_k, cache_v, bias, yb)


CONV_PAD = 16
CONV_CHUNK = 128


def _layer_norm(x, g, b):
    mu = jnp.mean(x, axis=-1, keepdims=True)
    xc = x - mu
    var = jnp.mean(xc * xc, axis=-1, keepdims=True)
    return xc * lax.rsqrt(var + EPS) * g + b


def _conv_kernel(z_ref, w_ref, cb_ref, g_ref, b_ref, *refs, s_len):
    o_ref, pad_ref = refs[-2:]
    z = z_ref[...]
    u = z[:, :C_WIDTH] * _sigmoid(z[:, C_WIDTH:])
    pad_ref[pl.ds(0, CONV_PAD), :] = jnp.zeros((CONV_PAD, C_WIDTH), F32)
    pad_ref[pl.ds(CONV_PAD + s_len, CONV_PAD), :] = jnp.zeros((CONV_PAD, C_WIDTH), F32)
    pad_ref[pl.ds(CONV_PAD, s_len), :] = u
    w = w_ref[...]
    shift = CONV_PAD - CONV_WIDTH // 2

    def chunk(c, carry):
        base = pl.multiple_of(c * CONV_CHUNK, CONV_CHUNK)
        acc = jnp.zeros((CONV_CHUNK, C_WIDTH), F32)
        for r in range(SUBLANES):
            part = None
            for k in range(CONV_WIDTH):
                if (k + shift) % SUBLANES != r:
                    continue
                rows = pad_ref[pl.ds(base + (k + shift - r), CONV_CHUNK + SUBLANES), :]
                term = rows * w[k:k + 1]
                part = term if part is None else part + term
            if part is not None:
                acc = acc + part[r:r + CONV_CHUNK]
        y = _layer_norm(acc + cb_ref[...], g_ref[...], b_ref[...])
        o_ref[pl.ds(base, CONV_CHUNK), :] = y * _sigmoid(y)
        return carry

    lax.fori_loop(0, s_len // CONV_CHUNK, chunk, 0)


def _conv_call(cz, w, cb, g, b, s_len, first_blk, n_seq, partial_out=None):
    vec = pl.BlockSpec((1, C_WIDTH), lambda i: (0, 0))
    extra = [] if partial_out is None else [partial_out]
    return pl.pallas_call(
        functools.partial(_conv_kernel, s_len=s_len),
        grid=(n_seq,),
        in_specs=[pl.BlockSpec((s_len, 2 * C_WIDTH), lambda i: (first_blk + i, 0)),
                  pl.BlockSpec((CONV_WIDTH, C_WIDTH), lambda i: (0, 0)), vec, vec, vec]
        + [pl.BlockSpec(memory_space=pl.ANY) for _ in extra],
        out_specs=pl.BlockSpec((s_len, C_WIDTH), lambda i: (first_blk + i, 0)),
        out_shape=jax.ShapeDtypeStruct((N_TOK, C_WIDTH), F32),
        input_output_aliases={5: 0} if extra else {},
        scratch_shapes=[pltpu.VMEM((s_len + 2 * CONV_PAD, C_WIDTH), F32)],
        compiler_params=_params(1),
        name="conformer_conv_%d" % s_len,
    )(cz, w, cb, g, b, *extra)


SGU_TM = 512
SGU_GW = SGU_WIDTH // SGU_GROUPS


def _sgu_kernel(z_ref, g_ref, b_ref, ws_ref, bs_ref, o_ref):
    z = z_ref[...]
    z = 0.5 * z * (1.0 + lax.erf(z * (2.0 ** -0.5)))
    u = z[:, :SGU_WIDTH]
    v = _layer_norm(z[:, SGU_WIDTH:], g_ref[...], b_ref[...])
    for c in range(SGU_TM // SGU_CHUNK):
        vc = v[c * SGU_CHUNK:(c + 1) * SGU_CHUNK]
        parts = [_bdot(ws_ref[g], vc[:, g * SGU_GW:(g + 1) * SGU_GW]) for g in range(SGU_GROUPS)]
        mixed = jnp.concatenate(parts, axis=-1) + bs_ref[...]
        o_ref[pl.ds(c * SGU_CHUNK, SGU_CHUNK), :] = u[c * SGU_CHUNK:(c + 1) * SGU_CHUNK] * mixed


def _sgu(dz, g, b, ws, bs_full):
    vec = pl.BlockSpec((1, SGU_WIDTH), lambda i: (0, 0))
    return pl.pallas_call(
        _sgu_kernel,
        grid=(N_TOK // SGU_TM,),
        in_specs=[pl.BlockSpec((SGU_TM, 2 * SGU_WIDTH), lambda i: (i, 0)), vec, vec,
                  pl.BlockSpec((SGU_GROUPS, SGU_CHUNK, SGU_CHUNK), lambda i: (0, 0, 0)),
                  pl.BlockSpec((SGU_CHUNK, SGU_WIDTH), lambda i: (0, 0))],
        out_specs=pl.BlockSpec((SGU_TM, SGU_WIDTH), lambda i: (i, 0)),
        out_shape=jax.ShapeDtypeStruct((N_TOK, SGU_WIDTH), F32),
        compiler_params=_params(1),
        name="chunk_sgu",
    )(dz, g, b, ws, bs_full)


MERGE_TM = 512


def _merge_kernel(xp_ref, xs_ref, ya_ref, yb_ref, yc_ref, yd_ref, mod_ref, g1_ref, w_in_hbm, bg_ref, wb_ref, wo_ref,
                  g2_ref, rwh_ref, rwl_ref, rb_ref, xm_ref, h2_ref, w_ref, q_ref, cnt_ref, off_ref,
                  wg_ref, sem, *, layer):
    @pl.when(pl.program_id(0) == 0)
    def _():
        cp = pltpu.make_async_copy(w_in_hbm.at[layer, :, pl.ds(MIX_COLS, N_BRANCH * D_MODEL)], wg_ref, sem)
        cp.start()
        cp.wait()

    m = mod_ref[...]
    x = _stream_tile(xp_ref, xs_ref, MERGE_TM)
    h = _norm_mod(x, g1_ref[...], m[0:1], m[1:2]).astype(BF16)
    merged = None
    for i, y_ref in enumerate((ya_ref, yb_ref, yc_ref, yd_ref)):
        logit = jnp.dot(h, wg_ref[:, i * D_MODEL:(i + 1) * D_MODEL], preferred_element_type=F32)
        gate = _sigmoid(logit + bg_ref[:, i * D_MODEL:(i + 1) * D_MODEL])
        term = gate * jnp.dot(y_ref[...].astype(BF16), wb_ref[i], preferred_element_type=F32)
        merged = term if merged is None else merged + term
    out = jnp.dot(merged.astype(BF16), wo_ref[...], preferred_element_type=F32)
    xm = x + m[2:3] * out
    xm_ref[...] = xm
    h2 = _norm_mod(xm, g2_ref[...], m[3:4], m[4:5])
    h2_hi = h2.astype(BF16)
    h2_lo = (h2 - h2_hi.astype(F32)).astype(BF16)
    lg = jnp.dot(h2_hi, rwh_ref[...], preferred_element_type=F32)
    lg = lg + jnp.dot(h2_hi, rwl_ref[...], preferred_element_type=F32)
    lg = lg + jnp.dot(h2_lo, rwh_ref[...], preferred_element_type=F32)
    lg = lg + rb_ref[...]
    h2_ref[...] = h2_hi
    for j in range(MERGE_TM // ROUTE_TM):
        rows = slice(j * ROUTE_TM, (j + 1) * ROUTE_TM)
        w_out, q_out, cnt, off = _route_tile(lg[rows])
        w_ref[rows, :] = w_out
        q_ref[rows, :] = q_out
        cnt_ref[j] = cnt
        off_ref[j] = off


def _merge(stream, ya, yb, yc, yd, mods_l, g1, w_in_b, bg, wb, wo, g2, rwh, rwl, rb, layer):
    def tok(w):
        return pl.BlockSpec((MERGE_TM, w), lambda i: (i, 0))

    def full(*shape):
        return pl.BlockSpec(shape, lambda i: (0,) * len(shape))

    tile_rows = pl.BlockSpec((MERGE_TM // ROUTE_TM, 1, N_EXPERTS), lambda i: (i, 0, 0))
    return pl.pallas_call(
        functools.partial(_merge_kernel, layer=layer),
        grid=(N_TOK // MERGE_TM,),
        in_specs=_stream_specs(MERGE_TM, stream) + [
                  tok(BRANCH_W), tok(BRANCH_W), tok(BRANCH_W), tok(BRANCH_W),
                  pl.BlockSpec((None, N_MOD, D_MODEL), lambda i: (_mod_row(i * MERGE_TM), 0, 0)),
                  full(1, D_MODEL), pl.BlockSpec(memory_space=pl.ANY), full(1, N_BRANCH * D_MODEL),
                  full(N_BRANCH, BRANCH_W, D_MODEL), full(D_MODEL, D_MODEL), full(1, D_MODEL),
                  full(D_MODEL, N_EXPERTS), full(D_MODEL, N_EXPERTS), full(1, N_EXPERTS)],
        out_specs=[tok(D_MODEL), tok(D_MODEL), tok(TOP_K), tok(TOP_K), tile_rows, tile_rows],
        out_shape=[jax.ShapeDtypeStruct((N_TOK, D_MODEL), F32),
                   jax.ShapeDtypeStruct((N_TOK, D_MODEL), BF16),
                   jax.ShapeDtypeStruct((N_TOK, TOP_K), F32),
                   jax.ShapeDtypeStruct((N_TOK, TOP_K), jnp.int32),
                   jax.ShapeDtypeStruct((N_TILES, 1, N_EXPERTS), jnp.int32),
                   jax.ShapeDtypeStruct((N_TILES, 1, N_EXPERTS), jnp.int32)],
        scratch_shapes=[pltpu.VMEM((D_MODEL, N_BRANCH * D_MODEL), BF16), pltpu.SemaphoreType.DMA(())],
        compiler_params=_params(1),
        name="merge",
    )(stream[0], stream[1], ya, yb, yc, yd, mods_l, g1, w_in_b, bg, wb, wo, g2, rwh, rwl, rb)


ROUTE_TM = 256
TILE_ROWS = ROUTE_TM * TOP_K
N_TILES = N_TOK // ROUTE_TM


def _route_tile(lg):
    lane = lax.broadcasted_iota(jnp.int32, lg.shape, 1)
    sels, vals = [], []
    for _ in range(TOP_K):
        mx = lg.max(axis=-1, keepdims=True)
        idx = jnp.where(lg == mx, lane, N_EXPERTS).min(axis=-1, keepdims=True)
        sel = lane == idx
        sels.append(sel)
        vals.append(mx)
        lg = jnp.where(sel, -jnp.inf, lg)
    exps = [jnp.exp(v - vals[0]) for v in vals]
    den = exps[0] + exps[1] + exps[2] + exps[3]
    onehot = jnp.zeros(lg.shape, F32)
    for sel in sels:
        onehot = onehot + sel.astype(F32)
    row = lax.broadcasted_iota(jnp.int32, (ROUTE_TM, ROUTE_TM), 0)
    col = lax.broadcasted_iota(jnp.int32, (ROUTE_TM, ROUTE_TM), 1)
    tri = jnp.where(col < row, 1.0, 0.0).astype(BF16)
    rank = jnp.dot(tri, onehot.astype(BF16), preferred_element_type=F32)
    cnt = jnp.sum(onehot, axis=0, keepdims=True)
    erow = lax.broadcasted_iota(jnp.int32, (N_EXPERTS, N_EXPERTS), 0)
    ecol = lax.broadcasted_iota(jnp.int32, (N_EXPERTS, N_EXPERTS), 1)
    upper = jnp.where(erow < ecol, 1.0, 0.0).astype(BF16)
    off = jnp.dot(jnp.broadcast_to(cnt, (SUBLANES, N_EXPERTS)).astype(BF16), upper,
                  preferred_element_type=F32)[0:1]
    slot = rank + off
    k_lane = lax.broadcasted_iota(jnp.int32, (ROUTE_TM, TOP_K), 1)
    w_out = jnp.zeros((ROUTE_TM, TOP_K), F32)
    q_out = jnp.zeros((ROUTE_TM, TOP_K), F32)
    for k in range(TOP_K):
        w_out = jnp.where(k_lane == k, exps[k] / den, w_out)
        qk = jnp.sum(jnp.where(sels[k], slot, 0.0), axis=-1, keepdims=True)
        q_out = jnp.where(k_lane == k, qk, q_out)
    return w_out, q_out.astype(jnp.int32), cnt.astype(jnp.int32), off.astype(jnp.int32)


PIECE_SIZES = (32, 16, 8, 4, 2, 1)
PIECE_SLOTS = TILE_ROWS // PIECE_SIZES[0]
assert PIECE_SLOTS >= N_EXPERTS


def _compact(valid, *values):
    pos = jnp.cumsum(valid.astype(jnp.int32), axis=1) - 1
    slot = jnp.arange(PIECE_SLOTS, dtype=jnp.int32)
    hit = jnp.logical_and(valid[:, :, None], pos[:, :, None] == slot[None, None, :])
    packed = [jnp.sum(jnp.where(hit, v[:, :, None], 0), axis=1).astype(jnp.int32) for v in values]
    return packed, jnp.sum(valid.astype(jnp.int32), axis=1)


def _piece_lists(cnt, off, row):
    big = PIECE_SIZES[0]
    n_big = cnt // big
    p = jnp.arange(ROUTE_TM // big, dtype=jnp.int32)
    valid = (p[None, None, :] < n_big[:, :, None]).reshape(N_TILES, -1)
    src = (off[:, :, None] + big * p).reshape(N_TILES, -1)
    dst = (row[:, :, None] + big * p).reshape(N_TILES, -1)
    lists = [_compact(valid, src, dst)]
    rem = cnt - n_big * big
    for size in PIECE_SIZES[1:]:
        start = n_big * big + (rem & ~(2 * size - 1))
        lists.append(_compact((rem & size) != 0, off + start, row + start))
    counts = jnp.stack([n for _, n in lists], axis=1).reshape(-1)
    local_rows = jnp.stack([v[0] for v, _ in lists], axis=1).reshape(-1)
    global_rows = jnp.stack([v[1] for v, _ in lists], axis=1).reshape(-1)
    return counts, local_rows, global_rows


def _slab_pieces(tile, count_ref, local_ref, global_ref, fn):
    for k, size in enumerate(PIECE_SIZES):
        lst = tile * len(PIECE_SIZES) + k

        def body(j, carry, lst=lst, size=size):
            fn(local_ref[lst * PIECE_SLOTS + j], global_ref[lst * PIECE_SLOTS + j], size)
            return carry

        lax.fori_loop(0, count_ref[lst], body, 0)


def _rows(ref, row, n_rows):
    start = row * ROW_TILES
    if not isinstance(row, int):
        start = pl.multiple_of(start, ROW_TILES)
    return ref.at[pl.ds(start, n_rows * ROW_TILES)]


def _onehot_rows(q, values=None):
    lane = lax.broadcasted_iota(jnp.int32, (ROUTE_TM, TILE_ROWS), 1)
    s = jnp.zeros((ROUTE_TM, TILE_ROWS), F32)
    for k in range(TOP_K):
        v = 1.0 if values is None else values[:, k:k + 1]
        s = jnp.where(lane == q[:, k:k + 1], v, s)
    return s


def _wait_tile_rows(hbm_ref, buf_slot_ref, sem_slot):
    pltpu.make_async_copy(_rows(hbm_ref, 0, TILE_ROWS), buf_slot_ref, sem_slot).wait()


def _dispatch_kernel(count_ref, local_ref, global_ref, q_ref, h2_ref, xs_ref, buf_ref, sem):
    tile = pl.program_id(0)
    slot = tile % 2
    buf = buf_ref.at[slot]

    @pl.when(tile >= 2)
    def _():
        _wait_tile_rows(xs_ref, buf, sem.at[slot])

    sel = _onehot_rows(q_ref[...]).astype(BF16)
    xg = lax.dot_general(sel, h2_ref[...], (((0,), (0,)), ((), ())), preferred_element_type=F32)
    for c in range(ROW_TILES):
        buf[pl.ds(c, TILE_ROWS, stride=ROW_TILES), :] = xg[:, c * LANES:(c + 1) * LANES]

    def start(local_row, global_row, n_rows):
        pltpu.make_async_copy(_rows(buf, local_row, n_rows), _rows(xs_ref, global_row, n_rows),
                              sem.at[slot]).start()

    _slab_pieces(tile, count_ref, local_ref, global_ref, start)

    @pl.when(tile == N_TILES - 1)
    def _():
        _wait_tile_rows(xs_ref, buf, sem.at[slot])
        _wait_tile_rows(xs_ref, buf_ref.at[1 - slot], sem.at[1 - slot])


def _dispatch(pieces, q, h2):
    grid_spec = pltpu.PrefetchScalarGridSpec(
        num_scalar_prefetch=3,
        grid=(N_TILES,),
        in_specs=[pl.BlockSpec((ROUTE_TM, TOP_K), lambda i, *_: (i, 0)),
                  pl.BlockSpec((ROUTE_TM, D_MODEL), lambda i, *_: (i, 0))],
        out_specs=pl.BlockSpec(memory_space=pl.ANY),
        scratch_shapes=[pltpu.VMEM((2, TILE_ROWS * ROW_TILES, LANES), F32), pltpu.SemaphoreType.DMA((2,))],
    )
    return pl.pallas_call(
        _dispatch_kernel,
        grid_spec=grid_spec,
        out_shape=jax.ShapeDtypeStruct((N_ROWS * ROW_TILES, LANES), F32),
        compiler_params=_params(1),
        name="dispatch",
    )(*pieces, q, h2)


CAST_ROWS = 128
W_SLOTS = 2


def _expert_kernel(blk_exp_ref, nvalid_ref, first_ref, head_ref, slot_ref, next_ref, io_blk_ref,
                   xs_ref, w1_hbm, b1_ref, w2_hbm, b2_ref, y_ref,
                   w1f_ref, w2f_ref, w1b_ref, w2b_ref, sem, *, layer):
    b = pl.program_id(0)
    e = blk_exp_ref[b]
    nvalid = nvalid_ref[b]
    slot = slot_ref[b]

    def start_weights(expert, ahead):
        s = (slot + ahead) % W_SLOTS
        pltpu.make_async_copy(w1_hbm.at[layer, expert], w1f_ref.at[s], sem.at[0, s]).start()
        pltpu.make_async_copy(w2_hbm.at[layer, expert], w2f_ref.at[s], sem.at[1, s]).start()

    @pl.when(first_ref[b] == 1)
    def _():
        @pl.when(head_ref[b] == 1)
        def _():
            start_weights(e, 0)

        pltpu.make_async_copy(w1_hbm.at[layer, e], w1f_ref.at[slot], sem.at[0, slot]).wait()
        pltpu.make_async_copy(w2_hbm.at[layer, e], w2f_ref.at[slot], sem.at[1, slot]).wait()

        @pl.when(next_ref[b] >= 0)
        def _():
            start_weights(next_ref[b], 1)

        def cast1(i, carry):
            r = pl.multiple_of(i * CAST_ROWS, CAST_ROWS)
            w1b_ref[pl.ds(r, CAST_ROWS), :] = w1f_ref[slot, pl.ds(r, CAST_ROWS), :].astype(BF16)
            return carry

        def cast2(i, carry):
            r = pl.multiple_of(i * CAST_ROWS, CAST_ROWS)
            w2b_ref[pl.ds(r, CAST_ROWS), :] = w2f_ref[slot, pl.ds(r, CAST_ROWS), :].astype(BF16)
            return carry

        lax.fori_loop(0, D_MODEL // CAST_ROWS, cast1, 0)
        lax.fori_loop(0, D_FF // CAST_ROWS, cast2, 0)

    def run_rows(n_rows):
        valid = lax.broadcasted_iota(jnp.int32, (n_rows, LANES), 0) < nvalid
        chunks = [jnp.where(valid, xs_ref[pl.ds(c, n_rows, stride=ROW_TILES), :], 0.0).astype(BF16)
                  for c in range(ROW_TILES)]
        xb = jnp.concatenate(chunks, axis=-1)
        hid = jnp.dot(xb, w1b_ref[...], preferred_element_type=F32) + b1_ref[...]
        glu = jnp.minimum(hid[:, :D_FF], SWIGLU_LIMIT)
        lin = jnp.clip(hid[:, D_FF:], -SWIGLU_LIMIT, SWIGLU_LIMIT)
        act = glu * _sigmoid(SWIGLU_ALPHA * glu) * (lin + 1.0)
        y = jnp.dot(act.astype(BF16), w2b_ref[...], preferred_element_type=F32) + b2_ref[...]
        for c in range(ROW_TILES):
            y_ref[pl.ds(c, n_rows, stride=ROW_TILES), :] = y[:, c * LANES:(c + 1) * LANES]
        if n_rows < EXP_BLOCK:
            rest = (EXP_BLOCK - n_rows) * ROW_TILES
            y_ref[pl.ds(n_rows * ROW_TILES, rest), :] = jnp.zeros((rest, LANES), F32)

    @pl.when(nvalid > EXP_BLOCK // 2)
    def _():
        run_rows(EXP_BLOCK)

    @pl.when(jnp.logical_and(nvalid > 0, nvalid <= EXP_BLOCK // 2))
    def _():
        run_rows(EXP_BLOCK // 2)


def _experts(tables, xs, w1, b1, w2, b2, layer):
    def blk(b, *tbl):
        return (tbl[-1][b], 0)

    def bias(b, be, *_):
        return (layer, be[b], 0, 0)

    grid_spec = pltpu.PrefetchScalarGridSpec(
        num_scalar_prefetch=7,
        grid=(N_BLOCKS,),
        in_specs=[
            pl.BlockSpec((EXP_BLOCK * ROW_TILES, LANES), blk),
            pl.BlockSpec(memory_space=pl.ANY),
            pl.BlockSpec((None, None, 1, 2 * D_FF), bias),
            pl.BlockSpec(memory_space=pl.ANY),
            pl.BlockSpec((None, None, 1, D_MODEL), bias),
        ],
        out_specs=pl.BlockSpec((EXP_BLOCK * ROW_TILES, LANES), blk),
        scratch_shapes=[pltpu.VMEM((W_SLOTS, D_MODEL, 2 * D_FF), F32), pltpu.VMEM((W_SLOTS, D_FF, D_MODEL), F32),
                        pltpu.VMEM((D_MODEL, 2 * D_FF), BF16), pltpu.VMEM((D_FF, D_MODEL), BF16),
                        pltpu.SemaphoreType.DMA((2, W_SLOTS))],
    )
    return pl.pallas_call(
        functools.partial(_expert_kernel, layer=layer),
        grid_spec=grid_spec,
        out_shape=jax.ShapeDtypeStruct((N_ROWS * ROW_TILES, LANES), F32),
        compiler_params=_params(1),
        name="experts",
    )(*tables, xs, w1, b1.reshape(DEPTH, N_EXPERTS, 1, 2 * D_FF), w2,
      b2.reshape(DEPTH, N_EXPERTS, 1, D_MODEL))


def _split_bf16(x):
    hi = x.astype(BF16)
    return hi, (x - hi.astype(F32)).astype(BF16)


P_TILES = N_P // ROUTE_TM


def _combine_kernel(count_ref, local_ref, global_ref, q_ref, w_ref, y_ref, xm_ref, mod_ref, fg_ref,
                    *refs, final):
    buf_ref, sem = refs[-2:]
    tile = pl.program_id(0)
    slot = tile % 2
    buf = buf_ref.at[slot]

    def fetch(t, s):
        def start(local_row, global_row, n_rows):
            pltpu.make_async_copy(_rows(y_ref, global_row, n_rows), _rows(buf_ref.at[s], local_row, n_rows),
                                  sem.at[s]).start()

        _slab_pieces(t, count_ref, local_ref, global_ref, start)

    @pl.when(tile == 0)
    def _():
        fetch(tile, slot)

    @pl.when(tile + 1 < N_TILES)
    def _():
        fetch(tile + 1, 1 - slot)

    weighted = _onehot_rows(q_ref[...], w_ref[...])
    pick = jnp.where(weighted != 0.0, 1.0, 0.0).astype(BF16)
    row_w = jnp.sum(weighted, axis=0, keepdims=True)
    row_w = jnp.transpose(jnp.broadcast_to(row_w, (LANES, TILE_ROWS)))
    _wait_tile_rows(y_ref, buf, sem.at[slot])
    rows = jnp.concatenate([buf[pl.ds(c, TILE_ROWS, stride=ROW_TILES), :] * row_w for c in range(ROW_TILES)],
                           axis=-1)
    r_hi, r_lo = _split_bf16(rows)
    moe = jnp.dot(pick, r_hi, preferred_element_type=F32) + jnp.dot(pick, r_lo, preferred_element_type=F32)
    m = mod_ref[...]
    x = xm_ref[...] + m[5:6] * moe
    if not final:
        refs[0][...] = x
        return
    xn = x * lax.rsqrt(jnp.mean(x * x, axis=-1, keepdims=True) + EPS) * fg_ref[...]
    yp_ref, ys_ref = refs[:2]

    @pl.when(tile < P_TILES)
    def _():
        yp_ref[...] = xn

    @pl.when(tile >= P_TILES)
    def _():
        ys_ref[...] = xn


def _combine(pieces, q, topw, y, xm, mods_l, final_g, final):
    def tok(w):
        return pl.BlockSpec((ROUTE_TM, w), lambda i, *_: (i, 0))

    if final:
        out_specs = [pl.BlockSpec((ROUTE_TM, D_MODEL), lambda i, *_: (jnp.minimum(i, P_TILES - 1), 0)),
                     pl.BlockSpec((ROUTE_TM, D_MODEL), lambda i, *_: (jnp.maximum(i - P_TILES, 0), 0))]
        out_shape = [jax.ShapeDtypeStruct((N_P, D_MODEL), F32), jax.ShapeDtypeStruct((N_S, D_MODEL), F32)]
    else:
        out_specs = [tok(D_MODEL)]
        out_shape = [jax.ShapeDtypeStruct((N_TOK, D_MODEL), F32)]
    grid_spec = pltpu.PrefetchScalarGridSpec(
        num_scalar_prefetch=3,
        grid=(N_TILES,),
        in_specs=[tok(TOP_K), tok(TOP_K),
                  pl.BlockSpec(memory_space=pl.ANY),
                  tok(D_MODEL),
                  pl.BlockSpec((None, N_MOD, D_MODEL), lambda i, *_: (_mod_row(i * ROUTE_TM), 0, 0)),
                  pl.BlockSpec((1, D_MODEL), lambda i, *_: (0, 0))],
        out_specs=out_specs,
        scratch_shapes=[pltpu.VMEM((2, TILE_ROWS * ROW_TILES, LANES), F32), pltpu.SemaphoreType.DMA((2,))],
    )
    return pl.pallas_call(
        functools.partial(_combine_kernel, final=final),
        grid_spec=grid_spec,
        out_shape=out_shape,
        compiler_params=_params(1),
        name="combine_final" if final else "combine",
    )(*pieces, q, topw, y, xm, mods_l, final_g)


def _rope_tables():
    t = np.arange(DEC_SEQ)
    row = (t // GRID_W).astype(np.float32)
    col = (t % GRID_W).astype(np.float32)
    inv = jnp.asarray(ROPE_THETA, F32) ** (-jnp.arange(ROPE_PAIRS, dtype=F32) / ROPE_PAIRS)
    ang = jnp.concatenate([jnp.asarray(row)[:, None] * inv, jnp.asarray(col)[:, None] * inv], axis=-1)
    cos = jnp.repeat(jnp.cos(ang), 2, axis=-1)
    sin = jnp.repeat(jnp.sin(ang), 2, axis=-1)
    sign = jnp.asarray(np.tile(np.array([-1.0, 1.0], np.float32), HEAD_DIM // 2))
    return jnp.tile(cos, (1, A_HEADS)), jnp.tile(sin * sign, (1, A_HEADS))


def _routing_tables(tile_cnt):
    i32 = jnp.int32
    carry = jnp.cumsum(tile_cnt, axis=0) - tile_cnt
    counts = jnp.sum(tile_cnt, axis=0)
    padded = (counts + EXP_BLOCK - 1) // EXP_BLOCK * EXP_BLOCK
    pad_end = jnp.cumsum(padded)
    pad_start = pad_end - padded
    rowstart = (pad_start[None, :] + carry).astype(i32)
    blk_row = jnp.arange(N_BLOCKS, dtype=i32) * EXP_BLOCK
    blk_exp = jnp.sum((blk_row[:, None] >= pad_end[None, :]).astype(i32), axis=1)
    blk_exp = jnp.minimum(blk_exp, N_EXPERTS - 1)
    eid = jnp.arange(N_EXPERTS, dtype=i32)

    def pick(table, idx):
        return jnp.sum(jnp.where(idx[:, None] == eid[None, :], table[None, :], 0), axis=1).astype(i32)

    blk_start = pick(pad_start, blk_exp)
    nvalid = jnp.clip(pick(counts, blk_exp) - (blk_row - blk_start), 0, EXP_BLOCK).astype(i32)
    first = jnp.logical_and(blk_row == blk_start, nvalid > 0)
    active = counts > 0
    act_rank = jnp.cumsum(active.astype(i32)) - 1
    later = jnp.logical_and(active[None, :], eid[None, :] > eid[:, None])
    nxt = jnp.min(jnp.where(later, eid[None, :], N_EXPERTS), axis=1)
    nxt = jnp.where(nxt == N_EXPERTS, -1, nxt).astype(i32)
    blk_rank = pick(act_rank, blk_exp)
    head = jnp.logical_and(first, blk_rank == 0)
    n_used = pad_end[-1] // EXP_BLOCK
    io_blk = jnp.minimum(jnp.arange(N_BLOCKS, dtype=i32), n_used - 1).astype(i32)
    tables = (blk_exp, nvalid, first.astype(i32), head.astype(i32), (blk_rank % W_SLOTS).astype(i32),
              pick(nxt + 1, blk_exp) - 1, io_blk)
    return rowstart, tables


def kernel(x_prompt, x_sample, cache_attn_k, cache_attn_v, cache_na_k, cache_na_v, c, c_ctx, w_mod, b_mod, norm1_g, norm2_g, w_in, b_gate, q_norm_g, k_norm_g, na_rpb, conv_w, conv_b, conv_ln_g, conv_ln_b, sgu_ln_g, sgu_ln_b, sgu_w, sgu_b, w_branch, w_out, router_w, router_b, exp_w1, exp_b1, exp_w2, exp_b2, final_g):
    stream = (x_prompt.reshape(N_P, D_MODEL), x_sample.reshape(N_S, D_MODEL), 0)
    cvec =jnp.zeros((SUBLANES, D_MODEL), F32).at[0].set(c_ctx).at[1:1 + DEC_BATCH].set(c)
    mods = _modulation(cvec, w_mod, b_mod).reshape(DEPTH, SUBLANES, N_MOD, D_MODEL)
    cos_t, sin_t = _rope_tables()
    cak = cache_attn_k.reshape(DEC_BATCH, DEPTH, PAST_LEN, A_KV)
    cav = cache_attn_v.reshape(DEC_BATCH, DEPTH, PAST_LEN, A_KV)
    cbk = cache_na_k.reshape(DEC_BATCH, DEPTH, PAST_LEN, B_W)
    cbv = cache_na_v.reshape(DEC_BATCH, DEPTH, PAST_LEN, B_W)
    w_in_b = w_in.astype(BF16)
    w_br = w_branch.astype(BF16)
    w_o = w_out.astype(BF16)
    rw_hi = router_w.astype(BF16)
    rw_lo = (router_w - rw_hi.astype(F32)).astype(BF16)
    final_g2 = final_g.reshape(1, D_MODEL)

    caches = ()
    outs = None
    for l in range(DEPTH):
        mods_l = mods[l]
        g1 = norm1_g[l].reshape(1, D_MODEL)
        gq = jnp.tile(q_norm_g[l], A_HEADS).reshape(1, A_Q)
        gk = jnp.tile(k_norm_g[l], A_KV_HEADS).reshape(1, A_KV)
        aq, ak, av, bq, bk, bv, cz, dz = _in_proj(stream, mods_l, g1, w_in_b, l)
        ya, yb, *caches = _prompt_attn(aq, ak, av, bq, bk, bv, gq, gk, caches, l)
        ya = _sample_attn(aq, ak, av, cak, cav, cos_t, sin_t, gq, gk, ya, l)
        yb = _na_attn(bq, bk, bv, cbk, cbv, _na_bias(na_rpb[l]), yb, l)
        cw = conv_w[l]
        cb = conv_b[l].reshape(1, C_WIDTH)
        cg = conv_ln_g[l].reshape(1, C_WIDTH)
        cbb = conv_ln_b[l].reshape(1, C_WIDTH)
        yc = _conv_call(cz, cw, cb, cg, cbb, SEQ, 0, BATCH)
        yc = _conv_call(cz, cw, cb, cg, cbb, DEC_SEQ, N_P // DEC_SEQ, DEC_BATCH, partial_out=yc)
        bs_full = jnp.repeat(sgu_b[l].T, SGU_GW, axis=1)
        yd = _sgu(dz, sgu_ln_g[l].reshape(1, SGU_WIDTH), sgu_ln_b[l].reshape(1, SGU_WIDTH),
                  sgu_w[l].astype(BF16), bs_full)
        xm, h2, top_w, q, tile_cnt, tile_off = _merge(
            stream, ya, yb, yc, yd, mods_l, g1, w_in_b, b_gate[l].reshape(1, N_BRANCH * D_MODEL), w_br[l], w_o[l],
            norm2_g[l].reshape(1, D_MODEL), rw_hi[l], rw_lo[l], router_b[l].reshape(1, N_EXPERTS), l)
        tile_cnt = tile_cnt.reshape(N_TILES, N_EXPERTS)
        rowstart, tables = _routing_tables(tile_cnt)
        pieces = _piece_lists(tile_cnt, tile_off.reshape(N_TILES, N_EXPERTS), rowstart)
        xs = _dispatch(pieces, q, h2)
        y = _experts(tables, xs, exp_w1, exp_b1, exp_w2, exp_b2, l)
        outs = _combine(pieces, q, top_w, y, xm, mods_l, final_g2, l == DEPTH - 1)
        stream = (outs[0], outs[0], N_P)

    new_k, new_v, new_bk, new_bv = caches
    return (outs[0].reshape(BATCH, SEQ, D_MODEL), outs[1].reshape(DEC_BATCH, DEC_SEQ, D_MODEL),
            new_k.reshape(BATCH, DEPTH, SEQ, A_KV_HEADS, HEAD_DIM),
            new_v.reshape(BATCH, DEPTH, SEQ, A_KV_HEADS, HEAD_DIM),
            new_bk.reshape(BATCH, DEPTH, SEQ, B_HEADS, HEAD_DIM),
            new_bv.reshape(BATCH, DEPTH, SEQ, B_HEADS, HEAD_DIM))
```

```python
import functools

import numpy as np
import jax
import jax.numpy as jnp
from jax import lax
from jax.experimental import pallas as pl
from jax.experimental.pallas import tpu as pltpu

D_MODEL = 1024
BATCH = 32
SEQ = 256
DEPTH = 2
DEC_BATCH = 2
DEC_SEQ = 1024
PAST_LEN = 512
GRID_W = 64
HEAD_DIM = 64
A_HEADS = 4
A_KV_HEADS = 2
B_HEADS = 4
NA_ROWS = 8
NA_COLS = 16
C_WIDTH = 256
CONV_WIDTH = 31
SGU_WIDTH = 256
SGU_GROUPS = 4
SGU_CHUNK = 128
N_BRANCH = 4
BRANCH_W = 256
N_EXPERTS = 32
TOP_K = 4
D_FF = 1024
SWIGLU_ALPHA = 1.702
SWIGLU_LIMIT = 7.0
MOE_BLOCK = 128
ROPE_THETA = 10000.0
ROPE_PAIRS = HEAD_DIM // 4
N_MOD = 6
EPS = 1e-6
NEG_INF = -1e30

A_Q = A_HEADS * HEAD_DIM
A_KV = A_KV_HEADS * HEAD_DIM
B_W = B_HEADS * HEAD_DIM
MIX_SIZES = (A_Q, A_KV, A_KV, B_W, B_W, B_W, 2 * C_WIDTH, 2 * SGU_WIDTH)
MIX_COLS = sum(MIX_SIZES)

N_P = BATCH * SEQ
N_S = DEC_BATCH * DEC_SEQ
N_TOK = N_P + N_S
N_ASSIGN = N_TOK * TOP_K
EXP_BLOCK = 512
N_BLOCKS = N_ASSIGN // EXP_BLOCK + N_EXPERTS
N_ROWS = N_BLOCKS * EXP_BLOCK
GRID_ROWS = DEC_SEQ // GRID_W
NA_WR = min(NA_ROWS, GRID_ROWS)
N_LOC = NA_WR * GRID_W

SUBLANES = 8
LANES = 128
ROW_TILES = D_MODEL // LANES
VMEM_LIMIT = 56 * 1024 * 1024

F32 = jnp.float32
BF16 = jnp.bfloat16


def _params(n_axes, vmem=None):
    return pltpu.CompilerParams(
        dimension_semantics=("arbitrary",) * n_axes,
        vmem_limit_bytes=vmem if vmem is not None else VMEM_LIMIT)


def _mod_row(start):
    return jnp.where(start < N_P, 0, 1 + (start - N_P) // DEC_SEQ)


def _bdot(a, b):
    return jnp.dot(a.astype(BF16), b.astype(BF16), preferred_element_type=F32)


def _bdot_nt(a, b):
    return lax.dot_general(a.astype(BF16), b.astype(BF16), (((1,), (1,)), ((), ())),
                           preferred_element_type=F32)


def _sigmoid(x):
    return 0.5 * jnp.tanh(0.5 * x) + 0.5


MOD_TN = 1536


def _mod_kernel(c_ref, w_ref, b_ref, o_ref):
    c = c_ref[...]
    s = c * _sigmoid(c)
    o_ref[...] = _bdot(s, w_ref[...]) + b_ref[...]


def _modulation(cvec, w_mod, b_mod):
    n_col = N_MOD * D_MODEL
    return pl.pallas_call(
        _mod_kernel,
        grid=(DEPTH, n_col // MOD_TN),
        in_specs=[
            pl.BlockSpec((SUBLANES, D_MODEL), lambda l, j: (0, 0)),
            pl.BlockSpec((None, D_MODEL, MOD_TN), lambda l, j: (l, 0, j)),
            pl.BlockSpec((None, 1, MOD_TN), lambda l, j: (l, 0, j)),
        ],
        out_specs=pl.BlockSpec((None, SUBLANES, MOD_TN), lambda l, j: (l, 0, j)),
        out_shape=jax.ShapeDtypeStruct((DEPTH, SUBLANES, n_col), F32),
        compiler_params=_params(2),
        name="modulation",
    )(cvec, w_mod, b_mod.reshape(DEPTH, 1, n_col))


IN_TM = 512


def _norm_mod(x, g, shift, scale):
    y = x * lax.rsqrt(jnp.mean(x * x, axis=-1, keepdims=True) + EPS) * g
    return y * (1.0 + scale) + shift


def _stream_specs(tm, stream):
    p_tiles = N_P // tm
    s_first = stream[2] // tm
    return [pl.BlockSpec((tm, D_MODEL), lambda i: (jnp.minimum(i, p_tiles - 1), 0)),
            pl.BlockSpec((tm, D_MODEL), lambda i: (jnp.maximum(i - p_tiles, 0) + s_first, 0))]


def _stream_tile(xp_ref, xs_ref, tm):
    return jnp.where(pl.program_id(0) < N_P // tm, xp_ref[...], xs_ref[...])


def _in_kernel(xp_ref, xs_ref, mod_ref, g_ref, w_ref, *out_refs):
    m = mod_ref[...]
    h = _norm_mod(_stream_tile(xp_ref, xs_ref, IN_TM), g_ref[...], m[0:1], m[1:2])
    z = jnp.dot(h.astype(BF16), w_ref[...], preferred_element_type=F32)
    off = 0
    for o_ref, sz in zip(out_refs, MIX_SIZES):
        o_ref[...] = z[:, off:off + sz]
        off += sz


def _in_proj(stream, mods_l, g1, w_in_b, layer):
    return pl.pallas_call(
        _in_kernel,
        grid=(N_TOK // IN_TM,),
        in_specs=_stream_specs(IN_TM, stream) + [
            pl.BlockSpec((None, N_MOD, D_MODEL), lambda i: (_mod_row(i * IN_TM), 0, 0)),
            pl.BlockSpec((1, D_MODEL), lambda i: (0, 0)),
            pl.BlockSpec((None, D_MODEL, MIX_COLS), lambda i: (layer, 0, 0)),
        ],
        out_specs=[pl.BlockSpec((IN_TM, sz), lambda i: (i, 0)) for sz in MIX_SIZES],
        out_shape=[jax.ShapeDtypeStruct((N_TOK, sz), F32) for sz in MIX_SIZES],
        compiler_params=_params(1),
        name="in_proj",
    )(stream[0], stream[1], mods_l, g1, w_in_b)


def _head_rms(x, g):
    width = x.shape[-1]
    seg_r = lax.broadcasted_iota(jnp.int32, (width, width), 0) // HEAD_DIM
    seg_c = lax.broadcasted_iota(jnp.int32, (width, width), 1) // HEAD_DIM
    avg = jnp.where(seg_r == seg_c, 1.0 / HEAD_DIM, 0.0).astype(BF16)
    hi, lo = _split_bf16(x * x)
    ms = jnp.dot(hi, avg, preferred_element_type=F32) + jnp.dot(lo, avg, preferred_element_type=F32)
    return x * lax.rsqrt(ms + EPS) * g


def _softmax_pv(score_parts, value_parts):
    m = score_parts[0].max(axis=-1, keepdims=True)
    for s in score_parts[1:]:
        m = jnp.maximum(m, s.max(axis=-1, keepdims=True))
    den = None
    acc = None
    for s, v in zip(score_parts, value_parts):
        e = jnp.exp(s - m)
        d = e.sum(axis=-1, keepdims=True)
        a = _bdot(e, v)
        den = d if den is None else den + d
        acc = a if acc is None else acc + a
    return acc / den


def _head(x, h):
    return x[:, h * HEAD_DIM:(h + 1) * HEAD_DIM]


SCALE = HEAD_DIM ** -0.5


def _prompt_attn_kernel(aq_ref, ak_ref, av_ref, bq_ref, bk_ref, bv_ref, gq_ref, gk_ref, *refs):
    ya_ref, yb_ref, nk_ref, nv_ref, nbk_ref, nbv_ref = refs[-6:]
    aq = _head_rms(aq_ref[...], gq_ref[...]) * SCALE
    ak = _head_rms(ak_ref[...], gk_ref[...])
    av = av_ref[...]
    for ref, val in ((nk_ref, ak), (nv_ref, av), (nbk_ref, bk_ref[...]), (nbv_ref, bv_ref[...])):
        n_heads = val.shape[-1] // HEAD_DIM
        for h in range(n_heads):
            ref[pl.ds(h, SEQ, stride=n_heads), :] = _head(val, h)
    grp = A_HEADS // A_KV_HEADS
    outs = []
    for h in range(A_HEADS):
        s = _bdot_nt(_head(aq, h), _head(ak, h // grp))
        outs.append(_softmax_pv([s], [_head(av, h // grp)]))
    ya_ref[...] = jnp.concatenate(outs, axis=-1)
    bq = bq_ref[...] * SCALE
    bk = bk_ref[...]
    bv = bv_ref[...]
    outs = []
    for h in range(B_HEADS):
        s = _bdot_nt(_head(bq, h), _head(bk, h))
        outs.append(_softmax_pv([s], [_head(bv, h)]))
    yb_ref[...] = jnp.concatenate(outs, axis=-1)


def _prompt_attn(aq, ak, av, bq, bk, bv, gq, gk, caches, layer):
    def spec(w):
        return pl.BlockSpec((SEQ, w), lambda b: (b, 0))

    def cache_spec(n_heads):
        return pl.BlockSpec((SEQ * n_heads, HEAD_DIM), lambda b: (b * DEPTH + layer, 0))

    cache_heads = (A_KV_HEADS, A_KV_HEADS, B_HEADS, B_HEADS)
    n_in = 8
    return pl.pallas_call(
        _prompt_attn_kernel,
        grid=(BATCH,),
        in_specs=[spec(A_Q), spec(A_KV), spec(A_KV), spec(B_W), spec(B_W), spec(B_W),
                  pl.BlockSpec((1, A_Q), lambda b: (0, 0)),
                  pl.BlockSpec((1, A_KV), lambda b: (0, 0))]
        + [pl.BlockSpec(memory_space=pl.ANY) for _ in caches],
        out_specs=[spec(A_Q), spec(B_W)] + [cache_spec(n) for n in cache_heads],
        out_shape=[jax.ShapeDtypeStruct((N_TOK, A_Q), F32),
                   jax.ShapeDtypeStruct((N_TOK, B_W), F32)]
        + [jax.ShapeDtypeStruct((BATCH * DEPTH * SEQ * n, HEAD_DIM), F32) for n in cache_heads],
        input_output_aliases={n_in + j: 2 + j for j in range(len(caches))},
        compiler_params=_params(1),
        name="prompt_attn",
    )(aq, ak, av, bq, bk, bv, gq, gk, *caches)


QB = 128


def _rope(x, cos, sin_signed):
    n = x.shape[-1]
    nxt = pltpu.roll(x, n - 1, 1)
    prv = pltpu.roll(x, 1, 1)
    even = (lax.broadcasted_iota(jnp.int32, x.shape, 1) % 2) == 0
    return x * cos + jnp.where(even, nxt, prv) * sin_signed


def _sample_attn_kernel(q_ref, k_ref, v_ref, ck_ref, cv_ref, cosq_ref, sinq_ref, cosk_ref, sink_ref,
                        gq_ref, gk_ref, ya_prompt_ref, o_ref, kr_ref):
    del ya_prompt_ref

    @pl.when(pl.program_id(1) == 0)
    def _():
        kr_ref[...] = _rope(_head_rms(k_ref[...], gk_ref[...]), cosk_ref[...], sink_ref[...]).astype(BF16)

    q = _rope(_head_rms(q_ref[...], gq_ref[...]), cosq_ref[...], sinq_ref[...]) * SCALE
    k = kr_ref[...]
    v = v_ref[...]
    ck = ck_ref[...]
    cv = cv_ref[...]
    grp = A_HEADS // A_KV_HEADS
    outs = []
    for h in range(A_HEADS):
        j = h // grp
        qh = _head(q, h)
        s1 = _bdot_nt(qh, _head(k, j))
        s2 = _bdot_nt(qh, _head(ck, j))
        outs.append(_softmax_pv([s1, s2], [_head(v, j), _head(cv, j)]))
    o_ref[...] = jnp.concatenate(outs, axis=-1)


def _sample_attn(aq, ak, av, cache_k, cache_v, cos_t, sin_t, gq, gk, ya, layer):
    nqb = DEC_SEQ // QB
    q0 = N_P // QB
    k0 = N_P // DEC_SEQ
    return pl.pallas_call(
        _sample_attn_kernel,
        grid=(DEC_BATCH, nqb),
        in_specs=[
            pl.BlockSpec((QB, A_Q), lambda b, i: (q0 + b * nqb + i, 0)),
            pl.BlockSpec((DEC_SEQ, A_KV), lambda b, i: (k0 + b, 0)),
            pl.BlockSpec((DEC_SEQ, A_KV), lambda b, i: (k0 + b, 0)),
            pl.BlockSpec((None, None, PAST_LEN, A_KV), lambda b, i: (b, layer, 0, 0)),
            pl.BlockSpec((None, None, PAST_LEN, A_KV), lambda b, i: (b, layer, 0, 0)),
            pl.BlockSpec((QB, A_Q), lambda b, i: (i, 0)),
            pl.BlockSpec((QB, A_Q), lambda b, i: (i, 0)),
            pl.BlockSpec((DEC_SEQ, A_KV), lambda b, i: (0, 0)),
            pl.BlockSpec((DEC_SEQ, A_KV), lambda b, i: (0, 0)),
            pl.BlockSpec((1, A_Q), lambda b, i: (0, 0)),
            pl.BlockSpec((1, A_KV), lambda b, i: (0, 0)),
            pl.BlockSpec(memory_space=pl.ANY),
        ],
        out_specs=pl.BlockSpec((QB, A_Q), lambda b, i: (q0 + b * nqb + i, 0)),
        out_shape=jax.ShapeDtypeStruct((N_TOK, A_Q), F32),
        input_output_aliases={11: 0},
        scratch_shapes=[pltpu.VMEM((DEC_SEQ, A_KV), BF16)],
        compiler_params=_params(2),
        name="sample_attn",
    )(aq, ak, av, cache_k, cache_v, cos_t, sin_t, cos_t, sin_t, gq, gk, ya)


N_ROW_OFF = 2 * NA_ROWS - 1
N_COL_OFF = 2 * NA_COLS - 1
NA_PAIRS = N_ROW_OFF - 1
assert NA_WR == NA_ROWS and NA_WR % 2 == 0 and 2 * GRID_W == LANES


def _na_bias_kernel(rpb_ref, o_ref):
    h = pl.program_id(0)
    qc = lax.broadcasted_iota(jnp.int32, (GRID_W, LANES), 0)
    lane = lax.broadcasted_iota(jnp.int32, (GRID_W, LANES), 1)
    right = lane >= GRID_W
    kc = jnp.where(right, lane - GRID_W, lane)
    c_start = jnp.clip(qc - NA_COLS // 2, 0, GRID_W - NA_COLS)
    col_in = jnp.logical_and(kc >= c_start, kc < c_start + NA_COLS)
    col_off = jnp.clip(kc - qc + NA_COLS - 1, 0, N_COL_OFF - 1)
    for p in range(NA_PAIRS):
        acc = jnp.zeros((GRID_W, LANES), F32)
        for o in range(N_COL_OFF):
            left_v = rpb_ref[(h * N_ROW_OFF + p) * N_COL_OFF + o]
            right_v = rpb_ref[(h * N_ROW_OFF + p + 1) * N_COL_OFF + o]
            acc = jnp.where(col_off == o, jnp.where(right, right_v, left_v), acc)
        o_ref[p] = jnp.where(col_in, acc, NEG_INF)


def _na_bias(rpb):
    return pl.pallas_call(
        _na_bias_kernel,
        grid_spec=pltpu.PrefetchScalarGridSpec(
            num_scalar_prefetch=1,
            grid=(B_HEADS,),
            in_specs=[],
            out_specs=pl.BlockSpec((None, NA_PAIRS, GRID_W, LANES), lambda h, *_: (h, 0, 0, 0)),
        ),
        out_shape=jax.ShapeDtypeStruct((B_HEADS, NA_PAIRS, GRID_W, LANES), F32),
        compiler_params=_params(1),
        name="na_bias",
    )(rpb.reshape(-1))


NA_STEP_ROWS = 2


def _na_kernel(q_ref, k_ref, v_ref, ck_ref, cv_ref, bias_ref, yb_prompt_ref, o_ref):
    del yb_prompt_ref
    q = q_ref[...] * SCALE
    ck = ck_ref[...]
    cv = cv_ref[...]
    ctx_scores = [_bdot_nt(_head(q, h), _head(ck, h)) for h in range(B_HEADS)]
    for i in range(NA_STEP_ROWS):
        r = pl.program_id(1) * NA_STEP_ROWS + i
        r_start = jnp.clip(r - NA_WR // 2, 0, GRID_ROWS - NA_WR)
        base = pl.multiple_of(r_start * GRID_W, GRID_W)
        row_off0 = r_start - r + NA_ROWS - 1
        rows = slice(i * GRID_W, (i + 1) * GRID_W)
        kb = k_ref[pl.ds(base, N_LOC), :]
        vb = v_ref[pl.ds(base, N_LOC), :]
        outs = []
        for h in range(B_HEADS):
            bias = jnp.concatenate([bias_ref[h, row_off0 + 2 * j] for j in range(NA_WR // 2)], axis=-1)
            s1 = _bdot_nt(_head(q[rows], h), _head(kb, h)) + bias
            outs.append(_softmax_pv([s1, ctx_scores[h][rows]], [_head(vb, h), _head(cv, h)]))
        o_ref[rows, :] = jnp.concatenate(outs, axis=-1)


def _na_attn(bq, bk, bv, cache_k, cache_v, bias, yb, layer):
    step_tok = NA_STEP_ROWS * GRID_W
    steps = GRID_ROWS // NA_STEP_ROWS
    q0 = N_P // step_tok
    k0 = N_P // DEC_SEQ
    return pl.pallas_call(
        _na_kernel,
        grid=(DEC_BATCH, steps),
        in_specs=[
            pl.BlockSpec((step_tok, B_W), lambda b, r: (q0 + b * steps + r, 0)),
            pl.BlockSpec((DEC_SEQ, B_W), lambda b, r: (k0 + b, 0)),
            pl.BlockSpec((DEC_SEQ, B_W), lambda b, r: (k0 + b, 0)),
            pl.BlockSpec((None, None, PAST_LEN, B_W), lambda b, r: (b, layer, 0, 0)),
            pl.BlockSpec((None, None, PAST_LEN, B_W), lambda b, r: (b, layer, 0, 0)),
            pl.BlockSpec((B_HEADS, NA_PAIRS, GRID_W, LANES), lambda b, r: (0, 0, 0, 0)),
            pl.BlockSpec(memory_space=pl.ANY),
        ],
        out_specs=pl.BlockSpec((step_tok, B_W), lambda b, r: (q0 + b * steps + r, 0)),
        out_shape=jax.ShapeDtypeStruct((N_TOK, B_W), F32),
        input_output_aliases={6: 0},
        compiler_params=_params(2),
        name="na_attn",
    )(bq, bk, bv, cache_k, cache_v, bias, yb)


CONV_PAD = 16
CONV_CHUNK = 128


def _layer_norm(x, g, b):
    mu = jnp.mean(x, axis=-1, keepdims=True)
    xc = x - mu
    var = jnp.mean(xc * xc, axis=-1, keepdims=True)
    return xc * lax.rsqrt(var + EPS) * g + b


def _conv_kernel(z_ref, w_ref, cb_ref, g_ref, b_ref, *refs, s_len):
    o_ref, pad_ref = refs[-2:]
    z = z_ref[...]
    u = z[:, :C_WIDTH] * _sigmoid(z[:, C_WIDTH:])
    pad_ref[pl.ds(0, CONV_PAD), :] = jnp.zeros((CONV_PAD, C_WIDTH), F32)
    pad_ref[pl.ds(CONV_PAD + s_len, CONV_PAD), :] = jnp.zeros((CONV_PAD, C_WIDTH), F32)
    pad_ref[pl.ds(CONV_PAD, s_len), :] = u
    w = w_ref[...]
    shift = CONV_PAD - CONV_WIDTH // 2

    def chunk(c, carry):
        base = pl.multiple_of(c * CONV_CHUNK, CONV_CHUNK)
        acc = jnp.zeros((CONV_CHUNK, C_WIDTH), F32)
        for r in range(SUBLANES):
            part = None
            for k in range(CONV_WIDTH):
                if (k + shift) % SUBLANES != r:
                    continue
                rows = pad_ref[pl.ds(base + (k + shift - r), CONV_CHUNK + SUBLANES), :]
                term = rows * w[k:k + 1]
                part = term if part is None else part + term
            if part is not None:
                acc = acc + part[r:r + CONV_CHUNK]
        y = _layer_norm(acc + cb_ref[...], g_ref[...], b_ref[...])
        o_ref[pl.ds(base, CONV_CHUNK), :] = y * _sigmoid(y)
        return carry

    lax.fori_loop(0, s_len // CONV_CHUNK, chunk, 0)


def _conv_call(cz, w, cb, g, b, s_len, first_blk, n_seq, partial_out=None):
    vec = pl.BlockSpec((1, C_WIDTH), lambda i: (0, 0))
    extra = [] if partial_out is None else [partial_out]
    return pl.pallas_call(
        functools.partial(_conv_kernel, s_len=s_len),
        grid=(n_seq,),
        in_specs=[pl.BlockSpec((s_len, 2 * C_WIDTH), lambda i: (first_blk + i, 0)),
                  pl.BlockSpec((CONV_WIDTH, C_WIDTH), lambda i: (0, 0)), vec, vec, vec]
        + [pl.BlockSpec(memory_space=pl.ANY) for _ in extra],
        out_specs=pl.BlockSpec((s_len, C_WIDTH), lambda i: (first_blk + i, 0)),
        out_shape=jax.ShapeDtypeStruct((N_TOK, C_WIDTH), F32),
        input_output_aliases={5: 0} if extra else {},
        scratch_shapes=[pltpu.VMEM((s_len + 2 * CONV_PAD, C_WIDTH), F32)],
        compiler_params=_params(1),
        name="conformer_conv_%d" % s_len,
    )(cz, w, cb, g, b, *extra)


SGU_TM = 512
SGU_GW = SGU_WIDTH // SGU_GROUPS


def _sgu_kernel(z_ref, g_ref, b_ref, ws_ref, bs_ref, o_ref):
    z = z_ref[...]
    z = 0.5 * z * (1.0 + lax.erf(z * (2.0 ** -0.5)))
    u = z[:, :SGU_WIDTH]
    v = _layer_norm(z[:, SGU_WIDTH:], g_ref[...], b_ref[...])
    for c in range(SGU_TM // SGU_CHUNK):
        vc = v[c * SGU_CHUNK:(c + 1) * SGU_CHUNK]
        parts = [_bdot(ws_ref[g], vc[:, g * SGU_GW:(g + 1) * SGU_GW]) for g in range(SGU_GROUPS)]
        mixed = jnp.concatenate(parts, axis=-1) + bs_ref[...]
        o_ref[pl.ds(c * SGU_CHUNK, SGU_CHUNK), :] = u[c * SGU_CHUNK:(c + 1) * SGU_CHUNK] * mixed


def _sgu(dz, g, b, ws, bs_full):
    vec = pl.BlockSpec((1, SGU_WIDTH), lambda i: (0, 0))
    return pl.pallas_call(
        _sgu_kernel,
        grid=(N_TOK // SGU_TM,),
        in_specs=[pl.BlockSpec((SGU_TM, 2 * SGU_WIDTH), lambda i: (i, 0)), vec, vec,
                  pl.BlockSpec((SGU_GROUPS, SGU_CHUNK, SGU_CHUNK), lambda i: (0, 0, 0)),
                  pl.BlockSpec((SGU_CHUNK, SGU_WIDTH), lambda i: (0, 0))],
        out_specs=pl.BlockSpec((SGU_TM, SGU_WIDTH), lambda i: (i, 0)),
        out_shape=jax.ShapeDtypeStruct((N_TOK, SGU_WIDTH), F32),
        compiler_params=_params(1),
        name="chunk_sgu",
    )(dz, g, b, ws, bs_full)


MERGE_TM = 512


def _merge_kernel(xp_ref, xs_ref, ya_ref, yb_ref, yc_ref, yd_ref, mod_ref, g1_ref, w_in_hbm, bg_ref, wb_ref, wo_ref,
                  g2_ref, rwh_ref, rwl_ref, rb_ref, xm_ref, h2_ref, w_ref, q_ref, cnt_ref, off_ref,
                  wg_ref, sem, *, layer):
    @pl.when(pl.program_id(0) == 0)
    def _():
        cp = pltpu.make_async_copy(w_in_hbm.at[layer, :, pl.ds(MIX_COLS, N_BRANCH * D_MODEL)], wg_ref, sem)
        cp.start()
        cp.wait()

    m = mod_ref[...]
    x = _stream_tile(xp_ref, xs_ref, MERGE_TM)
    h = _norm_mod(x, g1_ref[...], m[0:1], m[1:2]).astype(BF16)
    merged = None
    for i, y_ref in enumerate((ya_ref, yb_ref, yc_ref, yd_ref)):
        logit = jnp.dot(h, wg_ref[:, i * D_MODEL:(i + 1) * D_MODEL], preferred_element_type=F32)
        gate = _sigmoid(logit + bg_ref[:, i * D_MODEL:(i + 1) * D_MODEL])
        term = gate * jnp.dot(y_ref[...].astype(BF16), wb_ref[i], preferred_element_type=F32)
        merged = term if merged is None else merged + term
    out = jnp.dot(merged.astype(BF16), wo_ref[...], preferred_element_type=F32)
    xm = x + m[2:3] * out
    xm_ref[...] = xm
    h2 = _norm_mod(xm, g2_ref[...], m[3:4], m[4:5])
    h2_hi = h2.astype(BF16)
    h2_lo = (h2 - h2_hi.astype(F32)).astype(BF16)
    lg = jnp.dot(h2_hi, rwh_ref[...], preferred_element_type=F32)
    lg = lg + jnp.dot(h2_hi, rwl_ref[...], preferred_element_type=F32)
    lg = lg + jnp.dot(h2_lo, rwh_ref[...], preferred_element_type=F32)
    lg = lg + rb_ref[...]
    h2_ref[...] = h2_hi
    for j in range(MERGE_TM // ROUTE_TM):
        rows = slice(j * ROUTE_TM, (j + 1) * ROUTE_TM)
        w_out, q_out, cnt, off = _route_tile(lg[rows])
        w_ref[rows, :] = w_out
        q_ref[rows, :] = q_out
        cnt_ref[j] = cnt
        off_ref[j] = off


def _merge(stream, ya, yb, yc, yd, mods_l, g1, w_in_b, bg, wb, wo, g2, rwh, rwl, rb, layer):
    def tok(w):
        return pl.BlockSpec((MERGE_TM, w), lambda i: (i, 0))

    def full(*shape):
        return pl.BlockSpec(shape, lambda i: (0,) * len(shape))

    tile_rows = pl.BlockSpec((MERGE_TM // ROUTE_TM, 1, N_EXPERTS), lambda i: (i, 0, 0))
    return pl.pallas_call(
        functools.partial(_merge_kernel, layer=layer),
        grid=(N_TOK // MERGE_TM,),
        in_specs=_stream_specs(MERGE_TM, stream) + [
                  tok(BRANCH_W), tok(BRANCH_W), tok(BRANCH_W), tok(BRANCH_W),
                  pl.BlockSpec((None, N_MOD, D_MODEL), lambda i: (_mod_row(i * MERGE_TM), 0, 0)),
                  full(1, D_MODEL), pl.BlockSpec(memory_space=pl.ANY), full(1, N_BRANCH * D_MODEL),
                  full(N_BRANCH, BRANCH_W, D_MODEL), full(D_MODEL, D_MODEL), full(1, D_MODEL),
                  full(D_MODEL, N_EXPERTS), full(D_MODEL, N_EXPERTS), full(1, N_EXPERTS)],
        out_specs=[tok(D_MODEL), tok(D_MODEL), tok(TOP_K), tok(TOP_K), tile_rows, tile_rows],
        out_shape=[jax.ShapeDtypeStruct((N_TOK, D_MODEL), F32),
                   jax.ShapeDtypeStruct((N_TOK, D_MODEL), BF16),
                   jax.ShapeDtypeStruct((N_TOK, TOP_K), F32),
                   jax.ShapeDtypeStruct((N_TOK, TOP_K), jnp.int32),
                   jax.ShapeDtypeStruct((N_TILES, 1, N_EXPERTS), jnp.int32),
                   jax.ShapeDtypeStruct((N_TILES, 1, N_EXPERTS), jnp.int32)],
        scratch_shapes=[pltpu.VMEM((D_MODEL, N_BRANCH * D_MODEL), BF16), pltpu.SemaphoreType.DMA(())],
        compiler_params=_params(1),
        name="merge",
    )(stream[0], stream[1], ya, yb, yc, yd, mods_l, g1, w_in_b, bg, wb, wo, g2, rwh, rwl, rb)


ROUTE_TM = 256
TILE_ROWS = ROUTE_TM * TOP_K
N_TILES = N_TOK // ROUTE_TM


def _route_tile(lg):
    lane = lax.broadcasted_iota(jnp.int32, lg.shape, 1)
    sels, vals = [], []
    for _ in range(TOP_K):
        mx = lg.max(axis=-1, keepdims=True)
        idx = jnp.where(lg == mx, lane, N_EXPERTS).min(axis=-1, keepdims=True)
        sel = lane == idx
        sels.append(sel)
        vals.append(mx)
        lg = jnp.where(sel, -jnp.inf, lg)
    exps = [jnp.exp(v - vals[0]) for v in vals]
    den = exps[0] + exps[1] + exps[2] + exps[3]
    onehot = jnp.zeros(lg.shape, F32)
    for sel in sels:
        onehot = onehot + sel.astype(F32)
    row = lax.broadcasted_iota(jnp.int32, (ROUTE_TM, ROUTE_TM), 0)
    col = lax.broadcasted_iota(jnp.int32, (ROUTE_TM, ROUTE_TM), 1)
    tri = jnp.where(col < row, 1.0, 0.0).astype(BF16)
    rank = jnp.dot(tri, onehot.astype(BF16), preferred_element_type=F32)
    cnt = jnp.sum(onehot, axis=0, keepdims=True)
    erow = lax.broadcasted_iota(jnp.int32, (N_EXPERTS, N_EXPERTS), 0)
    ecol = lax.broadcasted_iota(jnp.int32, (N_EXPERTS, N_EXPERTS), 1)
    upper = jnp.where(erow < ecol, 1.0, 0.0).astype(BF16)
    off = jnp.dot(jnp.broadcast_to(cnt, (SUBLANES, N_EXPERTS)).astype(BF16), upper,
                  preferred_element_type=F32)[0:1]
    slot = rank + off
    k_lane = lax.broadcasted_iota(jnp.int32, (ROUTE_TM, TOP_K), 1)
    w_out = jnp.zeros((ROUTE_TM, TOP_K), F32)
    q_out = jnp.zeros((ROUTE_TM, TOP_K), F32)
    for k in range(TOP_K):
        w_out = jnp.where(k_lane == k, exps[k] / den, w_out)
        qk = jnp.sum(jnp.where(sels[k], slot, 0.0), axis=-1, keepdims=True)
        q_out = jnp.where(k_lane == k, qk, q_out)
    return w_out, q_out.astype(jnp.int32), cnt.astype(jnp.int32), off.astype(jnp.int32)


PIECE_SIZES = (32, 16, 8, 4, 2, 1)
PIECE_SLOTS = TILE_ROWS // PIECE_SIZES[0]
assert PIECE_SLOTS >= N_EXPERTS


def _compact(valid, *values):
    pos = jnp.cumsum(valid.astype(jnp.int32), axis=1) - 1
    slot = jnp.arange(PIECE_SLOTS, dtype=jnp.int32)
    hit = jnp.logical_and(valid[:, :, None], pos[:, :, None] == slot[None, None, :])
    packed = [jnp.sum(jnp.where(hit, v[:, :, None], 0), axis=1).astype(jnp.int32) for v in values]
    return packed, jnp.sum(valid.astype(jnp.int32), axis=1)


def _piece_lists(cnt, off, row):
    big = PIECE_SIZES[0]
    n_big = cnt // big
    p = jnp.arange(ROUTE_TM // big, dtype=jnp.int32)
    valid = (p[None, None, :] < n_big[:, :, None]).reshape(N_TILES, -1)
    src = (off[:, :, None] + big * p).reshape(N_TILES, -1)
    dst = (row[:, :, None] + big * p).reshape(N_TILES, -1)
    lists = [_compact(valid, src, dst)]
    rem = cnt - n_big * big
    for size in PIECE_SIZES[1:]:
        start = n_big * big + (rem & ~(2 * size - 1))
        lists.append(_compact((rem & size) != 0, off + start, row + start))
    counts = jnp.stack([n for _, n in lists], axis=1).reshape(-1)
    local_rows = jnp.stack([v[0] for v, _ in lists], axis=1).reshape(-1)
    global_rows = jnp.stack([v[1] for v, _ in lists], axis=1).reshape(-1)
    return counts, local_rows, global_rows


def _slab_pieces(tile, count_ref, local_ref, global_ref, fn):
    for k, size in enumerate(PIECE_SIZES):
        lst = tile * len(PIECE_SIZES) + k

        def body(j, carry, lst=lst, size=size):
            fn(local_ref[lst * PIECE_SLOTS + j], global_ref[lst * PIECE_SLOTS + j], size)
            return carry

        lax.fori_loop(0, count_ref[lst], body, 0)


def _rows(ref, row, n_rows):
    start = row * ROW_TILES
    if not isinstance(row, int):
        start = pl.multiple_of(start, ROW_TILES)
    return ref.at[pl.ds(start, n_rows * ROW_TILES)]


def _onehot_rows(q, values=None):
    lane = lax.broadcasted_iota(jnp.int32, (ROUTE_TM, TILE_ROWS), 1)
    s = jnp.zeros((ROUTE_TM, TILE_ROWS), F32)
    for k in range(TOP_K):
        v = 1.0 if values is None else values[:, k:k + 1]
        s = jnp.where(lane == q[:, k:k + 1], v, s)
    return s


def _wait_tile_rows(hbm_ref, buf_slot_ref, sem_slot):
    pltpu.make_async_copy(_rows(hbm_ref, 0, TILE_ROWS), buf_slot_ref, sem_slot).wait()


def _dispatch_kernel(count_ref, local_ref, global_ref, q_ref, h2_ref, xs_ref, buf_ref, sem):
    tile = pl.program_id(0)
    slot = tile % 2
    buf = buf_ref.at[slot]

    @pl.when(tile >= 2)
    def _():
        _wait_tile_rows(xs_ref, buf, sem.at[slot])

    sel = _onehot_rows(q_ref[...]).astype(BF16)
    xg = lax.dot_general(sel, h2_ref[...], (((0,), (0,)), ((), ())), preferred_element_type=F32)
    for c in range(ROW_TILES):
        buf[pl.ds(c, TILE_ROWS, stride=ROW_TILES), :] = xg[:, c * LANES:(c + 1) * LANES]

    def start(local_row, global_row, n_rows):
        pltpu.make_async_copy(_rows(buf, local_row, n_rows), _rows(xs_ref, global_row, n_rows),
                              sem.at[slot]).start()

    _slab_pieces(tile, count_ref, local_ref, global_ref, start)

    @pl.when(tile == N_TILES - 1)
    def _():
        _wait_tile_rows(xs_ref, buf, sem.at[slot])
        _wait_tile_rows(xs_ref, buf_ref.at[1 - slot], sem.at[1 - slot])


def _dispatch(pieces, q, h2):
    grid_spec = pltpu.PrefetchScalarGridSpec(
        num_scalar_prefetch=3,
        grid=(N_TILES,),
        in_specs=[pl.BlockSpec((ROUTE_TM, TOP_K), lambda i, *_: (i, 0)),
                  pl.BlockSpec((ROUTE_TM, D_MODEL), lambda i, *_: (i, 0))],
        out_specs=pl.BlockSpec(memory_space=pl.ANY),
        scratch_shapes=[pltpu.VMEM((2, TILE_ROWS * ROW_TILES, LANES), F32), pltpu.SemaphoreType.DMA((2,))],
    )
    return pl.pallas_call(
        _dispatch_kernel,
        grid_spec=grid_spec,
        out_shape=jax.ShapeDtypeStruct((N_ROWS * ROW_TILES, LANES), F32),
        compiler_params=_params(1),
        name="dispatch",
    )(*pieces, q, h2)


CAST_ROWS = 128
W_SLOTS = 2
EXP_ROW_STEP = 128


def _expert_kernel(blk_exp_ref, nvalid_ref, first_ref, head_ref, slot_ref, next_ref, io_blk_ref,
                   xs_ref, w1_hbm, b1_ref, w2_hbm, b2_ref, y_ref,
                   w1f_ref, w2f_ref, w1b_ref, w2b_ref, sem, *, layer):
    b = pl.program_id(0)
    e = blk_exp_ref[b]
    nvalid = nvalid_ref[b]
    slot = slot_ref[b]

    def start_weights(expert, ahead):
        s = (slot + ahead) % W_SLOTS
        pltpu.make_async_copy(w1_hbm.at[layer, expert], w1f_ref.at[s], sem.at[0, s]).start()
        pltpu.make_async_copy(w2_hbm.at[layer, expert], w2f_ref.at[s], sem.at[1, s]).start()

    @pl.when(first_ref[b] == 1)
    def _():
        @pl.when(head_ref[b] == 1)
        def _():
            start_weights(e, 0)

        pltpu.make_async_copy(w1_hbm.at[layer, e], w1f_ref.at[slot], sem.at[0, slot]).wait()
        pltpu.make_async_copy(w2_hbm.at[layer, e], w2f_ref.at[slot], sem.at[1, slot]).wait()

        @pl.when(next_ref[b] >= 0)
        def _():
            start_weights(next_ref[b], 1)

        def cast1(i, carry):
            r = pl.multiple_of(i * CAST_ROWS, CAST_ROWS)
            w1b_ref[pl.ds(r, CAST_ROWS), :] = w1f_ref[slot, pl.ds(r, CAST_ROWS), :].astype(BF16)
            return carry

        def cast2(i, carry):
            r = pl.multiple_of(i * CAST_ROWS, CAST_ROWS)
            w2b_ref[pl.ds(r, CAST_ROWS), :] = w2f_ref[slot, pl.ds(r, CAST_ROWS), :].astype(BF16)
            return carry

        lax.fori_loop(0, D_MODEL // CAST_ROWS, cast1, 0)
        lax.fori_loop(0, D_FF // CAST_ROWS, cast2, 0)

    def run_rows(n_rows):
        valid = lax.broadcasted_iota(jnp.int32, (n_rows, LANES), 0) < nvalid
        chunks = [jnp.where(valid, xs_ref[pl.ds(c, n_rows, stride=ROW_TILES), :], 0.0).astype(BF16)
                  for c in range(ROW_TILES)]
        xb = jnp.concatenate(chunks, axis=-1)
        hid = jnp.dot(xb, w1b_ref[...], preferred_element_type=F32) + b1_ref[...]
        glu = jnp.minimum(hid[:, :D_FF], SWIGLU_LIMIT)
        lin = jnp.clip(hid[:, D_FF:], -SWIGLU_LIMIT, SWIGLU_LIMIT)
        act = glu * _sigmoid(SWIGLU_ALPHA * glu) * (lin + 1.0)
        y = jnp.dot(act.astype(BF16), w2b_ref[...], preferred_element_type=F32) + b2_ref[...]
        for c in range(ROW_TILES):
            y_ref[pl.ds(c, n_rows, stride=ROW_TILES), :] = y[:, c * LANES:(c + 1) * LANES]
        if n_rows < EXP_BLOCK:
            rest = (EXP_BLOCK - n_rows) * ROW_TILES
            y_ref[pl.ds(n_rows * ROW_TILES, rest), :] = jnp.zeros((rest, LANES), F32)

    for n_rows in range(EXP_ROW_STEP, EXP_BLOCK + 1, EXP_ROW_STEP):
        @pl.when(jnp.logical_and(nvalid > n_rows - EXP_ROW_STEP, nvalid <= n_rows))
        def _(n_rows=n_rows):
            run_rows(n_rows)


def _experts(tables, xs, w1, b1, w2, b2, layer):
    def blk(b, *tbl):
        return (tbl[-1][b], 0)

    def bias(b, be, *_):
        return (layer, be[b], 0, 0)

    grid_spec = pltpu.PrefetchScalarGridSpec(
        num_scalar_prefetch=7,
        grid=(N_BLOCKS,),
        in_specs=[
            pl.BlockSpec((EXP_BLOCK * ROW_TILES, LANES), blk),
            pl.BlockSpec(memory_space=pl.ANY),
            pl.BlockSpec((None, None, 1, 2 * D_FF), bias),
            pl.BlockSpec(memory_space=pl.ANY),
            pl.BlockSpec((None, None, 1, D_MODEL), bias),
        ],
        out_specs=pl.BlockSpec((EXP_BLOCK * ROW_TILES, LANES), blk),
        scratch_shapes=[pltpu.VMEM((W_SLOTS, D_MODEL, 2 * D_FF), F32), pltpu.VMEM((W_SLOTS, D_FF, D_MODEL), F32),
                        pltpu.VMEM((D_MODEL, 2 * D_FF), BF16), pltpu.VMEM((D_FF, D_MODEL), BF16),
                        pltpu.SemaphoreType.DMA((2, W_SLOTS))],
    )
    return pl.pallas_call(
        functools.partial(_expert_kernel, layer=layer),
        grid_spec=grid_spec,
        out_shape=jax.ShapeDtypeStruct((N_ROWS * ROW_TILES, LANES), F32),
        compiler_params=_params(1),
        name="experts",
    )(*tables, xs, w1, b1.reshape(DEPTH, N_EXPERTS, 1, 2 * D_FF), w2,
      b2.reshape(DEPTH, N_EXPERTS, 1, D_MODEL))


def _split_bf16(x):
    hi = x.astype(BF16)
    return hi, (x - hi.astype(F32)).astype(BF16)


P_TILES = N_P // ROUTE_TM


def _combine_kernel(count_ref, local_ref, global_ref, q_ref, w_ref, y_ref, xm_ref, mod_ref, fg_ref,
                    *refs, final):
    buf_ref, sem = refs[-2:]
    tile = pl.program_id(0)
    slot = tile % 2
    buf = buf_ref.at[slot]

    def fetch(t, s):
        def start(local_row, global_row, n_rows):
            pltpu.make_async_copy(_rows(y_ref, global_row, n_rows), _rows(buf_ref.at[s], local_row, n_rows),
                                  sem.at[s]).start()

        _slab_pieces(t, count_ref, local_ref, global_ref, start)

    @pl.when(tile == 0)
    def _():
        fetch(tile, slot)

    @pl.when(tile + 1 < N_TILES)
    def _():
        fetch(tile + 1, 1 - slot)

    weighted = _onehot_rows(q_ref[...], w_ref[...])
    pick = jnp.where(weighted != 0.0, 1.0, 0.0).astype(BF16)
    row_w = jnp.sum(weighted, axis=0, keepdims=True)
    row_w = jnp.transpose(jnp.broadcast_to(row_w, (LANES, TILE_ROWS)))
    _wait_tile_rows(y_ref, buf, sem.at[slot])
    rows = jnp.concatenate([buf[pl.ds(c, TILE_ROWS, stride=ROW_TILES), :] * row_w for c in range(ROW_TILES)],
                           axis=-1)
    r_hi, r_lo = _split_bf16(rows)
    moe = jnp.dot(pick, r_hi, preferred_element_type=F32) + jnp.dot(pick, r_lo, preferred_element_type=F32)
    m = mod_ref[...]
    x = xm_ref[...] + m[5:6] * moe
    if not final:
        refs[0][...] = x
        return
    xn = x * lax.rsqrt(jnp.mean(x * x, axis=-1, keepdims=True) + EPS) * fg_ref[...]
    yp_ref, ys_ref = refs[:2]

    @pl.when(tile < P_TILES)
    def _():
        yp_ref[...] = xn

    @pl.when(tile >= P_TILES)
    def _():
        ys_ref[...] = xn


def _combine(pieces, q, topw, y, xm, mods_l, final_g, final):
    def tok(w):
        return pl.BlockSpec((ROUTE_TM, w), lambda i, *_: (i, 0))

    if final:
        out_specs = [pl.BlockSpec((ROUTE_TM, D_MODEL), lambda i, *_: (jnp.minimum(i, P_TILES - 1), 0)),
                     pl.BlockSpec((ROUTE_TM, D_MODEL), lambda i, *_: (jnp.maximum(i - P_TILES, 0), 0))]
        out_shape = [jax.ShapeDtypeStruct((N_P, D_MODEL), F32), jax.ShapeDtypeStruct((N_S, D_MODEL), F32)]
    else:
        out_specs = [tok(D_MODEL)]
        out_shape = [jax.ShapeDtypeStruct((N_TOK, D_MODEL), F32)]
    grid_spec = pltpu.PrefetchScalarGridSpec(
        num_scalar_prefetch=3,
        grid=(N_TILES,),
        in_specs=[tok(TOP_K), tok(TOP_K),
                  pl.BlockSpec(memory_space=pl.ANY),
                  tok(D_MODEL),
                  pl.BlockSpec((None, N_MOD, D_MODEL), lambda i, *_: (_mod_row(i * ROUTE_TM), 0, 0)),
                  pl.BlockSpec((1, D_MODEL), lambda i, *_: (0, 0))],
        out_specs=out_specs,
        scratch_shapes=[pltpu.VMEM((2, TILE_ROWS * ROW_TILES, LANES), F32), pltpu.SemaphoreType.DMA((2,))],
    )
    return pl.pallas_call(
        functools.partial(_combine_kernel, final=final),
        grid_spec=grid_spec,
        out_shape=out_shape,
        compiler_params=_params(1),
        name="combine_final" if final else "combine",
    )(*pieces, q, topw, y, xm, mods_l, final_g)


def _rope_tables():
    t = np.arange(DEC_SEQ)
    row = (t // GRID_W).astype(np.float32)
    col = (t % GRID_W).astype(np.float32)
    inv = jnp.asarray(ROPE_THETA, F32) ** (-jnp.arange(ROPE_PAIRS, dtype=F32) / ROPE_PAIRS)
    ang = jnp.concatenate([jnp.asarray(row)[:, None] * inv, jnp.asarray(col)[:, None] * inv], axis=-1)
    cos = jnp.repeat(jnp.cos(ang), 2, axis=-1)
    sin = jnp.repeat(jnp.sin(ang), 2, axis=-1)
    sign = jnp.asarray(np.tile(np.array([-1.0, 1.0], np.float32), HEAD_DIM // 2))
    return jnp.tile(cos, (1, A_HEADS)), jnp.tile(sin * sign, (1, A_HEADS))


def _routing_tables(tile_cnt):
    i32 = jnp.int32
    carry = jnp.cumsum(tile_cnt, axis=0) - tile_cnt
    counts = jnp.sum(tile_cnt, axis=0)
    padded = (counts + EXP_BLOCK - 1) // EXP_BLOCK * EXP_BLOCK
    pad_end = jnp.cumsum(padded)
    pad_start = pad_end - padded
    rowstart = (pad_start[None, :] + carry).astype(i32)
    blk_row = jnp.arange(N_BLOCKS, dtype=i32) * EXP_BLOCK
    blk_exp = jnp.sum((blk_row[:, None] >= pad_end[None, :]).astype(i32), axis=1)
    blk_exp = jnp.minimum(blk_exp, N_EXPERTS - 1)
    eid = jnp.arange(N_EXPERTS, dtype=i32)

    def pick(table, idx):
        return jnp.sum(jnp.where(idx[:, None] == eid[None, :], table[None, :], 0), axis=1).astype(i32)

    blk_start = pick(pad_start, blk_exp)
    nvalid = jnp.clip(pick(counts, blk_exp) - (blk_row - blk_start), 0, EXP_BLOCK).astype(i32)
    first = jnp.logical_and(blk_row == blk_start, nvalid > 0)
    active = counts > 0
    act_rank = jnp.cumsum(active.astype(i32)) - 1
    later = jnp.logical_and(active[None, :], eid[None, :] > eid[:, None])
    nxt = jnp.min(jnp.where(later, eid[None, :], N_EXPERTS), axis=1)
    nxt = jnp.where(nxt == N_EXPERTS, -1, nxt).astype(i32)
    blk_rank = pick(act_rank, blk_exp)
    head = jnp.logical_and(first, blk_rank == 0)
    n_used = pad_end[-1] // EXP_BLOCK
    io_blk = jnp.minimum(jnp.arange(N_BLOCKS, dtype=i32), n_used - 1).astype(i32)
    tables = (blk_exp, nvalid, first.astype(i32), head.astype(i32), (blk_rank % W_SLOTS).astype(i32),
              pick(nxt + 1, blk_exp) - 1, io_blk)
    return rowstart, tables


def kernel(x_prompt, x_sample, cache_attn_k, cache_attn_v, cache_na_k, cache_na_v, c, c_ctx, w_mod, b_mod, norm1_g, norm2_g, w_in, b_gate, q_norm_g, k_norm_g, na_rpb, conv_w, conv_b, conv_ln_g, conv_ln_b, sgu_ln_g, sgu_ln_b, sgu_w, sgu_b, w_branch, w_out, router_w, router_b, exp_w1, exp_b1, exp_w2, exp_b2, final_g):
    stream = (x_prompt.reshape(N_P, D_MODEL), x_sample.reshape(N_S, D_MODEL), 0)
    cvec =jnp.zeros((SUBLANES, D_MODEL), F32).at[0].set(c_ctx).at[1:1 + DEC_BATCH].set(c)
    mods = _modulation(cvec, w_mod, b_mod).reshape(DEPTH, SUBLANES, N_MOD, D_MODEL)
    cos_t, sin_t = _rope_tables()
    cak = cache_attn_k.reshape(DEC_BATCH, DEPTH, PAST_LEN, A_KV)
    cav = cache_attn_v.reshape(DEC_BATCH, DEPTH, PAST_LEN, A_KV)
    cbk = cache_na_k.reshape(DEC_BATCH, DEPTH, PAST_LEN, B_W)
    cbv = cache_na_v.reshape(DEC_BATCH, DEPTH, PAST_LEN, B_W)
    w_in_b = w_in.astype(BF16)
    w_br = w_branch.astype(BF16)
    w_o = w_out.astype(BF16)
    rw_hi = router_w.astype(BF16)
    rw_lo = (router_w - rw_hi.astype(F32)).astype(BF16)
    final_g2 = final_g.reshape(1, D_MODEL)

    caches = ()
    outs = None
    for l in range(DEPTH):
        mods_l = mods[l]
        g1 = norm1_g[l].reshape(1, D_MODEL)
        gq = jnp.tile(q_norm_g[l], A_HEADS).reshape(1, A_Q)
        gk = jnp.tile(k_norm_g[l], A_KV_HEADS).reshape(1, A_KV)
        aq, ak, av, bq, bk, bv, cz, dz = _in_proj(stream, mods_l, g1, w_in_b, l)
        ya, yb, *caches = _prompt_attn(aq, ak, av, bq, bk, bv, gq, gk, caches, l)
        ya = _sample_attn(aq, ak, av, cak, cav, cos_t, sin_t, gq, gk, ya, l)
        yb = _na_attn(bq, bk, bv, cbk, cbv, _na_bias(na_rpb[l]), yb, l)
        cw = conv_w[l]
        cb = conv_b[l].reshape(1, C_WIDTH)
        cg = conv_ln_g[l].reshape(1, C_WIDTH)
        cbb = conv_ln_b[l].reshape(1, C_WIDTH)
        yc = _conv_call(cz, cw, cb, cg, cbb, SEQ, 0, BATCH)
        yc = _conv_call(cz, cw, cb, cg, cbb, DEC_SEQ, N_P // DEC_SEQ, DEC_BATCH, partial_out=yc)
        bs_full = jnp.repeat(sgu_b[l].T, SGU_GW, axis=1)
        yd = _sgu(dz, sgu_ln_g[l].reshape(1, SGU_WIDTH), sgu_ln_b[l].reshape(1, SGU_WIDTH),
                  sgu_w[l].astype(BF16), bs_full)
        xm, h2, top_w, q, tile_cnt, tile_off = _merge(
            stream, ya, yb, yc, yd, mods_l, g1, w_in_b, b_gate[l].reshape(1, N_BRANCH * D_MODEL), w_br[l], w_o[l],
            norm2_g[l].reshape(1, D_MODEL), rw_hi[l], rw_lo[l], router_b[l].reshape(1, N_EXPERTS), l)
        tile_cnt = tile_cnt.reshape(N_TILES, N_EXPERTS)
        rowstart, tables = _routing_tables(tile_cnt)
        pieces = _piece_lists(tile_cnt, tile_off.reshape(N_TILES, N_EXPERTS), rowstart)
        xs = _dispatch(pieces, q, h2)
        y = _experts(tables, xs, exp_w1, exp_b1, exp_w2, exp_b2, l)
        outs = _combine(pieces, q, top_w, y, xm, mods_l, final_g2, l == DEPTH - 1)
        stream = (outs[0], outs[0], N_P)

    new_k, new_v, new_bk, new_bv = caches
    return (outs[0].reshape(BATCH, SEQ, D_MODEL), outs[1].reshape(DEC_BATCH, DEC_SEQ, D_MODEL),
            new_k.reshape(BATCH, DEPTH, SEQ, A_KV_HEADS, HEAD_DIM),
            new_v.reshape(BATCH, DEPTH, SEQ, A_KV_HEADS, HEAD_DIM),
            new_bk.reshape(BATCH, DEPTH, SEQ, B_HEADS, HEAD_DIM),
            new_bv.reshape(BATCH, DEPTH, SEQ, B_HEADS, HEAD_DIM))
```

```python
import functools

import numpy as np
import jax
import jax.numpy as jnp
from jax import lax
from jax.experimental import pallas as pl
from jax.experimental.pallas import tpu as pltpu

D_MODEL = 1024
BATCH = 32
SEQ = 256
DEPTH = 2
DEC_BATCH = 2
DEC_SEQ = 1024
PAST_LEN = 512
GRID_W = 64
HEAD_DIM = 64
A_HEADS = 4
A_KV_HEADS = 2
B_HEADS = 4
NA_ROWS = 8
NA_COLS = 16
C_WIDTH = 256
CONV_WIDTH = 31
SGU_WIDTH = 256
SGU_GROUPS = 4
SGU_CHUNK = 128
N_BRANCH = 4
BRANCH_W = 256
N_EXPERTS = 32
TOP_K = 4
D_FF = 1024
SWIGLU_ALPHA = 1.702
SWIGLU_LIMIT = 7.0
ROPE_THETA = 10000.0
ROPE_PAIRS = HEAD_DIM // 4
N_MOD = 6
EPS = 1e-6
NEG_INF = -1e30

A_Q = A_HEADS * HEAD_DIM
A_KV = A_KV_HEADS * HEAD_DIM
B_W = B_HEADS * HEAD_DIM
MIX_SIZES = (A_Q, A_KV, A_KV, B_W, B_W, B_W, 2 * C_WIDTH, 2 * SGU_WIDTH)
MIX_COLS = sum(MIX_SIZES)

N_P = BATCH * SEQ
N_S = DEC_BATCH * DEC_SEQ
N_TOK = N_P + N_S
N_ASSIGN = N_TOK * TOP_K
EXP_BLOCK = 512
N_BLOCKS = N_ASSIGN // EXP_BLOCK + N_EXPERTS
N_ROWS = N_BLOCKS * EXP_BLOCK
GRID_ROWS = DEC_SEQ // GRID_W
NA_WR = min(NA_ROWS, GRID_ROWS)
N_LOC = NA_WR * GRID_W

SUBLANES = 8
LANES = 128
ROW_TILES = D_MODEL // LANES
VMEM_CAPACITY_V7X = 64 * 1024 * 1024
VMEM_LIMIT = VMEM_CAPACITY_V7X - 8 * 1024 * 1024

F32 = jnp.float32
BF16 = jnp.bfloat16


def _params(n_axes, vmem=None):
    return pltpu.CompilerParams(
        dimension_semantics=("arbitrary",) * n_axes,
        vmem_limit_bytes=vmem if vmem is not None else VMEM_LIMIT)


def _mod_row(start):
    return jnp.where(start < N_P, 0, 1 + (start - N_P) // DEC_SEQ)


def _bdot(a, b):
    return jnp.dot(a.astype(BF16), b.astype(BF16), preferred_element_type=F32)


def _bdot_nt(a, b):
    return lax.dot_general(a.astype(BF16), b.astype(BF16), (((1,), (1,)), ((), ())),
                           preferred_element_type=F32)


def _sigmoid(x):
    return 0.5 * jnp.tanh(0.5 * x) + 0.5


MOD_TN = 1536


def _mod_kernel(c_ref, w_ref, b_ref, o_ref):
    c = c_ref[...]
    s = c * _sigmoid(c)
    o_ref[...] = _bdot(s, w_ref[...]) + b_ref[...]


def _modulation(cvec, w_mod, b_mod):
    n_col = N_MOD * D_MODEL
    return pl.pallas_call(
        _mod_kernel,
        grid=(DEPTH, n_col // MOD_TN),
        in_specs=[
            pl.BlockSpec((SUBLANES, D_MODEL), lambda l, j: (0, 0)),
            pl.BlockSpec((None, D_MODEL, MOD_TN), lambda l, j: (l, 0, j)),
            pl.BlockSpec((None, 1, MOD_TN), lambda l, j: (l, 0, j)),
        ],
        out_specs=pl.BlockSpec((None, SUBLANES, MOD_TN), lambda l, j: (l, 0, j)),
        out_shape=jax.ShapeDtypeStruct((DEPTH, SUBLANES, n_col), F32),
        compiler_params=_params(2),
        name="modulation",
    )(cvec, w_mod, b_mod.reshape(DEPTH, 1, n_col))


IN_TM = 512


def _norm_mod(x, g, shift, scale):
    y = x * lax.rsqrt(jnp.mean(x * x, axis=-1, keepdims=True) + EPS) * g
    return y * (1.0 + scale) + shift


def _stream_specs(tm, stream):
    p_tiles = N_P // tm
    s_first = stream[2] // tm
    return [pl.BlockSpec((tm, D_MODEL), lambda i: (jnp.minimum(i, p_tiles - 1), 0)),
            pl.BlockSpec((tm, D_MODEL), lambda i: (jnp.maximum(i - p_tiles, 0) + s_first, 0))]


def _stream_tile(xp_ref, xs_ref, tm):
    return jnp.where(pl.program_id(0) < N_P // tm, xp_ref[...], xs_ref[...])


def _in_kernel(xp_ref, xs_ref, mod_ref, g_ref, w_ref, *out_refs):
    m = mod_ref[...]
    h = _norm_mod(_stream_tile(xp_ref, xs_ref, IN_TM), g_ref[...], m[0:1], m[1:2])
    z = jnp.dot(h.astype(BF16), w_ref[...], preferred_element_type=F32)
    off = 0
    for o_ref, sz in zip(out_refs, MIX_SIZES):
        o_ref[...] = z[:, off:off + sz]
        off += sz


def _in_proj(stream, mods_l, g1, w_in_b, layer):
    return pl.pallas_call(
        _in_kernel,
        grid=(N_TOK // IN_TM,),
        in_specs=_stream_specs(IN_TM, stream) + [
            pl.BlockSpec((None, N_MOD, D_MODEL), lambda i: (_mod_row(i * IN_TM), 0, 0)),
            pl.BlockSpec((1, D_MODEL), lambda i: (0, 0)),
            pl.BlockSpec((None, D_MODEL, MIX_COLS), lambda i: (layer, 0, 0)),
        ],
        out_specs=[pl.BlockSpec((IN_TM, sz), lambda i: (i, 0)) for sz in MIX_SIZES],
        out_shape=[jax.ShapeDtypeStruct((N_TOK, sz), F32) for sz in MIX_SIZES],
        compiler_params=_params(1),
        name="in_proj",
    )(stream[0], stream[1], mods_l, g1, w_in_b)


def _head_rms(x, g):
    width = x.shape[-1]
    seg_r = lax.broadcasted_iota(jnp.int32, (width, width), 0) // HEAD_DIM
    seg_c = lax.broadcasted_iota(jnp.int32, (width, width), 1) // HEAD_DIM
    avg = jnp.where(seg_r == seg_c, 1.0 / HEAD_DIM, 0.0).astype(BF16)
    hi, lo = _split_bf16(x * x)
    ms = jnp.dot(hi, avg, preferred_element_type=F32) + jnp.dot(lo, avg, preferred_element_type=F32)
    return x * lax.rsqrt(ms + EPS) * g


def _softmax_pv(score_parts, value_parts):
    m = score_parts[0].max(axis=-1, keepdims=True)
    for s in score_parts[1:]:
        m = jnp.maximum(m, s.max(axis=-1, keepdims=True))
    den = None
    acc = None
    for s, v in zip(score_parts, value_parts):
        e = jnp.exp(s - m)
        d = e.sum(axis=-1, keepdims=True)
        a = _bdot(e, v)
        den = d if den is None else den + d
        acc = a if acc is None else acc + a
    return acc / den


def _head(x, h):
    return x[:, h * HEAD_DIM:(h + 1) * HEAD_DIM]


SCALE = HEAD_DIM ** -0.5


def _prompt_attn_kernel(aq_ref, ak_ref, av_ref, bq_ref, bk_ref, bv_ref, gq_ref, gk_ref, *refs):
    ya_ref, yb_ref, nk_ref, nv_ref, nbk_ref, nbv_ref = refs[-6:]
    aq = _head_rms(aq_ref[...], gq_ref[...]) * SCALE
    ak = _head_rms(ak_ref[...], gk_ref[...])
    av = av_ref[...]
    for ref, val in ((nk_ref, ak), (nv_ref, av), (nbk_ref, bk_ref[...]), (nbv_ref, bv_ref[...])):
        n_heads = val.shape[-1] // HEAD_DIM
        for h in range(n_heads):
            ref[pl.ds(h, SEQ, stride=n_heads), :] = _head(val, h)
    grp = A_HEADS // A_KV_HEADS
    outs = []
    for h in range(A_HEADS):
        s = _bdot_nt(_head(aq, h), _head(ak, h // grp))
        outs.append(_softmax_pv([s], [_head(av, h // grp)]))
    ya_ref[...] = jnp.concatenate(outs, axis=-1)
    bq = bq_ref[...] * SCALE
    bk = bk_ref[...]
    bv = bv_ref[...]
    outs = []
    for h in range(B_HEADS):
        s = _bdot_nt(_head(bq, h), _head(bk, h))
        outs.append(_softmax_pv([s], [_head(bv, h)]))
    yb_ref[...] = jnp.concatenate(outs, axis=-1)


def _prompt_attn(aq, ak, av, bq, bk, bv, gq, gk, caches, layer):
    def spec(w):
        return pl.BlockSpec((SEQ, w), lambda b: (b, 0))

    def cache_spec(n_heads):
        return pl.BlockSpec((SEQ * n_heads, HEAD_DIM), lambda b: (b * DEPTH + layer, 0))

    cache_heads = (A_KV_HEADS, A_KV_HEADS, B_HEADS, B_HEADS)
    n_in = 8
    return pl.pallas_call(
        _prompt_attn_kernel,
        grid=(BATCH,),
        in_specs=[spec(A_Q), spec(A_KV), spec(A_KV), spec(B_W), spec(B_W), spec(B_W),
                  pl.BlockSpec((1, A_Q), lambda b: (0, 0)),
                  pl.BlockSpec((1, A_KV), lambda b: (0, 0))]
        + [pl.BlockSpec(memory_space=pl.ANY) for _ in caches],
        out_specs=[spec(A_Q), spec(B_W)] + [cache_spec(n) for n in cache_heads],
        out_shape=[jax.ShapeDtypeStruct((N_TOK, A_Q), F32),
                   jax.ShapeDtypeStruct((N_TOK, B_W), F32)]
        + [jax.ShapeDtypeStruct((BATCH * DEPTH * SEQ * n, HEAD_DIM), F32) for n in cache_heads],
        input_output_aliases={n_in + j: 2 + j for j in range(len(caches))},
        compiler_params=_params(1),
        name="prompt_attn",
    )(aq, ak, av, bq, bk, bv, gq, gk, *caches)


QB = 128


def _rope(x, cos, sin_signed):
    n = x.shape[-1]
    nxt = pltpu.roll(x, n - 1, 1)
    prv = pltpu.roll(x, 1, 1)
    even = (lax.broadcasted_iota(jnp.int32, x.shape, 1) % 2) == 0
    return x * cos + jnp.where(even, nxt, prv) * sin_signed


def _sample_attn_kernel(q_ref, k_ref, v_ref, ck_ref, cv_ref, cosq_ref, sinq_ref, cosk_ref, sink_ref,
                        gq_ref, gk_ref, ya_prompt_ref, o_ref, kr_ref):
    del ya_prompt_ref

    @pl.when(pl.program_id(1) == 0)
    def _():
        kr_ref[...] = _rope(_head_rms(k_ref[...], gk_ref[...]), cosk_ref[...], sink_ref[...]).astype(BF16)

    q = _rope(_head_rms(q_ref[...], gq_ref[...]), cosq_ref[...], sinq_ref[...]) * SCALE
    k = kr_ref[...]
    v = v_ref[...]
    ck = ck_ref[...]
    cv = cv_ref[...]
    grp = A_HEADS // A_KV_HEADS
    outs = []
    for h in range(A_HEADS):
        j = h // grp
        qh = _head(q, h)
        s1 = _bdot_nt(qh, _head(k, j))
        s2 = _bdot_nt(qh, _head(ck, j))
        outs.append(_softmax_pv([s1, s2], [_head(v, j), _head(cv, j)]))
    o_ref[...] = jnp.concatenate(outs, axis=-1)


def _sample_attn(aq, ak, av, cache_k, cache_v, cos_t, sin_t, gq, gk, ya, layer):
    nqb = DEC_SEQ // QB
    q0 = N_P // QB
    k0 = N_P // DEC_SEQ
    return pl.pallas_call(
        _sample_attn_kernel,
        grid=(DEC_BATCH, nqb),
        in_specs=[
            pl.BlockSpec((QB, A_Q), lambda b, i: (q0 + b * nqb + i, 0)),
            pl.BlockSpec((DEC_SEQ, A_KV), lambda b, i: (k0 + b, 0)),
            pl.BlockSpec((DEC_SEQ, A_KV), lambda b, i: (k0 + b, 0)),
            pl.BlockSpec((None, None, PAST_LEN, A_KV), lambda b, i: (b, layer, 0, 0)),
            pl.BlockSpec((None, None, PAST_LEN, A_KV), lambda b, i: (b, layer, 0, 0)),
            pl.BlockSpec((QB, A_Q), lambda b, i: (i, 0)),
            pl.BlockSpec((QB, A_Q), lambda b, i: (i, 0)),
            pl.BlockSpec((DEC_SEQ, A_KV), lambda b, i: (0, 0)),
            pl.BlockSpec((DEC_SEQ, A_KV), lambda b, i: (0, 0)),
            pl.BlockSpec((1, A_Q), lambda b, i: (0, 0)),
            pl.BlockSpec((1, A_KV), lambda b, i: (0, 0)),
            pl.BlockSpec(memory_space=pl.ANY),
        ],
        out_specs=pl.BlockSpec((QB, A_Q), lambda b, i: (q0 + b * nqb + i, 0)),
        out_shape=jax.ShapeDtypeStruct((N_TOK, A_Q), F32),
        input_output_aliases={11: 0},
        scratch_shapes=[pltpu.VMEM((DEC_SEQ, A_KV), BF16)],
        compiler_params=_params(2),
        name="sample_attn",
    )(aq, ak, av, cache_k, cache_v, cos_t, sin_t, cos_t, sin_t, gq, gk, ya)


N_ROW_OFF = 2 * NA_ROWS - 1
N_COL_OFF = 2 * NA_COLS - 1
NA_PAIRS = N_ROW_OFF - 1
assert NA_WR == NA_ROWS and NA_WR % 2 == 0 and 2 * GRID_W == LANES


def _na_bias_kernel(rpb_ref, o_ref):
    h = pl.program_id(0)
    qc = lax.broadcasted_iota(jnp.int32, (GRID_W, LANES), 0)
    lane = lax.broadcasted_iota(jnp.int32, (GRID_W, LANES), 1)
    right = lane >= GRID_W
    kc = jnp.where(right, lane - GRID_W, lane)
    c_start = jnp.clip(qc - NA_COLS // 2, 0, GRID_W - NA_COLS)
    col_in = jnp.logical_and(kc >= c_start, kc < c_start + NA_COLS)
    col_off = jnp.clip(kc - qc + NA_COLS - 1, 0, N_COL_OFF - 1)
    for p in range(NA_PAIRS):
        acc = jnp.zeros((GRID_W, LANES), F32)
        for o in range(N_COL_OFF):
            left_v = rpb_ref[(h * N_ROW_OFF + p) * N_COL_OFF + o]
            right_v = rpb_ref[(h * N_ROW_OFF + p + 1) * N_COL_OFF + o]
            acc = jnp.where(col_off == o, jnp.where(right, right_v, left_v), acc)
        o_ref[p] = jnp.where(col_in, acc, NEG_INF)


def _na_bias(rpb):
    return pl.pallas_call(
        _na_bias_kernel,
        grid_spec=pltpu.PrefetchScalarGridSpec(
            num_scalar_prefetch=1,
            grid=(B_HEADS,),
            in_specs=[],
            out_specs=pl.BlockSpec((None, NA_PAIRS, GRID_W, LANES), lambda h, *_: (h, 0, 0, 0)),
        ),
        out_shape=jax.ShapeDtypeStruct((B_HEADS, NA_PAIRS, GRID_W, LANES), F32),
        compiler_params=_params(1),
        name="na_bias",
    )(rpb.reshape(-1))


NA_STEP_ROWS = 2


def _na_kernel(q_ref, k_ref, v_ref, ck_ref, cv_ref, bias_ref, yb_prompt_ref, o_ref):
    del yb_prompt_ref
    q = q_ref[...] * SCALE
    ck = ck_ref[...]
    cv = cv_ref[...]
    ctx_scores = [_bdot_nt(_head(q, h), _head(ck, h)) for h in range(B_HEADS)]
    for i in range(NA_STEP_ROWS):
        r = pl.program_id(1) * NA_STEP_ROWS + i
        r_start = jnp.clip(r - NA_WR // 2, 0, GRID_ROWS - NA_WR)
        base = pl.multiple_of(r_start * GRID_W, GRID_W)
        row_off0 = r_start - r + NA_ROWS - 1
        rows = slice(i * GRID_W, (i + 1) * GRID_W)
        kb = k_ref[pl.ds(base, N_LOC), :]
        vb = v_ref[pl.ds(base, N_LOC), :]
        outs = []
        for h in range(B_HEADS):
            bias = jnp.concatenate([bias_ref[h, row_off0 + 2 * j] for j in range(NA_WR // 2)], axis=-1)
            s1 = _bdot_nt(_head(q[rows], h), _head(kb, h)) + bias
            outs.append(_softmax_pv([s1, ctx_scores[h][rows]], [_head(vb, h), _head(cv, h)]))
        o_ref[rows, :] = jnp.concatenate(outs, axis=-1)


def _na_attn(bq, bk, bv, cache_k, cache_v, bias, yb, layer):
    step_tok = NA_STEP_ROWS * GRID_W
    steps = GRID_ROWS // NA_STEP_ROWS
    q0 = N_P // step_tok
    k0 = N_P // DEC_SEQ
    return pl.pallas_call(
        _na_kernel,
        grid=(DEC_BATCH, steps),
        in_specs=[
            pl.BlockSpec((step_tok, B_W), lambda b, r: (q0 + b * steps + r, 0)),
            pl.BlockSpec((DEC_SEQ, B_W), lambda b, r: (k0 + b, 0)),
            pl.BlockSpec((DEC_SEQ, B_W), lambda b, r: (k0 + b, 0)),
            pl.BlockSpec((None, None, PAST_LEN, B_W), lambda b, r: (b, layer, 0, 0)),
            pl.BlockSpec((None, None, PAST_LEN, B_W), lambda b, r: (b, layer, 0, 0)),
            pl.BlockSpec((B_HEADS, NA_PAIRS, GRID_W, LANES), lambda b, r: (0, 0, 0, 0)),
            pl.BlockSpec(memory_space=pl.ANY),
        ],
        out_specs=pl.BlockSpec((step_tok, B_W), lambda b, r: (q0 + b * steps + r, 0)),
        out_shape=jax.ShapeDtypeStruct((N_TOK, B_W), F32),
        input_output_aliases={6: 0},
        compiler_params=_params(2),
        name="na_attn",
    )(bq, bk, bv, cache_k, cache_v, bias, yb)


CONV_PAD = 16
CONV_CHUNK = 128


def _layer_norm(x, g, b):
    mu = jnp.mean(x, axis=-1, keepdims=True)
    xc = x - mu
    var = jnp.mean(xc * xc, axis=-1, keepdims=True)
    return xc * lax.rsqrt(var + EPS) * g + b


def _conv_kernel(z_ref, w_ref, cb_ref, g_ref, b_ref, *refs, s_len):
    o_ref, pad_ref = refs[-2:]
    z = z_ref[...]
    u = z[:, :C_WIDTH] * _sigmoid(z[:, C_WIDTH:])
    pad_ref[pl.ds(0, CONV_PAD), :] = jnp.zeros((CONV_PAD, C_WIDTH), F32)
    pad_ref[pl.ds(CONV_PAD + s_len, CONV_PAD), :] = jnp.zeros((CONV_PAD, C_WIDTH), F32)
    pad_ref[pl.ds(CONV_PAD, s_len), :] = u
    w = w_ref[...]
    shift = CONV_PAD - CONV_WIDTH // 2

    def chunk(c, carry):
        base = pl.multiple_of(c * CONV_CHUNK, CONV_CHUNK)
        acc = jnp.zeros((CONV_CHUNK, C_WIDTH), F32)
        for r in range(SUBLANES):
            part = None
            for k in range(CONV_WIDTH):
                if (k + shift) % SUBLANES != r:
                    continue
                rows = pad_ref[pl.ds(base + (k + shift - r), CONV_CHUNK + SUBLANES), :]
                term = rows * w[k:k + 1]
                part = term if part is None else part + term
            if part is not None:
                acc = acc + part[r:r + CONV_CHUNK]
        y = _layer_norm(acc + cb_ref[...], g_ref[...], b_ref[...])
        o_ref[pl.ds(base, CONV_CHUNK), :] = y * _sigmoid(y)
        return carry

    lax.fori_loop(0, s_len // CONV_CHUNK, chunk, 0)


def _conv_call(cz, w, cb, g, b, s_len, first_blk, n_seq, partial_out=None):
    vec = pl.BlockSpec((1, C_WIDTH), lambda i: (0, 0))
    extra = [] if partial_out is None else [partial_out]
    return pl.pallas_call(
        functools.partial(_conv_kernel, s_len=s_len),
        grid=(n_seq,),
        in_specs=[pl.BlockSpec((s_len, 2 * C_WIDTH), lambda i: (first_blk + i, 0)),
                  pl.BlockSpec((CONV_WIDTH, C_WIDTH), lambda i: (0, 0)), vec, vec, vec]
        + [pl.BlockSpec(memory_space=pl.ANY) for _ in extra],
        out_specs=pl.BlockSpec((s_len, C_WIDTH), lambda i: (first_blk + i, 0)),
        out_shape=jax.ShapeDtypeStruct((N_TOK, C_WIDTH), F32),
        input_output_aliases={5: 0} if extra else {},
        scratch_shapes=[pltpu.VMEM((s_len + 2 * CONV_PAD, C_WIDTH), F32)],
        compiler_params=_params(1),
        name="conformer_conv_%d" % s_len,
    )(cz, w, cb, g, b, *extra)


SGU_TM = 512
SGU_GW = SGU_WIDTH // SGU_GROUPS


def _sgu_kernel(z_ref, g_ref, b_ref, ws_ref, bs_ref, o_ref):
    z = z_ref[...]
    z = 0.5 * z * (1.0 + lax.erf(z * (2.0 ** -0.5)))
    u = z[:, :SGU_WIDTH]
    v = _layer_norm(z[:, SGU_WIDTH:], g_ref[...], b_ref[...])
    for c in range(SGU_TM // SGU_CHUNK):
        vc = v[c * SGU_CHUNK:(c + 1) * SGU_CHUNK]
        parts = [_bdot(ws_ref[g], vc[:, g * SGU_GW:(g + 1) * SGU_GW]) for g in range(SGU_GROUPS)]
        mixed = jnp.concatenate(parts, axis=-1) + bs_ref[...]
        o_ref[pl.ds(c * SGU_CHUNK, SGU_CHUNK), :] = u[c * SGU_CHUNK:(c + 1) * SGU_CHUNK] * mixed


def _sgu(dz, g, b, ws, bs_full):
    vec = pl.BlockSpec((1, SGU_WIDTH), lambda i: (0, 0))
    return pl.pallas_call(
        _sgu_kernel,
        grid=(N_TOK // SGU_TM,),
        in_specs=[pl.BlockSpec((SGU_TM, 2 * SGU_WIDTH), lambda i: (i, 0)), vec, vec,
                  pl.BlockSpec((SGU_GROUPS, SGU_CHUNK, SGU_CHUNK), lambda i: (0, 0, 0)),
                  pl.BlockSpec((SGU_CHUNK, SGU_WIDTH), lambda i: (0, 0))],
        out_specs=pl.BlockSpec((SGU_TM, SGU_WIDTH), lambda i: (i, 0)),
        out_shape=jax.ShapeDtypeStruct((N_TOK, SGU_WIDTH), F32),
        compiler_params=_params(1),
        name="chunk_sgu",
    )(dz, g, b, ws, bs_full)


MERGE_TM = 512


def _merge_kernel(xp_ref, xs_ref, ya_ref, yb_ref, yc_ref, yd_ref, mod_ref, g1_ref, w_in_hbm, bg_ref, wb_ref, wo_ref,
                  g2_ref, rwh_ref, rwl_ref, rb_ref, xm_ref, h2_ref, w_ref, q_ref, cnt_ref, off_ref,
                  wg_ref, sem, *, layer):
    @pl.when(pl.program_id(0) == 0)
    def _():
        cp = pltpu.make_async_copy(w_in_hbm.at[layer, :, pl.ds(MIX_COLS, N_BRANCH * D_MODEL)], wg_ref, sem)
        cp.start()
        cp.wait()

    m = mod_ref[...]
    x = _stream_tile(xp_ref, xs_ref, MERGE_TM)
    h = _norm_mod(x, g1_ref[...], m[0:1], m[1:2]).astype(BF16)
    merged = None
    for i, y_ref in enumerate((ya_ref, yb_ref, yc_ref, yd_ref)):
        logit = jnp.dot(h, wg_ref[:, i * D_MODEL:(i + 1) * D_MODEL], preferred_element_type=F32)
        gate = _sigmoid(logit + bg_ref[:, i * D_MODEL:(i + 1) * D_MODEL])
        term = gate * jnp.dot(y_ref[...].astype(BF16), wb_ref[i], preferred_element_type=F32)
        merged = term if merged is None else merged + term
    out = jnp.dot(merged.astype(BF16), wo_ref[...], preferred_element_type=F32)
    xm = x + m[2:3] * out
    xm_ref[...] = xm
    h2 = _norm_mod(xm, g2_ref[...], m[3:4], m[4:5])
    h2_hi = h2.astype(BF16)
    h2_lo = (h2 - h2_hi.astype(F32)).astype(BF16)
    lg = jnp.dot(h2_hi, rwh_ref[...], preferred_element_type=F32)
    lg = lg + jnp.dot(h2_hi, rwl_ref[...], preferred_element_type=F32)
    lg = lg + jnp.dot(h2_lo, rwh_ref[...], preferred_element_type=F32)
    lg = lg + rb_ref[...]
    h2_ref[...] = h2_hi
    for j in range(MERGE_TM // ROUTE_TM):
        rows = slice(j * ROUTE_TM, (j + 1) * ROUTE_TM)
        w_out, q_out, cnt, off = _route_tile(lg[rows])
        w_ref[rows, :] = w_out
        q_ref[rows, :] = q_out
        cnt_ref[j] = cnt
        off_ref[j] = off


def _merge(stream, ya, yb, yc, yd, mods_l, g1, w_in_b, bg, wb, wo, g2, rwh, rwl, rb, layer):
    def tok(w):
        return pl.BlockSpec((MERGE_TM, w), lambda i: (i, 0))

    def full(*shape):
        return pl.BlockSpec(shape, lambda i: (0,) * len(shape))

    tile_rows = pl.BlockSpec((MERGE_TM // ROUTE_TM, 1, N_EXPERTS), lambda i: (i, 0, 0))
    return pl.pallas_call(
        functools.partial(_merge_kernel, layer=layer),
        grid=(N_TOK // MERGE_TM,),
        in_specs=_stream_specs(MERGE_TM, stream) + [
                  tok(BRANCH_W), tok(BRANCH_W), tok(BRANCH_W), tok(BRANCH_W),
                  pl.BlockSpec((None, N_MOD, D_MODEL), lambda i: (_mod_row(i * MERGE_TM), 0, 0)),
                  full(1, D_MODEL), pl.BlockSpec(memory_space=pl.ANY), full(1, N_BRANCH * D_MODEL),
                  full(N_BRANCH, BRANCH_W, D_MODEL), full(D_MODEL, D_MODEL), full(1, D_MODEL),
                  full(D_MODEL, N_EXPERTS), full(D_MODEL, N_EXPERTS), full(1, N_EXPERTS)],
        out_specs=[tok(D_MODEL), tok(D_MODEL), tok(TOP_K), tok(TOP_K), tile_rows, tile_rows],
        out_shape=[jax.ShapeDtypeStruct((N_TOK, D_MODEL), F32),
                   jax.ShapeDtypeStruct((N_TOK, D_MODEL), BF16),
                   jax.ShapeDtypeStruct((N_TOK, TOP_K), F32),
                   jax.ShapeDtypeStruct((N_TOK, TOP_K), jnp.int32),
                   jax.ShapeDtypeStruct((N_TILES, 1, N_EXPERTS), jnp.int32),
                   jax.ShapeDtypeStruct((N_TILES, 1, N_EXPERTS), jnp.int32)],
        scratch_shapes=[pltpu.VMEM((D_MODEL, N_BRANCH * D_MODEL), BF16), pltpu.SemaphoreType.DMA(())],
        compiler_params=_params(1),
        name="merge",
    )(stream[0], stream[1], ya, yb, yc, yd, mods_l, g1, w_in_b, bg, wb, wo, g2, rwh, rwl, rb)


ROUTE_TM = 256
TILE_ROWS = ROUTE_TM * TOP_K
N_TILES = N_TOK // ROUTE_TM


def _route_tile(lg):
    lane = lax.broadcasted_iota(jnp.int32, lg.shape, 1)
    sels, vals = [], []
    for _ in range(TOP_K):
        mx = lg.max(axis=-1, keepdims=True)
        idx = jnp.where(lg == mx, lane, N_EXPERTS).min(axis=-1, keepdims=True)
        sel = lane == idx
        sels.append(sel)
        vals.append(mx)
        lg = jnp.where(sel, -jnp.inf, lg)
    exps = [jnp.exp(v - vals[0]) for v in vals]
    den = exps[0] + exps[1] + exps[2] + exps[3]
    onehot = jnp.zeros(lg.shape, F32)
    for sel in sels:
        onehot = onehot + sel.astype(F32)
    row = lax.broadcasted_iota(jnp.int32, (ROUTE_TM, ROUTE_TM), 0)
    col = lax.broadcasted_iota(jnp.int32, (ROUTE_TM, ROUTE_TM), 1)
    tri = jnp.where(col < row, 1.0, 0.0).astype(BF16)
    rank = jnp.dot(tri, onehot.astype(BF16), preferred_element_type=F32)
    cnt = jnp.sum(onehot, axis=0, keepdims=True)
    erow = lax.broadcasted_iota(jnp.int32, (N_EXPERTS, N_EXPERTS), 0)
    ecol = lax.broadcasted_iota(jnp.int32, (N_EXPERTS, N_EXPERTS), 1)
    upper = jnp.where(erow < ecol, 1.0, 0.0).astype(BF16)
    off = jnp.dot(jnp.broadcast_to(cnt, (SUBLANES, N_EXPERTS)).astype(BF16), upper,
                  preferred_element_type=F32)[0:1]
    slot = rank + off
    k_lane = lax.broadcasted_iota(jnp.int32, (ROUTE_TM, TOP_K), 1)
    w_out = jnp.zeros((ROUTE_TM, TOP_K), F32)
    q_out = jnp.zeros((ROUTE_TM, TOP_K), F32)
    for k in range(TOP_K):
        w_out = jnp.where(k_lane == k, exps[k] / den, w_out)
        qk = jnp.sum(jnp.where(sels[k], slot, 0.0), axis=-1, keepdims=True)
        q_out = jnp.where(k_lane == k, qk, q_out)
    return w_out, q_out.astype(jnp.int32), cnt.astype(jnp.int32), off.astype(jnp.int32)


PIECE_SIZES = (32, 16, 8, 4, 2, 1)
PIECE_SLOTS = TILE_ROWS // PIECE_SIZES[0]
assert PIECE_SLOTS >= N_EXPERTS


def _compact(valid, *values):
    pos = jnp.cumsum(valid.astype(jnp.int32), axis=1) - 1
    slot = jnp.arange(PIECE_SLOTS, dtype=jnp.int32)
    hit = jnp.logical_and(valid[:, :, None], pos[:, :, None] == slot[None, None, :])
    packed = [jnp.sum(jnp.where(hit, v[:, :, None], 0), axis=1).astype(jnp.int32) for v in values]
    return packed, jnp.sum(valid.astype(jnp.int32), axis=1)


def _piece_lists(cnt, off, row):
    big = PIECE_SIZES[0]
    n_big = cnt // big
    p = jnp.arange(ROUTE_TM // big, dtype=jnp.int32)
    valid = (p[None, None, :] < n_big[:, :, None]).reshape(N_TILES, -1)
    src = (off[:, :, None] + big * p).reshape(N_TILES, -1)
    dst = (row[:, :, None] + big * p).reshape(N_TILES, -1)
    lists = [_compact(valid, src, dst)]
    rem = cnt - n_big * big
    for size in PIECE_SIZES[1:]:
        start = n_big * big + (rem & ~(2 * size - 1))
        lists.append(_compact((rem & size) != 0, off + start, row + start))
    counts = jnp.stack([n for _, n in lists], axis=1).reshape(-1)
    local_rows = jnp.stack([v[0] for v, _ in lists], axis=1).reshape(-1)
    global_rows = jnp.stack([v[1] for v, _ in lists], axis=1).reshape(-1)
    return counts, local_rows, global_rows


def _slab_pieces(tile, count_ref, local_ref, global_ref, fn):
    for k, size in enumerate(PIECE_SIZES):
        lst = tile * len(PIECE_SIZES) + k

        def body(j, carry, lst=lst, size=size):
            fn(local_ref[lst * PIECE_SLOTS + j], global_ref[lst * PIECE_SLOTS + j], size)
            return carry

        lax.fori_loop(0, count_ref[lst], body, 0)


def _rows(ref, row, n_rows):
    start = row * ROW_TILES
    if not isinstance(row, int):
        start = pl.multiple_of(start, ROW_TILES)
    return ref.at[pl.ds(start, n_rows * ROW_TILES)]


def _onehot_rows(q, values=None):
    lane = lax.broadcasted_iota(jnp.int32, (ROUTE_TM, TILE_ROWS), 1)
    s = jnp.zeros((ROUTE_TM, TILE_ROWS), F32)
    for k in range(TOP_K):
        v = 1.0 if values is None else values[:, k:k + 1]
        s = jnp.where(lane == q[:, k:k + 1], v, s)
    return s


def _wait_tile_rows(hbm_ref, buf_slot_ref, sem_slot):
    pltpu.make_async_copy(_rows(hbm_ref, 0, TILE_ROWS), buf_slot_ref, sem_slot).wait()


def _dispatch_kernel(count_ref, local_ref, global_ref, q_ref, h2_ref, xs_ref, buf_ref, sem):
    tile = pl.program_id(0)
    slot = tile % 2
    buf = buf_ref.at[slot]

    @pl.when(tile >= 2)
    def _():
        _wait_tile_rows(xs_ref, buf, sem.at[slot])

    sel = _onehot_rows(q_ref[...]).astype(BF16)
    xg = lax.dot_general(sel, h2_ref[...], (((0,), (0,)), ((), ())), preferred_element_type=F32)
    for c in range(ROW_TILES):
        buf[pl.ds(c, TILE_ROWS, stride=ROW_TILES), :] = xg[:, c * LANES:(c + 1) * LANES]

    def start(local_row, global_row, n_rows):
        pltpu.make_async_copy(_rows(buf, local_row, n_rows), _rows(xs_ref, global_row, n_rows),
                              sem.at[slot]).start()

    _slab_pieces(tile, count_ref, local_ref, global_ref, start)

    @pl.when(tile == N_TILES - 1)
    def _():
        _wait_tile_rows(xs_ref, buf, sem.at[slot])
        _wait_tile_rows(xs_ref, buf_ref.at[1 - slot], sem.at[1 - slot])


def _dispatch(pieces, q, h2):
    grid_spec = pltpu.PrefetchScalarGridSpec(
        num_scalar_prefetch=3,
        grid=(N_TILES,),
        in_specs=[pl.BlockSpec((ROUTE_TM, TOP_K), lambda i, *_: (i, 0)),
                  pl.BlockSpec((ROUTE_TM, D_MODEL), lambda i, *_: (i, 0))],
        out_specs=pl.BlockSpec(memory_space=pl.ANY),
        scratch_shapes=[pltpu.VMEM((2, TILE_ROWS * ROW_TILES, LANES), F32), pltpu.SemaphoreType.DMA((2,))],
    )
    return pl.pallas_call(
        _dispatch_kernel,
        grid_spec=grid_spec,
        out_shape=jax.ShapeDtypeStruct((N_ROWS * ROW_TILES, LANES), F32),
        compiler_params=_params(1),
        name="dispatch",
    )(*pieces, q, h2)


CAST_ROWS = 128
W_SLOTS = 2
EXP_ROW_STEP = 128


def _expert_kernel(blk_exp_ref, nvalid_ref, first_ref, head_ref, slot_ref, next_ref, io_blk_ref,
                   xs_ref, w1_hbm, b1_ref, w2_hbm, b2_ref, y_ref,
                   w1f_ref, w2f_ref, w1b_ref, w2b_ref, sem, *, layer):
    b = pl.program_id(0)
    e = blk_exp_ref[b]
    nvalid = nvalid_ref[b]
    slot = slot_ref[b]

    def start_weights(expert, ahead):
        s = (slot + ahead) % W_SLOTS
        pltpu.make_async_copy(w1_hbm.at[layer, expert], w1f_ref.at[s], sem.at[0, s]).start()
        pltpu.make_async_copy(w2_hbm.at[layer, expert], w2f_ref.at[s], sem.at[1, s]).start()

    @pl.when(first_ref[b] == 1)
    def _():
        @pl.when(head_ref[b] == 1)
        def _():
            start_weights(e, 0)

        pltpu.make_async_copy(w1_hbm.at[layer, e], w1f_ref.at[slot], sem.at[0, slot]).wait()
        pltpu.make_async_copy(w2_hbm.at[layer, e], w2f_ref.at[slot], sem.at[1, slot]).wait()

        @pl.when(next_ref[b] >= 0)
        def _():
            start_weights(next_ref[b], 1)

        def cast1(i, carry):
            r = pl.multiple_of(i * CAST_ROWS, CAST_ROWS)
            w1b_ref[pl.ds(r, CAST_ROWS), :] = w1f_ref[slot, pl.ds(r, CAST_ROWS), :].astype(BF16)
            return carry

        def cast2(i, carry):
            r = pl.multiple_of(i * CAST_ROWS, CAST_ROWS)
            w2b_ref[pl.ds(r, CAST_ROWS), :] = w2f_ref[slot, pl.ds(r, CAST_ROWS), :].astype(BF16)
            return carry

        lax.fori_loop(0, D_MODEL // CAST_ROWS, cast1, 0)
        lax.fori_loop(0, D_FF // CAST_ROWS, cast2, 0)

    def run_rows(n_rows):
        valid = lax.broadcasted_iota(jnp.int32, (n_rows, LANES), 0) < nvalid
        chunks = [jnp.where(valid, xs_ref[pl.ds(c, n_rows, stride=ROW_TILES), :], 0.0).astype(BF16)
                  for c in range(ROW_TILES)]
        xb = jnp.concatenate(chunks, axis=-1)
        hid = jnp.dot(xb, w1b_ref[...], preferred_element_type=F32) + b1_ref[...]
        glu = jnp.minimum(hid[:, :D_FF], SWIGLU_LIMIT)
        lin = jnp.clip(hid[:, D_FF:], -SWIGLU_LIMIT, SWIGLU_LIMIT)
        act = glu * _sigmoid(SWIGLU_ALPHA * glu) * (lin + 1.0)
        y = jnp.dot(act.astype(BF16), w2b_ref[...], preferred_element_type=F32) + b2_ref[...]
        for c in range(ROW_TILES):
            y_ref[pl.ds(c, n_rows, stride=ROW_TILES), :] = y[:, c * LANES:(c + 1) * LANES]
        if n_rows < EXP_BLOCK:
            rest = (EXP_BLOCK - n_rows) * ROW_TILES
            y_ref[pl.ds(n_rows * ROW_TILES, rest), :] = jnp.zeros((rest, LANES), F32)

    for n_rows in range(EXP_ROW_STEP, EXP_BLOCK + 1, EXP_ROW_STEP):
        @pl.when(jnp.logical_and(nvalid > n_rows - EXP_ROW_STEP, nvalid <= n_rows))
        def _(n_rows=n_rows):
            run_rows(n_rows)


def _experts(tables, xs, w1, b1, w2, b2, layer):
    def blk(b, *tbl):
        return (tbl[-1][b], 0)

    def bias(b, be, *_):
        return (layer, be[b], 0, 0)

    grid_spec = pltpu.PrefetchScalarGridSpec(
        num_scalar_prefetch=7,
        grid=(N_BLOCKS,),
        in_specs=[
            pl.BlockSpec((EXP_BLOCK * ROW_TILES, LANES), blk),
            pl.BlockSpec(memory_space=pl.ANY),
            pl.BlockSpec((None, None, 1, 2 * D_FF), bias),
            pl.BlockSpec(memory_space=pl.ANY),
            pl.BlockSpec((None, None, 1, D_MODEL), bias),
        ],
        out_specs=pl.BlockSpec((EXP_BLOCK * ROW_TILES, LANES), blk),
        scratch_shapes=[pltpu.VMEM((W_SLOTS, D_MODEL, 2 * D_FF), F32), pltpu.VMEM((W_SLOTS, D_FF, D_MODEL), F32),
                        pltpu.VMEM((D_MODEL, 2 * D_FF), BF16), pltpu.VMEM((D_FF, D_MODEL), BF16),
                        pltpu.SemaphoreType.DMA((2, W_SLOTS))],
    )
    return pl.pallas_call(
        functools.partial(_expert_kernel, layer=layer),
        grid_spec=grid_spec,
        out_shape=jax.ShapeDtypeStruct((N_ROWS * ROW_TILES, LANES), F32),
        compiler_params=_params(1),
        name="experts",
    )(*tables, xs, w1, b1.reshape(DEPTH, N_EXPERTS, 1, 2 * D_FF), w2,
      b2.reshape(DEPTH, N_EXPERTS, 1, D_MODEL))


def _split_bf16(x):
    hi = x.astype(BF16)
    return hi, (x - hi.astype(F32)).astype(BF16)


P_TILES = N_P // ROUTE_TM


def _combine_kernel(count_ref, local_ref, global_ref, q_ref, w_ref, y_ref, xm_ref, mod_ref, fg_ref,
                    *refs, final):
    buf_ref, sem = refs[-2:]
    tile = pl.program_id(0)
    slot = tile % 2
    buf = buf_ref.at[slot]

    def fetch(t, s):
        def start(local_row, global_row, n_rows):
            pltpu.make_async_copy(_rows(y_ref, global_row, n_rows), _rows(buf_ref.at[s], local_row, n_rows),
                                  sem.at[s]).start()

        _slab_pieces(t, count_ref, local_ref, global_ref, start)

    @pl.when(tile == 0)
    def _():
        fetch(tile, slot)

    @pl.when(tile + 1 < N_TILES)
    def _():
        fetch(tile + 1, 1 - slot)

    weighted = _onehot_rows(q_ref[...], w_ref[...])
    pick = jnp.where(weighted != 0.0, 1.0, 0.0).astype(BF16)
    row_w = jnp.sum(weighted, axis=0, keepdims=True)
    row_w = jnp.transpose(jnp.broadcast_to(row_w, (LANES, TILE_ROWS)))
    _wait_tile_rows(y_ref, buf, sem.at[slot])
    rows = jnp.concatenate([buf[pl.ds(c, TILE_ROWS, stride=ROW_TILES), :] * row_w for c in range(ROW_TILES)],
                           axis=-1)
    r_hi, r_lo = _split_bf16(rows)
    moe = jnp.dot(pick, r_hi, preferred_element_type=F32) + jnp.dot(pick, r_lo, preferred_element_type=F32)
    m = mod_ref[...]
    x = xm_ref[...] + m[5:6] * moe
    if not final:
        refs[0][...] = x
        return
    xn = x * lax.rsqrt(jnp.mean(x * x, axis=-1, keepdims=True) + EPS) * fg_ref[...]
    yp_ref, ys_ref = refs[:2]

    @pl.when(tile < P_TILES)
    def _():
        yp_ref[...] = xn

    @pl.when(tile >= P_TILES)
    def _():
        ys_ref[...] = xn


def _combine(pieces, q, topw, y, xm, mods_l, final_g, final):
    def tok(w):
        return pl.BlockSpec((ROUTE_TM, w), lambda i, *_: (i, 0))

    if final:
        out_specs = [pl.BlockSpec((ROUTE_TM, D_MODEL), lambda i, *_: (jnp.minimum(i, P_TILES - 1), 0)),
                     pl.BlockSpec((ROUTE_TM, D_MODEL), lambda i, *_: (jnp.maximum(i - P_TILES, 0), 0))]
        out_shape = [jax.ShapeDtypeStruct((N_P, D_MODEL), F32), jax.ShapeDtypeStruct((N_S, D_MODEL), F32)]
    else:
        out_specs = [tok(D_MODEL)]
        out_shape = [jax.ShapeDtypeStruct((N_TOK, D_MODEL), F32)]
    grid_spec = pltpu.PrefetchScalarGridSpec(
        num_scalar_prefetch=3,
        grid=(N_TILES,),
        in_specs=[tok(TOP_K), tok(TOP_K),
                  pl.BlockSpec(memory_space=pl.ANY),
                  tok(D_MODEL),
                  pl.BlockSpec((None, N_MOD, D_MODEL), lambda i, *_: (_mod_row(i * ROUTE_TM), 0, 0)),
                  pl.BlockSpec((1, D_MODEL), lambda i, *_: (0, 0))],
        out_specs=out_specs,
        scratch_shapes=[pltpu.VMEM((2, TILE_ROWS * ROW_TILES, LANES), F32), pltpu.SemaphoreType.DMA((2,))],
    )
    return pl.pallas_call(
        functools.partial(_combine_kernel, final=final),
        grid_spec=grid_spec,
        out_shape=out_shape,
        compiler_params=_params(1),
        name="combine_final" if final else "combine",
    )(*pieces, q, topw, y, xm, mods_l, final_g)


def _rope_tables():
    t = np.arange(DEC_SEQ)
    row = (t // GRID_W).astype(np.float32)
    col = (t % GRID_W).astype(np.float32)
    inv = jnp.asarray(ROPE_THETA, F32) ** (-jnp.arange(ROPE_PAIRS, dtype=F32) / ROPE_PAIRS)
    ang = jnp.concatenate([jnp.asarray(row)[:, None] * inv, jnp.asarray(col)[:, None] * inv], axis=-1)
    cos = jnp.repeat(jnp.cos(ang), 2, axis=-1)
    sin = jnp.repeat(jnp.sin(ang), 2, axis=-1)
    sign = jnp.asarray(np.tile(np.array([-1.0, 1.0], np.float32), HEAD_DIM // 2))
    return jnp.tile(cos, (1, A_HEADS)), jnp.tile(sin * sign, (1, A_HEADS))


def _routing_tables(tile_cnt):
    i32 = jnp.int32
    carry = jnp.cumsum(tile_cnt, axis=0) - tile_cnt
    counts = jnp.sum(tile_cnt, axis=0)
    padded = (counts + EXP_BLOCK - 1) // EXP_BLOCK * EXP_BLOCK
    pad_end = jnp.cumsum(padded)
    pad_start = pad_end - padded
    rowstart = (pad_start[None, :] + carry).astype(i32)
    blk_row = jnp.arange(N_BLOCKS, dtype=i32) * EXP_BLOCK
    blk_exp = jnp.sum((blk_row[:, None] >= pad_end[None, :]).astype(i32), axis=1)
    blk_exp = jnp.minimum(blk_exp, N_EXPERTS - 1)
    eid = jnp.arange(N_EXPERTS, dtype=i32)

    def pick(table, idx):
        return jnp.sum(jnp.where(idx[:, None] == eid[None, :], table[None, :], 0), axis=1).astype(i32)

    blk_start = pick(pad_start, blk_exp)
    nvalid = jnp.clip(pick(counts, blk_exp) - (blk_row - blk_start), 0, EXP_BLOCK).astype(i32)
    first = jnp.logical_and(blk_row == blk_start, nvalid > 0)
    active = counts > 0
    act_rank = jnp.cumsum(active.astype(i32)) - 1
    later = jnp.logical_and(active[None, :], eid[None, :] > eid[:, None])
    nxt = jnp.min(jnp.where(later, eid[None, :], N_EXPERTS), axis=1)
    nxt = jnp.where(nxt == N_EXPERTS, -1, nxt).astype(i32)
    blk_rank = pick(act_rank, blk_exp)
    head = jnp.logical_and(first, blk_rank == 0)
    n_used = pad_end[-1] // EXP_BLOCK
    io_blk = jnp.minimum(jnp.arange(N_BLOCKS, dtype=i32), n_used - 1).astype(i32)
    tables = (blk_exp, nvalid, first.astype(i32), head.astype(i32), (blk_rank % W_SLOTS).astype(i32),
              pick(nxt + 1, blk_exp) - 1, io_blk)
    return rowstart, tables


def kernel(x_prompt, x_sample, cache_attn_k, cache_attn_v, cache_na_k, cache_na_v, c, c_ctx, w_mod, b_mod, norm1_g, norm2_g, w_in, b_gate, q_norm_g, k_norm_g, na_rpb, conv_w, conv_b, conv_ln_g, conv_ln_b, sgu_ln_g, sgu_ln_b, sgu_w, sgu_b, w_branch, w_out, router_w, router_b, exp_w1, exp_b1, exp_w2, exp_b2, final_g):
    stream = (x_prompt.reshape(N_P, D_MODEL), x_sample.reshape(N_S, D_MODEL), 0)
    cvec =jnp.zeros((SUBLANES, D_MODEL), F32).at[0].set(c_ctx).at[1:1 + DEC_BATCH].set(c)
    mods = _modulation(cvec, w_mod, b_mod).reshape(DEPTH, SUBLANES, N_MOD, D_MODEL)
    cos_t, sin_t = _rope_tables()
    cak = cache_attn_k.reshape(DEC_BATCH, DEPTH, PAST_LEN, A_KV)
    cav = cache_attn_v.reshape(DEC_BATCH, DEPTH, PAST_LEN, A_KV)
    cbk = cache_na_k.reshape(DEC_BATCH, DEPTH, PAST_LEN, B_W)
    cbv = cache_na_v.reshape(DEC_BATCH, DEPTH, PAST_LEN, B_W)
    w_in_b = w_in.astype(BF16)
    w_br = w_branch.astype(BF16)
    w_o = w_out.astype(BF16)
    rw_hi = router_w.astype(BF16)
    rw_lo = (router_w - rw_hi.astype(F32)).astype(BF16)
    final_g2 = final_g.reshape(1, D_MODEL)

    caches = ()
    outs = None
    for l in range(DEPTH):
        mods_l = mods[l]
        g1 = norm1_g[l].reshape(1, D_MODEL)
        gq = jnp.tile(q_norm_g[l], A_HEADS).reshape(1, A_Q)
        gk = jnp.tile(k_norm_g[l], A_KV_HEADS).reshape(1, A_KV)
        aq, ak, av, bq, bk, bv, cz, dz = _in_proj(stream, mods_l, g1, w_in_b, l)
        ya, yb, *caches = _prompt_attn(aq, ak, av, bq, bk, bv, gq, gk, caches, l)
        ya = _sample_attn(aq, ak, av, cak, cav, cos_t, sin_t, gq, gk, ya, l)
        yb = _na_attn(bq, bk, bv, cbk, cbv, _na_bias(na_rpb[l]), yb, l)
        cw = conv_w[l]
        cb = conv_b[l].reshape(1, C_WIDTH)
        cg = conv_ln_g[l].reshape(1, C_WIDTH)
        cbb = conv_ln_b[l].reshape(1, C_WIDTH)
        yc = _conv_call(cz, cw, cb, cg, cbb, SEQ, 0, BATCH)
        yc = _conv_call(cz, cw, cb, cg, cbb, DEC_SEQ, N_P // DEC_SEQ, DEC_BATCH, partial_out=yc)
        bs_full = jnp.repeat(sgu_b[l].T, SGU_GW, axis=1)
        yd = _sgu(dz, sgu_ln_g[l].reshape(1, SGU_WIDTH), sgu_ln_b[l].reshape(1, SGU_WIDTH),
                  sgu_w[l].astype(BF16), bs_full)
        xm, h2, top_w, q, tile_cnt, tile_off = _merge(
            stream, ya, yb, yc, yd, mods_l, g1, w_in_b, b_gate[l].reshape(1, N_BRANCH * D_MODEL), w_br[l], w_o[l],
            norm2_g[l].reshape(1, D_MODEL), rw_hi[l], rw_lo[l], router_b[l].reshape(1, N_EXPERTS), l)
        tile_cnt = tile_cnt.reshape(N_TILES, N_EXPERTS)
        rowstart, tables = _routing_tables(tile_cnt)
        pieces = _piece_lists(tile_cnt, tile_off.reshape(N_TILES, N_EXPERTS), rowstart)
        xs = _dispatch(pieces, q, h2)
        y = _experts(tables, xs, exp_w1, exp_b1, exp_w2, exp_b2, l)
        outs = _combine(pieces, q, top_w, y, xm, mods_l, final_g2, l == DEPTH - 1)
        stream = (outs[0], outs[0], N_P)

    new_k, new_v, new_bk, new_bv = caches
    return (outs[0].reshape(BATCH, SEQ, D_MODEL), outs[1].reshape(DEC_BATCH, DEC_SEQ, D_MODEL),
            new_k.reshape(BATCH, DEPTH, SEQ, A_KV_HEADS, HEAD_DIM),
            new_v.reshape(BATCH, DEPTH, SEQ, A_KV_HEADS, HEAD_DIM),
            new_bk.reshape(BATCH, DEPTH, SEQ, B_HEADS, HEAD_DIM),
            new_bv.reshape(BATCH, DEPTH, SEQ, B_HEADS, HEAD_DIM))
```

```python
import functools

import numpy as np
import jax
import jax.numpy as jnp
from jax import lax
from jax.experimental import pallas as pl
from jax.experimental.pallas import tpu as pltpu

D_MODEL = 1024
BATCH = 32
SEQ = 256
DEPTH = 2
DEC_BATCH = 2
DEC_SEQ = 1024
PAST_LEN = 512
GRID_W = 64
HEAD_DIM = 64
A_HEADS = 4
A_KV_HEADS = 2
B_HEADS = 4
NA_ROWS = 8
NA_COLS = 16
C_WIDTH = 256
CONV_WIDTH = 31
SGU_WIDTH = 256
SGU_GROUPS = 4
SGU_CHUNK = 128
N_BRANCH = 4
BRANCH_W = 256
N_EXPERTS = 32
TOP_K = 4
D_FF = 1024
SWIGLU_ALPHA = 1.702
SWIGLU_LIMIT = 7.0
ROPE_THETA = 10000.0
ROPE_PAIRS = HEAD_DIM // 4
N_MOD = 6
EPS = 1e-6
NEG_INF = -1e30

A_Q = A_HEADS * HEAD_DIM
A_KV = A_KV_HEADS * HEAD_DIM
B_W = B_HEADS * HEAD_DIM
MIX_SIZES = (A_Q, A_KV, A_KV, B_W, B_W, B_W, 2 * C_WIDTH, 2 * SGU_WIDTH)
MIX_COLS = sum(MIX_SIZES)

N_P = BATCH * SEQ
N_S = DEC_BATCH * DEC_SEQ
N_TOK = N_P + N_S
N_ASSIGN = N_TOK * TOP_K
EXP_BLOCK = 512
N_BLOCKS = N_ASSIGN // EXP_BLOCK + N_EXPERTS
N_ROWS = N_BLOCKS * EXP_BLOCK
GRID_ROWS = DEC_SEQ // GRID_W
NA_WR = min(NA_ROWS, GRID_ROWS)
N_LOC = NA_WR * GRID_W

SUBLANES = 8
LANES = 128
ROW_TILES = D_MODEL // LANES
VMEM_CAPACITY_V7X = 64 * 1024 * 1024
VMEM_LIMIT = VMEM_CAPACITY_V7X - 8 * 1024 * 1024

F32 = jnp.float32
BF16 = jnp.bfloat16


def _params(n_axes, vmem=None):
    return pltpu.CompilerParams(
        dimension_semantics=("arbitrary",) * n_axes,
        vmem_limit_bytes=vmem if vmem is not None else VMEM_LIMIT)


def _mod_row(start):
    return jnp.where(start < N_P, 0, 1 + (start - N_P) // DEC_SEQ)


def _bdot(a, b):
    return jnp.dot(a.astype(BF16), b.astype(BF16), preferred_element_type=F32)


def _bdot_nt(a, b):
    return lax.dot_general(a.astype(BF16), b.astype(BF16), (((1,), (1,)), ((), ())),
                           preferred_element_type=F32)


def _sigmoid(x):
    return 0.5 * jnp.tanh(0.5 * x) + 0.5


MOD_TN = 1536


def _mod_kernel(c_ref, w_ref, b_ref, o_ref):
    c = c_ref[...]
    s = c * _sigmoid(c)
    o_ref[...] = _bdot(s, w_ref[...]) + b_ref[...]


def _modulation(cvec, w_mod, b_mod):
    n_col = N_MOD * D_MODEL
    return pl.pallas_call(
        _mod_kernel,
        grid=(DEPTH, n_col // MOD_TN),
        in_specs=[
            pl.BlockSpec((SUBLANES, D_MODEL), lambda l, j: (0, 0)),
            pl.BlockSpec((None, D_MODEL, MOD_TN), lambda l, j: (l, 0, j)),
            pl.BlockSpec((None, 1, MOD_TN), lambda l, j: (l, 0, j)),
        ],
        out_specs=pl.BlockSpec((None, SUBLANES, MOD_TN), lambda l, j: (l, 0, j)),
        out_shape=jax.ShapeDtypeStruct((DEPTH, SUBLANES, n_col), F32),
        compiler_params=_params(2),
        name="modulation",
    )(cvec, w_mod, b_mod.reshape(DEPTH, 1, n_col))


IN_TM = 512


def _norm_mod(x, g, shift, scale):
    y = x * lax.rsqrt(jnp.mean(x * x, axis=-1, keepdims=True) + EPS) * g
    return y * (1.0 + scale) + shift


def _stream_specs(tm, stream):
    p_tiles = N_P // tm
    s_first = stream[2] // tm
    return [pl.BlockSpec((tm, D_MODEL), lambda i: (jnp.minimum(i, p_tiles - 1), 0)),
            pl.BlockSpec((tm, D_MODEL), lambda i: (jnp.maximum(i - p_tiles, 0) + s_first, 0))]


def _stream_tile(xp_ref, xs_ref, tm):
    return jnp.where(pl.program_id(0) < N_P // tm, xp_ref[...], xs_ref[...])


def _in_kernel(xp_ref, xs_ref, mod_ref, g_ref, w_ref, *out_refs):
    m = mod_ref[...]
    h = _norm_mod(_stream_tile(xp_ref, xs_ref, IN_TM), g_ref[...], m[0:1], m[1:2])
    z = jnp.dot(h.astype(BF16), w_ref[...], preferred_element_type=F32)
    off = 0
    for o_ref, sz in zip(out_refs, MIX_SIZES):
        o_ref[...] = z[:, off:off + sz]
        off += sz


def _in_proj(stream, mods_l, g1, w_in_b, layer):
    return pl.pallas_call(
        _in_kernel,
        grid=(N_TOK // IN_TM,),
        in_specs=_stream_specs(IN_TM, stream) + [
            pl.BlockSpec((None, N_MOD, D_MODEL), lambda i: (_mod_row(i * IN_TM), 0, 0)),
            pl.BlockSpec((1, D_MODEL), lambda i: (0, 0)),
            pl.BlockSpec((None, D_MODEL, MIX_COLS), lambda i: (layer, 0, 0)),
        ],
        out_specs=[pl.BlockSpec((IN_TM, sz), lambda i: (i, 0)) for sz in MIX_SIZES],
        out_shape=[jax.ShapeDtypeStruct((N_TOK, sz), F32) for sz in MIX_SIZES],
        compiler_params=_params(1),
        name="in_proj",
    )(stream[0], stream[1], mods_l, g1, w_in_b)


def _head_rms(x, g):
    width = x.shape[-1]
    seg_r = lax.broadcasted_iota(jnp.int32, (width, width), 0) // HEAD_DIM
    seg_c = lax.broadcasted_iota(jnp.int32, (width, width), 1) // HEAD_DIM
    avg = jnp.where(seg_r == seg_c, 1.0 / HEAD_DIM, 0.0).astype(BF16)
    hi, lo = _split_bf16(x * x)
    ms = jnp.dot(hi, avg, preferred_element_type=F32) + jnp.dot(lo, avg, preferred_element_type=F32)
    return x * lax.rsqrt(ms + EPS) * g


def _softmax_pv(score_parts, value_parts):
    m = score_parts[0].max(axis=-1, keepdims=True)
    for s in score_parts[1:]:
        m = jnp.maximum(m, s.max(axis=-1, keepdims=True))
    den = None
    acc = None
    for s, v in zip(score_parts, value_parts):
        e = jnp.exp(s - m)
        d = e.sum(axis=-1, keepdims=True)
        a = _bdot(e, v)
        den = d if den is None else den + d
        acc = a if acc is None else acc + a
    return acc / den


def _head(x, h):
    return x[:, h * HEAD_DIM:(h + 1) * HEAD_DIM]


SCALE = HEAD_DIM ** -0.5


def _prompt_attn_kernel(aq_ref, ak_ref, av_ref, bq_ref, bk_ref, bv_ref, gq_ref, gk_ref, *refs):
    ya_ref, yb_ref, nk_ref, nv_ref, nbk_ref, nbv_ref = refs[-6:]
    aq = _head_rms(aq_ref[...], gq_ref[...]) * SCALE
    ak = _head_rms(ak_ref[...], gk_ref[...])
    av = av_ref[...]
    for ref, val in ((nk_ref, ak), (nv_ref, av), (nbk_ref, bk_ref[...]), (nbv_ref, bv_ref[...])):
        n_heads = val.shape[-1] // HEAD_DIM
        for h in range(n_heads):
            ref[pl.ds(h, SEQ, stride=n_heads), :] = _head(val, h)
    grp = A_HEADS // A_KV_HEADS
    outs = []
    for h in range(A_HEADS):
        s = _bdot_nt(_head(aq, h), _head(ak, h // grp))
        outs.append(_softmax_pv([s], [_head(av, h // grp)]))
    ya_ref[...] = jnp.concatenate(outs, axis=-1)
    bq = bq_ref[...] * SCALE
    bk = bk_ref[...]
    bv = bv_ref[...]
    outs = []
    for h in range(B_HEADS):
        s = _bdot_nt(_head(bq, h), _head(bk, h))
        outs.append(_softmax_pv([s], [_head(bv, h)]))
    yb_ref[...] = jnp.concatenate(outs, axis=-1)


def _prompt_attn(aq, ak, av, bq, bk, bv, gq, gk, caches, layer):
    def spec(w):
        return pl.BlockSpec((SEQ, w), lambda b: (b, 0))

    def cache_spec(n_heads):
        return pl.BlockSpec((SEQ * n_heads, HEAD_DIM), lambda b: (b * DEPTH + layer, 0))

    cache_heads = (A_KV_HEADS, A_KV_HEADS, B_HEADS, B_HEADS)
    n_in = 8
    return pl.pallas_call(
        _prompt_attn_kernel,
        grid=(BATCH,),
        in_specs=[spec(A_Q), spec(A_KV), spec(A_KV), spec(B_W), spec(B_W), spec(B_W),
                  pl.BlockSpec((1, A_Q), lambda b: (0, 0)),
                  pl.BlockSpec((1, A_KV), lambda b: (0, 0))]
        + [pl.BlockSpec(memory_space=pl.ANY) for _ in caches],
        out_specs=[spec(A_Q), spec(B_W)] + [cache_spec(n) for n in cache_heads],
        out_shape=[jax.ShapeDtypeStruct((N_TOK, A_Q), F32),
                   jax.ShapeDtypeStruct((N_TOK, B_W), F32)]
        + [jax.ShapeDtypeStruct((BATCH * DEPTH * SEQ * n, HEAD_DIM), F32) for n in cache_heads],
        input_output_aliases={n_in + j: 2 + j for j in range(len(caches))},
        compiler_params=_params(1),
        name="prompt_attn",
    )(aq, ak, av, bq, bk, bv, gq, gk, *caches)


QB = 256


def _rope(x, cos, sin_signed):
    n = x.shape[-1]
    nxt = pltpu.roll(x, n - 1, 1)
    prv = pltpu.roll(x, 1, 1)
    even = (lax.broadcasted_iota(jnp.int32, x.shape, 1) % 2) == 0
    return x * cos + jnp.where(even, nxt, prv) * sin_signed


def _sample_attn_kernel(q_ref, k_ref, v_ref, ck_ref, cv_ref, cosq_ref, sinq_ref, cosk_ref, sink_ref,
                        gq_ref, gk_ref, ya_prompt_ref, o_ref, kr_ref):
    del ya_prompt_ref

    @pl.when(pl.program_id(1) == 0)
    def _():
        kr_ref[...] = _rope(_head_rms(k_ref[...], gk_ref[...]), cosk_ref[...], sink_ref[...]).astype(BF16)

    q = _rope(_head_rms(q_ref[...], gq_ref[...]), cosq_ref[...], sinq_ref[...]) * SCALE
    k = kr_ref[...]
    v = v_ref[...]
    ck = ck_ref[...]
    cv = cv_ref[...]
    grp = A_HEADS // A_KV_HEADS
    outs = []
    for h in range(A_HEADS):
        j = h // grp
        qh = _head(q, h)
        s1 = _bdot_nt(qh, _head(k, j))
        s2 = _bdot_nt(qh, _head(ck, j))
        outs.append(_softmax_pv([s1, s2], [_head(v, j), _head(cv, j)]))
    o_ref[...] = jnp.concatenate(outs, axis=-1)


def _sample_attn(aq, ak, av, cache_k, cache_v, cos_t, sin_t, gq, gk, ya, layer):
    nqb = DEC_SEQ // QB
    q0 = N_P // QB
    k0 = N_P // DEC_SEQ
    return pl.pallas_call(
        _sample_attn_kernel,
        grid=(DEC_BATCH, nqb),
        in_specs=[
            pl.BlockSpec((QB, A_Q), lambda b, i: (q0 + b * nqb + i, 0)),
            pl.BlockSpec((DEC_SEQ, A_KV), lambda b, i: (k0 + b, 0)),
            pl.BlockSpec((DEC_SEQ, A_KV), lambda b, i: (k0 + b, 0)),
            pl.BlockSpec((None, None, PAST_LEN, A_KV), lambda b, i: (b, layer, 0, 0)),
            pl.BlockSpec((None, None, PAST_LEN, A_KV), lambda b, i: (b, layer, 0, 0)),
            pl.BlockSpec((QB, A_Q), lambda b, i: (i, 0)),
            pl.BlockSpec((QB, A_Q), lambda b, i: (i, 0)),
            pl.BlockSpec((DEC_SEQ, A_KV), lambda b, i: (0, 0)),
            pl.BlockSpec((DEC_SEQ, A_KV), lambda b, i: (0, 0)),
            pl.BlockSpec((1, A_Q), lambda b, i: (0, 0)),
            pl.BlockSpec((1, A_KV), lambda b, i: (0, 0)),
            pl.BlockSpec(memory_space=pl.ANY),
        ],
        out_specs=pl.BlockSpec((QB, A_Q), lambda b, i: (q0 + b * nqb + i, 0)),
        out_shape=jax.ShapeDtypeStruct((N_TOK, A_Q), F32),
        input_output_aliases={11: 0},
        scratch_shapes=[pltpu.VMEM((DEC_SEQ, A_KV), BF16)],
        compiler_params=_params(2),
        name="sample_attn",
    )(aq, ak, av, cache_k, cache_v, cos_t, sin_t, cos_t, sin_t, gq, gk, ya)


N_ROW_OFF = 2 * NA_ROWS - 1
N_COL_OFF = 2 * NA_COLS - 1
NA_PAIRS = N_ROW_OFF - 1
assert NA_WR == NA_ROWS and NA_WR % 2 == 0 and 2 * GRID_W == LANES


def _na_bias_kernel(rpb_ref, o_ref):
    h = pl.program_id(0)
    qc = lax.broadcasted_iota(jnp.int32, (GRID_W, LANES), 0)
    lane = lax.broadcasted_iota(jnp.int32, (GRID_W, LANES), 1)
    right = lane >= GRID_W
    kc = jnp.where(right, lane - GRID_W, lane)
    c_start = jnp.clip(qc - NA_COLS // 2, 0, GRID_W - NA_COLS)
    col_in = jnp.logical_and(kc >= c_start, kc < c_start + NA_COLS)
    col_off = jnp.clip(kc - qc + NA_COLS - 1, 0, N_COL_OFF - 1)
    for p in range(NA_PAIRS):
        acc = jnp.zeros((GRID_W, LANES), F32)
        for o in range(N_COL_OFF):
            left_v = rpb_ref[(h * N_ROW_OFF + p) * N_COL_OFF + o]
            right_v = rpb_ref[(h * N_ROW_OFF + p + 1) * N_COL_OFF + o]
            acc = jnp.where(col_off == o, jnp.where(right, right_v, left_v), acc)
        o_ref[p] = jnp.where(col_in, acc, NEG_INF)


def _na_bias(rpb):
    return pl.pallas_call(
        _na_bias_kernel,
        grid_spec=pltpu.PrefetchScalarGridSpec(
            num_scalar_prefetch=1,
            grid=(B_HEADS,),
            in_specs=[],
            out_specs=pl.BlockSpec((None, NA_PAIRS, GRID_W, LANES), lambda h, *_: (h, 0, 0, 0)),
        ),
        out_shape=jax.ShapeDtypeStruct((B_HEADS, NA_PAIRS, GRID_W, LANES), F32),
        compiler_params=_params(1),
        name="na_bias",
    )(rpb.reshape(-1))


NA_STEP_ROWS = 4


def _na_kernel(q_ref, k_ref, v_ref, ck_ref, cv_ref, bias_ref, yb_prompt_ref, o_ref):
    del yb_prompt_ref
    q = q_ref[...] * SCALE
    ck = ck_ref[...]
    cv = cv_ref[...]
    ctx_scores = [_bdot_nt(_head(q, h), _head(ck, h)) for h in range(B_HEADS)]
    for i in range(NA_STEP_ROWS):
        r = pl.program_id(1) * NA_STEP_ROWS + i
        r_start = jnp.clip(r - NA_WR // 2, 0, GRID_ROWS - NA_WR)
        base = pl.multiple_of(r_start * GRID_W, GRID_W)
        row_off0 = r_start - r + NA_ROWS - 1
        rows = slice(i * GRID_W, (i + 1) * GRID_W)
        kb = k_ref[pl.ds(base, N_LOC), :]
        vb = v_ref[pl.ds(base, N_LOC), :]
        outs = []
        for h in range(B_HEADS):
            bias = jnp.concatenate([bias_ref[h, row_off0 + 2 * j] for j in range(NA_WR // 2)], axis=-1)
            s1 = _bdot_nt(_head(q[rows], h), _head(kb, h)) + bias
            outs.append(_softmax_pv([s1, ctx_scores[h][rows]], [_head(vb, h), _head(cv, h)]))
        o_ref[rows, :] = jnp.concatenate(outs, axis=-1)


def _na_attn(bq, bk, bv, cache_k, cache_v, bias, yb, layer):
    step_tok = NA_STEP_ROWS * GRID_W
    steps = GRID_ROWS // NA_STEP_ROWS
    q0 = N_P // step_tok
    k0 = N_P // DEC_SEQ
    return pl.pallas_call(
        _na_kernel,
        grid=(DEC_BATCH, steps),
        in_specs=[
            pl.BlockSpec((step_tok, B_W), lambda b, r: (q0 + b * steps + r, 0)),
            pl.BlockSpec((DEC_SEQ, B_W), lambda b, r: (k0 + b, 0)),
            pl.BlockSpec((DEC_SEQ, B_W), lambda b, r: (k0 + b, 0)),
            pl.BlockSpec((None, None, PAST_LEN, B_W), lambda b, r: (b, layer, 0, 0)),
            pl.BlockSpec((None, None, PAST_LEN, B_W), lambda b, r: (b, layer, 0, 0)),
            pl.BlockSpec((B_HEADS, NA_PAIRS, GRID_W, LANES), lambda b, r: (0, 0, 0, 0)),
            pl.BlockSpec(memory_space=pl.ANY),
        ],
        out_specs=pl.BlockSpec((step_tok, B_W), lambda b, r: (q0 + b * steps + r, 0)),
        out_shape=jax.ShapeDtypeStruct((N_TOK, B_W), F32),
        input_output_aliases={6: 0},
        compiler_params=_params(2),
        name="na_attn",
    )(bq, bk, bv, cache_k, cache_v, bias, yb)


CONV_PAD = 16
CONV_CHUNK = 128


def _layer_norm(x, g, b):
    mu = jnp.mean(x, axis=-1, keepdims=True)
    xc = x - mu
    var = jnp.mean(xc * xc, axis=-1, keepdims=True)
    return xc * lax.rsqrt(var + EPS) * g + b


def _conv_kernel(z_ref, w_ref, cb_ref, g_ref, b_ref, *refs, s_len):
    o_ref, pad_ref = refs[-2:]
    z = z_ref[...]
    u = z[:, :C_WIDTH] * _sigmoid(z[:, C_WIDTH:])
    pad_ref[pl.ds(0, CONV_PAD), :] = jnp.zeros((CONV_PAD, C_WIDTH), F32)
    pad_ref[pl.ds(CONV_PAD + s_len, CONV_PAD), :] = jnp.zeros((CONV_PAD, C_WIDTH), F32)
    pad_ref[pl.ds(CONV_PAD, s_len), :] = u
    w = w_ref[...]
    shift = CONV_PAD - CONV_WIDTH // 2

    def chunk(c, carry):
        base = pl.multiple_of(c * CONV_CHUNK, CONV_CHUNK)
        acc = jnp.zeros((CONV_CHUNK, C_WIDTH), F32)
        for r in range(SUBLANES):
            part = None
            for k in range(CONV_WIDTH):
                if (k + shift) % SUBLANES != r:
                    continue
                rows = pad_ref[pl.ds(base + (k + shift - r), CONV_CHUNK + SUBLANES), :]
                term = rows * w[k:k + 1]
                part = term if part is None else part + term
            if part is not None:
                acc = acc + part[r:r + CONV_CHUNK]
        y = _layer_norm(acc + cb_ref[...], g_ref[...], b_ref[...])
        o_ref[pl.ds(base, CONV_CHUNK), :] = y * _sigmoid(y)
        return carry

    lax.fori_loop(0, s_len // CONV_CHUNK, chunk, 0)


def _conv_call(cz, w, cb, g, b, s_len, first_blk, n_seq, partial_out=None):
    vec = pl.BlockSpec((1, C_WIDTH), lambda i: (0, 0))
    extra = [] if partial_out is None else [partial_out]
    return pl.pallas_call(
        functools.partial(_conv_kernel, s_len=s_len),
        grid=(n_seq,),
        in_specs=[pl.BlockSpec((s_len, 2 * C_WIDTH), lambda i: (first_blk + i, 0)),
                  pl.BlockSpec((CONV_WIDTH, C_WIDTH), lambda i: (0, 0)), vec, vec, vec]
        + [pl.BlockSpec(memory_space=pl.ANY) for _ in extra],
        out_specs=pl.BlockSpec((s_len, C_WIDTH), lambda i: (first_blk + i, 0)),
        out_shape=jax.ShapeDtypeStruct((N_TOK, C_WIDTH), F32),
        input_output_aliases={5: 0} if extra else {},
        scratch_shapes=[pltpu.VMEM((s_len + 2 * CONV_PAD, C_WIDTH), F32)],
        compiler_params=_params(1),
        name="conformer_conv_%d" % s_len,
    )(cz, w, cb, g, b, *extra)


SGU_TM = 512
SGU_GW = SGU_WIDTH // SGU_GROUPS


def _sgu_kernel(z_ref, g_ref, b_ref, ws_ref, bs_ref, o_ref):
    z = z_ref[...]
    z = 0.5 * z * (1.0 + lax.erf(z * (2.0 ** -0.5)))
    u = z[:, :SGU_WIDTH]
    v = _layer_norm(z[:, SGU_WIDTH:], g_ref[...], b_ref[...])
    for c in range(SGU_TM // SGU_CHUNK):
        vc = v[c * SGU_CHUNK:(c + 1) * SGU_CHUNK]
        parts = [_bdot(ws_ref[g], vc[:, g * SGU_GW:(g + 1) * SGU_GW]) for g in range(SGU_GROUPS)]
        mixed = jnp.concatenate(parts, axis=-1) + bs_ref[...]
        o_ref[pl.ds(c * SGU_CHUNK, SGU_CHUNK), :] = u[c * SGU_CHUNK:(c + 1) * SGU_CHUNK] * mixed


def _sgu(dz, g, b, ws, bs_full):
    vec = pl.BlockSpec((1, SGU_WIDTH), lambda i: (0, 0))
    return pl.pallas_call(
        _sgu_kernel,
        grid=(N_TOK // SGU_TM,),
        in_specs=[pl.BlockSpec((SGU_TM, 2 * SGU_WIDTH), lambda i: (i, 0)), vec, vec,
                  pl.BlockSpec((SGU_GROUPS, SGU_CHUNK, SGU_CHUNK), lambda i: (0, 0, 0)),
                  pl.BlockSpec((SGU_CHUNK, SGU_WIDTH), lambda i: (0, 0))],
        out_specs=pl.BlockSpec((SGU_TM, SGU_WIDTH), lambda i: (i, 0)),
        out_shape=jax.ShapeDtypeStruct((N_TOK, SGU_WIDTH), F32),
        compiler_params=_params(1),
        name="chunk_sgu",
    )(dz, g, b, ws, bs_full)


MERGE_TM = 512


def _merge_kernel(xp_ref, xs_ref, ya_ref, yb_ref, yc_ref, yd_ref, mod_ref, g1_ref, w_in_hbm, bg_ref, wb_ref, wo_ref,
                  g2_ref, rwh_ref, rwl_ref, rb_ref, xm_ref, h2_ref, w_ref, q_ref, cnt_ref, off_ref,
                  wg_ref, sem, *, layer):
    @pl.when(pl.program_id(0) == 0)
    def _():
        cp = pltpu.make_async_copy(w_in_hbm.at[layer, :, pl.ds(MIX_COLS, N_BRANCH * D_MODEL)], wg_ref, sem)
        cp.start()
        cp.wait()

    m = mod_ref[...]
    x = _stream_tile(xp_ref, xs_ref, MERGE_TM)
    h = _norm_mod(x, g1_ref[...], m[0:1], m[1:2]).astype(BF16)
    merged = None
    for i, y_ref in enumerate((ya_ref, yb_ref, yc_ref, yd_ref)):
        logit = jnp.dot(h, wg_ref[:, i * D_MODEL:(i + 1) * D_MODEL], preferred_element_type=F32)
        gate = _sigmoid(logit + bg_ref[:, i * D_MODEL:(i + 1) * D_MODEL])
        term = gate * jnp.dot(y_ref[...].astype(BF16), wb_ref[i], preferred_element_type=F32)
        merged = term if merged is None else merged + term
    out = jnp.dot(merged.astype(BF16), wo_ref[...], preferred_element_type=F32)
    xm = x + m[2:3] * out
    xm_ref[...] = xm
    h2 = _norm_mod(xm, g2_ref[...], m[3:4], m[4:5])
    h2_hi = h2.astype(BF16)
    h2_lo = (h2 - h2_hi.astype(F32)).astype(BF16)
    lg = jnp.dot(h2_hi, rwh_ref[...], preferred_element_type=F32)
    lg = lg + jnp.dot(h2_hi, rwl_ref[...], preferred_element_type=F32)
    lg = lg + jnp.dot(h2_lo, rwh_ref[...], preferred_element_type=F32)
    lg = lg + rb_ref[...]
    h2_ref[...] = h2_hi
    for j in range(MERGE_TM // ROUTE_TM):
        rows = slice(j * ROUTE_TM, (j + 1) * ROUTE_TM)
        w_out, q_out, cnt, off = _route_tile(lg[rows])
        w_ref[rows, :] = w_out
        q_ref[rows, :] = q_out
        cnt_ref[j] = cnt
        off_ref[j] = off


def _merge(stream, ya, yb, yc, yd, mods_l, g1, w_in_b, bg, wb, wo, g2, rwh, rwl, rb, layer):
    def tok(w):
        return pl.BlockSpec((MERGE_TM, w), lambda i: (i, 0))

    def full(*shape):
        return pl.BlockSpec(shape, lambda i: (0,) * len(shape))

    tile_rows = pl.BlockSpec((MERGE_TM // ROUTE_TM, 1, N_EXPERTS), lambda i: (i, 0, 0))
    return pl.pallas_call(
        functools.partial(_merge_kernel, layer=layer),
        grid=(N_TOK // MERGE_TM,),
        in_specs=_stream_specs(MERGE_TM, stream) + [
                  tok(BRANCH_W), tok(BRANCH_W), tok(BRANCH_W), tok(BRANCH_W),
                  pl.BlockSpec((None, N_MOD, D_MODEL), lambda i: (_mod_row(i * MERGE_TM), 0, 0)),
                  full(1, D_MODEL), pl.BlockSpec(memory_space=pl.ANY), full(1, N_BRANCH * D_MODEL),
                  full(N_BRANCH, BRANCH_W, D_MODEL), full(D_MODEL, D_MODEL), full(1, D_MODEL),
                  full(D_MODEL, N_EXPERTS), full(D_MODEL, N_EXPERTS), full(1, N_EXPERTS)],
        out_specs=[tok(D_MODEL), tok(D_MODEL), tok(TOP_K), tok(TOP_K), tile_rows, tile_rows],
        out_shape=[jax.ShapeDtypeStruct((N_TOK, D_MODEL), F32),
                   jax.ShapeDtypeStruct((N_TOK, D_MODEL), BF16),
                   jax.ShapeDtypeStruct((N_TOK, TOP_K), F32),
                   jax.ShapeDtypeStruct((N_TOK, TOP_K), jnp.int32),
                   jax.ShapeDtypeStruct((N_TILES, 1, N_EXPERTS), jnp.int32),
                   jax.ShapeDtypeStruct((N_TILES, 1, N_EXPERTS), jnp.int32)],
        scratch_shapes=[pltpu.VMEM((D_MODEL, N_BRANCH * D_MODEL), BF16), pltpu.SemaphoreType.DMA(())],
        compiler_params=_params(1),
        name="merge",
    )(stream[0], stream[1], ya, yb, yc, yd, mods_l, g1, w_in_b, bg, wb, wo, g2, rwh, rwl, rb)


ROUTE_TM = 256
TILE_ROWS = ROUTE_TM * TOP_K
N_TILES = N_TOK // ROUTE_TM


def _route_tile(lg):
    lane = lax.broadcasted_iota(jnp.int32, lg.shape, 1)
    sels, vals = [], []
    for _ in range(TOP_K):
        mx = lg.max(axis=-1, keepdims=True)
        idx = jnp.where(lg == mx, lane, N_EXPERTS).min(axis=-1, keepdims=True)
        sel = lane == idx
        sels.append(sel)
        vals.append(mx)
        lg = jnp.where(sel, -jnp.inf, lg)
    exps = [jnp.exp(v - vals[0]) for v in vals]
    den = exps[0] + exps[1] + exps[2] + exps[3]
    onehot = jnp.zeros(lg.shape, F32)
    for sel in sels:
        onehot = onehot + sel.astype(F32)
    row = lax.broadcasted_iota(jnp.int32, (ROUTE_TM, ROUTE_TM), 0)
    col = lax.broadcasted_iota(jnp.int32, (ROUTE_TM, ROUTE_TM), 1)
    tri = jnp.where(col < row, 1.0, 0.0).astype(BF16)
    rank = jnp.dot(tri, onehot.astype(BF16), preferred_element_type=F32)
    cnt = jnp.sum(onehot, axis=0, keepdims=True)
    erow = lax.broadcasted_iota(jnp.int32, (N_EXPERTS, N_EXPERTS), 0)
    ecol = lax.broadcasted_iota(jnp.int32, (N_EXPERTS, N_EXPERTS), 1)
    upper = jnp.where(erow < ecol, 1.0, 0.0).astype(BF16)
    off = jnp.dot(jnp.broadcast_to(cnt, (SUBLANES, N_EXPERTS)).astype(BF16), upper,
                  preferred_element_type=F32)[0:1]
    slot = rank + off
    k_lane = lax.broadcasted_iota(jnp.int32, (ROUTE_TM, TOP_K), 1)
    w_out = jnp.zeros((ROUTE_TM, TOP_K), F32)
    q_out = jnp.zeros((ROUTE_TM, TOP_K), F32)
    for k in range(TOP_K):
        w_out = jnp.where(k_lane == k, exps[k] / den, w_out)
        qk = jnp.sum(jnp.where(sels[k], slot, 0.0), axis=-1, keepdims=True)
        q_out = jnp.where(k_lane == k, qk, q_out)
    return w_out, q_out.astype(jnp.int32), cnt.astype(jnp.int32), off.astype(jnp.int32)


PIECE_SIZES = (32, 16, 8, 4, 2, 1)
PIECE_SLOTS = TILE_ROWS // PIECE_SIZES[0]
assert PIECE_SLOTS >= N_EXPERTS


def _compact(valid, *values):
    pos = jnp.cumsum(valid.astype(jnp.int32), axis=1) - 1
    slot = jnp.arange(PIECE_SLOTS, dtype=jnp.int32)
    hit = jnp.logical_and(valid[:, :, None], pos[:, :, None] == slot[None, None, :])
    packed = [jnp.sum(jnp.where(hit, v[:, :, None], 0), axis=1).astype(jnp.int32) for v in values]
    return packed, jnp.sum(valid.astype(jnp.int32), axis=1)


def _piece_lists(cnt, off, row):
    big = PIECE_SIZES[0]
    n_big = cnt // big
    p = jnp.arange(ROUTE_TM // big, dtype=jnp.int32)
    valid = (p[None, None, :] < n_big[:, :, None]).reshape(N_TILES, -1)
    src = (off[:, :, None] + big * p).reshape(N_TILES, -1)
    dst = (row[:, :, None] + big * p).reshape(N_TILES, -1)
    lists = [_compact(valid, src, dst)]
    rem = cnt - n_big * big
    for size in PIECE_SIZES[1:]:
        start = n_big * big + (rem & ~(2 * size - 1))
        lists.append(_compact((rem & size) != 0, off + start, row + start))
    counts = jnp.stack([n for _, n in lists], axis=1).reshape(-1)
    local_rows = jnp.stack([v[0] for v, _ in lists], axis=1).reshape(-1)
    global_rows = jnp.stack([v[1] for v, _ in lists], axis=1).reshape(-1)
    return counts, local_rows, global_rows


def _slab_pieces(tile, count_ref, local_ref, global_ref, fn):
    for k, size in enumerate(PIECE_SIZES):
        lst = tile * len(PIECE_SIZES) + k

        def body(j, carry, lst=lst, size=size):
            fn(local_ref[lst * PIECE_SLOTS + j], global_ref[lst * PIECE_SLOTS + j], size)
            return carry

        lax.fori_loop(0, count_ref[lst], body, 0)


def _rows(ref, row, n_rows):
    start = row * ROW_TILES
    if not isinstance(row, int):
        start = pl.multiple_of(start, ROW_TILES)
    return ref.at[pl.ds(start, n_rows * ROW_TILES)]


def _onehot_rows(q, values=None):
    lane = lax.broadcasted_iota(jnp.int32, (ROUTE_TM, TILE_ROWS), 1)
    s = jnp.zeros((ROUTE_TM, TILE_ROWS), F32)
    for k in range(TOP_K):
        v = 1.0 if values is None else values[:, k:k + 1]
        s = jnp.where(lane == q[:, k:k + 1], v, s)
    return s


def _wait_tile_rows(hbm_ref, buf_slot_ref, sem_slot):
    pltpu.make_async_copy(_rows(hbm_ref, 0, TILE_ROWS), buf_slot_ref, sem_slot).wait()


def _dispatch_kernel(count_ref, local_ref, global_ref, q_ref, h2_ref, xs_ref, buf_ref, sem):
    tile = pl.program_id(0)
    slot = tile % 2
    buf = buf_ref.at[slot]

    @pl.when(tile >= 2)
    def _():
        _wait_tile_rows(xs_ref, buf, sem.at[slot])

    sel = _onehot_rows(q_ref[...]).astype(BF16)
    xg = lax.dot_general(sel, h2_ref[...], (((0,), (0,)), ((), ())), preferred_element_type=F32)
    for c in range(ROW_TILES):
        buf[pl.ds(c, TILE_ROWS, stride=ROW_TILES), :] = xg[:, c * LANES:(c + 1) * LANES]

    def start(local_row, global_row, n_rows):
        pltpu.make_async_copy(_rows(buf, local_row, n_rows), _rows(xs_ref, global_row, n_rows),
                              sem.at[slot]).start()

    _slab_pieces(tile, count_ref, local_ref, global_ref, start)

    @pl.when(tile == N_TILES - 1)
    def _():
        _wait_tile_rows(xs_ref, buf, sem.at[slot])
        _wait_tile_rows(xs_ref, buf_ref.at[1 - slot], sem.at[1 - slot])


def _dispatch(pieces, q, h2):
    grid_spec = pltpu.PrefetchScalarGridSpec(
        num_scalar_prefetch=3,
        grid=(N_TILES,),
        in_specs=[pl.BlockSpec((ROUTE_TM, TOP_K), lambda i, *_: (i, 0)),
                  pl.BlockSpec((ROUTE_TM, D_MODEL), lambda i, *_: (i, 0))],
        out_specs=pl.BlockSpec(memory_space=pl.ANY),
        scratch_shapes=[pltpu.VMEM((2, TILE_ROWS * ROW_TILES, LANES), F32), pltpu.SemaphoreType.DMA((2,))],
    )
    return pl.pallas_call(
        _dispatch_kernel,
        grid_spec=grid_spec,
        out_shape=jax.ShapeDtypeStruct((N_ROWS * ROW_TILES, LANES), F32),
        compiler_params=_params(1),
        name="dispatch",
    )(*pieces, q, h2)


CAST_ROWS = 128
W_SLOTS = 2
EXP_ROW_STEP = 128


def _expert_kernel(blk_exp_ref, nvalid_ref, first_ref, head_ref, slot_ref, next_ref, io_blk_ref,
                   xs_ref, w1_hbm, b1_ref, w2_hbm, b2_ref, y_ref,
                   w1f_ref, w2f_ref, w1b_ref, w2b_ref, sem, *, layer):
    b = pl.program_id(0)
    e = blk_exp_ref[b]
    nvalid = nvalid_ref[b]
    slot = slot_ref[b]

    def start_weights(expert, ahead):
        s = (slot + ahead) % W_SLOTS
        pltpu.make_async_copy(w1_hbm.at[layer, expert], w1f_ref.at[s], sem.at[0, s]).start()
        pltpu.make_async_copy(w2_hbm.at[layer, expert], w2f_ref.at[s], sem.at[1, s]).start()

    @pl.when(first_ref[b] == 1)
    def _():
        @pl.when(head_ref[b] == 1)
        def _():
            start_weights(e, 0)

        pltpu.make_async_copy(w1_hbm.at[layer, e], w1f_ref.at[slot], sem.at[0, slot]).wait()
        pltpu.make_async_copy(w2_hbm.at[layer, e], w2f_ref.at[slot], sem.at[1, slot]).wait()

        @pl.when(next_ref[b] >= 0)
        def _():
            start_weights(next_ref[b], 1)

        def cast1(i, carry):
            r = pl.multiple_of(i * CAST_ROWS, CAST_ROWS)
            w1b_ref[pl.ds(r, CAST_ROWS), :] = w1f_ref[slot, pl.ds(r, CAST_ROWS), :].astype(BF16)
            return carry

        def cast2(i, carry):
            r = pl.multiple_of(i * CAST_ROWS, CAST_ROWS)
            w2b_ref[pl.ds(r, CAST_ROWS), :] = w2f_ref[slot, pl.ds(r, CAST_ROWS), :].astype(BF16)
            return carry

        lax.fori_loop(0, D_MODEL // CAST_ROWS, cast1, 0)
        lax.fori_loop(0, D_FF // CAST_ROWS, cast2, 0)

    def run_rows(n_rows):
        valid = lax.broadcasted_iota(jnp.int32, (n_rows, LANES), 0) < nvalid
        chunks = [jnp.where(valid, xs_ref[pl.ds(c, n_rows, stride=ROW_TILES), :], 0.0).astype(BF16)
                  for c in range(ROW_TILES)]
        xb = jnp.concatenate(chunks, axis=-1)
        hid = jnp.dot(xb, w1b_ref[...], preferred_element_type=F32) + b1_ref[...]
        glu = jnp.minimum(hid[:, :D_FF], SWIGLU_LIMIT)
        lin = jnp.clip(hid[:, D_FF:], -SWIGLU_LIMIT, SWIGLU_LIMIT)
        act = glu * _sigmoid(SWIGLU_ALPHA * glu) * (lin + 1.0)
        y = jnp.dot(act.astype(BF16), w2b_ref[...], preferred_element_type=F32) + b2_ref[...]
        for c in range(ROW_TILES):
            y_ref[pl.ds(c, n_rows, stride=ROW_TILES), :] = y[:, c * LANES:(c + 1) * LANES]
        if n_rows < EXP_BLOCK:
            rest = (EXP_BLOCK - n_rows) * ROW_TILES
            y_ref[pl.ds(n_rows * ROW_TILES, rest), :] = jnp.zeros((rest, LANES), F32)

    for n_rows in range(EXP_ROW_STEP, EXP_BLOCK + 1, EXP_ROW_STEP):
        @pl.when(jnp.logical_and(nvalid > n_rows - EXP_ROW_STEP, nvalid <= n_rows))
        def _(n_rows=n_rows):
            run_rows(n_rows)


def _experts(tables, xs, w1, b1, w2, b2, layer):
    def blk(b, *tbl):
        return (tbl[-1][b], 0)

    def bias(b, be, *_):
        return (layer, be[b], 0, 0)

    grid_spec = pltpu.PrefetchScalarGridSpec(
        num_scalar_prefetch=7,
        grid=(N_BLOCKS,),
        in_specs=[
            pl.BlockSpec((EXP_BLOCK * ROW_TILES, LANES), blk),
            pl.BlockSpec(memory_space=pl.ANY),
            pl.BlockSpec((None, None, 1, 2 * D_FF), bias),
            pl.BlockSpec(memory_space=pl.ANY),
            pl.BlockSpec((None, None, 1, D_MODEL), bias),
        ],
        out_specs=pl.BlockSpec((EXP_BLOCK * ROW_TILES, LANES), blk),
        scratch_shapes=[pltpu.VMEM((W_SLOTS, D_MODEL, 2 * D_FF), F32), pltpu.VMEM((W_SLOTS, D_FF, D_MODEL), F32),
                        pltpu.VMEM((D_MODEL, 2 * D_FF), BF16), pltpu.VMEM((D_FF, D_MODEL), BF16),
                        pltpu.SemaphoreType.DMA((2, W_SLOTS))],
    )
    return pl.pallas_call(
        functools.partial(_expert_kernel, layer=layer),
        grid_spec=grid_spec,
        out_shape=jax.ShapeDtypeStruct((N_ROWS * ROW_TILES, LANES), F32),
        compiler_params=_params(1),
        name="experts",
    )(*tables, xs, w1, b1.reshape(DEPTH, N_EXPERTS, 1, 2 * D_FF), w2,
      b2.reshape(DEPTH, N_EXPERTS, 1, D_MODEL))


def _split_bf16(x):
    hi = x.astype(BF16)
    return hi, (x - hi.astype(F32)).astype(BF16)


P_TILES = N_P // ROUTE_TM


def _combine_kernel(count_ref, local_ref, global_ref, q_ref, w_ref, y_ref, xm_ref, mod_ref, fg_ref,
                    *refs, final):
    buf_ref, sem = refs[-2:]
    tile = pl.program_id(0)
    slot = tile % 2
    buf = buf_ref.at[slot]

    def fetch(t, s):
        def start(local_row, global_row, n_rows):
            pltpu.make_async_copy(_rows(y_ref, global_row, n_rows), _rows(buf_ref.at[s], local_row, n_rows),
                                  sem.at[s]).start()

        _slab_pieces(t, count_ref, local_ref, global_ref, start)

    @pl.when(tile == 0)
    def _():
        fetch(tile, slot)

    @pl.when(tile + 1 < N_TILES)
    def _():
        fetch(tile + 1, 1 - slot)

    weighted = _onehot_rows(q_ref[...], w_ref[...])
    pick = jnp.where(weighted != 0.0, 1.0, 0.0).astype(BF16)
    row_w = jnp.sum(weighted, axis=0, keepdims=True)
    row_w = jnp.transpose(jnp.broadcast_to(row_w, (LANES, TILE_ROWS)))
    _wait_tile_rows(y_ref, buf, sem.at[slot])
    rows = jnp.concatenate([buf[pl.ds(c, TILE_ROWS, stride=ROW_TILES), :] * row_w for c in range(ROW_TILES)],
                           axis=-1)
    r_hi, r_lo = _split_bf16(rows)
    moe = jnp.dot(pick, r_hi, preferred_element_type=F32) + jnp.dot(pick, r_lo, preferred_element_type=F32)
    m = mod_ref[...]
    x = xm_ref[...] + m[5:6] * moe
    if not final:
        refs[0][...] = x
        return
    xn = x * lax.rsqrt(jnp.mean(x * x, axis=-1, keepdims=True) + EPS) * fg_ref[...]
    yp_ref, ys_ref = refs[:2]

    @pl.when(tile < P_TILES)
    def _():
        yp_ref[...] = xn

    @pl.when(tile >= P_TILES)
    def _():
        ys_ref[...] = xn


def _combine(pieces, q, topw, y, xm, mods_l, final_g, final):
    def tok(w):
        return pl.BlockSpec((ROUTE_TM, w), lambda i, *_: (i, 0))

    if final:
        out_specs = [pl.BlockSpec((ROUTE_TM, D_MODEL), lambda i, *_: (jnp.minimum(i, P_TILES - 1), 0)),
                     pl.BlockSpec((ROUTE_TM, D_MODEL), lambda i, *_: (jnp.maximum(i - P_TILES, 0), 0))]
        out_shape = [jax.ShapeDtypeStruct((N_P, D_MODEL), F32), jax.ShapeDtypeStruct((N_S, D_MODEL), F32)]
    else:
        out_specs = [tok(D_MODEL)]
        out_shape = [jax.ShapeDtypeStruct((N_TOK, D_MODEL), F32)]
    grid_spec = pltpu.PrefetchScalarGridSpec(
        num_scalar_prefetch=3,
        grid=(N_TILES,),
        in_specs=[tok(TOP_K), tok(TOP_K),
                  pl.BlockSpec(memory_space=pl.ANY),
                  tok(D_MODEL),
                  pl.BlockSpec((None, N_MOD, D_MODEL), lambda i, *_: (_mod_row(i * ROUTE_TM), 0, 0)),
                  pl.BlockSpec((1, D_MODEL), lambda i, *_: (0, 0))],
        out_specs=out_specs,
        scratch_shapes=[pltpu.VMEM((2, TILE_ROWS * ROW_TILES, LANES), F32), pltpu.SemaphoreType.DMA((2,))],
    )
    return pl.pallas_call(
        functools.partial(_combine_kernel, final=final),
        grid_spec=grid_spec,
        out_shape=out_shape,
        compiler_params=_params(1),
        name="combine_final" if final else "combine",
    )(*pieces, q, topw, y, xm, mods_l, final_g)


def _rope_tables():
    t = np.arange(DEC_SEQ)
    row = (t // GRID_W).astype(np.float32)
    col = (t % GRID_W).astype(np.float32)
    inv = jnp.asarray(ROPE_THETA, F32) ** (-jnp.arange(ROPE_PAIRS, dtype=F32) / ROPE_PAIRS)
    ang = jnp.concatenate([jnp.asarray(row)[:, None] * inv, jnp.asarray(col)[:, None] * inv], axis=-1)
    cos = jnp.repeat(jnp.cos(ang), 2, axis=-1)
    sin = jnp.repeat(jnp.sin(ang), 2, axis=-1)
    sign = jnp.asarray(np.tile(np.array([-1.0, 1.0], np.float32), HEAD_DIM // 2))
    return jnp.tile(cos, (1, A_HEADS)), jnp.tile(sin * sign, (1, A_HEADS))


def _routing_tables(tile_cnt):
    i32 = jnp.int32
    carry = jnp.cumsum(tile_cnt, axis=0) - tile_cnt
    counts = jnp.sum(tile_cnt, axis=0)
    padded = (counts + EXP_BLOCK - 1) // EXP_BLOCK * EXP_BLOCK
    pad_end = jnp.cumsum(padded)
    pad_start = pad_end - padded
    rowstart = (pad_start[None, :] + carry).astype(i32)
    blk_row = jnp.arange(N_BLOCKS, dtype=i32) * EXP_BLOCK
    blk_exp = jnp.sum((blk_row[:, None] >= pad_end[None, :]).astype(i32), axis=1)
    blk_exp = jnp.minimum(blk_exp, N_EXPERTS - 1)
    eid = jnp.arange(N_EXPERTS, dtype=i32)

    def pick(table, idx):
        return jnp.sum(jnp.where(idx[:, None] == eid[None, :], table[None, :], 0), axis=1).astype(i32)

    blk_start = pick(pad_start, blk_exp)
    nvalid = jnp.clip(pick(counts, blk_exp) - (blk_row - blk_start), 0, EXP_BLOCK).astype(i32)
    first = jnp.logical_and(blk_row == blk_start, nvalid > 0)
    active = counts > 0
    act_rank = jnp.cumsum(active.astype(i32)) - 1
    later = jnp.logical_and(active[None, :], eid[None, :] > eid[:, None])
    nxt = jnp.min(jnp.where(later, eid[None, :], N_EXPERTS), axis=1)
    nxt = jnp.where(nxt == N_EXPERTS, -1, nxt).astype(i32)
    blk_rank = pick(act_rank, blk_exp)
    head = jnp.logical_and(first, blk_rank == 0)
    n_used = pad_end[-1] // EXP_BLOCK
    io_blk = jnp.minimum(jnp.arange(N_BLOCKS, dtype=i32), n_used - 1).astype(i32)
    tables = (blk_exp, nvalid, first.astype(i32), head.astype(i32), (blk_rank % W_SLOTS).astype(i32),
              pick(nxt + 1, blk_exp) - 1, io_blk)
    return rowstart, tables


def kernel(x_prompt, x_sample, cache_attn_k, cache_attn_v, cache_na_k, cache_na_v, c, c_ctx, w_mod, b_mod, norm1_g, norm2_g, w_in, b_gate, q_norm_g, k_norm_g, na_rpb, conv_w, conv_b, conv_ln_g, conv_ln_b, sgu_ln_g, sgu_ln_b, sgu_w, sgu_b, w_branch, w_out, router_w, router_b, exp_w1, exp_b1, exp_w2, exp_b2, final_g):
    stream = (x_prompt.reshape(N_P, D_MODEL), x_sample.reshape(N_S, D_MODEL), 0)
    cvec =jnp.zeros((SUBLANES, D_MODEL), F32).at[0].set(c_ctx).at[1:1 + DEC_BATCH].set(c)
    mods = _modulation(cvec, w_mod, b_mod).reshape(DEPTH, SUBLANES, N_MOD, D_MODEL)
    cos_t, sin_t = _rope_tables()
    cak = cache_attn_k.reshape(DEC_BATCH, DEPTH, PAST_LEN, A_KV)
    cav = cache_attn_v.reshape(DEC_BATCH, DEPTH, PAST_LEN, A_KV)
    cbk = cache_na_k.reshape(DEC_BATCH, DEPTH, PAST_LEN, B_W)
    cbv = cache_na_v.reshape(DEC_BATCH, DEPTH, PAST_LEN, B_W)
    w_in_b = w_in.astype(BF16)
    w_br = w_branch.astype(BF16)
    w_o = w_out.astype(BF16)
    rw_hi = router_w.astype(BF16)
    rw_lo = (router_w - rw_hi.astype(F32)).astype(BF16)
    final_g2 = final_g.reshape(1, D_MODEL)

    caches = ()
    outs = None
    for l in range(DEPTH):
        mods_l = mods[l]
        g1 = norm1_g[l].reshape(1, D_MODEL)
        gq = jnp.tile(q_norm_g[l], A_HEADS).reshape(1, A_Q)
        gk = jnp.tile(k_norm_g[l], A_KV_HEADS).reshape(1, A_KV)
        aq, ak, av, bq, bk, bv, cz, dz = _in_proj(stream, mods_l, g1, w_in_b, l)
        ya, yb, *caches = _prompt_attn(aq, ak, av, bq, bk, bv, gq, gk, caches, l)
        ya = _sample_attn(aq, ak, av, cak, cav, cos_t, sin_t, gq, gk, ya, l)
        yb = _na_attn(bq, bk, bv, cbk, cbv, _na_bias(na_rpb[l]), yb, l)
        cw = conv_w[l]
        cb = conv_b[l].reshape(1, C_WIDTH)
        cg = conv_ln_g[l].reshape(1, C_WIDTH)
        cbb = conv_ln_b[l].reshape(1, C_WIDTH)
        yc = _conv_call(cz, cw, cb, cg, cbb, SEQ, 0, BATCH)
        yc = _conv_call(cz, cw, cb, cg, cbb, DEC_SEQ, N_P // DEC_SEQ, DEC_BATCH, partial_out=yc)
        bs_full = jnp.repeat(sgu_b[l].T, SGU_GW, axis=1)
        yd = _sgu(dz, sgu_ln_g[l].reshape(1, SGU_WIDTH), sgu_ln_b[l].reshape(1, SGU_WIDTH),
                  sgu_w[l].astype(BF16), bs_full)
        xm, h2, top_w, q, tile_cnt, tile_off = _merge(
            stream, ya, yb, yc, yd, mods_l, g1, w_in_b, b_gate[l].reshape(1, N_BRANCH * D_MODEL), w_br[l], w_o[l],
            norm2_g[l].reshape(1, D_MODEL), rw_hi[l], rw_lo[l], router_b[l].reshape(1, N_EXPERTS), l)
        tile_cnt = tile_cnt.reshape(N_TILES, N_EXPERTS)
        rowstart, tables = _routing_tables(tile_cnt)
        pieces = _piece_lists(tile_cnt, tile_off.reshape(N_TILES, N_EXPERTS), rowstart)
        xs = _dispatch(pieces, q, h2)
        y = _experts(tables, xs, exp_w1, exp_b1, exp_w2, exp_b2, l)
        outs = _combine(pieces, q, top_w, y, xm, mods_l, final_g2, l == DEPTH - 1)
        stream = (outs[0], outs[0], N_P)

    new_k, new_v, new_bk, new_bv = caches
    return (outs[0].reshape(BATCH, SEQ, D_MODEL), outs[1].reshape(DEC_BATCH, DEC_SEQ, D_MODEL),
            new_k.reshape(BATCH, DEPTH, SEQ, A_KV_HEADS, HEAD_DIM),
            new_v.reshape(BATCH, DEPTH, SEQ, A_KV_HEADS, HEAD_DIM),
            new_bk.reshape(BATCH, DEPTH, SEQ, B_HEADS, HEAD_DIM),
            new_bv.reshape(BATCH, DEPTH, SEQ, B_HEADS, HEAD_DIM))
```
